```python
import math
import numpy as np
import jax
import jax.numpy as jnp
from jax import lax

D_MODEL = 1024
BATCH = 2
SEQ = 8192
DEPTH = 4
DEC_BATCH = 32
DEC_SEQ = 2048
PAST_LEN = 128

DIFF_HEADS = 4
DIFF_QK_DIM = 32
DIFF_V_DIM = 64
DIL_HEADS = 6
DIL_HEAD_DIM = 64
DIL_PAIRS = ((128, 1), (512, 4), (2048, 16))
MLA_HEADS = 6
MLA_NOPE_DIM = 64
MLA_ROPE_DIM = 32
MLA_V_DIM = 64
MLA_Q_RANK = 256
MLA_KV_RANK = 128
MLA_ROPE_THETA = 10000.0
MIX_WIDTH = DIFF_HEADS * DIFF_V_DIM + DIL_HEADS * DIL_HEAD_DIM + MLA_HEADS * MLA_V_DIM
IN_SPLITS = (DIFF_HEADS * DIFF_QK_DIM,) * 4 + (DIFF_HEADS * DIFF_V_DIM,) + (DIL_HEADS * DIL_HEAD_DIM,) * 3 + (MLA_Q_RANK, MLA_KV_RANK, MLA_ROPE_DIM)
IN_WIDTH = sum(IN_SPLITS)
ROPE_THETA = 500000.0
ROT_FRACTION = 4
N_EXPERTS = 16
N_GROUPS = 4
GROUP_SIZE = N_EXPERTS // N_GROUPS
TOP_K = 2
D_FF_EXPERT = 256
Q_BLOCK = 128
ALPHA = (2 * DEPTH) ** 0.25
BETA = (8 * DEPTH) ** -0.25
NEG = -1e30

kernel_name = 'hybrid_diff_dilated_mla_moe_encoder'


def _layer_norm(x, g, b, eps=1e-5):
    xf = x.astype(jnp.float32)
    mu = jnp.mean(xf, -1, keepdims=True)
    var = jnp.mean(jnp.square(xf - mu), -1, keepdims=True)
    y = (xf - mu) * lax.rsqrt(var + eps) * g.astype(jnp.float32) + b.astype(jnp.float32)
    return y.astype(x.dtype)


def _rms_norm(x, g, eps=1e-6):
    xf = x.astype(jnp.float32)
    y = xf * lax.rsqrt(jnp.mean(jnp.square(xf), -1, keepdims=True) + eps) * g.astype(jnp.float32)
    return y.astype(x.dtype)


def _rope(x, rot_dim, theta):
    seq = x.shape[1]
    half = rot_dim // 2
    inv_freq = theta ** (-2.0 * jnp.arange(half, dtype=jnp.float32) / rot_dim)
    ang = jnp.arange(seq, dtype=jnp.float32)[:, None] * inv_freq[None, :]
    cos = jnp.cos(ang)[None, :, None, :].astype(x.dtype)
    sin = jnp.sin(ang)[None, :, None, :].astype(x.dtype)
    x1 = x[..., :half]
    x2 = x[..., half:rot_dim]
    return jnp.concatenate([x1 * cos - x2 * sin, x2 * cos + x1 * sin, x[..., rot_dim:]], axis=-1)


def _q_blocks(t):
    b, s = t.shape[:2]
    return jnp.moveaxis(t.reshape(b, s // Q_BLOCK, Q_BLOCK, *t.shape[2:]), 1, 0)


def _unblock(t):
    t = jnp.moveaxis(t, 0, 1)
    return t.reshape(t.shape[0], -1, *t.shape[3:])


def _diff_attention(q1, q2, k1, k2, v, lam, lam_init, g_subln):
    b, s, h, _ = v.shape
    scale = DIFF_QK_DIM ** -0.5

    def block(qs):
        q1b, q2b = qs
        a1 = jax.nn.softmax(jnp.einsum('bqhd,bkhd->bhqk', q1b, k1).astype(jnp.float32) * scale, axis=-1)
        a2 = jax.nn.softmax(jnp.einsum('bqhd,bkhd->bhqk', q2b, k2).astype(jnp.float32) * scale, axis=-1)
        return jnp.einsum('bhqk,bkhd->bqhd', (a1 - lam * a2).astype(v.dtype), v)

    o = _unblock(lax.map(block, (_q_blocks(q1), _q_blocks(q2))))
    o = _rms_norm(o, g_subln) * (1.0 - lam_init)
    return o.reshape(b, s, h * DIFF_V_DIM)


def _band_attention(q, k, v, radius):
    z, length, d = q.shape
    blk = radius
    nb = -(-length // blk)
    lp = nb * blk
    qb = jnp.pad(q, ((0, 0), (0, lp - length), (0, 0))).reshape(z, nb, blk, d)

    def ctx(t):
        tp = jnp.pad(t, ((0, 0), (blk, lp - length + blk), (0, 0))).reshape(z, nb + 2, blk, d)
        return jnp.concatenate([tp[:, :-2], tp[:, 1:-1], tp[:, 2:]], axis=2)

    kc = ctx(k)
    vc = ctx(v)
    s = jnp.einsum('znqd,znkd->znqk', qb, kc).astype(jnp.float32) * (d ** -0.5)
    qpos = jnp.arange(nb)[:, None] * blk + jnp.arange(blk)[None, :]
    kpos = jnp.arange(nb)[:, None] * blk - blk + jnp.arange(3 * blk)[None, :]
    valid = (jnp.abs(kpos[:, None, :] - qpos[:, :, None]) <= radius) & ((kpos >= 0) & (kpos < length))[:, None, :]
    s = jnp.where(valid, s, NEG)
    m = jnp.max(s, -1, keepdims=True)
    e = jnp.exp(s - m)
    den = jnp.sum(e, -1, keepdims=True)
    o = jnp.einsum('znqk,znkd->znqd', (e / den).astype(v.dtype), vc).reshape(z, lp, d)[:, :length]
    lse = (m + jnp.log(den))[..., 0].reshape(z, lp)[:, :length]
    return o, lse


def _dilated_mixture(q, k, v):
    b, s, h, d = q.shape
    outs = []
    lses = []
    for window, dil in DIL_PAIRS:
        radius = window // (2 * dil)
        length = s // dil

        def to_sub(t):
            return t.reshape(b, length, dil, h, d).transpose(0, 2, 3, 1, 4).reshape(b * dil * h, length, d)

        o, lse = _band_attention(to_sub(q), to_sub(k), to_sub(v), radius)
        outs.append(o.reshape(b, dil, h, length, d).transpose(0, 3, 1, 2, 4).reshape(b, s, h, d))
        lses.append(lse.reshape(b, dil, h, length).transpose(0, 3, 1, 2).reshape(b, s, h))
    w = jax.nn.softmax(jnp.stack(lses, 0), axis=0)
    o = jnp.einsum('pbsh,pbshd->bshd', w.astype(q.dtype), jnp.stack(outs, 0))
    return o.reshape(b, s, h * d)


def _mla(cq, ckv, kr, g_cq, g_ckv, w_uq, w_ukv):
    b, s, _ = cq.shape
    q = (_rms_norm(cq, g_cq) @ w_uq).reshape(b, s, MLA_HEADS, MLA_NOPE_DIM + MLA_ROPE_DIM)
    q_nope = q[..., :MLA_NOPE_DIM]
    q_rope = _rope(q[..., MLA_NOPE_DIM:], MLA_ROPE_DIM, MLA_ROPE_THETA)
    kv = (_rms_norm(ckv, g_ckv) @ w_ukv).reshape(b, s, MLA_HEADS, MLA_NOPE_DIM + MLA_V_DIM)
    k_nope = kv[..., :MLA_NOPE_DIM]
    v = kv[..., MLA_NOPE_DIM:]
    k_rope = _rope(kr[:, :, None, :], MLA_ROPE_DIM, MLA_ROPE_THETA)[:, :, 0, :]
    scale = (MLA_NOPE_DIM + MLA_ROPE_DIM) ** -0.5

    def block(qs):
        qn, qr = qs
        sc = jnp.einsum('bqhd,bkhd->bhqk', qn, k_nope) + jnp.einsum('bqhd,bkd->bhqk', qr, k_rope)
        p = jax.nn.softmax(sc.astype(jnp.float32) * scale, axis=-1)
        return jnp.einsum('bhqk,bkhd->bqhd', p.astype(v.dtype), v)

    o = _unblock(lax.map(block, (_q_blocks(q_nope), _q_blocks(q_rope))))
    return o.reshape(b, s, MLA_HEADS * MLA_V_DIM)


def _token_mixer(h, l, p):
    b, s, _ = h.shape
    f32 = jnp.float32
    offsets = np.cumsum(IN_SPLITS)[:-1].tolist()
    qa1, qa2, ka1, ka2, va, qb, kb, vb, cq, ckv, kr = jnp.split(h @ p['w_in'][l], offsets, axis=-1)

    def heads(t, n):
        return t.reshape(b, s, n, -1)

    rot_a = DIFF_QK_DIM // ROT_FRACTION
    qa1, qa2, ka1, ka2 = [_rope(heads(t, DIFF_HEADS), rot_a, ROPE_THETA) for t in (qa1, qa2, ka1, ka2)]
    lam_init = 0.8 - 0.6 * math.exp(-0.3 * l)
    lam = (jnp.exp(jnp.sum(p['lam_q1'][l].astype(f32) * p['lam_k1'][l].astype(f32)))
           - jnp.exp(jnp.sum(p['lam_q2'][l].astype(f32) * p['lam_k2'][l].astype(f32))) + lam_init)
    out_a = _diff_attention(qa1, qa2, ka1, ka2, heads(va, DIFF_HEADS), lam, lam_init, p['g_subln'][l])
    rot_b = DIL_HEAD_DIM // ROT_FRACTION
    out_b = _dilated_mixture(_rope(heads(qb, DIL_HEADS), rot_b, ROPE_THETA),
                             _rope(heads(kb, DIL_HEADS), rot_b, ROPE_THETA),
                             heads(vb, DIL_HEADS))
    out_c = _mla(cq, ckv, kr, p['g_cq'][l], p['g_ckv'][l], p['w_uq'][l], p['w_ukv'][l])
    return jnp.concatenate([out_a, out_b, out_c], axis=-1) @ p['w_out'][l]


def _moe(h, w_router, b_router, w1, w3, w2):
    b, s, d = h.shape
    t = h.reshape(-1, d)
    f32 = jnp.float32
    score = jax.nn.sigmoid((t @ w_router).astype(f32))
    biased = (score + b_router.astype(f32)).reshape(-1, N_GROUPS, GROUP_SIZE)
    group_score = lax.top_k(biased, 2)[0].sum(-1)
    gsel = jnp.argmax(group_score, axis=-1)
    in_group = jnp.arange(N_GROUPS)[None, :] == gsel[:, None]
    cand = jnp.where(in_group[:, :, None], biased, NEG).reshape(-1, N_EXPERTS)
    _, idx = lax.top_k(cand, TOP_K)
    sel = jnp.take_along_axis(score, idx, axis=-1)
    wts = sel / jnp.sum(sel, -1, keepdims=True)
    gate = jnp.einsum('tk,tke->te', wts, jax.nn.one_hot(idx, N_EXPERTS, dtype=f32)).astype(t.dtype)
    y = jnp.zeros_like(t)
    for e in range(N_EXPERTS):
        hid = jax.nn.silu(t @ w1[e]) * (t @ w3[e])
        y = y + gate[:, e:e + 1] * (hid @ w2[e])
    return y.reshape(b, s, d)


def _encoder_trunk(x, c, p):
    x = _layer_norm(x, p['ln_in_g'], p['ln_in_b'])
    cs = jax.nn.silu(c)
    for l in range(DEPTH):
        mod = cs @ p['w_ada'][l] + p['b_ada'][l]
        sh1, sc1, g1, sh2, sc2, g2 = jnp.split(mod[:, None, :], 6, axis=-1)
        h = x * (1.0 + sc1) + sh1
        x = _layer_norm(ALPHA * x + g1 * _token_mixer(h, l, p), p['ln1_g'][l], p['ln1_b'][l])
        h = x * (1.0 + sc2) + sh2
        y = _moe(h, p['w_router'], p['b_router'], p['w1'][l], p['w3'][l], p['w2'][l])
        x = _layer_norm(ALPHA * x + g2 * y, p['ln2_g'][l], p['ln2_b'][l])
    return x


def setup_inputs(seed: int = 0) -> dict:
    key = jax.random.key(seed)
    ks = jax.random.split(key, 32)
    f32 = jnp.float32
    D = D_MODEL

    def nrm(k, shape, scale):
        return jax.random.normal(k, shape, f32) * scale

    return {
        'x_prompt': nrm(ks[0], (BATCH, SEQ, D), 1.0),
        'x_sample': nrm(ks[1], (DEC_BATCH, DEC_SEQ, D), 1.0),
        'c_prompt': nrm(ks[2], (BATCH, D), 1.0),
        'c_sample': nrm(ks[3], (DEC_BATCH, D), 1.0),
        'ln_in_g': 1.0 + nrm(ks[4], (D,), 0.02),
        'ln_in_b': nrm(ks[5], (D,), 0.02),
        'w_router': nrm(ks[6], (D, N_EXPERTS), D ** -0.5),
        'b_router': nrm(ks[7], (N_EXPERTS,), 0.01),
        'w_ada': nrm(ks[8], (DEPTH, D, 6 * D), 0.5 * D ** -0.5),
        'b_ada': nrm(ks[9], (DEPTH, 6 * D), 0.02),
        'w_in': nrm(ks[10], (DEPTH, D, IN_WIDTH), D ** -0.5),
        'lam_q1': nrm(ks[11], (DEPTH, DIFF_QK_DIM), 0.1),
        'lam_k1': nrm(ks[12], (DEPTH, DIFF_QK_DIM), 0.1),
        'lam_q2': nrm(ks[13], (DEPTH, DIFF_QK_DIM), 0.1),
        'lam_k2': nrm(ks[14], (DEPTH, DIFF_QK_DIM), 0.1),
        'g_subln': 1.0 + nrm(ks[15], (DEPTH, DIFF_V_DIM), 0.02),
        'g_cq': 1.0 + nrm(ks[16], (DEPTH, MLA_Q_RANK), 0.02),
        'g_ckv': 1.0 + nrm(ks[17], (DEPTH, MLA_KV_RANK), 0.02),
        'w_uq': nrm(ks[18], (DEPTH, MLA_Q_RANK, MLA_HEADS * (MLA_NOPE_DIM + MLA_ROPE_DIM)), MLA_Q_RANK ** -0.5),
        'w_ukv': nrm(ks[19], (DEPTH, MLA_KV_RANK, MLA_HEADS * (MLA_NOPE_DIM + MLA_V_DIM)), MLA_KV_RANK ** -0.5),
        'w_out': nrm(ks[20], (DEPTH, MIX_WIDTH, D), BETA * MIX_WIDTH ** -0.5),
        'ln1_g': 1.0 + nrm(ks[21], (DEPTH, D), 0.02),
        'ln1_b': nrm(ks[22], (DEPTH, D), 0.02),
        'w1': nrm(ks[23], (DEPTH, N_EXPERTS, D, D_FF_EXPERT), D ** -0.5),
        'w3': nrm(ks[24], (DEPTH, N_EXPERTS, D, D_FF_EXPERT), D ** -0.5),
        'w2': nrm(ks[25], (DEPTH, N_EXPERTS, D_FF_EXPERT, D), BETA * D_FF_EXPERT ** -0.5),
        'ln2_g': 1.0 + nrm(ks[26], (DEPTH, D), 0.02),
        'ln2_b': nrm(ks[27], (DEPTH, D), 0.02),
    }


def reference(x_prompt, x_sample, c_prompt, c_sample, ln_in_g, ln_in_b, w_router, b_router,
              w_ada, b_ada, w_in, lam_q1, lam_k1, lam_q2, lam_k2, g_subln, g_cq, g_ckv,
              w_uq, w_ukv, w_out, ln1_g, ln1_b, w1, w3, w2, ln2_g, ln2_b):
    p = {
        'ln_in_g': ln_in_g, 'ln_in_b': ln_in_b, 'w_router': w_router, 'b_router': b_router,
        'w_ada': w_ada, 'b_ada': b_ada, 'w_in': w_in,
        'lam_q1': lam_q1, 'lam_k1': lam_k1, 'lam_q2': lam_q2, 'lam_k2': lam_k2,
        'g_subln': g_subln, 'g_cq': g_cq, 'g_ckv': g_ckv, 'w_uq': w_uq, 'w_ukv': w_ukv,
        'w_out': w_out, 'ln1_g': ln1_g, 'ln1_b': ln1_b, 'w1': w1, 'w3': w3, 'w2': w2,
        'ln2_g': ln2_g, 'ln2_b': ln2_b,
    }
    y_prompt = _encoder_trunk(x_prompt, c_prompt, p)
    y_sample = _encoder_trunk(x_sample, c_sample, p)
    return (y_prompt, y_sample)
```

```python
import functools
import math

import numpy as np
import jax
import jax.numpy as jnp
from jax import lax
from jax.experimental import pallas as pl
from jax.experimental.pallas import tpu as pltpu

F32 = jnp.float32
MXU_DTYPE = jnp.bfloat16

LANES = 128
D_MODEL = 1024
DEPTH = 4
DIFF_HEADS = 4
DIFF_QK = 32
DIFF_V = 64
DIL_HEADS = 6
DIL_DIM = 64
DIL_PAIRS = ((128, 1), (512, 4), (2048, 16))
DIL_RADIUS = 64
MLA_HEADS = 6
MLA_NOPE = 64
MLA_ROPE = 32
MLA_V = 64
MLA_Q_RANK = 256
MLA_KV_RANK = 128
MLA_THETA = 10000.0
ROPE_THETA = 500000.0
N_EXPERTS = 16
GROUP_SIZE = 4
D_FF = 256
ALPHA = (2 * DEPTH) ** 0.25
NEG = -1e30

C_Q1, C_Q2, C_K1, C_K2, C_VA = 0, 128, 256, 384, 512
C_QB, C_KB, C_VB = 768, 1152, 1536
C_CQ, C_CKV, C_KR, C_END = 1920, 2176, 2304, 2432
KR_LANE = 64

VMEM_LIMIT = 48 * 1024 * 1024


def _cparams(sem):
    return pltpu.CompilerParams(dimension_semantics=sem, vmem_limit_bytes=VMEM_LIMIT)


def _lane_iota(shape):
    return lax.broadcasted_iota(jnp.int32, shape, len(shape) - 1)


def _layer_norm(z, g, b):
    mu = jnp.mean(z, axis=-1, keepdims=True)
    zc = z - mu
    var = jnp.mean(zc * zc, axis=-1, keepdims=True)
    return zc * lax.rsqrt(var + 1e-5) * g + b


def _rope(t, tab_ref, typ, half):
    c = tab_ref[3 * typ]
    s_up = tab_ref[3 * typ + 1]
    s_dn = tab_ref[3 * typ + 2]
    return t * c + pltpu.roll(t, LANES - half, 1) * s_up + pltpu.roll(t, half, 1) * s_dn


def _ln_in_kernel(x_ref, g_ref, b_ref, o_ref):
    o_ref[0] = _layer_norm(x_ref[0], g_ref[...], b_ref[...])


def _ln_in(x, g, b, tm):
    B, S, D = x.shape
    return pl.pallas_call(
        _ln_in_kernel,
        grid=(B, S // tm),
        in_specs=[pl.BlockSpec((1, tm, D), lambda b, i: (b, i, 0)),
                  pl.BlockSpec((1, D), lambda b, i: (0, 0)),
                  pl.BlockSpec((1, D), lambda b, i: (0, 0))],
        out_specs=pl.BlockSpec((1, tm, D), lambda b, i: (b, i, 0)),
        out_shape=jax.ShapeDtypeStruct(x.shape, F32),
        compiler_params=_cparams(("parallel", "parallel")),
        name="ln_in",
    )(x, g.reshape(1, D), b.reshape(1, D))


def _ada_kernel(c_ref, w_ref, b_ref, o_ref):
    c = c_ref[...]
    cs = (c * (1.0 / (1.0 + jnp.exp(-c)))).astype(MXU_DTYPE)
    o_ref[0] = jnp.dot(cs, w_ref[0].astype(MXU_DTYPE), preferred_element_type=F32) + b_ref[0]


def _ada(c, w_ada, b_ada, tn=1536):
    R, D = c.shape
    L, _, N = w_ada.shape
    return pl.pallas_call(
        _ada_kernel,
        grid=(L, N // tn),
        in_specs=[pl.BlockSpec((R, D), lambda l, j: (0, 0)),
                  pl.BlockSpec((1, D, tn), lambda l, j: (l, 0, j)),
                  pl.BlockSpec((1, 1, tn), lambda l, j: (l, 0, j))],
        out_specs=pl.BlockSpec((1, R, tn), lambda l, j: (l, 0, j)),
        out_shape=jax.ShapeDtypeStruct((L, R, N), F32),
        compiler_params=_cparams(("parallel", "parallel")),
        name="ada",
    )(c, w_ada, b_ada.reshape(L, 1, N))


def _proj_kernel(x_ref, mod_ref, tab_ref, w_ref, gcq_ref, gckv_ref, wuq_ref, wuk_ref, wuv_ref,
                 qa_ref, ka_ref, va_ref, qb_ref, kb_ref, vb_ref, mq_ref, mk_ref, mv_ref):
    x = x_ref[0]
    sh1 = mod_ref[0, 0:1, :]
    sc1 = mod_ref[0, 1:2, :]
    h = (x * (1.0 + sc1) + sh1).astype(MXU_DTYPE)

    def mm(a, b):
        return jnp.dot(h, w_ref[:, a:b], preferred_element_type=F32)

    qa = mm(C_Q1, C_K1)
    ka = mm(C_K1, C_VA)
    sa = DIFF_QK ** -0.5
    for j in range(2):
        blk = slice(j * LANES, (j + 1) * LANES)
        qa_ref[0, :, blk] = (_rope(qa[:, blk], tab_ref, 0, DIFF_QK // 8) * sa).astype(qa_ref.dtype)
        ka_ref[0, :, blk] = _rope(ka[:, blk], tab_ref, 0, DIFF_QK // 8).astype(ka_ref.dtype)
    va_ref[0] = mm(C_VA, C_QB).astype(va_ref.dtype)

    qb = mm(C_QB, C_KB)
    kb = mm(C_KB, C_VB)
    sb = DIL_DIM ** -0.5
    for j in range(3):
        blk = slice(j * LANES, (j + 1) * LANES)
        qb_ref[0, :, blk] = (_rope(qb[:, blk], tab_ref, 1, DIL_DIM // 8) * sb).astype(qb_ref.dtype)
        kb_ref[0, :, blk] = _rope(kb[:, blk], tab_ref, 1, DIL_DIM // 8).astype(kb_ref.dtype)
    vb_ref[0] = mm(C_VB, C_CQ).astype(vb_ref.dtype)

    def rms(t, g):
        return (t * lax.rsqrt(jnp.mean(t * t, axis=-1, keepdims=True) + 1e-6) * g).astype(MXU_DTYPE)

    cq = rms(mm(C_CQ, C_CKV), gcq_ref[...])
    ckv = rms(mm(C_CKV, C_KR), gckv_ref[...])
    kr = _rope(mm(C_KR, C_END), tab_ref, 2, MLA_ROPE // 2)
    q = jnp.dot(cq, wuq_ref[...], preferred_element_type=F32)
    kn = jnp.dot(ckv, wuk_ref[...], preferred_element_type=F32)
    sc = (MLA_NOPE + MLA_ROPE) ** -0.5
    for j in range(MLA_HEADS):
        blk = slice(j * LANES, (j + 1) * LANES)
        mq_ref[0, :, blk] = (_rope(q[:, blk], tab_ref, 2, MLA_ROPE // 2) * sc).astype(mq_ref.dtype)
        mk_ref[0, :, blk] = (kn[:, blk] + kr).astype(mk_ref.dtype)
    mv_ref[0] = jnp.dot(ckv, wuv_ref[...], preferred_element_type=F32).astype(mv_ref.dtype)


def _proj(x, mod, tabs, w_in, g_cq, g_ckv, w_uq, w_uk, w_uv, tm):
    B, S, D = x.shape
    widths = (256, 256, 256, 384, 384, 384, 768, 768, 384)
    row = lambda b, i: (b, i, 0)
    const2 = lambda b, i: (0, 0)
    return pl.pallas_call(
        _proj_kernel,
        grid=(B, S // tm),
        in_specs=[pl.BlockSpec((1, tm, D), row),
                  pl.BlockSpec((1, 6, D), lambda b, i: (b, 0, 0)),
                  pl.BlockSpec((9, tm, LANES), lambda b, i: (0, i, 0)),
                  pl.BlockSpec(w_in.shape, const2),
                  pl.BlockSpec((1, MLA_Q_RANK), const2),
                  pl.BlockSpec((1, MLA_KV_RANK), const2),
                  pl.BlockSpec(w_uq.shape, const2),
                  pl.BlockSpec(w_uk.shape, const2),
                  pl.BlockSpec(w_uv.shape, const2)],
        out_specs=[pl.BlockSpec((1, tm, w), row) for w in widths],
        out_shape=[jax.ShapeDtypeStruct((B, S, w), MXU_DTYPE) for w in widths],
        compiler_params=_cparams(("parallel", "parallel")),
        name="proj_in",
    )(x, mod, tabs, w_in, g_cq.reshape(1, -1), g_ckv.reshape(1, -1), w_uq, w_uk, w_uv)


def _softmax_step(u, q, k, v, m_scr, l_scr, acc_scr):
    s = lax.dot_general(q, k, (((1,), (1,)), ((), ())), preferred_element_type=F32)
    m_prev = m_scr[u]
    m_new = jnp.maximum(m_prev, jnp.max(s, axis=-1, keepdims=True))
    alpha = jnp.exp(m_prev - m_new)
    p = jnp.exp(s - m_new)
    l_scr[u] = alpha * l_scr[u] + jnp.sum(p, axis=-1, keepdims=True)
    acc_scr[u] = alpha * acc_scr[u] + jnp.dot(p.astype(MXU_DTYPE), v, preferred_element_type=F32)
    m_scr[u] = m_new


def _init_softmax(m_scr, l_scr, acc_scr):
    m_scr[...] = jnp.full(m_scr.shape, NEG, F32)
    l_scr[...] = jnp.zeros(l_scr.shape, F32)
    acc_scr[...] = jnp.zeros(acc_scr.shape, F32)


def _diff_attn_kernel(lam_init, q1_ref, q2_ref, k1_ref, k2_ref, v_ref, lamp_ref, g_ref, o_ref,
                      q_scr, m_scr, l_scr, acc_scr):
    j = pl.program_id(1)
    ik = pl.program_id(3)

    @pl.when(ik == 0)
    def _():
        _init_softmax(m_scr, l_scr, acc_scr)
        lane = _lane_iota(q1_ref.shape[1:])
        for hh in range(2):
            lo = (2 * j + hh) * DIFF_QK
            msk = (lane >= lo) & (lane < lo + DIFF_QK)
            q_scr[2 * hh] = jnp.where(msk, q1_ref[0].astype(F32), 0.0).astype(q_scr.dtype)
            q_scr[2 * hh + 1] = jnp.where(msk, q2_ref[0].astype(F32), 0.0).astype(q_scr.dtype)

    v = v_ref[0]
    k1 = k1_ref[0]
    k2 = k2_ref[0]
    for hh in range(2):
        _softmax_step(2 * hh, q_scr[2 * hh], k1, v, m_scr, l_scr, acc_scr)
        _softmax_step(2 * hh + 1, q_scr[2 * hh + 1], k2, v, m_scr, l_scr, acc_scr)

    @pl.when(ik == pl.num_programs(3) - 1)
    def _():
        lp = lamp_ref[...]
        lam = (jnp.exp(jnp.sum(lp[0:1] * lp[1:2], axis=-1, keepdims=True))
               - jnp.exp(jnp.sum(lp[2:3] * lp[3:4], axis=-1, keepdims=True)) + lam_init)
        lane = _lane_iota(acc_scr.shape[1:])
        outs = []
        for hh in range(2):
            o = acc_scr[2 * hh] / l_scr[2 * hh] - lam * (acc_scr[2 * hh + 1] / l_scr[2 * hh + 1])
            half = (lane >= hh * DIFF_V) & (lane < (hh + 1) * DIFF_V)
            ms = jnp.sum(jnp.where(half, o * o, 0.0), axis=-1, keepdims=True) * (1.0 / DIFF_V)
            outs.append(o * lax.rsqrt(ms + 1e-6) * g_ref[...] * (1.0 - lam_init))
        o_ref[0] = jnp.where(lane < DIFF_V, outs[0], outs[1]).astype(o_ref.dtype)


def _diff_attn(qa, ka, va, lam_pack, g_sub, lam_init, tq, tk):
    B, S, _ = qa.shape
    nv = 4
    kern = functools.partial(_diff_attn_kernel, lam_init)
    return pl.pallas_call(
        kern,
        grid=(B, DIFF_HEADS // 2, S // tq, S // tk),
        in_specs=[pl.BlockSpec((1, tq, LANES), lambda b, j, iq, ik: (b, iq, 0)),
                  pl.BlockSpec((1, tq, LANES), lambda b, j, iq, ik: (b, iq, 1)),
                  pl.BlockSpec((1, tk, LANES), lambda b, j, iq, ik: (b, ik, 0)),
                  pl.BlockSpec((1, tk, LANES), lambda b, j, iq, ik: (b, ik, 1)),
                  pl.BlockSpec((1, tk, LANES), lambda b, j, iq, ik: (b, ik, j)),
                  pl.BlockSpec((4, LANES), lambda b, j, iq, ik: (0, 0)),
                  pl.BlockSpec((1, LANES), lambda b, j, iq, ik: (0, 0))],
        out_specs=pl.BlockSpec((1, tq, LANES), lambda b, j, iq, ik: (b, iq, j)),
        out_shape=jax.ShapeDtypeStruct((B, S, DIFF_HEADS * DIFF_V), MXU_DTYPE),
        scratch_shapes=[pltpu.VMEM((nv, tq, LANES), MXU_DTYPE),
                        pltpu.VMEM((nv, tq, 1), F32),
                        pltpu.VMEM((nv, tq, 1), F32),
                        pltpu.VMEM((nv, tq, LANES), F32)],
        compiler_params=_cparams(("parallel", "parallel", "parallel", "arbitrary")),
        name="diff_attn",
    )(qa, qa, ka, ka, va, lam_pack, g_sub)


def _mla_attn_kernel(q_ref, k_ref, v_ref, o_ref, m_scr, l_scr, acc_scr):
    ik = pl.program_id(3)

    @pl.when(ik == 0)
    def _():
        _init_softmax(m_scr, l_scr, acc_scr)

    v = v_ref[0]
    for hh in range(2):
        blk = slice(hh * LANES, (hh + 1) * LANES)
        _softmax_step(hh, q_ref[0, :, blk], k_ref[0, :, blk], v, m_scr, l_scr, acc_scr)

    @pl.when(ik == pl.num_programs(3) - 1)
    def _():
        lane = _lane_iota(acc_scr.shape[1:])
        o0 = acc_scr[0] / l_scr[0]
        o1 = acc_scr[1] / l_scr[1]
        o_ref[0] = jnp.where(lane < MLA_V, o0, o1).astype(o_ref.dtype)


def _mla_attn(mq, mk, mv, tq, tk):
    B, S, _ = mq.shape
    return pl.pallas_call(
        _mla_attn_kernel,
        grid=(B, MLA_HEADS // 2, S // tq, S // tk),
        in_specs=[pl.BlockSpec((1, tq, 2 * LANES), lambda b, j, iq, ik: (b, iq, j)),
                  pl.BlockSpec((1, tk, 2 * LANES), lambda b, j, iq, ik: (b, ik, j)),
                  pl.BlockSpec((1, tk, LANES), lambda b, j, iq, ik: (b, ik, j))],
        out_specs=pl.BlockSpec((1, tq, LANES), lambda b, j, iq, ik: (b, iq, j)),
        out_shape=jax.ShapeDtypeStruct((B, S, MLA_HEADS * MLA_V), MXU_DTYPE),
        scratch_shapes=[pltpu.VMEM((2, tq, 1), F32),
                        pltpu.VMEM((2, tq, 1), F32),
                        pltpu.VMEM((2, tq, LANES), F32)],
        compiler_params=_cparams(("parallel", "parallel", "parallel", "arbitrary")),
        name="mla_attn",
    )(mq, mk, mv)


DIL_SUB = 128
DIL_WIDTH = DIL_HEADS * DIL_DIM


def _dil_kernel(length, q_ref, kp_ref, kc_ref, kn_ref, vp_ref, vc_ref, vn_ref, o_ref, lse_ref,
                kbuf, vbuf):
    i = pl.program_id(2)
    tq = q_ref.shape[1]
    R = DIL_RADIUS
    kbuf[0:R] = kp_ref[0]
    kbuf[R:R + tq] = kc_ref[0]
    kbuf[R + tq:] = kn_ref[0]
    vbuf[0:R] = vp_ref[0]
    vbuf[R:R + tq] = vc_ref[0]
    vbuf[R + tq:] = vn_ref[0]
    sub = min(DIL_SUB, tq)
    win = sub + 2 * R
    row = lax.broadcasted_iota(jnp.int32, (sub, win), 0)
    col = lax.broadcasted_iota(jnp.int32, (sub, win), 1)
    lane = _lane_iota((sub, LANES))
    for t in range(tq // sub):
        kpos = i * tq + t * sub - R + col
        rel = col - row
        valid = (rel >= 0) & (rel <= 2 * R) & (kpos >= 0) & (kpos < length)
        for hp in range(DIL_HEADS // 2):
            blk = slice(hp * LANES, (hp + 1) * LANES)
            q = q_ref[0, t * sub:(t + 1) * sub, blk].astype(F32)
            kw = kbuf[t * sub:t * sub + win, blk]
            vw = vbuf[t * sub:t * sub + win, blk]
            outs = []
            lses = []
            for hh in range(2):
                half = (lane >= hh * DIL_DIM) & (lane < (hh + 1) * DIL_DIM)
                qm = jnp.where(half, q, 0.0).astype(MXU_DTYPE)
                s = lax.dot_general(qm, kw, (((1,), (1,)), ((), ())), preferred_element_type=F32)
                s = jnp.where(valid, s, NEG)
                m = jnp.max(s, axis=-1, keepdims=True)
                e = jnp.exp(s - m)
                den = jnp.sum(e, axis=-1, keepdims=True)
                outs.append(jnp.dot((e / den).astype(MXU_DTYPE), vw, preferred_element_type=F32))
                lses.append(m + jnp.log(den))
            o_ref[0, t * sub:(t + 1) * sub, blk] = jnp.where(lane < DIL_DIM, outs[0], outs[1])
            lse_ref[0, t * sub:(t + 1) * sub, blk] = jnp.where(lane < DIL_DIM, lses[0], lses[1])


def _dil_attn(qb, kb, vb, dil, tq_max=512):
    B, S, W = qb.shape
    L = S // dil
    tq = min(tq_max, L)
    R = DIL_RADIUS
    nr = L // R
    view = lambda t: t.reshape(B, L, dil * W)
    cur = pl.BlockSpec((1, tq, W), lambda b, r, i: (b, i, r))
    prv = pl.BlockSpec((1, R, W), lambda b, r, i: (b, jnp.maximum(i * (tq // R) - 1, 0), r))
    nxt = pl.BlockSpec((1, R, W), lambda b, r, i: (b, jnp.minimum((i + 1) * (tq // R), nr - 1), r))
    o, lse = pl.pallas_call(
        functools.partial(_dil_kernel, L),
        grid=(B, dil, L // tq),
        in_specs=[cur, prv, cur, nxt, prv, cur, nxt],
        out_specs=[cur, cur],
        out_shape=[jax.ShapeDtypeStruct((B, L, dil * W), F32)] * 2,
        scratch_shapes=[pltpu.VMEM((tq + 2 * R, W), MXU_DTYPE)] * 2,
        compiler_params=_cparams(("parallel", "parallel", "parallel")),
        name=f"dil_attn_{dil}",
    )(view(qb), view(kb), view(kb), view(kb), view(vb), view(vb), view(vb))
    return o.reshape(B, S, W), lse.reshape(B, S, W)


def _partner(x, k, lane):
    up = pltpu.roll(x, LANES - k, 1)
    dn = pltpu.roll(x, k, 1)
    return jnp.where((lane & k) == 0, up, dn)


def _route(logits, bias, lane):
    valid = lane < N_EXPERTS
    lane_f = lane.astype(F32)
    score = 1.0 / (1.0 + jnp.exp(-logits))
    biased = jnp.where(valid, score + bias, NEG)
    p1 = _partner(biased, 1, lane)
    hi = jnp.maximum(biased, p1)
    lo = jnp.minimum(biased, p1)
    hi_p = _partner(hi, 2, lane)
    lo_p = _partner(lo, 2, lane)
    top1 = jnp.maximum(hi, hi_p)
    top2 = jnp.maximum(jnp.minimum(hi, hi_p), jnp.maximum(lo, lo_p))
    gscore = jnp.where(valid, top1 + top2, NEG)
    gidx = (lane // GROUP_SIZE).astype(F32)
    big = float(LANES)
    gmax = jnp.max(gscore, axis=-1, keepdims=True)
    gsel = jnp.min(jnp.where(gscore == gmax, gidx, big), axis=-1, keepdims=True)
    cand = jnp.where((gidx == gsel) & valid, biased, NEG)
    m1 = jnp.max(cand, axis=-1, keepdims=True)
    i1 = jnp.min(jnp.where(cand == m1, lane_f, big), axis=-1, keepdims=True)
    oh1 = lane_f == i1
    cand2 = jnp.where(oh1, NEG, cand)
    m2 = jnp.max(cand2, axis=-1, keepdims=True)
    i2 = jnp.min(jnp.where((cand2 == m2) & jnp.logical_not(oh1), lane_f, big), axis=-1, keepdims=True)
    oh2 = lane_f == i2
    s1 = jnp.sum(jnp.where(oh1, score, 0.0), axis=-1, keepdims=True)
    s2 = jnp.sum(jnp.where(oh2, score, 0.0), axis=-1, keepdims=True)
    den = s1 + s2
    return jnp.where(oh1, s1 / den, 0.0) + jnp.where(oh2, s2 / den, 0.0)


def _post_kernel(x_ref, mod_ref, oa_ref, oc_ref, o0_ref, o1_ref, o2_ref, l0_ref, l1_ref, l2_ref,
                 wo_ref, g_ref, b_ref, wr_ref, br_ref, x1_ref, h2_ref, gate_ref):
    l0, l1, l2 = l0_ref[0], l1_ref[0], l2_ref[0]
    m = jnp.maximum(jnp.maximum(l0, l1), l2)
    e0, e1, e2 = jnp.exp(l0 - m), jnp.exp(l1 - m), jnp.exp(l2 - m)
    den = e0 + e1 + e2
    ob = ((e0 / den) * o0_ref[0] + (e1 / den) * o1_ref[0] + (e2 / den) * o2_ref[0]).astype(MXU_DTYPE)

    na = DIFF_HEADS * DIFF_V
    nb = na + DIL_WIDTH
    y = (jnp.dot(oa_ref[0], wo_ref[0:na], preferred_element_type=F32)
         + jnp.dot(ob, wo_ref[na:nb], preferred_element_type=F32)
         + jnp.dot(oc_ref[0], wo_ref[nb:], preferred_element_type=F32))
    g1 = mod_ref[0, 2:3, :]
    sh2 = mod_ref[0, 3:4, :]
    sc2 = mod_ref[0, 4:5, :]
    x1 = _layer_norm(ALPHA * x_ref[0] + g1 * y, g_ref[...], b_ref[...])
    x1_ref[0] = x1
    h2 = x1 * (1.0 + sc2) + sh2
    h2_ref[0] = h2.astype(h2_ref.dtype)
    logits = jnp.dot(h2, wr_ref[...], preferred_element_type=F32, precision=lax.Precision.HIGHEST)
    gate_ref[0] = _route(logits, br_ref[...], _lane_iota(logits.shape))


def _post(x, mod, oa, oc, o_pats, lse_pats, w_out, g, b, w_r, b_r, tm):
    B, S, D = x.shape
    row = lambda b_, i: (b_, i, 0)
    const2 = lambda b_, i: (0, 0)
    pat = pl.BlockSpec((1, tm, DIL_WIDTH), row)
    return pl.pallas_call(
        _post_kernel,
        grid=(B, S // tm),
        in_specs=[pl.BlockSpec((1, tm, D), row),
                  pl.BlockSpec((1, 6, D), lambda b_, i: (b_, 0, 0)),
                  pl.BlockSpec((1, tm, oa.shape[-1]), row),
                  pl.BlockSpec((1, tm, oc.shape[-1]), row),
                  pat, pat, pat, pat, pat, pat,
                  pl.BlockSpec(w_out.shape, const2),
                  pl.BlockSpec((1, D), const2),
                  pl.BlockSpec((1, D), const2),
                  pl.BlockSpec(w_r.shape, const2),
                  pl.BlockSpec((1, LANES), const2)],
        out_specs=[pl.BlockSpec((1, tm, D), row),
                   pl.BlockSpec((1, tm, D), row),
                   pl.BlockSpec((1, tm, LANES), row)],
        out_shape=[jax.ShapeDtypeStruct((B, S, D), F32),
                   jax.ShapeDtypeStruct((B, S, D), MXU_DTYPE),
                   jax.ShapeDtypeStruct((B, S, LANES), F32)],
        compiler_params=_cparams(("parallel", "parallel")),
        name="post_mix",
    )(x, mod, oa, oc, *o_pats, *lse_pats, w_out, g.reshape(1, D), b.reshape(1, D), w_r, b_r)


def _moe_kernel(ec, x1_ref, mod_ref, h2_ref, gate_ref, w1_ref, w3_ref, w2_ref, g_ref, b_ref,
                o_ref, acc_scr):
    c = pl.program_id(2)

    @pl.when(c == 0)
    def _():
        acc_scr[...] = jnp.zeros(acc_scr.shape, F32)

    h = h2_ref[0]
    gate = gate_ref[0]
    lane = _lane_iota(gate.shape)
    for k in range(ec):
        ge = jnp.sum(jnp.where(lane == c * ec + k, gate, 0.0), axis=-1, keepdims=True)
        a = jnp.dot(h, w1_ref[k], preferred_element_type=F32)
        bb = jnp.dot(h, w3_ref[k], preferred_element_type=F32)
        hid = (a * (1.0 / (1.0 + jnp.exp(-a)))) * bb * ge
        acc_scr[...] += jnp.dot(hid.astype(MXU_DTYPE), w2_ref[k], preferred_element_type=F32)

    @pl.when(c == pl.num_programs(2) - 1)
    def _():
        g2 = mod_ref[0, 5:6, :]
        o_ref[0] = _layer_norm(ALPHA * x1_ref[0] + g2 * acc_scr[...], g_ref[...], b_ref[...])


def _moe(x1, mod, h2, gate, w1, w3, w2, g, b, tm, ec=4):
    B, S, D = x1.shape
    E = w1.shape[0]
    row = lambda b_, i, c: (b_, i, 0)
    const2 = lambda b_, i, c: (0, 0)
    return pl.pallas_call(
        functools.partial(_moe_kernel, ec),
        grid=(B, S // tm, E // ec),
        in_specs=[pl.BlockSpec((1, tm, D), row),
                  pl.BlockSpec((1, 6, D), lambda b_, i, c: (b_, 0, 0)),
                  pl.BlockSpec((1, tm, D), row),
                  pl.BlockSpec((1, tm, LANES), row),
                  pl.BlockSpec((ec, D, D_FF), lambda b_, i, c: (c, 0, 0)),
                  pl.BlockSpec((ec, D, D_FF), lambda b_, i, c: (c, 0, 0)),
                  pl.BlockSpec((ec, D_FF, D), lambda b_, i, c: (c, 0, 0)),
                  pl.BlockSpec((1, D), const2),
                  pl.BlockSpec((1, D), const2)],
        out_specs=pl.BlockSpec((1, tm, D), row),
        out_shape=jax.ShapeDtypeStruct((B, S, D), F32),
        scratch_shapes=[pltpu.VMEM((tm, D), F32)],
        compiler_params=_cparams(("parallel", "parallel", "arbitrary")),
        name="moe_ffn",
    )(x1, mod, h2, gate, w1, w3, w2, g.reshape(1, D), b.reshape(1, D))


def _rope_tables(seq):
    lane = np.arange(LANES)

    def tables(period, offset, rot, theta):
        half = rot // 2
        d = lane % period - offset
        first = (d >= 0) & (d < half)
        second = (d >= half) & (d < rot)
        fidx = np.where(first, d, np.where(second, d - half, 0))
        inv_freq = theta ** (-2.0 * jnp.arange(half, dtype=F32) / rot)
        ang = jnp.arange(seq, dtype=F32)[:, None] * inv_freq[None, :]
        cos_l = jnp.cos(ang)[:, fidx]
        sin_l = jnp.sin(ang)[:, fidx]
        cos = jnp.where((first | second)[None, :], cos_l, 1.0)
        s_up = jnp.where(first[None, :], -sin_l, 0.0)
        s_dn = jnp.where(second[None, :], sin_l, 0.0)
        return [cos, s_up, s_dn]

    tabs = (tables(DIFF_QK, 0, DIFF_QK // 4, ROPE_THETA)
            + tables(DIL_DIM, 0, DIL_DIM // 4, ROPE_THETA)
            + tables(LANES, KR_LANE, MLA_ROPE, MLA_THETA))
    return jnp.stack(tabs).astype(F32)


def _prep_layer(l, w_in, w_uq, w_ukv, w_out, w1, w3, w2):
    D = w_in.shape[1]
    wi = w_in[l]
    zeros = lambda n: jnp.zeros((D, n), wi.dtype)
    wi = jnp.concatenate([wi[:, :C_KR], zeros(KR_LANE), wi[:, C_KR:], zeros(LANES - KR_LANE - MLA_ROPE)], axis=1)
    uq = w_uq[l].reshape(MLA_Q_RANK, MLA_HEADS, MLA_NOPE + MLA_ROPE)
    uq = jnp.pad(uq, ((0, 0), (0, 0), (0, LANES - MLA_NOPE - MLA_ROPE))).reshape(MLA_Q_RANK, MLA_HEADS * LANES)
    ukv = w_ukv[l].reshape(MLA_KV_RANK, MLA_HEADS, MLA_NOPE + MLA_V)
    uk = jnp.pad(ukv[..., :MLA_NOPE], ((0, 0), (0, 0), (0, LANES - MLA_NOPE))).reshape(MLA_KV_RANK, MLA_HEADS * LANES)
    uv = ukv[..., MLA_NOPE:].reshape(MLA_KV_RANK, MLA_HEADS * MLA_V)
    c = lambda t: t.astype(MXU_DTYPE)
    return dict(w_in=c(wi), w_uq=c(uq), w_uk=c(uk), w_uv=c(uv), w_out=c(w_out[l]),
                w1=c(w1[l]), w3=c(w3[l]), w2=c(w2[l]))


def _tile(n, pref):
    return pref if n % pref == 0 else n


def _trunk(x, mods, tabs, layers, p):
    B, S, D = x.shape
    tm = _tile(S, 512)
    tq = _tile(S, 512)
    x = _ln_in(x, p['ln_in_g'], p['ln_in_b'], tm)
    for l, lw in enumerate(layers):
        mod = mods[l]
        lam_init = 0.8 - 0.6 * math.exp(-0.3 * l)
        qa, ka, va, qb, kb, vb, mq, mk, mv = _proj(
            x, mod, tabs, lw['w_in'], p['g_cq'][l], p['g_ckv'][l], lw['w_uq'], lw['w_uk'], lw['w_uv'], tm)
        oa = _diff_attn(qa, ka, va, p['lam_pack'][l], p['g_sub'][l], lam_init, tq, tq)
        oc = _mla_attn(mq, mk, mv, tq, tq)
        pats = [_dil_attn(qb, kb, vb, dil) for _, dil in DIL_PAIRS]
        x1, h2, gate = _post(x, mod, oa, oc, [o for o, _ in pats], [s for _, s in pats], lw['w_out'],
                             p['ln1_g'][l], p['ln1_b'][l], p['w_r'], p['b_r'], tm)
        x = _moe(x1, mod, h2, gate, lw['w1'], lw['w3'], lw['w2'], p['ln2_g'][l], p['ln2_b'][l], tm)
    return x


def kernel(x_prompt, x_sample, c_prompt, c_sample, ln_in_g, ln_in_b, w_router, b_router, w_ada, b_ada, w_in, lam_q1, lam_k1, lam_q2, lam_k2, g_subln, g_cq, g_ckv, w_uq, w_ukv, w_out, ln1_g, ln1_b, w1, w3, w2, ln2_g, ln2_b):
    depth = w_in.shape[0]
    D = x_prompt.shape[-1]
    bp, bs = c_prompt.shape[0], c_sample.shape[0]
    rows = -(-(bp + bs) // 8) * 8
    c_all = jnp.concatenate([c_prompt, c_sample, jnp.zeros((rows - bp - bs, D), F32)], axis=0)
    mods = _ada(c_all, w_ada, b_ada)
    mods_p = mods[:, :bp].reshape(depth, bp, 6, D)
    mods_s = mods[:, bp:bp + bs].reshape(depth, bs, 6, D)

    pad_l = lambda t: jnp.pad(t, ((0, 0), (0, LANES - t.shape[-1])))
    p = dict(
        ln_in_g=ln_in_g, ln_in_b=ln_in_b, g_cq=g_cq, g_ckv=g_ckv,
        ln1_g=ln1_g, ln1_b=ln1_b, ln2_g=ln2_g, ln2_b=ln2_b,
        lam_pack=jnp.stack([pad_l(lam_q1), pad_l(lam_k1), pad_l(lam_q2), pad_l(lam_k2)], axis=1),
        g_sub=jnp.tile(g_subln, (1, LANES // DIFF_V)).reshape(depth, 1, LANES),
        w_r=pad_l(w_router), b_r=pad_l(b_router.reshape(1, -1)),
    )
    layers = [_prep_layer(l, w_in, w_uq, w_ukv, w_out, w1, w3, w2) for l in range(depth)]
    tabs = _rope_tables(max(x_prompt.shape[1], x_sample.shape[1]))
    y_prompt = _trunk(x_prompt, mods_p, tabs, layers, p)
    y_sample = _trunk(x_sample, mods_s, tabs, layers, p)
    return (y_prompt, y_sample)
```

```python
import functools
import math

import numpy as np
import jax
import jax.numpy as jnp
from jax import lax
from jax.experimental import pallas as pl
from jax.experimental.pallas import tpu as pltpu

F32 = jnp.float32
MXU_DTYPE = jnp.bfloat16

LANES = 128
D_MODEL = 1024
DEPTH = 4
DIFF_HEADS = 4
DIFF_QK = 32
DIFF_V = 64
DIL_HEADS = 6
DIL_DIM = 64
DIL_PAIRS = ((128, 1), (512, 4), (2048, 16))
DIL_RADIUS = 64
MLA_HEADS = 6
MLA_NOPE = 64
MLA_ROPE = 32
MLA_V = 64
MLA_Q_RANK = 256
MLA_KV_RANK = 128
MLA_THETA = 10000.0
ROPE_THETA = 500000.0
N_EXPERTS = 16
GROUP_SIZE = 4
D_FF = 256
ALPHA = (2 * DEPTH) ** 0.25
NEG = -1e30
LOG2E = math.log2(math.e)
ATTN_TQ = 512
ATTN_TK = 2048

C_Q1, C_Q2, C_K1, C_K2, C_VA = 0, 128, 256, 384, 512
C_QB, C_KB, C_VB = 768, 1152, 1536
C_CQ, C_CKV, C_KR, C_END = 1920, 2176, 2304, 2432
KR_LANE = 64

VMEM_LIMIT = 48 * 1024 * 1024


def _cparams(sem):
    return pltpu.CompilerParams(dimension_semantics=sem, vmem_limit_bytes=VMEM_LIMIT)


def _lane_iota(shape):
    return lax.broadcasted_iota(jnp.int32, shape, len(shape) - 1)


def _layer_norm(z, g, b):
    mu = jnp.mean(z, axis=-1, keepdims=True)
    zc = z - mu
    var = jnp.mean(zc * zc, axis=-1, keepdims=True)
    return zc * lax.rsqrt(var + 1e-5) * g + b


def _rope(t, tab_ref, typ, half):
    c = tab_ref[3 * typ]
    s_up = tab_ref[3 * typ + 1]
    s_dn = tab_ref[3 * typ + 2]
    return t * c + pltpu.roll(t, LANES - half, 1) * s_up + pltpu.roll(t, half, 1) * s_dn


def _ln_in_kernel(x_ref, g_ref, b_ref, o_ref):
    o_ref[0] = _layer_norm(x_ref[0], g_ref[...], b_ref[...])


def _ln_in(x, g, b, tm):
    B, S, D = x.shape
    return pl.pallas_call(
        _ln_in_kernel,
        grid=(B, S // tm),
        in_specs=[pl.BlockSpec((1, tm, D), lambda b, i: (b, i, 0)),
                  pl.BlockSpec((1, D), lambda b, i: (0, 0)),
                  pl.BlockSpec((1, D), lambda b, i: (0, 0))],
        out_specs=pl.BlockSpec((1, tm, D), lambda b, i: (b, i, 0)),
        out_shape=jax.ShapeDtypeStruct(x.shape, F32),
        compiler_params=_cparams(("parallel", "parallel")),
        name="ln_in",
    )(x, g.reshape(1, D), b.reshape(1, D))


def _ada_kernel(c_ref, w_ref, b_ref, o_ref):
    c = c_ref[...]
    cs = (c * (1.0 / (1.0 + jnp.exp(-c)))).astype(MXU_DTYPE)
    o_ref[0] = jnp.dot(cs, w_ref[0].astype(MXU_DTYPE), preferred_element_type=F32) + b_ref[0]


def _ada(c, w_ada, b_ada, tn=1536):
    R, D = c.shape
    L, _, N = w_ada.shape
    return pl.pallas_call(
        _ada_kernel,
        grid=(L, N // tn),
        in_specs=[pl.BlockSpec((R, D), lambda l, j: (0, 0)),
                  pl.BlockSpec((1, D, tn), lambda l, j: (l, 0, j)),
                  pl.BlockSpec((1, 1, tn), lambda l, j: (l, 0, j))],
        out_specs=pl.BlockSpec((1, R, tn), lambda l, j: (l, 0, j)),
        out_shape=jax.ShapeDtypeStruct((L, R, N), F32),
        compiler_params=_cparams(("parallel", "parallel")),
        name="ada",
    )(c, w_ada, b_ada.reshape(L, 1, N))


def _proj_kernel(x_ref, mod_ref, tab_ref, w_ref, gcq_ref, gckv_ref, wuq_ref, wuk_ref, wuv_ref,
                 qa_ref, ka_ref, va_ref, qb_ref, kb_ref, vb_ref, mq_ref, mk_ref, mv_ref):
    x = x_ref[0]
    sh1 = mod_ref[0, 0:1, :]
    sc1 = mod_ref[0, 1:2, :]
    h = (x * (1.0 + sc1) + sh1).astype(MXU_DTYPE)

    def mm(a, b):
        return jnp.dot(h, w_ref[:, a:b], preferred_element_type=F32)

    qa = mm(C_Q1, C_K1)
    ka = mm(C_K1, C_VA)
    sa = DIFF_QK ** -0.5 * LOG2E
    for j in range(2):
        blk = slice(j * LANES, (j + 1) * LANES)
        qa_ref[0, :, blk] = (_rope(qa[:, blk], tab_ref, 0, DIFF_QK // 8) * sa).astype(qa_ref.dtype)
        ka_ref[0, :, blk] = _rope(ka[:, blk], tab_ref, 0, DIFF_QK // 8).astype(ka_ref.dtype)
    va_ref[0] = mm(C_VA, C_QB).T.astype(va_ref.dtype)

    qb = mm(C_QB, C_KB)
    kb = mm(C_KB, C_VB)
    sb = DIL_DIM ** -0.5
    for j in range(3):
        blk = slice(j * LANES, (j + 1) * LANES)
        qb_ref[0, :, blk] = (_rope(qb[:, blk], tab_ref, 1, DIL_DIM // 8) * sb).astype(qb_ref.dtype)
        kb_ref[0, :, blk] = _rope(kb[:, blk], tab_ref, 1, DIL_DIM // 8).astype(kb_ref.dtype)
    vb_ref[0] = mm(C_VB, C_CQ).astype(vb_ref.dtype)

    def rms(t, g):
        return (t * lax.rsqrt(jnp.mean(t * t, axis=-1, keepdims=True) + 1e-6) * g).astype(MXU_DTYPE)

    cq = rms(mm(C_CQ, C_CKV), gcq_ref[...])
    ckv = rms(mm(C_CKV, C_KR), gckv_ref[...])
    kr = _rope(mm(C_KR, C_END), tab_ref, 2, MLA_ROPE // 2)
    q = jnp.dot(cq, wuq_ref[...], preferred_element_type=F32)
    kn = jnp.dot(ckv, wuk_ref[...], preferred_element_type=F32)
    sc = (MLA_NOPE + MLA_ROPE) ** -0.5 * LOG2E
    for j in range(MLA_HEADS):
        blk = slice(j * LANES, (j + 1) * LANES)
        mq_ref[0, :, blk] = (_rope(q[:, blk], tab_ref, 2, MLA_ROPE // 2) * sc).astype(mq_ref.dtype)
        mk_ref[0, :, blk] = (kn[:, blk] + kr).astype(mk_ref.dtype)
    mv_ref[0] = jnp.dot(ckv, wuv_ref[...], preferred_element_type=F32).T.astype(mv_ref.dtype)


def _proj(x, mod, tabs, w_in, g_cq, g_ckv, w_uq, w_uk, w_uv, tm):
    B, S, D = x.shape
    widths = (256, 256, 256, 384, 384, 384, 768, 768, 384)
    transposed = (2, 8)
    row = lambda b, i: (b, i, 0)
    col = lambda b, i: (b, 0, i)
    const2 = lambda b, i: (0, 0)
    out_specs = [pl.BlockSpec((1, w, tm), col) if n in transposed else pl.BlockSpec((1, tm, w), row)
                 for n, w in enumerate(widths)]
    out_shape = [jax.ShapeDtypeStruct((B, w, S) if n in transposed else (B, S, w), MXU_DTYPE)
                 for n, w in enumerate(widths)]
    return pl.pallas_call(
        _proj_kernel,
        grid=(B, S // tm),
        in_specs=[pl.BlockSpec((1, tm, D), row),
                  pl.BlockSpec((1, 6, D), lambda b, i: (b, 0, 0)),
                  pl.BlockSpec((9, tm, LANES), lambda b, i: (0, i, 0)),
                  pl.BlockSpec(w_in.shape, const2),
                  pl.BlockSpec((1, MLA_Q_RANK), const2),
                  pl.BlockSpec((1, MLA_KV_RANK), const2),
                  pl.BlockSpec(w_uq.shape, const2),
                  pl.BlockSpec(w_uk.shape, const2),
                  pl.BlockSpec(w_uv.shape, const2)],
        out_specs=out_specs,
        out_shape=out_shape,
        compiler_params=_cparams(("parallel", "parallel")),
        name="proj_in",
    )(x, mod, tabs, w_in, g_cq.reshape(1, -1), g_ckv.reshape(1, -1), w_uq, w_uk, w_uv)


def _softmax_step(u, q, k, vt, m_scr, l_scr, acc_scr):
    st = lax.dot_general(k, q, (((1,), (1,)), ((), ())), preferred_element_type=F32)
    m_prev = m_scr[u]
    m_new = jnp.maximum(m_prev, jnp.max(st, axis=0, keepdims=True))
    alpha = jnp.exp2(m_prev - m_new)
    p = jnp.exp2(st - m_new)
    l_scr[u] = alpha * l_scr[u] + jnp.sum(p, axis=0, keepdims=True)
    acc_scr[u] = alpha * acc_scr[u] + jnp.dot(vt, p.astype(MXU_DTYPE), preferred_element_type=F32)
    m_scr[u] = m_new


def _init_softmax(m_scr, l_scr, acc_scr):
    m_scr[...] = jnp.full(m_scr.shape, NEG, F32)
    l_scr[...] = jnp.zeros(l_scr.shape, F32)
    acc_scr[...] = jnp.zeros(acc_scr.shape, F32)


def _diff_attn_kernel(lam_init, q1_ref, q2_ref, k1_ref, k2_ref, vt_ref, lamp_ref, g_ref, o_ref,
                      q_scr, m_scr, l_scr, acc_scr):
    h = pl.program_id(1)
    ik = pl.program_id(3)

    @pl.when(ik == 0)
    def _():
        _init_softmax(m_scr, l_scr, acc_scr)
        lane = _lane_iota(q1_ref.shape[1:])
        msk = (lane >= h * DIFF_QK) & (lane < (h + 1) * DIFF_QK)
        q_scr[0] = jnp.where(msk, q1_ref[0].astype(F32), 0.0).astype(q_scr.dtype)
        q_scr[1] = jnp.where(msk, q2_ref[0].astype(F32), 0.0).astype(q_scr.dtype)

    vt = vt_ref[0]
    _softmax_step(0, q_scr[0], k1_ref[0], vt, m_scr, l_scr, acc_scr)
    _softmax_step(1, q_scr[1], k2_ref[0], vt, m_scr, l_scr, acc_scr)

    @pl.when(ik == pl.num_programs(3) - 1)
    def _():
        lp = lamp_ref[...]
        lam = (jnp.exp(jnp.sum(lp[0:1] * lp[1:2], axis=-1, keepdims=True))
               - jnp.exp(jnp.sum(lp[2:3] * lp[3:4], axis=-1, keepdims=True)) + lam_init)
        o = acc_scr[0] / l_scr[0] - lam * (acc_scr[1] / l_scr[1])
        ms = jnp.mean(o * o, axis=0, keepdims=True)
        o_ref[0] = (o * lax.rsqrt(ms + 1e-6) * g_ref[...] * (1.0 - lam_init)).astype(o_ref.dtype)


def _diff_attn(qa, ka, vat, lam_pack, g_sub, lam_init, tq, tk):
    B, S, _ = qa.shape
    kern = functools.partial(_diff_attn_kernel, lam_init)
    return pl.pallas_call(
        kern,
        grid=(B, DIFF_HEADS, S // tq, S // tk),
        in_specs=[pl.BlockSpec((1, tq, LANES), lambda b, h, iq, ik: (b, iq, 0)),
                  pl.BlockSpec((1, tq, LANES), lambda b, h, iq, ik: (b, iq, 1)),
                  pl.BlockSpec((1, tk, LANES), lambda b, h, iq, ik: (b, ik, 0)),
                  pl.BlockSpec((1, tk, LANES), lambda b, h, iq, ik: (b, ik, 1)),
                  pl.BlockSpec((1, DIFF_V, tk), lambda b, h, iq, ik: (b, h, ik)),
                  pl.BlockSpec((4, LANES), lambda b, h, iq, ik: (0, 0)),
                  pl.BlockSpec((DIFF_V, 1), lambda b, h, iq, ik: (0, 0))],
        out_specs=pl.BlockSpec((1, DIFF_V, tq), lambda b, h, iq, ik: (b, h, iq)),
        out_shape=jax.ShapeDtypeStruct((B, DIFF_HEADS * DIFF_V, S), MXU_DTYPE),
        scratch_shapes=[pltpu.VMEM((2, tq, LANES), MXU_DTYPE),
                        pltpu.VMEM((2, 1, tq), F32),
                        pltpu.VMEM((2, 1, tq), F32),
                        pltpu.VMEM((2, DIFF_V, tq), F32)],
        compiler_params=_cparams(("parallel", "parallel", "parallel", "arbitrary")),
        name="diff_attn",
    )(qa, qa, ka, ka, vat, lam_pack, g_sub)


def _mla_attn_kernel(q_ref, k_ref, vt_ref, o_ref, m_scr, l_scr, acc_scr):
    ik = pl.program_id(3)

    @pl.when(ik == 0)
    def _():
        _init_softmax(m_scr, l_scr, acc_scr)

    for hh in range(2):
        blk = slice(hh * LANES, (hh + 1) * LANES)
        vt = vt_ref[0, hh * MLA_V:(hh + 1) * MLA_V, :]
        _softmax_step(hh, q_ref[0, :, blk], k_ref[0, :, blk], vt, m_scr, l_scr, acc_scr)

    @pl.when(ik == pl.num_programs(3) - 1)
    def _():
        for hh in range(2):
            o_ref[0, hh * MLA_V:(hh + 1) * MLA_V, :] = (acc_scr[hh] / l_scr[hh]).astype(o_ref.dtype)


def _mla_attn(mq, mk, mvt, tq, tk):
    B, S, _ = mq.shape
    return pl.pallas_call(
        _mla_attn_kernel,
        grid=(B, MLA_HEADS // 2, S // tq, S // tk),
        in_specs=[pl.BlockSpec((1, tq, 2 * LANES), lambda b, j, iq, ik: (b, iq, j)),
                  pl.BlockSpec((1, tk, 2 * LANES), lambda b, j, iq, ik: (b, ik, j)),
                  pl.BlockSpec((1, 2 * MLA_V, tk), lambda b, j, iq, ik: (b, j, ik))],
        out_specs=pl.BlockSpec((1, 2 * MLA_V, tq), lambda b, j, iq, ik: (b, j, iq)),
        out_shape=jax.ShapeDtypeStruct((B, MLA_HEADS * MLA_V, S), MXU_DTYPE),
        scratch_shapes=[pltpu.VMEM((2, 1, tq), F32),
                        pltpu.VMEM((2, 1, tq), F32),
                        pltpu.VMEM((2, MLA_V, tq), F32)],
        compiler_params=_cparams(("parallel", "parallel", "parallel", "arbitrary")),
        name="mla_attn",
    )(mq, mk, mvt)


DIL_SUB = 128
DIL_WIDTH = DIL_HEADS * DIL_DIM


def _dil_kernel(length, q_ref, kp_ref, kc_ref, kn_ref, vp_ref, vc_ref, vn_ref, o_ref, lse_ref,
                kbuf, vbuf):
    i = pl.program_id(2)
    tq = q_ref.shape[1]
    R = DIL_RADIUS
    kbuf[0:R] = kp_ref[0]
    kbuf[R:R + tq] = kc_ref[0]
    kbuf[R + tq:] = kn_ref[0]
    vbuf[0:R] = vp_ref[0]
    vbuf[R:R + tq] = vc_ref[0]
    vbuf[R + tq:] = vn_ref[0]
    sub = min(DIL_SUB, tq)
    win = sub + 2 * R
    row = lax.broadcasted_iota(jnp.int32, (sub, win), 0)
    col = lax.broadcasted_iota(jnp.int32, (sub, win), 1)
    lane = _lane_iota((sub, LANES))
    for t in range(tq // sub):
        kpos = i * tq + t * sub - R + col
        rel = col - row
        valid = (rel >= 0) & (rel <= 2 * R) & (kpos >= 0) & (kpos < length)
        for hp in range(DIL_HEADS // 2):
            blk = slice(hp * LANES, (hp + 1) * LANES)
            q = q_ref[0, t * sub:(t + 1) * sub, blk].astype(F32)
            kw = kbuf[t * sub:t * sub + win, blk]
            vw = vbuf[t * sub:t * sub + win, blk]
            outs = []
            lses = []
            for hh in range(2):
                half = (lane >= hh * DIL_DIM) & (lane < (hh + 1) * DIL_DIM)
                qm = jnp.where(half, q, 0.0).astype(MXU_DTYPE)
                s = lax.dot_general(qm, kw, (((1,), (1,)), ((), ())), preferred_element_type=F32)
                s = jnp.where(valid, s, NEG)
                m = jnp.max(s, axis=-1, keepdims=True)
                e = jnp.exp(s - m)
                den = jnp.sum(e, axis=-1, keepdims=True)
                outs.append(jnp.dot((e / den).astype(MXU_DTYPE), vw, preferred_element_type=F32))
                lses.append(m + jnp.log(den))
            o_ref[0, t * sub:(t + 1) * sub, blk] = jnp.where(lane < DIL_DIM, outs[0], outs[1])
            lse_ref[0, t * sub:(t + 1) * sub, blk] = jnp.where(lane < DIL_DIM, lses[0], lses[1])


def _dil_attn(qb, kb, vb, dil, tq_max=512):
    B, S, W = qb.shape
    L = S // dil
    tq = min(tq_max, L)
    R = DIL_RADIUS
    nr = L // R
    view = lambda t: t.reshape(B, L, dil * W)
    cur = pl.BlockSpec((1, tq, W), lambda b, r, i: (b, i, r))
    prv = pl.BlockSpec((1, R, W), lambda b, r, i: (b, jnp.maximum(i * (tq // R) - 1, 0), r))
    nxt = pl.BlockSpec((1, R, W), lambda b, r, i: (b, jnp.minimum((i + 1) * (tq // R), nr - 1), r))
    o, lse = pl.pallas_call(
        functools.partial(_dil_kernel, L),
        grid=(B, dil, L // tq),
        in_specs=[cur, prv, cur, nxt, prv, cur, nxt],
        out_specs=[cur, cur],
        out_shape=[jax.ShapeDtypeStruct((B, L, dil * W), F32)] * 2,
        scratch_shapes=[pltpu.VMEM((tq + 2 * R, W), MXU_DTYPE)] * 2,
        compiler_params=_cparams(("parallel", "parallel", "parallel")),
        name=f"dil_attn_{dil}",
    )(view(qb), view(kb), view(kb), view(kb), view(vb), view(vb), view(vb))
    return o.reshape(B, S, W), lse.reshape(B, S, W)


def _partner(x, k, lane):
    up = pltpu.roll(x, LANES - k, 1)
    dn = pltpu.roll(x, k, 1)
    return jnp.where((lane & k) == 0, up, dn)


def _route(logits, bias, lane):
    valid = lane < N_EXPERTS
    lane_f = lane.astype(F32)
    score = 1.0 / (1.0 + jnp.exp(-logits))
    biased = jnp.where(valid, score + bias, NEG)
    p1 = _partner(biased, 1, lane)
    hi = jnp.maximum(biased, p1)
    lo = jnp.minimum(biased, p1)
    hi_p = _partner(hi, 2, lane)
    lo_p = _partner(lo, 2, lane)
    top1 = jnp.maximum(hi, hi_p)
    top2 = jnp.maximum(jnp.minimum(hi, hi_p), jnp.maximum(lo, lo_p))
    gscore = jnp.where(valid, top1 + top2, NEG)
    gidx = (lane // GROUP_SIZE).astype(F32)
    big = float(LANES)
    gmax = jnp.max(gscore, axis=-1, keepdims=True)
    gsel = jnp.min(jnp.where(gscore == gmax, gidx, big), axis=-1, keepdims=True)
    cand = jnp.where((gidx == gsel) & valid, biased, NEG)
    m1 = jnp.max(cand, axis=-1, keepdims=True)
    i1 = jnp.min(jnp.where(cand == m1, lane_f, big), axis=-1, keepdims=True)
    oh1 = lane_f == i1
    cand2 = jnp.where(oh1, NEG, cand)
    m2 = jnp.max(cand2, axis=-1, keepdims=True)
    i2 = jnp.min(jnp.where((cand2 == m2) & jnp.logical_not(oh1), lane_f, big), axis=-1, keepdims=True)
    oh2 = lane_f == i2
    s1 = jnp.sum(jnp.where(oh1, score, 0.0), axis=-1, keepdims=True)
    s2 = jnp.sum(jnp.where(oh2, score, 0.0), axis=-1, keepdims=True)
    den = s1 + s2
    return jnp.where(oh1, s1 / den, 0.0) + jnp.where(oh2, s2 / den, 0.0)


def _post_kernel(x_ref, mod_ref, oa_ref, oc_ref, o0_ref, o1_ref, o2_ref, l0_ref, l1_ref, l2_ref,
                 wo_ref, g_ref, b_ref, wr_ref, br_ref, x1_ref, h2_ref, gate_ref):
    l0, l1, l2 = l0_ref[0], l1_ref[0], l2_ref[0]
    m = jnp.maximum(jnp.maximum(l0, l1), l2)
    e0, e1, e2 = jnp.exp(l0 - m), jnp.exp(l1 - m), jnp.exp(l2 - m)
    den = e0 + e1 + e2
    ob = ((e0 / den) * o0_ref[0] + (e1 / den) * o1_ref[0] + (e2 / den) * o2_ref[0]).astype(MXU_DTYPE)

    na = DIFF_HEADS * DIFF_V
    nb = na + DIL_WIDTH
    tn = (((0,), (0,)), ((), ()))
    y = (lax.dot_general(oa_ref[0], wo_ref[0:na], tn, preferred_element_type=F32)
         + jnp.dot(ob, wo_ref[na:nb], preferred_element_type=F32)
         + lax.dot_general(oc_ref[0], wo_ref[nb:], tn, preferred_element_type=F32))
    g1 = mod_ref[0, 2:3, :]
    sh2 = mod_ref[0, 3:4, :]
    sc2 = mod_ref[0, 4:5, :]
    x1 = _layer_norm(ALPHA * x_ref[0] + g1 * y, g_ref[...], b_ref[...])
    x1_ref[0] = x1
    h2 = x1 * (1.0 + sc2) + sh2
    h2_ref[0] = h2.astype(h2_ref.dtype)
    logits = jnp.dot(h2, wr_ref[...], preferred_element_type=F32, precision=lax.Precision.HIGHEST)
    gate_ref[0] = _route(logits, br_ref[...], _lane_iota(logits.shape))


def _post(x, mod, oa, oc, o_pats, lse_pats, w_out, g, b, w_r, b_r, tm):
    B, S, D = x.shape
    row = lambda b_, i: (b_, i, 0)
    const2 = lambda b_, i: (0, 0)
    pat = pl.BlockSpec((1, tm, DIL_WIDTH), row)
    return pl.pallas_call(
        _post_kernel,
        grid=(B, S // tm),
        in_specs=[pl.BlockSpec((1, tm, D), row),
                  pl.BlockSpec((1, 6, D), lambda b_, i: (b_, 0, 0)),
                  pl.BlockSpec((1, oa.shape[1], tm), lambda b_, i: (b_, 0, i)),
                  pl.BlockSpec((1, oc.shape[1], tm), lambda b_, i: (b_, 0, i)),
                  pat, pat, pat, pat, pat, pat,
                  pl.BlockSpec(w_out.shape, const2),
                  pl.BlockSpec((1, D), const2),
                  pl.BlockSpec((1, D), const2),
                  pl.BlockSpec(w_r.shape, const2),
                  pl.BlockSpec((1, LANES), const2)],
        out_specs=[pl.BlockSpec((1, tm, D), row),
                   pl.BlockSpec((1, tm, D), row),
                   pl.BlockSpec((1, tm, LANES), row)],
        out_shape=[jax.ShapeDtypeStruct((B, S, D), F32),
                   jax.ShapeDtypeStruct((B, S, D), MXU_DTYPE),
                   jax.ShapeDtypeStruct((B, S, LANES), F32)],
        compiler_params=_cparams(("parallel", "parallel")),
        name="post_mix",
    )(x, mod, oa, oc, *o_pats, *lse_pats, w_out, g.reshape(1, D), b.reshape(1, D), w_r, b_r)


def _moe_kernel(ec, x1_ref, mod_ref, h2_ref, gate_ref, w1_ref, w3_ref, w2_ref, g_ref, b_ref,
                o_ref, acc_scr):
    c = pl.program_id(2)

    @pl.when(c == 0)
    def _():
        acc_scr[...] = jnp.zeros(acc_scr.shape, F32)

    h = h2_ref[0]
    gate = gate_ref[0]
    lane = _lane_iota(gate.shape)
    for k in range(ec):
        ge = jnp.sum(jnp.where(lane == c * ec + k, gate, 0.0), axis=-1, keepdims=True)
        a = jnp.dot(h, w1_ref[k], preferred_element_type=F32)
        bb = jnp.dot(h, w3_ref[k], preferred_element_type=F32)
        hid = (a * (1.0 / (1.0 + jnp.exp(-a)))) * bb * ge
        acc_scr[...] += jnp.dot(hid.astype(MXU_DTYPE), w2_ref[k], preferred_element_type=F32)

    @pl.when(c == pl.num_programs(2) - 1)
    def _():
        g2 = mod_ref[0, 5:6, :]
        o_ref[0] = _layer_norm(ALPHA * x1_ref[0] + g2 * acc_scr[...], g_ref[...], b_ref[...])


def _moe(x1, mod, h2, gate, w1, w3, w2, g, b, tm, ec=4):
    B, S, D = x1.shape
    E = w1.shape[0]
    row = lambda b_, i, c: (b_, i, 0)
    const2 = lambda b_, i, c: (0, 0)
    return pl.pallas_call(
        functools.partial(_moe_kernel, ec),
        grid=(B, S // tm, E // ec),
        in_specs=[pl.BlockSpec((1, tm, D), row),
                  pl.BlockSpec((1, 6, D), lambda b_, i, c: (b_, 0, 0)),
                  pl.BlockSpec((1, tm, D), row),
                  pl.BlockSpec((1, tm, LANES), row),
                  pl.BlockSpec((ec, D, D_FF), lambda b_, i, c: (c, 0, 0)),
                  pl.BlockSpec((ec, D, D_FF), lambda b_, i, c: (c, 0, 0)),
                  pl.BlockSpec((ec, D_FF, D), lambda b_, i, c: (c, 0, 0)),
                  pl.BlockSpec((1, D), const2),
                  pl.BlockSpec((1, D), const2)],
        out_specs=pl.BlockSpec((1, tm, D), row),
        out_shape=jax.ShapeDtypeStruct((B, S, D), F32),
        scratch_shapes=[pltpu.VMEM((tm, D), F32)],
        compiler_params=_cparams(("parallel", "parallel", "arbitrary")),
        name="moe_ffn",
    )(x1, mod, h2, gate, w1, w3, w2, g.reshape(1, D), b.reshape(1, D))


def _rope_tables(seq):
    lane = np.arange(LANES)

    def tables(period, offset, rot, theta):
        half = rot // 2
        d = lane % period - offset
        first = (d >= 0) & (d < half)
        second = (d >= half) & (d < rot)
        fidx = np.where(first, d, np.where(second, d - half, 0))
        inv_freq = theta ** (-2.0 * jnp.arange(half, dtype=F32) / rot)
        ang = jnp.arange(seq, dtype=F32)[:, None] * inv_freq[None, :]
        cos_l = jnp.cos(ang)[:, fidx]
        sin_l = jnp.sin(ang)[:, fidx]
        cos = jnp.where((first | second)[None, :], cos_l, 1.0)
        s_up = jnp.where(first[None, :], -sin_l, 0.0)
        s_dn = jnp.where(second[None, :], sin_l, 0.0)
        return [cos, s_up, s_dn]

    tabs = (tables(DIFF_QK, 0, DIFF_QK // 4, ROPE_THETA)
            + tables(DIL_DIM, 0, DIL_DIM // 4, ROPE_THETA)
            + tables(LANES, KR_LANE, MLA_ROPE, MLA_THETA))
    return jnp.stack(tabs).astype(F32)


def _prep_layer(l, w_in, w_uq, w_ukv, w_out, w1, w3, w2):
    D = w_in.shape[1]
    wi = w_in[l]
    zeros = lambda n: jnp.zeros((D, n), wi.dtype)
    wi = jnp.concatenate([wi[:, :C_KR], zeros(KR_LANE), wi[:, C_KR:], zeros(LANES - KR_LANE - MLA_ROPE)], axis=1)
    uq = w_uq[l].reshape(MLA_Q_RANK, MLA_HEADS, MLA_NOPE + MLA_ROPE)
    uq = jnp.pad(uq, ((0, 0), (0, 0), (0, LANES - MLA_NOPE - MLA_ROPE))).reshape(MLA_Q_RANK, MLA_HEADS * LANES)
    ukv = w_ukv[l].reshape(MLA_KV_RANK, MLA_HEADS, MLA_NOPE + MLA_V)
    uk = jnp.pad(ukv[..., :MLA_NOPE], ((0, 0), (0, 0), (0, LANES - MLA_NOPE))).reshape(MLA_KV_RANK, MLA_HEADS * LANES)
    uv = ukv[..., MLA_NOPE:].reshape(MLA_KV_RANK, MLA_HEADS * MLA_V)
    c = lambda t: t.astype(MXU_DTYPE)
    return dict(w_in=c(wi), w_uq=c(uq), w_uk=c(uk), w_uv=c(uv), w_out=c(w_out[l]),
                w1=c(w1[l]), w3=c(w3[l]), w2=c(w2[l]))


def _tile(n, pref):
    return pref if n % pref == 0 else n


def _trunk(x, mods, tabs, layers, p):
    B, S, D = x.shape
    tm = _tile(S, 512)
    tq = _tile(S, ATTN_TQ)
    tk = _tile(S, ATTN_TK)
    x = _ln_in(x, p['ln_in_g'], p['ln_in_b'], tm)
    for l, lw in enumerate(layers):
        mod = mods[l]
        lam_init = 0.8 - 0.6 * math.exp(-0.3 * l)
        qa, ka, va, qb, kb, vb, mq, mk, mv = _proj(
            x, mod, tabs, lw['w_in'], p['g_cq'][l], p['g_ckv'][l], lw['w_uq'], lw['w_uk'], lw['w_uv'], tm)
        oa = _diff_attn(qa, ka, va, p['lam_pack'][l], p['g_sub'][l], lam_init, tq, tk)
        oc = _mla_attn(mq, mk, mv, tq, tk)
        pats = [_dil_attn(qb, kb, vb, dil) for _, dil in DIL_PAIRS]
        x1, h2, gate = _post(x, mod, oa, oc, [o for o, _ in pats], [s for _, s in pats], lw['w_out'],
                             p['ln1_g'][l], p['ln1_b'][l], p['w_r'], p['b_r'], tm)
        x = _moe(x1, mod, h2, gate, lw['w1'], lw['w3'], lw['w2'], p['ln2_g'][l], p['ln2_b'][l], tm)
    return x


def kernel(x_prompt, x_sample, c_prompt, c_sample, ln_in_g, ln_in_b, w_router, b_router, w_ada, b_ada, w_in, lam_q1, lam_k1, lam_q2, lam_k2, g_subln, g_cq, g_ckv, w_uq, w_ukv, w_out, ln1_g, ln1_b, w1, w3, w2, ln2_g, ln2_b):
    depth = w_in.shape[0]
    D = x_prompt.shape[-1]
    bp, bs = c_prompt.shape[0], c_sample.shape[0]
    rows = -(-(bp + bs) // 8) * 8
    c_all = jnp.concatenate([c_prompt, c_sample, jnp.zeros((rows - bp - bs, D), F32)], axis=0)
    mods = _ada(c_all, w_ada, b_ada)
    mods_p = mods[:, :bp].reshape(depth, bp, 6, D)
    mods_s = mods[:, bp:bp + bs].reshape(depth, bs, 6, D)

    pad_l = lambda t: jnp.pad(t, ((0, 0), (0, LANES - t.shape[-1])))
    p = dict(
        ln_in_g=ln_in_g, ln_in_b=ln_in_b, g_cq=g_cq, g_ckv=g_ckv,
        ln1_g=ln1_g, ln1_b=ln1_b, ln2_g=ln2_g, ln2_b=ln2_b,
        lam_pack=jnp.stack([pad_l(lam_q1), pad_l(lam_k1), pad_l(lam_q2), pad_l(lam_k2)], axis=1),
        g_sub=g_subln.reshape(depth, DIFF_V, 1),
        w_r=pad_l(w_router), b_r=pad_l(b_router.reshape(1, -1)),
    )
    layers = [_prep_layer(l, w_in, w_uq, w_ukv, w_out, w1, w3, w2) for l in range(depth)]
    tabs = _rope_tables(max(x_prompt.shape[1], x_sample.shape[1]))
    y_prompt = _trunk(x_prompt, mods_p, tabs, layers, p)
    y_sample = _trunk(x_sample, mods_s, tabs, layers, p)
    return (y_prompt, y_sample)
```

```python
import functools
import math

import numpy as np
import jax
import jax.numpy as jnp
from jax import lax
from jax.experimental import pallas as pl
from jax.experimental.pallas import tpu as pltpu

F32 = jnp.float32
MXU_DTYPE = jnp.bfloat16

LANES = 128
D_MODEL = 1024
DEPTH = 4
DIFF_HEADS = 4
DIFF_QK = 32
DIFF_V = 64
DIL_HEADS = 6
DIL_DIM = 64
DIL_PAIRS = ((128, 1), (512, 4), (2048, 16))
DIL_RADIUS = 64
MLA_HEADS = 6
MLA_NOPE = 64
MLA_ROPE = 32
MLA_V = 64
MLA_Q_RANK = 256
MLA_KV_RANK = 128
MLA_THETA = 10000.0
ROPE_THETA = 500000.0
N_EXPERTS = 16
GROUP_SIZE = 4
D_FF = 256
ALPHA = (2 * DEPTH) ** 0.25
NEG = -1e30
LOG2E = math.log2(math.e)
ATTN_TQ = 512
ATTN_TK = 2048

C_Q1, C_Q2, C_K1, C_K2, C_VA = 0, 128, 256, 384, 512
C_QB, C_KB, C_VB = 768, 1152, 1536
C_CQ, C_CKV, C_KR, C_END = 1920, 2176, 2304, 2432
KR_LANE = 64

VMEM_LIMIT = 48 * 1024 * 1024


def _cparams(sem):
    return pltpu.CompilerParams(dimension_semantics=sem, vmem_limit_bytes=VMEM_LIMIT)


def _lane_iota(shape):
    return lax.broadcasted_iota(jnp.int32, shape, len(shape) - 1)


def _layer_norm(z, g, b):
    mu = jnp.mean(z, axis=-1, keepdims=True)
    zc = z - mu
    var = jnp.mean(zc * zc, axis=-1, keepdims=True)
    return zc * lax.rsqrt(var + 1e-5) * g + b


def _rope(t, tab_ref, typ, half):
    c = tab_ref[3 * typ]
    s_up = tab_ref[3 * typ + 1]
    s_dn = tab_ref[3 * typ + 2]
    return t * c + pltpu.roll(t, LANES - half, 1) * s_up + pltpu.roll(t, half, 1) * s_dn


def _ln_in_kernel(x_ref, g_ref, b_ref, o_ref):
    o_ref[0] = _layer_norm(x_ref[0], g_ref[...], b_ref[...])


def _ln_in(x, g, b, tm):
    B, S, D = x.shape
    return pl.pallas_call(
        _ln_in_kernel,
        grid=(B, S // tm),
        in_specs=[pl.BlockSpec((1, tm, D), lambda b, i: (b, i, 0)),
                  pl.BlockSpec((1, D), lambda b, i: (0, 0)),
                  pl.BlockSpec((1, D), lambda b, i: (0, 0))],
        out_specs=pl.BlockSpec((1, tm, D), lambda b, i: (b, i, 0)),
        out_shape=jax.ShapeDtypeStruct(x.shape, F32),
        compiler_params=_cparams(("parallel", "parallel")),
        name="ln_in",
    )(x, g.reshape(1, D), b.reshape(1, D))


def _ada_kernel(c_ref, w_ref, b_ref, o_ref):
    c = c_ref[...]
    cs = (c * (1.0 / (1.0 + jnp.exp(-c)))).astype(MXU_DTYPE)
    o_ref[0] = jnp.dot(cs, w_ref[0].astype(MXU_DTYPE), preferred_element_type=F32) + b_ref[0]


def _ada(c, w_ada, b_ada, tn=1536):
    R, D = c.shape
    L, _, N = w_ada.shape
    return pl.pallas_call(
        _ada_kernel,
        grid=(L, N // tn),
        in_specs=[pl.BlockSpec((R, D), lambda l, j: (0, 0)),
                  pl.BlockSpec((1, D, tn), lambda l, j: (l, 0, j)),
                  pl.BlockSpec((1, 1, tn), lambda l, j: (l, 0, j))],
        out_specs=pl.BlockSpec((1, R, tn), lambda l, j: (l, 0, j)),
        out_shape=jax.ShapeDtypeStruct((L, R, N), F32),
        compiler_params=_cparams(("parallel", "parallel")),
        name="ada",
    )(c, w_ada, b_ada.reshape(L, 1, N))


def _proj_kernel(x_ref, mod_ref, tab_ref, w_ref, gcq_ref, gckv_ref, wuq_ref, wuk_ref, wuv_ref,
                 qa_ref, ka_ref, va_ref, qb_ref, kb_ref, vb_ref, mq_ref, mk_ref, mv_ref):
    x = x_ref[0]
    sh1 = mod_ref[0, 0:1, :]
    sc1 = mod_ref[0, 1:2, :]
    h = (x * (1.0 + sc1) + sh1).astype(MXU_DTYPE)

    def mm(a, b):
        return jnp.dot(h, w_ref[:, a:b], preferred_element_type=F32)

    qa = mm(C_Q1, C_K1)
    ka = mm(C_K1, C_VA)
    sa = DIFF_QK ** -0.5 * LOG2E
    for j in range(2):
        blk = slice(j * LANES, (j + 1) * LANES)
        qa_ref[0, :, blk] = (_rope(qa[:, blk], tab_ref, 0, DIFF_QK // 8) * sa).astype(qa_ref.dtype)
        ka_ref[0, :, blk] = _rope(ka[:, blk], tab_ref, 0, DIFF_QK // 8).astype(ka_ref.dtype)
    va_ref[0] = mm(C_VA, C_QB).T.astype(va_ref.dtype)

    qb = mm(C_QB, C_KB)
    kb = mm(C_KB, C_VB)
    vb = mm(C_VB, C_CQ)
    sb = DIL_DIM ** -0.5 * LOG2E
    for j in range(DIL_HEADS // 2):
        blk = slice(j * LANES, (j + 1) * LANES)
        qb_ref[j, 0] = _rope(qb[:, blk], tab_ref, 1, DIL_DIM // 8) * sb
        kb_ref[j, 0] = _rope(kb[:, blk], tab_ref, 1, DIL_DIM // 8)
        vb_ref[j, 0] = vb[:, blk]

    def rms(t, g):
        return (t * lax.rsqrt(jnp.mean(t * t, axis=-1, keepdims=True) + 1e-6) * g).astype(MXU_DTYPE)

    cq = rms(mm(C_CQ, C_CKV), gcq_ref[...])
    ckv = rms(mm(C_CKV, C_KR), gckv_ref[...])
    kr = _rope(mm(C_KR, C_END), tab_ref, 2, MLA_ROPE // 2)
    q = jnp.dot(cq, wuq_ref[...], preferred_element_type=F32)
    kn = jnp.dot(ckv, wuk_ref[...], preferred_element_type=F32)
    sc = (MLA_NOPE + MLA_ROPE) ** -0.5 * LOG2E
    for j in range(MLA_HEADS):
        blk = slice(j * LANES, (j + 1) * LANES)
        mq_ref[0, :, blk] = (_rope(q[:, blk], tab_ref, 2, MLA_ROPE // 2) * sc).astype(mq_ref.dtype)
        mk_ref[0, :, blk] = (kn[:, blk] + kr).astype(mk_ref.dtype)
    mv_ref[0] = jnp.dot(ckv, wuv_ref[...], preferred_element_type=F32).T.astype(mv_ref.dtype)


def _proj(x, mod, tabs, w_in, g_cq, g_ckv, w_uq, w_uk, w_uv, tm):
    B, S, D = x.shape
    widths = (256, 256, 256, 384, 384, 384, 768, 768, 384)
    transposed = (2, 8)
    row = lambda b, i: (b, i, 0)
    col = lambda b, i: (b, 0, i)
    const2 = lambda b, i: (0, 0)
    out_specs = [pl.BlockSpec((1, w, tm), col) if n in transposed else pl.BlockSpec((1, tm, w), row)
                 for n, w in enumerate(widths)]
    out_shape = [jax.ShapeDtypeStruct((B, w, S) if n in transposed else (B, S, w), MXU_DTYPE)
                 for n, w in enumerate(widths)]
    for n in (3, 4, 5):
        out_specs[n] = pl.BlockSpec((DIL_HEADS // 2, 1, tm, LANES), lambda b, i: (0, b, i, 0))
        out_shape[n] = jax.ShapeDtypeStruct((DIL_HEADS // 2, B, S, LANES), F32)
    return pl.pallas_call(
        _proj_kernel,
        grid=(B, S // tm),
        in_specs=[pl.BlockSpec((1, tm, D), row),
                  pl.BlockSpec((1, 6, D), lambda b, i: (b, 0, 0)),
                  pl.BlockSpec((9, tm, LANES), lambda b, i: (0, i, 0)),
                  pl.BlockSpec(w_in.shape, const2),
                  pl.BlockSpec((1, MLA_Q_RANK), const2),
                  pl.BlockSpec((1, MLA_KV_RANK), const2),
                  pl.BlockSpec(w_uq.shape, const2),
                  pl.BlockSpec(w_uk.shape, const2),
                  pl.BlockSpec(w_uv.shape, const2)],
        out_specs=out_specs,
        out_shape=out_shape,
        compiler_params=_cparams(("parallel", "parallel")),
        name="proj_in",
    )(x, mod, tabs, w_in, g_cq.reshape(1, -1), g_ckv.reshape(1, -1), w_uq, w_uk, w_uv)


def _softmax_step(u, q, k, vt, m_scr, l_scr, acc_scr):
    st = lax.dot_general(k, q, (((1,), (1,)), ((), ())), preferred_element_type=F32)
    m_prev = m_scr[u]
    m_new = jnp.maximum(m_prev, jnp.max(st, axis=0, keepdims=True))
    alpha = jnp.exp2(m_prev - m_new)
    p = jnp.exp2(st - m_new)
    l_scr[u] = alpha * l_scr[u] + jnp.sum(p, axis=0, keepdims=True)
    acc_scr[u] = alpha * acc_scr[u] + jnp.dot(vt, p.astype(MXU_DTYPE), preferred_element_type=F32)
    m_scr[u] = m_new


def _init_softmax(m_scr, l_scr, acc_scr):
    m_scr[...] = jnp.full(m_scr.shape, NEG, F32)
    l_scr[...] = jnp.zeros(l_scr.shape, F32)
    acc_scr[...] = jnp.zeros(acc_scr.shape, F32)


def _diff_attn_kernel(lam_init, q1_ref, q2_ref, k1_ref, k2_ref, vt_ref, lamp_ref, g_ref, o_ref,
                      q_scr, m_scr, l_scr, acc_scr):
    h = pl.program_id(1)
    ik = pl.program_id(3)

    @pl.when(ik == 0)
    def _():
        _init_softmax(m_scr, l_scr, acc_scr)
        lane = _lane_iota(q1_ref.shape[1:])
        msk = (lane >= h * DIFF_QK) & (lane < (h + 1) * DIFF_QK)
        q_scr[0] = jnp.where(msk, q1_ref[0].astype(F32), 0.0).astype(q_scr.dtype)
        q_scr[1] = jnp.where(msk, q2_ref[0].astype(F32), 0.0).astype(q_scr.dtype)

    vt = vt_ref[0]
    _softmax_step(0, q_scr[0], k1_ref[0], vt, m_scr, l_scr, acc_scr)
    _softmax_step(1, q_scr[1], k2_ref[0], vt, m_scr, l_scr, acc_scr)

    @pl.when(ik == pl.num_programs(3) - 1)
    def _():
        lp = lamp_ref[...]
        lam = (jnp.exp(jnp.sum(lp[0:1] * lp[1:2], axis=-1, keepdims=True))
               - jnp.exp(jnp.sum(lp[2:3] * lp[3:4], axis=-1, keepdims=True)) + lam_init)
        o = acc_scr[0] / l_scr[0] - lam * (acc_scr[1] / l_scr[1])
        ms = jnp.mean(o * o, axis=0, keepdims=True)
        o_ref[0] = (o * lax.rsqrt(ms + 1e-6) * g_ref[...] * (1.0 - lam_init)).astype(o_ref.dtype)


def _diff_attn(qa, ka, vat, lam_pack, g_sub, lam_init, tq, tk):
    B, S, _ = qa.shape
    kern = functools.partial(_diff_attn_kernel, lam_init)
    return pl.pallas_call(
        kern,
        grid=(B, DIFF_HEADS, S // tq, S // tk),
        in_specs=[pl.BlockSpec((1, tq, LANES), lambda b, h, iq, ik: (b, iq, 0)),
                  pl.BlockSpec((1, tq, LANES), lambda b, h, iq, ik: (b, iq, 1)),
                  pl.BlockSpec((1, tk, LANES), lambda b, h, iq, ik: (b, ik, 0)),
                  pl.BlockSpec((1, tk, LANES), lambda b, h, iq, ik: (b, ik, 1)),
                  pl.BlockSpec((1, DIFF_V, tk), lambda b, h, iq, ik: (b, h, ik)),
                  pl.BlockSpec((4, LANES), lambda b, h, iq, ik: (0, 0)),
                  pl.BlockSpec((DIFF_V, 1), lambda b, h, iq, ik: (0, 0))],
        out_specs=pl.BlockSpec((1, DIFF_V, tq), lambda b, h, iq, ik: (b, h, iq)),
        out_shape=jax.ShapeDtypeStruct((B, DIFF_HEADS * DIFF_V, S), MXU_DTYPE),
        scratch_shapes=[pltpu.VMEM((2, tq, LANES), MXU_DTYPE),
                        pltpu.VMEM((2, 1, tq), F32),
                        pltpu.VMEM((2, 1, tq), F32),
                        pltpu.VMEM((2, DIFF_V, tq), F32)],
        compiler_params=_cparams(("parallel", "parallel", "parallel", "arbitrary")),
        name="diff_attn",
    )(qa, qa, ka, ka, vat, lam_pack, g_sub)


def _mla_attn_kernel(q_ref, k_ref, vt_ref, o_ref, m_scr, l_scr, acc_scr):
    ik = pl.program_id(3)

    @pl.when(ik == 0)
    def _():
        _init_softmax(m_scr, l_scr, acc_scr)

    for hh in range(2):
        blk = slice(hh * LANES, (hh + 1) * LANES)
        vt = vt_ref[0, hh * MLA_V:(hh + 1) * MLA_V, :]
        _softmax_step(hh, q_ref[0, :, blk], k_ref[0, :, blk], vt, m_scr, l_scr, acc_scr)

    @pl.when(ik == pl.num_programs(3) - 1)
    def _():
        for hh in range(2):
            o_ref[0, hh * MLA_V:(hh + 1) * MLA_V, :] = (acc_scr[hh] / l_scr[hh]).astype(o_ref.dtype)


def _mla_attn(mq, mk, mvt, tq, tk):
    B, S, _ = mq.shape
    return pl.pallas_call(
        _mla_attn_kernel,
        grid=(B, MLA_HEADS // 2, S // tq, S // tk),
        in_specs=[pl.BlockSpec((1, tq, 2 * LANES), lambda b, j, iq, ik: (b, iq, j)),
                  pl.BlockSpec((1, tk, 2 * LANES), lambda b, j, iq, ik: (b, ik, j)),
                  pl.BlockSpec((1, 2 * MLA_V, tk), lambda b, j, iq, ik: (b, j, ik))],
        out_specs=pl.BlockSpec((1, 2 * MLA_V, tq), lambda b, j, iq, ik: (b, j, iq)),
        out_shape=jax.ShapeDtypeStruct((B, MLA_HEADS * MLA_V, S), MXU_DTYPE),
        scratch_shapes=[pltpu.VMEM((2, 1, tq), F32),
                        pltpu.VMEM((2, 1, tq), F32),
                        pltpu.VMEM((2, MLA_V, tq), F32)],
        compiler_params=_cparams(("parallel", "parallel", "parallel", "arbitrary")),
        name="mla_attn",
    )(mq, mk, mvt)


DIL_SUB = 128
DIL_WIDTH = DIL_HEADS * DIL_DIM


DIL_DILS = tuple(d for _, d in DIL_PAIRS)
DIL_HALO = DIL_RADIUS * DIL_DILS[-1]
DIL_TILE = 2048
DIL_UNROLL = 4


def _dil_kernel(seq, q_ref, kp_ref, kc_ref, kn_ref, vp_ref, vc_ref, vn_ref, o_ref, kbuf, vbuf, pat_o, pat_l):
    i = pl.program_id(2)
    tt = q_ref.shape[2]
    H = DIL_HALO
    R = DIL_RADIUS
    kbuf[0:H] = kp_ref[0, 0]
    kbuf[H:H + tt] = kc_ref[0, 0]
    kbuf[H + tt:] = kn_ref[0, 0]
    vbuf[0:H] = vp_ref[0, 0]
    vbuf[H:H + tt] = vc_ref[0, 0]
    vbuf[H + tt:] = vn_ref[0, 0]
    sub = DIL_SUB
    win = sub + 2 * R
    row_k = lax.broadcasted_iota(jnp.int32, (win, sub), 0)
    col_q = lax.broadcasted_iota(jnp.int32, (win, sub), 1)
    rel = row_k - col_q
    band = (rel >= 0) & (rel <= 2 * R)
    lane = _lane_iota((sub, LANES))
    for p, d in enumerate(DIL_DILS):
        n_sub = tt // d // sub
        length = seq // d

        def body(c, carry, p=p, d=d, n_sub=n_sub, length=length):
            r = c // n_sub
            u = c % n_sub
            q_start = r + d * sub * u
            k_start = r + d * (H // d + u * sub - R)
            qf = q_ref[0, 0, pl.ds(q_start, sub, stride=d), :]
            kw = kbuf[pl.ds(k_start, win, stride=d), :].astype(MXU_DTYPE)
            vt = vbuf[pl.ds(k_start, win, stride=d), :].T.astype(MXU_DTYPE)
            kpos = i * (tt // d) + u * sub - R + row_k
            valid = band & (kpos >= 0) & (kpos < length)
            qcat = jnp.concatenate([jnp.where(lane < DIL_DIM, qf, 0.0), jnp.where(lane >= DIL_DIM, qf, 0.0)],
                                   axis=0).astype(MXU_DTYPE)
            s = lax.dot_general(kw, qcat, (((1,), (1,)), ((), ())), preferred_element_type=F32)
            s = jnp.where(jnp.concatenate([valid, valid], axis=1), s, NEG)
            m = jnp.max(s, axis=0, keepdims=True)
            e = jnp.exp2(s - m)
            l = jnp.sum(e, axis=0, keepdims=True)
            ot = jnp.dot(vt, e.astype(MXU_DTYPE), preferred_element_type=F32) / l
            lse = m + jnp.log2(l)
            o_t = jnp.concatenate([ot[:DIL_DIM, :sub], ot[DIL_DIM:, sub:]], axis=0)
            l_t = jnp.concatenate([jnp.broadcast_to(lse[:, :sub], (DIL_DIM, sub)),
                                   jnp.broadcast_to(lse[:, sub:], (DIL_DIM, sub))], axis=0)
            pat_o[p, pl.ds(q_start, sub, stride=d), :] = o_t.T
            pat_l[p, pl.ds(q_start, sub, stride=d), :] = l_t.T
            return carry

        lax.fori_loop(0, tt // sub, body, 0, unroll=DIL_UNROLL)

    l0, l1, l2 = pat_l[0], pat_l[1], pat_l[2]
    mx = jnp.maximum(jnp.maximum(l0, l1), l2)
    e0, e1, e2 = jnp.exp2(l0 - mx), jnp.exp2(l1 - mx), jnp.exp2(l2 - mx)
    den = e0 + e1 + e2
    o_ref[0] = ((e0 / den) * pat_o[0] + (e1 / den) * pat_o[1] + (e2 / den) * pat_o[2]).astype(o_ref.dtype)


def _dil_attn(qb, kb, vb):
    P, B, S, _ = qb.shape
    tt = _tile(S, DIL_TILE)
    H = DIL_HALO
    nh = S // H
    cur = pl.BlockSpec((1, 1, tt, LANES), lambda b, hp, i: (hp, b, i, 0))
    prv = pl.BlockSpec((1, 1, H, LANES), lambda b, hp, i: (hp, b, jnp.maximum(i * (tt // H) - 1, 0), 0))
    nxt = pl.BlockSpec((1, 1, H, LANES), lambda b, hp, i: (hp, b, jnp.minimum((i + 1) * (tt // H), nh - 1), 0))
    return pl.pallas_call(
        functools.partial(_dil_kernel, S),
        grid=(B, P, S // tt),
        in_specs=[cur, prv, cur, nxt, prv, cur, nxt],
        out_specs=pl.BlockSpec((1, tt, LANES), lambda b, hp, i: (b, i, hp)),
        out_shape=jax.ShapeDtypeStruct((B, S, P * LANES), MXU_DTYPE),
        scratch_shapes=[pltpu.VMEM((tt + 2 * H, LANES), F32),
                        pltpu.VMEM((tt + 2 * H, LANES), F32),
                        pltpu.VMEM((len(DIL_DILS), tt, LANES), F32),
                        pltpu.VMEM((len(DIL_DILS), tt, LANES), F32)],
        compiler_params=_cparams(("parallel", "parallel", "parallel")),
        name="dil_attn",
    )(qb, kb, kb, kb, vb, vb, vb)


def _partner(x, k, lane):
    up = pltpu.roll(x, LANES - k, 1)
    dn = pltpu.roll(x, k, 1)
    return jnp.where((lane & k) == 0, up, dn)


def _route(logits, bias, lane):
    valid = lane < N_EXPERTS
    lane_f = lane.astype(F32)
    score = 1.0 / (1.0 + jnp.exp(-logits))
    biased = jnp.where(valid, score + bias, NEG)
    p1 = _partner(biased, 1, lane)
    hi = jnp.maximum(biased, p1)
    lo = jnp.minimum(biased, p1)
    hi_p = _partner(hi, 2, lane)
    lo_p = _partner(lo, 2, lane)
    top1 = jnp.maximum(hi, hi_p)
    top2 = jnp.maximum(jnp.minimum(hi, hi_p), jnp.maximum(lo, lo_p))
    gscore = jnp.where(valid, top1 + top2, NEG)
    gidx = (lane // GROUP_SIZE).astype(F32)
    big = float(LANES)
    gmax = jnp.max(gscore, axis=-1, keepdims=True)
    gsel = jnp.min(jnp.where(gscore == gmax, gidx, big), axis=-1, keepdims=True)
    cand = jnp.where((gidx == gsel) & valid, biased, NEG)
    m1 = jnp.max(cand, axis=-1, keepdims=True)
    i1 = jnp.min(jnp.where(cand == m1, lane_f, big), axis=-1, keepdims=True)
    oh1 = lane_f == i1
    cand2 = jnp.where(oh1, NEG, cand)
    m2 = jnp.max(cand2, axis=-1, keepdims=True)
    i2 = jnp.min(jnp.where((cand2 == m2) & jnp.logical_not(oh1), lane_f, big), axis=-1, keepdims=True)
    oh2 = lane_f == i2
    s1 = jnp.sum(jnp.where(oh1, score, 0.0), axis=-1, keepdims=True)
    s2 = jnp.sum(jnp.where(oh2, score, 0.0), axis=-1, keepdims=True)
    den = s1 + s2
    return jnp.where(oh1, s1 / den, 0.0) + jnp.where(oh2, s2 / den, 0.0)


def _post_kernel(x_ref, mod_ref, oa_ref, ob_ref, oc_ref, wo_ref, g_ref, b_ref, wrh_ref, wrl_ref, br_ref,
                 x1_ref, h2_ref, gate_ref):
    na = DIFF_HEADS * DIFF_V
    nb = na + DIL_WIDTH
    tn = (((0,), (0,)), ((), ()))
    y = (lax.dot_general(oa_ref[0], wo_ref[0:na], tn, preferred_element_type=F32)
         + jnp.dot(ob_ref[0], wo_ref[na:nb], preferred_element_type=F32)
         + lax.dot_general(oc_ref[0], wo_ref[nb:], tn, preferred_element_type=F32))
    g1 = mod_ref[0, 2:3, :]
    sh2 = mod_ref[0, 3:4, :]
    sc2 = mod_ref[0, 4:5, :]
    x1 = _layer_norm(ALPHA * x_ref[0] + g1 * y, g_ref[...], b_ref[...])
    x1_ref[0] = x1
    h2 = x1 * (1.0 + sc2) + sh2
    h2_ref[0] = h2.astype(h2_ref.dtype)
    h_hi = h2.astype(MXU_DTYPE)
    h_lo = (h2 - h_hi.astype(F32)).astype(MXU_DTYPE)
    logits = (jnp.dot(h_hi, wrh_ref[...], preferred_element_type=F32)
              + jnp.dot(h_lo, wrh_ref[...], preferred_element_type=F32)
              + jnp.dot(h_hi, wrl_ref[...], preferred_element_type=F32))
    gate_ref[0] = _route(logits, br_ref[...], _lane_iota(logits.shape))


def _post(x, mod, oa, ob, oc, w_out, g, b, w_r_hi, w_r_lo, b_r, tm):
    B, S, D = x.shape
    row = lambda b_, i: (b_, i, 0)
    const2 = lambda b_, i: (0, 0)
    return pl.pallas_call(
        _post_kernel,
        grid=(B, S // tm),
        in_specs=[pl.BlockSpec((1, tm, D), row),
                  pl.BlockSpec((1, 6, D), lambda b_, i: (b_, 0, 0)),
                  pl.BlockSpec((1, oa.shape[1], tm), lambda b_, i: (b_, 0, i)),
                  pl.BlockSpec((1, tm, DIL_WIDTH), row),
                  pl.BlockSpec((1, oc.shape[1], tm), lambda b_, i: (b_, 0, i)),
                  pl.BlockSpec(w_out.shape, const2),
                  pl.BlockSpec((1, D), const2),
                  pl.BlockSpec((1, D), const2),
                  pl.BlockSpec(w_r_hi.shape, const2),
                  pl.BlockSpec(w_r_lo.shape, const2),
                  pl.BlockSpec((1, LANES), const2)],
        out_specs=[pl.BlockSpec((1, tm, D), row),
                   pl.BlockSpec((1, tm, D), row),
                   pl.BlockSpec((1, tm, LANES), row)],
        out_shape=[jax.ShapeDtypeStruct((B, S, D), F32),
                   jax.ShapeDtypeStruct((B, S, D), MXU_DTYPE),
                   jax.ShapeDtypeStruct((B, S, LANES), F32)],
        compiler_params=_cparams(("parallel", "parallel")),
        name="post_mix",
    )(x, mod, oa, ob, oc, w_out, g.reshape(1, D), b.reshape(1, D), w_r_hi, w_r_lo, b_r)


def _moe_kernel(ec, x1_ref, mod_ref, h2_ref, gate_ref, w1_ref, w3_ref, w2_ref, g_ref, b_ref,
                o_ref, acc_scr):
    c = pl.program_id(2)

    @pl.when(c == 0)
    def _():
        acc_scr[...] = jnp.zeros(acc_scr.shape, F32)

    h = h2_ref[0]
    gate = gate_ref[0]
    lane = _lane_iota(gate.shape)
    for k in range(ec):
        ge = jnp.sum(jnp.where(lane == c * ec + k, gate, 0.0), axis=-1, keepdims=True)
        a = jnp.dot(h, w1_ref[k], preferred_element_type=F32)
        bb = jnp.dot(h, w3_ref[k], preferred_element_type=F32)
        hid = (a * (1.0 / (1.0 + jnp.exp(-a)))) * bb * ge
        acc_scr[...] += jnp.dot(hid.astype(MXU_DTYPE), w2_ref[k], preferred_element_type=F32)

    @pl.when(c == pl.num_programs(2) - 1)
    def _():
        g2 = mod_ref[0, 5:6, :]
        o_ref[0] = _layer_norm(ALPHA * x1_ref[0] + g2 * acc_scr[...], g_ref[...], b_ref[...])


def _moe(x1, mod, h2, gate, w1, w3, w2, g, b, tm, ec=4):
    B, S, D = x1.shape
    E = w1.shape[0]
    row = lambda b_, i, c: (b_, i, 0)
    const2 = lambda b_, i, c: (0, 0)
    return pl.pallas_call(
        functools.partial(_moe_kernel, ec),
        grid=(B, S // tm, E // ec),
        in_specs=[pl.BlockSpec((1, tm, D), row),
                  pl.BlockSpec((1, 6, D), lambda b_, i, c: (b_, 0, 0)),
                  pl.BlockSpec((1, tm, D), row),
                  pl.BlockSpec((1, tm, LANES), row),
                  pl.BlockSpec((ec, D, D_FF), lambda b_, i, c: (c, 0, 0)),
                  pl.BlockSpec((ec, D, D_FF), lambda b_, i, c: (c, 0, 0)),
                  pl.BlockSpec((ec, D_FF, D), lambda b_, i, c: (c, 0, 0)),
                  pl.BlockSpec((1, D), const2),
                  pl.BlockSpec((1, D), const2)],
        out_specs=pl.BlockSpec((1, tm, D), row),
        out_shape=jax.ShapeDtypeStruct((B, S, D), F32),
        scratch_shapes=[pltpu.VMEM((tm, D), F32)],
        compiler_params=_cparams(("parallel", "parallel", "arbitrary")),
        name="moe_ffn",
    )(x1, mod, h2, gate, w1, w3, w2, g.reshape(1, D), b.reshape(1, D))


def _rope_tables(seq):
    lane = np.arange(LANES)

    def tables(period, offset, rot, theta):
        half = rot // 2
        d = lane % period - offset
        first = (d >= 0) & (d < half)
        second = (d >= half) & (d < rot)
        fidx = np.where(first, d, np.where(second, d - half, 0))
        inv_freq = theta ** (-2.0 * jnp.arange(half, dtype=F32) / rot)
        ang = jnp.arange(seq, dtype=F32)[:, None] * inv_freq[None, :]
        cos_l = jnp.cos(ang)[:, fidx]
        sin_l = jnp.sin(ang)[:, fidx]
        cos = jnp.where((first | second)[None, :], cos_l, 1.0)
        s_up = jnp.where(first[None, :], -sin_l, 0.0)
        s_dn = jnp.where(second[None, :], sin_l, 0.0)
        return [cos, s_up, s_dn]

    tabs = (tables(DIFF_QK, 0, DIFF_QK // 4, ROPE_THETA)
            + tables(DIL_DIM, 0, DIL_DIM // 4, ROPE_THETA)
            + tables(LANES, KR_LANE, MLA_ROPE, MLA_THETA))
    return jnp.stack(tabs).astype(F32)


def _prep_layer(l, w_in, w_uq, w_ukv, w_out, w1, w3, w2):
    D = w_in.shape[1]
    wi = w_in[l]
    zeros = lambda n: jnp.zeros((D, n), wi.dtype)
    wi = jnp.concatenate([wi[:, :C_KR], zeros(KR_LANE), wi[:, C_KR:], zeros(LANES - KR_LANE - MLA_ROPE)], axis=1)
    uq = w_uq[l].reshape(MLA_Q_RANK, MLA_HEADS, MLA_NOPE + MLA_ROPE)
    uq = jnp.pad(uq, ((0, 0), (0, 0), (0, LANES - MLA_NOPE - MLA_ROPE))).reshape(MLA_Q_RANK, MLA_HEADS * LANES)
    ukv = w_ukv[l].reshape(MLA_KV_RANK, MLA_HEADS, MLA_NOPE + MLA_V)
    uk = jnp.pad(ukv[..., :MLA_NOPE], ((0, 0), (0, 0), (0, LANES - MLA_NOPE))).reshape(MLA_KV_RANK, MLA_HEADS * LANES)
    uv = ukv[..., MLA_NOPE:].reshape(MLA_KV_RANK, MLA_HEADS * MLA_V)
    c = lambda t: t.astype(MXU_DTYPE)
    return dict(w_in=c(wi), w_uq=c(uq), w_uk=c(uk), w_uv=c(uv), w_out=c(w_out[l]),
                w1=c(w1[l]), w3=c(w3[l]), w2=c(w2[l]))


def _tile(n, pref):
    return pref if n % pref == 0 else n


def _trunk(x, mods, tabs, layers, p):
    B, S, D = x.shape
    tm = _tile(S, 512)
    tq = _tile(S, ATTN_TQ)
    tk = _tile(S, ATTN_TK)
    x = _ln_in(x, p['ln_in_g'], p['ln_in_b'], tm)
    for l, lw in enumerate(layers):
        mod = mods[l]
        lam_init = 0.8 - 0.6 * math.exp(-0.3 * l)
        qa, ka, va, qb, kb, vb, mq, mk, mv = _proj(
            x, mod, tabs, lw['w_in'], p['g_cq'][l], p['g_ckv'][l], lw['w_uq'], lw['w_uk'], lw['w_uv'], tm)
        oa = _diff_attn(qa, ka, va, p['lam_pack'][l], p['g_sub'][l], lam_init, tq, tk)
        oc = _mla_attn(mq, mk, mv, tq, tk)
        ob = _dil_attn(qb, kb, vb)
        x1, h2, gate = _post(x, mod, oa, ob, oc, lw['w_out'], p['ln1_g'][l], p['ln1_b'][l],
                             p['w_r_hi'], p['w_r_lo'], p['b_r'], tm)
        x = _moe(x1, mod, h2, gate, lw['w1'], lw['w3'], lw['w2'], p['ln2_g'][l], p['ln2_b'][l], tm)
    return x


def kernel(x_prompt, x_sample, c_prompt, c_sample, ln_in_g, ln_in_b, w_router, b_router, w_ada, b_ada, w_in, lam_q1, lam_k1, lam_q2, lam_k2, g_subln, g_cq, g_ckv, w_uq, w_ukv, w_out, ln1_g, ln1_b, w1, w3, w2, ln2_g, ln2_b):
    depth = w_in.shape[0]
    D = x_prompt.shape[-1]
    bp, bs = c_prompt.shape[0], c_sample.shape[0]
    rows = -(-(bp + bs) // 8) * 8
    c_all = jnp.concatenate([c_prompt, c_sample, jnp.zeros((rows - bp - bs, D), F32)], axis=0)
    mods = _ada(c_all, w_ada, b_ada)
    mods_p = mods[:, :bp].reshape(depth, bp, 6, D)
    mods_s = mods[:, bp:bp + bs].reshape(depth, bs, 6, D)

    pad_l = lambda t: jnp.pad(t, ((0, 0), (0, LANES - t.shape[-1])))
    p = dict(
        ln_in_g=ln_in_g, ln_in_b=ln_in_b, g_cq=g_cq, g_ckv=g_ckv,
        ln1_g=ln1_g, ln1_b=ln1_b, ln2_g=ln2_g, ln2_b=ln2_b,
        lam_pack=jnp.stack([pad_l(lam_q1), pad_l(lam_k1), pad_l(lam_q2), pad_l(lam_k2)], axis=1),
        g_sub=g_subln.reshape(depth, DIFF_V, 1),
        b_r=pad_l(b_router.reshape(1, -1)),
    )
    w_r = pad_l(w_router)
    p['w_r_hi'] = w_r.astype(MXU_DTYPE)
    p['w_r_lo'] = (w_r - p['w_r_hi'].astype(F32)).astype(MXU_DTYPE)
    layers = [_prep_layer(l, w_in, w_uq, w_ukv, w_out, w1, w3, w2) for l in range(depth)]
    tabs = _rope_tables(max(x_prompt.shape[1], x_sample.shape[1]))
    y_prompt = _trunk(x_prompt, mods_p, tabs, layers, p)
    y_sample = _trunk(x_sample, mods_s, tabs, layers, p)
    return (y_prompt, y_sample)
```

```python
import functools
import math

import numpy as np
import jax
import jax.numpy as jnp
from jax import lax
from jax.experimental import pallas as pl
from jax.experimental.pallas import tpu as pltpu

F32 = jnp.float32
MXU_DTYPE = jnp.bfloat16

LANES = 128
D_MODEL = 1024
DEPTH = 4
DIFF_HEADS = 4
DIFF_QK = 32
DIFF_V = 64
DIL_HEADS = 6
DIL_DIM = 64
DIL_PAIRS = ((128, 1), (512, 4), (2048, 16))
DIL_RADIUS = 64
MLA_HEADS = 6
MLA_NOPE = 64
MLA_ROPE = 32
MLA_V = 64
MLA_Q_RANK = 256
MLA_KV_RANK = 128
MLA_THETA = 10000.0
ROPE_THETA = 500000.0
N_EXPERTS = 16
GROUP_SIZE = 4
D_FF = 256
ALPHA = (2 * DEPTH) ** 0.25
NEG = -1e30
LOG2E = math.log2(math.e)
ATTN_TQ = 512
ATTN_TK = 2048

C_Q1, C_Q2, C_K1, C_K2, C_VA = 0, 128, 256, 384, 512
C_QB, C_KB, C_VB = 768, 1152, 1536
C_CQ, C_CKV, C_KR, C_END = 1920, 2176, 2304, 2432
KR_LANE = 64

VMEM_LIMIT = 48 * 1024 * 1024


def _cparams(sem):
    return pltpu.CompilerParams(dimension_semantics=sem, vmem_limit_bytes=VMEM_LIMIT)


def _lane_iota(shape):
    return lax.broadcasted_iota(jnp.int32, shape, len(shape) - 1)


def _layer_norm(z, g, b):
    mu = jnp.mean(z, axis=-1, keepdims=True)
    zc = z - mu
    var = jnp.mean(zc * zc, axis=-1, keepdims=True)
    return zc * lax.rsqrt(var + 1e-5) * g + b


def _rope(t, tab_ref, typ, half):
    c = tab_ref[3 * typ]
    s_up = tab_ref[3 * typ + 1]
    s_dn = tab_ref[3 * typ + 2]
    return t * c + pltpu.roll(t, LANES - half, 1) * s_up + pltpu.roll(t, half, 1) * s_dn


def _ln_in_kernel(x_ref, g_ref, b_ref, o_ref):
    o_ref[0] = _layer_norm(x_ref[0], g_ref[...], b_ref[...])


def _ln_in(x, g, b, tm):
    B, S, D = x.shape
    return pl.pallas_call(
        _ln_in_kernel,
        grid=(B, S // tm),
        in_specs=[pl.BlockSpec((1, tm, D), lambda b, i: (b, i, 0)),
                  pl.BlockSpec((1, D), lambda b, i: (0, 0)),
                  pl.BlockSpec((1, D), lambda b, i: (0, 0))],
        out_specs=pl.BlockSpec((1, tm, D), lambda b, i: (b, i, 0)),
        out_shape=jax.ShapeDtypeStruct(x.shape, F32),
        compiler_params=_cparams(("parallel", "parallel")),
        name="ln_in",
    )(x, g.reshape(1, D), b.reshape(1, D))


def _ada_kernel(c_ref, w_ref, b_ref, o_ref):
    c = c_ref[...]
    cs = (c * (1.0 / (1.0 + jnp.exp(-c)))).astype(MXU_DTYPE)
    o_ref[0] = jnp.dot(cs, w_ref[0].astype(MXU_DTYPE), preferred_element_type=F32) + b_ref[0]


def _ada(c, w_ada, b_ada, tn=1536):
    R, D = c.shape
    L, _, N = w_ada.shape
    return pl.pallas_call(
        _ada_kernel,
        grid=(L, N // tn),
        in_specs=[pl.BlockSpec((R, D), lambda l, j: (0, 0)),
                  pl.BlockSpec((1, D, tn), lambda l, j: (l, 0, j)),
                  pl.BlockSpec((1, 1, tn), lambda l, j: (l, 0, j))],
        out_specs=pl.BlockSpec((1, R, tn), lambda l, j: (l, 0, j)),
        out_shape=jax.ShapeDtypeStruct((L, R, N), F32),
        compiler_params=_cparams(("parallel", "parallel")),
        name="ada",
    )(c, w_ada, b_ada.reshape(L, 1, N))


def _proj_kernel(x_ref, mod_ref, tab_ref, w_ref, gcq_ref, gckv_ref, wuq_ref, wuk_ref, wuv_ref,
                 qa_ref, ka_ref, va_ref, qb_ref, kb_ref, vb_ref, mq_ref, mk_ref, mv_ref):
    x = x_ref[0]
    sh1 = mod_ref[0, 0:1, :]
    sc1 = mod_ref[0, 1:2, :]
    h = (x * (1.0 + sc1) + sh1).astype(MXU_DTYPE)

    def mm(a, b):
        return jnp.dot(h, w_ref[:, a:b], preferred_element_type=F32)

    qa = mm(C_Q1, C_K1)
    ka = mm(C_K1, C_VA)
    sa = DIFF_QK ** -0.5 * LOG2E
    for j in range(2):
        blk = slice(j * LANES, (j + 1) * LANES)
        qa_ref[0, :, blk] = (_rope(qa[:, blk], tab_ref, 0, DIFF_QK // 8) * sa).astype(qa_ref.dtype)
        ka_ref[0, :, blk] = _rope(ka[:, blk], tab_ref, 0, DIFF_QK // 8).astype(ka_ref.dtype)
    va_ref[0] = mm(C_VA, C_QB).T.astype(va_ref.dtype)

    qb = mm(C_QB, C_KB)
    kb = mm(C_KB, C_VB)
    vb = mm(C_VB, C_CQ)
    sb = DIL_DIM ** -0.5 * LOG2E
    for j in range(DIL_HEADS // 2):
        blk = slice(j * LANES, (j + 1) * LANES)
        qb_ref[j, 0] = _rope(qb[:, blk], tab_ref, 1, DIL_DIM // 8) * sb
        kb_ref[j, 0] = _rope(kb[:, blk], tab_ref, 1, DIL_DIM // 8)
        vb_ref[j, 0] = vb[:, blk]

    def rms(t, g):
        return (t * lax.rsqrt(jnp.mean(t * t, axis=-1, keepdims=True) + 1e-6) * g).astype(MXU_DTYPE)

    cq = rms(mm(C_CQ, C_CKV), gcq_ref[...])
    ckv = rms(mm(C_CKV, C_KR), gckv_ref[...])
    kr = _rope(mm(C_KR, C_END), tab_ref, 2, MLA_ROPE // 2)
    q = jnp.dot(cq, wuq_ref[...], preferred_element_type=F32)
    kn = jnp.dot(ckv, wuk_ref[...], preferred_element_type=F32)
    sc = (MLA_NOPE + MLA_ROPE) ** -0.5 * LOG2E
    for j in range(MLA_HEADS):
        blk = slice(j * LANES, (j + 1) * LANES)
        mq_ref[0, :, blk] = (_rope(q[:, blk], tab_ref, 2, MLA_ROPE // 2) * sc).astype(mq_ref.dtype)
        mk_ref[0, :, blk] = (kn[:, blk] + kr).astype(mk_ref.dtype)
    mv_ref[0] = jnp.dot(ckv, wuv_ref[...], preferred_element_type=F32).T.astype(mv_ref.dtype)


def _proj(x, mod, tabs, w_in, g_cq, g_ckv, w_uq, w_uk, w_uv, tm):
    B, S, D = x.shape
    widths = (256, 256, 256, 384, 384, 384, 768, 768, 384)
    transposed = (2, 8)
    row = lambda b, i: (b, i, 0)
    col = lambda b, i: (b, 0, i)
    const2 = lambda b, i: (0, 0)
    out_specs = [pl.BlockSpec((1, w, tm), col) if n in transposed else pl.BlockSpec((1, tm, w), row)
                 for n, w in enumerate(widths)]
    out_shape = [jax.ShapeDtypeStruct((B, w, S) if n in transposed else (B, S, w), MXU_DTYPE)
                 for n, w in enumerate(widths)]
    for n in (3, 4, 5):
        out_specs[n] = pl.BlockSpec((DIL_HEADS // 2, 1, tm, LANES), lambda b, i: (0, b, i, 0))
        out_shape[n] = jax.ShapeDtypeStruct((DIL_HEADS // 2, B, S, LANES), F32)
    return pl.pallas_call(
        _proj_kernel,
        grid=(B, S // tm),
        in_specs=[pl.BlockSpec((1, tm, D), row),
                  pl.BlockSpec((1, 6, D), lambda b, i: (b, 0, 0)),
                  pl.BlockSpec((9, tm, LANES), lambda b, i: (0, i, 0)),
                  pl.BlockSpec(w_in.shape, const2),
                  pl.BlockSpec((1, MLA_Q_RANK), const2),
                  pl.BlockSpec((1, MLA_KV_RANK), const2),
                  pl.BlockSpec(w_uq.shape, const2),
                  pl.BlockSpec(w_uk.shape, const2),
                  pl.BlockSpec(w_uv.shape, const2)],
        out_specs=out_specs,
        out_shape=out_shape,
        compiler_params=_cparams(("parallel", "parallel")),
        name="proj_in",
    )(x, mod, tabs, w_in, g_cq.reshape(1, -1), g_ckv.reshape(1, -1), w_uq, w_uk, w_uv)


def _softmax_step(u, q, k, vt, m_scr, l_scr, acc_scr):
    st = lax.dot_general(k, q, (((1,), (1,)), ((), ())), preferred_element_type=F32)
    m_prev = m_scr[u]
    m_new = jnp.maximum(m_prev, jnp.max(st, axis=0, keepdims=True))
    alpha = jnp.exp2(m_prev - m_new)
    p = jnp.exp2(st - m_new)
    l_scr[u] = alpha * l_scr[u] + jnp.sum(p, axis=0, keepdims=True)
    acc_scr[u] = alpha * acc_scr[u] + jnp.dot(vt, p.astype(MXU_DTYPE), preferred_element_type=F32)
    m_scr[u] = m_new


def _init_softmax(m_scr, l_scr, acc_scr):
    m_scr[...] = jnp.full(m_scr.shape, NEG, F32)
    l_scr[...] = jnp.zeros(l_scr.shape, F32)
    acc_scr[...] = jnp.zeros(acc_scr.shape, F32)


def _diff_attn_kernel(lam_init, q1_ref, q2_ref, k1_ref, k2_ref, vt_ref, lamp_ref, g_ref, o_ref,
                      q_scr, m_scr, l_scr, acc_scr):
    h = pl.program_id(1)
    ik = pl.program_id(3)

    @pl.when(ik == 0)
    def _():
        _init_softmax(m_scr, l_scr, acc_scr)
        lane = _lane_iota(q1_ref.shape[1:])
        msk = (lane >= h * DIFF_QK) & (lane < (h + 1) * DIFF_QK)
        q_scr[0] = jnp.where(msk, q1_ref[0].astype(F32), 0.0).astype(q_scr.dtype)
        q_scr[1] = jnp.where(msk, q2_ref[0].astype(F32), 0.0).astype(q_scr.dtype)

    vt = vt_ref[0]
    _softmax_step(0, q_scr[0], k1_ref[0], vt, m_scr, l_scr, acc_scr)
    _softmax_step(1, q_scr[1], k2_ref[0], vt, m_scr, l_scr, acc_scr)

    @pl.when(ik == pl.num_programs(3) - 1)
    def _():
        lp = lamp_ref[...]
        lam = (jnp.exp(jnp.sum(lp[0:1] * lp[1:2], axis=-1, keepdims=True))
               - jnp.exp(jnp.sum(lp[2:3] * lp[3:4], axis=-1, keepdims=True)) + lam_init)
        o = acc_scr[0] / l_scr[0] - lam * (acc_scr[1] / l_scr[1])
        ms = jnp.mean(o * o, axis=0, keepdims=True)
        o_ref[0] = (o * lax.rsqrt(ms + 1e-6) * g_ref[...] * (1.0 - lam_init)).astype(o_ref.dtype)


def _diff_attn(qa, ka, vat, lam_pack, g_sub, lam_init, tq, tk):
    B, S, _ = qa.shape
    kern = functools.partial(_diff_attn_kernel, lam_init)
    return pl.pallas_call(
        kern,
        grid=(B, DIFF_HEADS, S // tq, S // tk),
        in_specs=[pl.BlockSpec((1, tq, LANES), lambda b, h, iq, ik: (b, iq, 0)),
                  pl.BlockSpec((1, tq, LANES), lambda b, h, iq, ik: (b, iq, 1)),
                  pl.BlockSpec((1, tk, LANES), lambda b, h, iq, ik: (b, ik, 0)),
                  pl.BlockSpec((1, tk, LANES), lambda b, h, iq, ik: (b, ik, 1)),
                  pl.BlockSpec((1, DIFF_V, tk), lambda b, h, iq, ik: (b, h, ik)),
                  pl.BlockSpec((4, LANES), lambda b, h, iq, ik: (0, 0)),
                  pl.BlockSpec((DIFF_V, 1), lambda b, h, iq, ik: (0, 0))],
        out_specs=pl.BlockSpec((1, DIFF_V, tq), lambda b, h, iq, ik: (b, h, iq)),
        out_shape=jax.ShapeDtypeStruct((B, DIFF_HEADS * DIFF_V, S), MXU_DTYPE),
        scratch_shapes=[pltpu.VMEM((2, tq, LANES), MXU_DTYPE),
                        pltpu.VMEM((2, 1, tq), F32),
                        pltpu.VMEM((2, 1, tq), F32),
                        pltpu.VMEM((2, DIFF_V, tq), F32)],
        compiler_params=_cparams(("parallel", "parallel", "parallel", "arbitrary")),
        name="diff_attn",
    )(qa, qa, ka, ka, vat, lam_pack, g_sub)


def _mla_attn_kernel(q_ref, k_ref, vt_ref, o_ref, m_scr, l_scr, acc_scr):
    ik = pl.program_id(3)

    @pl.when(ik == 0)
    def _():
        _init_softmax(m_scr, l_scr, acc_scr)

    for hh in range(2):
        blk = slice(hh * LANES, (hh + 1) * LANES)
        vt = vt_ref[0, hh * MLA_V:(hh + 1) * MLA_V, :]
        _softmax_step(hh, q_ref[0, :, blk], k_ref[0, :, blk], vt, m_scr, l_scr, acc_scr)

    @pl.when(ik == pl.num_programs(3) - 1)
    def _():
        for hh in range(2):
            o_ref[0, hh * MLA_V:(hh + 1) * MLA_V, :] = (acc_scr[hh] / l_scr[hh]).astype(o_ref.dtype)


def _mla_attn(mq, mk, mvt, tq, tk):
    B, S, _ = mq.shape
    return pl.pallas_call(
        _mla_attn_kernel,
        grid=(B, MLA_HEADS // 2, S // tq, S // tk),
        in_specs=[pl.BlockSpec((1, tq, 2 * LANES), lambda b, j, iq, ik: (b, iq, j)),
                  pl.BlockSpec((1, tk, 2 * LANES), lambda b, j, iq, ik: (b, ik, j)),
                  pl.BlockSpec((1, 2 * MLA_V, tk), lambda b, j, iq, ik: (b, j, ik))],
        out_specs=pl.BlockSpec((1, 2 * MLA_V, tq), lambda b, j, iq, ik: (b, j, iq)),
        out_shape=jax.ShapeDtypeStruct((B, MLA_HEADS * MLA_V, S), MXU_DTYPE),
        scratch_shapes=[pltpu.VMEM((2, 1, tq), F32),
                        pltpu.VMEM((2, 1, tq), F32),
                        pltpu.VMEM((2, MLA_V, tq), F32)],
        compiler_params=_cparams(("parallel", "parallel", "parallel", "arbitrary")),
        name="mla_attn",
    )(mq, mk, mvt)


DIL_SUB = 128
DIL_WIDTH = DIL_HEADS * DIL_DIM


DIL_DILS = tuple(d for _, d in DIL_PAIRS)
DIL_HALO = DIL_RADIUS * DIL_DILS[-1]
DIL_TILE = 2048


def _dil_kernel(seq, q_ref, kp_ref, kc_ref, kn_ref, vp_ref, vc_ref, vn_ref, o_ref,
                kbuf, vbuf, pat_o, pat_l, qg, kg, vg):
    i = pl.program_id(2)
    tt = q_ref.shape[2]
    H = DIL_HALO
    R = DIL_RADIUS
    kbuf[0:H] = kp_ref[0, 0]
    kbuf[H:H + tt] = kc_ref[0, 0]
    kbuf[H + tt:] = kn_ref[0, 0]
    vbuf[0:H] = vp_ref[0, 0]
    vbuf[H:H + tt] = vc_ref[0, 0]
    vbuf[H + tt:] = vn_ref[0, 0]
    sub = DIL_SUB
    win = sub + 2 * R
    nc = tt // sub
    cidx = lax.broadcasted_iota(jnp.int32, (nc, win, sub), 0)
    row_k = lax.broadcasted_iota(jnp.int32, (nc, win, sub), 1)
    col_q = lax.broadcasted_iota(jnp.int32, (nc, win, sub), 2)
    rel = row_k - col_q
    band = (rel >= 0) & (rel <= 2 * R)
    lane = _lane_iota((sub, LANES))
    for p, d in enumerate(DIL_DILS):
        n_sub = tt // d // sub
        length = seq // d
        q_starts = [c // n_sub + d * sub * (c % n_sub) for c in range(nc)]
        for c in range(nc):
            k_start = c // n_sub + d * (H // d + (c % n_sub) * sub - R)
            qf = q_ref[0, 0, pl.ds(q_starts[c], sub, stride=d), :]
            qg[c, :sub] = jnp.where(lane < DIL_DIM, qf, 0.0).astype(qg.dtype)
            qg[c, sub:] = jnp.where(lane >= DIL_DIM, qf, 0.0).astype(qg.dtype)
            kg[c] = kbuf[pl.ds(k_start, win, stride=d), :].astype(kg.dtype)
            vg[c] = vbuf[pl.ds(k_start, win, stride=d), :].T.astype(vg.dtype)
        u_of_c = cidx % n_sub if n_sub > 1 else jnp.zeros_like(cidx)
        kpos = i * (tt // d) + u_of_c * sub - R + row_k
        valid = band & (kpos >= 0) & (kpos < length)
        s = lax.dot_general(kg[...], qg[...], (((2,), (2,)), ((0,), (0,))),
                            preferred_element_type=F32)
        s = jnp.where(jnp.concatenate([valid, valid], axis=2), s, NEG)
        m = jnp.max(s, axis=1, keepdims=True)
        e = jnp.exp2(s - m)
        l = jnp.sum(e, axis=1, keepdims=True)
        ot = lax.dot_general(vg[...], e.astype(MXU_DTYPE), (((2,), (1,)), ((0,), (0,))),
                             preferred_element_type=F32) / l
        lse = m + jnp.log2(l)
        for c in range(nc):
            o_t = jnp.concatenate([ot[c, :DIL_DIM, :sub], ot[c, DIL_DIM:, sub:]], axis=0)
            l_t = jnp.concatenate([jnp.broadcast_to(lse[c, :, :sub], (DIL_DIM, sub)),
                                   jnp.broadcast_to(lse[c, :, sub:], (DIL_DIM, sub))], axis=0)
            pat_o[p, pl.ds(q_starts[c], sub, stride=d), :] = o_t.T
            pat_l[p, pl.ds(q_starts[c], sub, stride=d), :] = l_t.T

    l0, l1, l2 = pat_l[0], pat_l[1], pat_l[2]
    mx = jnp.maximum(jnp.maximum(l0, l1), l2)
    e0, e1, e2 = jnp.exp2(l0 - mx), jnp.exp2(l1 - mx), jnp.exp2(l2 - mx)
    den = e0 + e1 + e2
    o_ref[0] = ((e0 / den) * pat_o[0] + (e1 / den) * pat_o[1] + (e2 / den) * pat_o[2]).astype(o_ref.dtype)


def _dil_attn(qb, kb, vb):
    P, B, S, _ = qb.shape
    tt = _tile(S, DIL_TILE)
    H = DIL_HALO
    nh = S // H
    cur = pl.BlockSpec((1, 1, tt, LANES), lambda b, hp, i: (hp, b, i, 0))
    prv = pl.BlockSpec((1, 1, H, LANES), lambda b, hp, i: (hp, b, jnp.maximum(i * (tt // H) - 1, 0), 0))
    nxt = pl.BlockSpec((1, 1, H, LANES), lambda b, hp, i: (hp, b, jnp.minimum((i + 1) * (tt // H), nh - 1), 0))
    return pl.pallas_call(
        functools.partial(_dil_kernel, S),
        grid=(B, P, S // tt),
        in_specs=[cur, prv, cur, nxt, prv, cur, nxt],
        out_specs=pl.BlockSpec((1, tt, LANES), lambda b, hp, i: (b, i, hp)),
        out_shape=jax.ShapeDtypeStruct((B, S, P * LANES), MXU_DTYPE),
        scratch_shapes=[pltpu.VMEM((tt + 2 * H, LANES), F32),
                        pltpu.VMEM((tt + 2 * H, LANES), F32),
                        pltpu.VMEM((len(DIL_DILS), tt, LANES), F32),
                        pltpu.VMEM((len(DIL_DILS), tt, LANES), F32),
                        pltpu.VMEM((tt // DIL_SUB, 2 * DIL_SUB, LANES), MXU_DTYPE),
                        pltpu.VMEM((tt // DIL_SUB, DIL_SUB + 2 * DIL_RADIUS, LANES), MXU_DTYPE),
                        pltpu.VMEM((tt // DIL_SUB, LANES, DIL_SUB + 2 * DIL_RADIUS), MXU_DTYPE)],
        compiler_params=_cparams(("parallel", "parallel", "parallel")),
        name="dil_attn",
    )(qb, kb, kb, kb, vb, vb, vb)


def _partner(x, k, lane):
    up = pltpu.roll(x, LANES - k, 1)
    dn = pltpu.roll(x, k, 1)
    return jnp.where((lane & k) == 0, up, dn)


def _route(logits, bias, lane):
    valid = lane < N_EXPERTS
    lane_f = lane.astype(F32)
    score = 1.0 / (1.0 + jnp.exp(-logits))
    biased = jnp.where(valid, score + bias, NEG)
    p1 = _partner(biased, 1, lane)
    hi = jnp.maximum(biased, p1)
    lo = jnp.minimum(biased, p1)
    hi_p = _partner(hi, 2, lane)
    lo_p = _partner(lo, 2, lane)
    top1 = jnp.maximum(hi, hi_p)
    top2 = jnp.maximum(jnp.minimum(hi, hi_p), jnp.maximum(lo, lo_p))
    gscore = jnp.where(valid, top1 + top2, NEG)
    gidx = (lane // GROUP_SIZE).astype(F32)
    big = float(LANES)
    gmax = jnp.max(gscore, axis=-1, keepdims=True)
    gsel = jnp.min(jnp.where(gscore == gmax, gidx, big), axis=-1, keepdims=True)
    cand = jnp.where((gidx == gsel) & valid, biased, NEG)
    m1 = jnp.max(cand, axis=-1, keepdims=True)
    i1 = jnp.min(jnp.where(cand == m1, lane_f, big), axis=-1, keepdims=True)
    oh1 = lane_f == i1
    cand2 = jnp.where(oh1, NEG, cand)
    m2 = jnp.max(cand2, axis=-1, keepdims=True)
    i2 = jnp.min(jnp.where((cand2 == m2) & jnp.logical_not(oh1), lane_f, big), axis=-1, keepdims=True)
    oh2 = lane_f == i2
    s1 = jnp.sum(jnp.where(oh1, score, 0.0), axis=-1, keepdims=True)
    s2 = jnp.sum(jnp.where(oh2, score, 0.0), axis=-1, keepdims=True)
    den = s1 + s2
    return jnp.where(oh1, s1 / den, 0.0) + jnp.where(oh2, s2 / den, 0.0)


def _post_kernel(x_ref, mod_ref, oa_ref, ob_ref, oc_ref, wo_ref, g_ref, b_ref, wrh_ref, wrl_ref, br_ref,
                 x1_ref, h2_ref, gate_ref):
    na = DIFF_HEADS * DIFF_V
    nb = na + DIL_WIDTH
    tn = (((0,), (0,)), ((), ()))
    y = (lax.dot_general(oa_ref[0], wo_ref[0:na], tn, preferred_element_type=F32)
         + jnp.dot(ob_ref[0], wo_ref[na:nb], preferred_element_type=F32)
         + lax.dot_general(oc_ref[0], wo_ref[nb:], tn, preferred_element_type=F32))
    g1 = mod_ref[0, 2:3, :]
    sh2 = mod_ref[0, 3:4, :]
    sc2 = mod_ref[0, 4:5, :]
    x1 = _layer_norm(ALPHA * x_ref[0] + g1 * y, g_ref[...], b_ref[...])
    x1_ref[0] = x1
    h2 = x1 * (1.0 + sc2) + sh2
    h2_ref[0] = h2.astype(h2_ref.dtype)
    h_hi = h2.astype(MXU_DTYPE)
    h_lo = (h2 - h_hi.astype(F32)).astype(MXU_DTYPE)
    logits = (jnp.dot(h_hi, wrh_ref[...], preferred_element_type=F32)
              + jnp.dot(h_lo, wrh_ref[...], preferred_element_type=F32)
              + jnp.dot(h_hi, wrl_ref[...], preferred_element_type=F32))
    gate_ref[0] = _route(logits, br_ref[...], _lane_iota(logits.shape))


def _post(x, mod, oa, ob, oc, w_out, g, b, w_r_hi, w_r_lo, b_r, tm):
    B, S, D = x.shape
    row = lambda b_, i: (b_, i, 0)
    const2 = lambda b_, i: (0, 0)
    return pl.pallas_call(
        _post_kernel,
        grid=(B, S // tm),
        in_specs=[pl.BlockSpec((1, tm, D), row),
                  pl.BlockSpec((1, 6, D), lambda b_, i: (b_, 0, 0)),
                  pl.BlockSpec((1, oa.shape[1], tm), lambda b_, i: (b_, 0, i)),
                  pl.BlockSpec((1, tm, DIL_WIDTH), row),
                  pl.BlockSpec((1, oc.shape[1], tm), lambda b_, i: (b_, 0, i)),
                  pl.BlockSpec(w_out.shape, const2),
                  pl.BlockSpec((1, D), const2),
                  pl.BlockSpec((1, D), const2),
                  pl.BlockSpec(w_r_hi.shape, const2),
                  pl.BlockSpec(w_r_lo.shape, const2),
                  pl.BlockSpec((1, LANES), const2)],
        out_specs=[pl.BlockSpec((1, tm, D), row),
                   pl.BlockSpec((1, tm, D), row),
                   pl.BlockSpec((1, tm, LANES), row)],
        out_shape=[jax.ShapeDtypeStruct((B, S, D), F32),
                   jax.ShapeDtypeStruct((B, S, D), MXU_DTYPE),
                   jax.ShapeDtypeStruct((B, S, LANES), F32)],
        compiler_params=_cparams(("parallel", "parallel")),
        name="post_mix",
    )(x, mod, oa, ob, oc, w_out, g.reshape(1, D), b.reshape(1, D), w_r_hi, w_r_lo, b_r)


def _moe_kernel(ec, x1_ref, mod_ref, h2_ref, gate_ref, w1_ref, w3_ref, w2_ref, g_ref, b_ref,
                o_ref, acc_scr):
    c = pl.program_id(2)

    @pl.when(c == 0)
    def _():
        acc_scr[...] = jnp.zeros(acc_scr.shape, F32)

    h = h2_ref[0]
    gate = gate_ref[0]
    lane = _lane_iota(gate.shape)
    for k in range(ec):
        ge = jnp.sum(jnp.where(lane == c * ec + k, gate, 0.0), axis=-1, keepdims=True)
        a = jnp.dot(h, w1_ref[k], preferred_element_type=F32)
        bb = jnp.dot(h, w3_ref[k], preferred_element_type=F32)
        hid = (a * (1.0 / (1.0 + jnp.exp(-a)))) * bb * ge
        acc_scr[...] += jnp.dot(hid.astype(MXU_DTYPE), w2_ref[k], preferred_element_type=F32)

    @pl.when(c == pl.num_programs(2) - 1)
    def _():
        g2 = mod_ref[0, 5:6, :]
        o_ref[0] = _layer_norm(ALPHA * x1_ref[0] + g2 * acc_scr[...], g_ref[...], b_ref[...])


MOE_TM = 1024
MOE_EC = 2


def _moe(x1, mod, h2, gate, w1, w3, w2, g, b, tm, ec):
    B, S, D = x1.shape
    E = w1.shape[0]
    row = lambda b_, i, c: (b_, i, 0)
    const2 = lambda b_, i, c: (0, 0)
    return pl.pallas_call(
        functools.partial(_moe_kernel, ec),
        grid=(B, S // tm, E // ec),
        in_specs=[pl.BlockSpec((1, tm, D), row),
                  pl.BlockSpec((1, 6, D), lambda b_, i, c: (b_, 0, 0)),
                  pl.BlockSpec((1, tm, D), row),
                  pl.BlockSpec((1, tm, LANES), row),
                  pl.BlockSpec((ec, D, D_FF), lambda b_, i, c: (c, 0, 0)),
                  pl.BlockSpec((ec, D, D_FF), lambda b_, i, c: (c, 0, 0)),
                  pl.BlockSpec((ec, D_FF, D), lambda b_, i, c: (c, 0, 0)),
                  pl.BlockSpec((1, D), const2),
                  pl.BlockSpec((1, D), const2)],
        out_specs=pl.BlockSpec((1, tm, D), row),
        out_shape=jax.ShapeDtypeStruct((B, S, D), F32),
        scratch_shapes=[pltpu.VMEM((tm, D), F32)],
        compiler_params=_cparams(("parallel", "parallel", "arbitrary")),
        name="moe_ffn",
    )(x1, mod, h2, gate, w1, w3, w2, g.reshape(1, D), b.reshape(1, D))


def _rope_tables(seq):
    lane = np.arange(LANES)

    def tables(period, offset, rot, theta):
        half = rot // 2
        d = lane % period - offset
        first = (d >= 0) & (d < half)
        second = (d >= half) & (d < rot)
        fidx = np.where(first, d, np.where(second, d - half, 0))
        inv_freq = theta ** (-2.0 * jnp.arange(half, dtype=F32) / rot)
        ang = jnp.arange(seq, dtype=F32)[:, None] * inv_freq[None, :]
        cos_l = jnp.cos(ang)[:, fidx]
        sin_l = jnp.sin(ang)[:, fidx]
        cos = jnp.where((first | second)[None, :], cos_l, 1.0)
        s_up = jnp.where(first[None, :], -sin_l, 0.0)
        s_dn = jnp.where(second[None, :], sin_l, 0.0)
        return [cos, s_up, s_dn]

    tabs = (tables(DIFF_QK, 0, DIFF_QK // 4, ROPE_THETA)
            + tables(DIL_DIM, 0, DIL_DIM // 4, ROPE_THETA)
            + tables(LANES, KR_LANE, MLA_ROPE, MLA_THETA))
    return jnp.stack(tabs).astype(F32)


def _prep_layer(l, w_in, w_uq, w_ukv, w_out, w1, w3, w2):
    D = w_in.shape[1]
    wi = w_in[l]
    zeros = lambda n: jnp.zeros((D, n), wi.dtype)
    wi = jnp.concatenate([wi[:, :C_KR], zeros(KR_LANE), wi[:, C_KR:], zeros(LANES - KR_LANE - MLA_ROPE)], axis=1)
    uq = w_uq[l].reshape(MLA_Q_RANK, MLA_HEADS, MLA_NOPE + MLA_ROPE)
    uq = jnp.pad(uq, ((0, 0), (0, 0), (0, LANES - MLA_NOPE - MLA_ROPE))).reshape(MLA_Q_RANK, MLA_HEADS * LANES)
    ukv = w_ukv[l].reshape(MLA_KV_RANK, MLA_HEADS, MLA_NOPE + MLA_V)
    uk = jnp.pad(ukv[..., :MLA_NOPE], ((0, 0), (0, 0), (0, LANES - MLA_NOPE))).reshape(MLA_KV_RANK, MLA_HEADS * LANES)
    uv = ukv[..., MLA_NOPE:].reshape(MLA_KV_RANK, MLA_HEADS * MLA_V)
    c = lambda t: t.astype(MXU_DTYPE)
    return dict(w_in=c(wi), w_uq=c(uq), w_uk=c(uk), w_uv=c(uv), w_out=c(w_out[l]),
                w1=c(w1[l]), w3=c(w3[l]), w2=c(w2[l]))


def _tile(n, pref):
    return pref if n % pref == 0 else n


def _trunk(x, mods, tabs, layers, p):
    B, S, D = x.shape
    tm = _tile(S, 512)
    tq = _tile(S, ATTN_TQ)
    tk = _tile(S, ATTN_TK)
    x = _ln_in(x, p['ln_in_g'], p['ln_in_b'], tm)
    for l, lw in enumerate(layers):
        mod = mods[l]
        lam_init = 0.8 - 0.6 * math.exp(-0.3 * l)
        qa, ka, va, qb, kb, vb, mq, mk, mv = _proj(
            x, mod, tabs, lw['w_in'], p['g_cq'][l], p['g_ckv'][l], lw['w_uq'], lw['w_uk'], lw['w_uv'], tm)
        oa = _diff_attn(qa, ka, va, p['lam_pack'][l], p['g_sub'][l], lam_init, tq, tk)
        oc = _mla_attn(mq, mk, mv, tq, tk)
        ob = _dil_attn(qb, kb, vb)
        x1, h2, gate = _post(x, mod, oa, ob, oc, lw['w_out'], p['ln1_g'][l], p['ln1_b'][l],
                             p['w_r_hi'], p['w_r_lo'], p['b_r'], tm)
        x = _moe(x1, mod, h2, gate, lw['w1'], lw['w3'], lw['w2'], p['ln2_g'][l], p['ln2_b'][l],
                 _tile(S, MOE_TM), MOE_EC)
    return x


def kernel(x_prompt, x_sample, c_prompt, c_sample, ln_in_g, ln_in_b, w_router, b_router, w_ada, b_ada, w_in, lam_q1, lam_k1, lam_q2, lam_k2, g_subln, g_cq, g_ckv, w_uq, w_ukv, w_out, ln1_g, ln1_b, w1, w3, w2, ln2_g, ln2_b):
    depth = w_in.shape[0]
    D = x_prompt.shape[-1]
    bp, bs = c_prompt.shape[0], c_sample.shape[0]
    rows = -(-(bp + bs) // 8) * 8
    c_all = jnp.concatenate([c_prompt, c_sample, jnp.zeros((rows - bp - bs, D), F32)], axis=0)
    mods = _ada(c_all, w_ada, b_ada)
    mods_p = mods[:, :bp].reshape(depth, bp, 6, D)
    mods_s = mods[:, bp:bp + bs].reshape(depth, bs, 6, D)

    pad_l = lambda t: jnp.pad(t, ((0, 0), (0, LANES - t.shape[-1])))
    p = dict(
        ln_in_g=ln_in_g, ln_in_b=ln_in_b, g_cq=g_cq, g_ckv=g_ckv,
        ln1_g=ln1_g, ln1_b=ln1_b, ln2_g=ln2_g, ln2_b=ln2_b,
        lam_pack=jnp.stack([pad_l(lam_q1), pad_l(lam_k1), pad_l(lam_q2), pad_l(lam_k2)], axis=1),
        g_sub=g_subln.reshape(depth, DIFF_V, 1),
        b_r=pad_l(b_router.reshape(1, -1)),
    )
    w_r = pad_l(w_router)
    p['w_r_hi'] = w_r.astype(MXU_DTYPE)
    p['w_r_lo'] = (w_r - p['w_r_hi'].astype(F32)).astype(MXU_DTYPE)
    layers = [_prep_layer(l, w_in, w_uq, w_ukv, w_out, w1, w3, w2) for l in range(depth)]
    tabs = _rope_tables(max(x_prompt.shape[1], x_sample.shape[1]))
    y_prompt = _trunk(x_prompt, mods_p, tabs, layers, p)
    y_sample = _trunk(x_sample, mods_s, tabs, layers, p)
    return (y_prompt, y_sample)
```

```python
import functools
import math

import numpy as np
import jax
import jax.numpy as jnp
from jax import lax
from jax.experimental import pallas as pl
from jax.experimental.pallas import tpu as pltpu

F32 = jnp.float32
MXU_DTYPE = jnp.bfloat16

LANES = 128
D_MODEL = 1024
DEPTH = 4
DIFF_HEADS = 4
DIFF_QK = 32
DIFF_V = 64
DIL_HEADS = 6
DIL_DIM = 64
DIL_PAIRS = ((128, 1), (512, 4), (2048, 16))
DIL_RADIUS = 64
MLA_HEADS = 6
MLA_NOPE = 64
MLA_ROPE = 32
MLA_V = 64
MLA_Q_RANK = 256
MLA_KV_RANK = 128
MLA_THETA = 10000.0
ROPE_THETA = 500000.0
N_EXPERTS = 16
GROUP_SIZE = 4
D_FF = 256
ALPHA = (2 * DEPTH) ** 0.25
NEG = -1e30
LOG2E = math.log2(math.e)
ATTN_TQ = 512
ATTN_TK = 2048

C_Q1, C_Q2, C_K1, C_K2, C_VA = 0, 128, 256, 384, 512
C_QB, C_KB, C_VB = 768, 1152, 1536
C_CQ, C_CKV, C_KR, C_END = 1920, 2176, 2304, 2432
KR_LANE = 64

VMEM_LIMIT = 48 * 1024 * 1024


def _cparams(sem):
    return pltpu.CompilerParams(dimension_semantics=sem, vmem_limit_bytes=VMEM_LIMIT)


def _lane_iota(shape):
    return lax.broadcasted_iota(jnp.int32, shape, len(shape) - 1)


def _layer_norm(z, g, b):
    mu = jnp.mean(z, axis=-1, keepdims=True)
    zc = z - mu
    var = jnp.mean(zc * zc, axis=-1, keepdims=True)
    return zc * lax.rsqrt(var + 1e-5) * g + b


def _rope(t, tab_ref, typ, half):
    c = tab_ref[3 * typ]
    s_up = tab_ref[3 * typ + 1]
    s_dn = tab_ref[3 * typ + 2]
    return t * c + pltpu.roll(t, LANES - half, 1) * s_up + pltpu.roll(t, half, 1) * s_dn


def _ln_in_kernel(x_ref, g_ref, b_ref, o_ref):
    o_ref[0] = _layer_norm(x_ref[0], g_ref[...], b_ref[...])


def _ln_in(x, g, b, tm):
    B, S, D = x.shape
    return pl.pallas_call(
        _ln_in_kernel,
        grid=(B, S // tm),
        in_specs=[pl.BlockSpec((1, tm, D), lambda b, i: (b, i, 0)),
                  pl.BlockSpec((1, D), lambda b, i: (0, 0)),
                  pl.BlockSpec((1, D), lambda b, i: (0, 0))],
        out_specs=pl.BlockSpec((1, tm, D), lambda b, i: (b, i, 0)),
        out_shape=jax.ShapeDtypeStruct(x.shape, F32),
        compiler_params=_cparams(("parallel", "parallel")),
        name="ln_in",
    )(x, g.reshape(1, D), b.reshape(1, D))


def _ada_kernel(c_ref, w_ref, b_ref, o_ref):
    c = c_ref[...]
    cs = (c * (1.0 / (1.0 + jnp.exp(-c)))).astype(MXU_DTYPE)
    o_ref[0] = jnp.dot(cs, w_ref[0].astype(MXU_DTYPE), preferred_element_type=F32) + b_ref[0]


def _ada(c, w_ada, b_ada, tn=1536):
    R, D = c.shape
    L, _, N = w_ada.shape
    return pl.pallas_call(
        _ada_kernel,
        grid=(L, N // tn),
        in_specs=[pl.BlockSpec((R, D), lambda l, j: (0, 0)),
                  pl.BlockSpec((1, D, tn), lambda l, j: (l, 0, j)),
                  pl.BlockSpec((1, 1, tn), lambda l, j: (l, 0, j))],
        out_specs=pl.BlockSpec((1, R, tn), lambda l, j: (l, 0, j)),
        out_shape=jax.ShapeDtypeStruct((L, R, N), F32),
        compiler_params=_cparams(("parallel", "parallel")),
        name="ada",
    )(c, w_ada, b_ada.reshape(L, 1, N))


def _proj_kernel(x_ref, mod_ref, tab_ref, w_ref, gcq_ref, gckv_ref, wuq_ref, wuk_ref, wuv_ref,
                 qa_ref, ka_ref, va_ref, qb_ref, kb_ref, vb_ref, mq_ref, mk_ref, mv_ref):
    x = x_ref[0]
    sh1 = mod_ref[0, 0:1, :]
    sc1 = mod_ref[0, 1:2, :]
    h = (x * (1.0 + sc1) + sh1).astype(MXU_DTYPE)

    def mm(a, b):
        return jnp.dot(h, w_ref[:, a:b], preferred_element_type=F32)

    qa = mm(C_Q1, C_K1)
    ka = mm(C_K1, C_VA)
    sa = DIFF_QK ** -0.5 * LOG2E
    for j in range(2):
        blk = slice(j * LANES, (j + 1) * LANES)
        qa_ref[0, :, blk] = (_rope(qa[:, blk], tab_ref, 0, DIFF_QK // 8) * sa).astype(qa_ref.dtype)
        ka_ref[0, :, blk] = _rope(ka[:, blk], tab_ref, 0, DIFF_QK // 8).astype(ka_ref.dtype)
    va_ref[0] = mm(C_VA, C_QB).T.astype(va_ref.dtype)

    qb = mm(C_QB, C_KB)
    kb = mm(C_KB, C_VB)
    vb = mm(C_VB, C_CQ)
    sb = DIL_DIM ** -0.5 * LOG2E
    for j in range(DIL_HEADS // 2):
        blk = slice(j * LANES, (j + 1) * LANES)
        qb_ref[j, 0] = _rope(qb[:, blk], tab_ref, 1, DIL_DIM // 8) * sb
        kb_ref[j, 0] = _rope(kb[:, blk], tab_ref, 1, DIL_DIM // 8)
        vb_ref[j, 0] = vb[:, blk]

    def rms(t, g):
        return (t * lax.rsqrt(jnp.mean(t * t, axis=-1, keepdims=True) + 1e-6) * g).astype(MXU_DTYPE)

    cq = rms(mm(C_CQ, C_CKV), gcq_ref[...])
    ckv = rms(mm(C_CKV, C_KR), gckv_ref[...])
    kr = _rope(mm(C_KR, C_END), tab_ref, 2, MLA_ROPE // 2)
    q = jnp.dot(cq, wuq_ref[...], preferred_element_type=F32)
    kn = jnp.dot(ckv, wuk_ref[...], preferred_element_type=F32)
    sc = (MLA_NOPE + MLA_ROPE) ** -0.5 * LOG2E
    for j in range(MLA_HEADS):
        blk = slice(j * LANES, (j + 1) * LANES)
        mq_ref[0, :, blk] = (_rope(q[:, blk], tab_ref, 2, MLA_ROPE // 2) * sc).astype(mq_ref.dtype)
        mk_ref[0, :, blk] = (kn[:, blk] + kr).astype(mk_ref.dtype)
    mv_ref[0] = jnp.dot(ckv, wuv_ref[...], preferred_element_type=F32).T.astype(mv_ref.dtype)


def _proj(x, mod, tabs, w_in, g_cq, g_ckv, w_uq, w_uk, w_uv, tm):
    B, S, D = x.shape
    widths = (256, 256, 256, 384, 384, 384, 768, 768, 384)
    transposed = (2, 8)
    row = lambda b, i: (b, i, 0)
    col = lambda b, i: (b, 0, i)
    const2 = lambda b, i: (0, 0)
    out_specs = [pl.BlockSpec((1, w, tm), col) if n in transposed else pl.BlockSpec((1, tm, w), row)
                 for n, w in enumerate(widths)]
    out_shape = [jax.ShapeDtypeStruct((B, w, S) if n in transposed else (B, S, w), MXU_DTYPE)
                 for n, w in enumerate(widths)]
    for n in (3, 4, 5):
        out_specs[n] = pl.BlockSpec((DIL_HEADS // 2, 1, tm, LANES), lambda b, i: (0, b, i, 0))
        out_shape[n] = jax.ShapeDtypeStruct((DIL_HEADS // 2, B, S, LANES), F32)
    return pl.pallas_call(
        _proj_kernel,
        grid=(B, S // tm),
        in_specs=[pl.BlockSpec((1, tm, D), row),
                  pl.BlockSpec((1, 6, D), lambda b, i: (b, 0, 0)),
                  pl.BlockSpec((9, tm, LANES), lambda b, i: (0, i, 0)),
                  pl.BlockSpec(w_in.shape, const2),
                  pl.BlockSpec((1, MLA_Q_RANK), const2),
                  pl.BlockSpec((1, MLA_KV_RANK), const2),
                  pl.BlockSpec(w_uq.shape, const2),
                  pl.BlockSpec(w_uk.shape, const2),
                  pl.BlockSpec(w_uv.shape, const2)],
        out_specs=out_specs,
        out_shape=out_shape,
        compiler_params=_cparams(("parallel", "parallel")),
        name="proj_in",
    )(x, mod, tabs, w_in, g_cq.reshape(1, -1), g_ckv.reshape(1, -1), w_uq, w_uk, w_uv)


def _softmax_steps(chains, m_scr, l_scr, acc_scr):
    scores = [lax.dot_general(k, q, (((1,), (1,)), ((), ())), preferred_element_type=F32)
              for q, k, _ in chains]
    for u, (st, (_, _, vt)) in enumerate(zip(scores, chains)):
        m_prev = m_scr[u]
        m_new = jnp.maximum(m_prev, jnp.max(st, axis=0, keepdims=True))
        alpha = jnp.exp2(m_prev - m_new)
        p = jnp.exp2(st - m_new)
        l_scr[u] = alpha * l_scr[u] + jnp.sum(p, axis=0, keepdims=True)
        acc_scr[u] = alpha * acc_scr[u] + jnp.dot(vt, p.astype(MXU_DTYPE), preferred_element_type=F32)
        m_scr[u] = m_new


def _init_softmax(m_scr, l_scr, acc_scr):
    m_scr[...] = jnp.full(m_scr.shape, NEG, F32)
    l_scr[...] = jnp.zeros(l_scr.shape, F32)
    acc_scr[...] = jnp.zeros(acc_scr.shape, F32)


def _diff_attn_kernel(lam_init, q1_ref, q2_ref, k1_ref, k2_ref, vt_ref, lamp_ref, g_ref, o_ref,
                      q_scr, m_scr, l_scr, acc_scr):
    j = pl.program_id(1)
    ik = pl.program_id(3)

    @pl.when(ik == 0)
    def _():
        _init_softmax(m_scr, l_scr, acc_scr)
        lane = _lane_iota(q1_ref.shape[1:])
        for hh in range(2):
            h = 2 * j + hh
            msk = (lane >= h * DIFF_QK) & (lane < (h + 1) * DIFF_QK)
            q_scr[2 * hh] = jnp.where(msk, q1_ref[0].astype(F32), 0.0).astype(q_scr.dtype)
            q_scr[2 * hh + 1] = jnp.where(msk, q2_ref[0].astype(F32), 0.0).astype(q_scr.dtype)

    chains = []
    for hh in range(2):
        vt = vt_ref[0, hh * DIFF_V:(hh + 1) * DIFF_V, :]
        chains.append((q_scr[2 * hh], k1_ref[0], vt))
        chains.append((q_scr[2 * hh + 1], k2_ref[0], vt))
    _softmax_steps(chains, m_scr, l_scr, acc_scr)

    @pl.when(ik == pl.num_programs(3) - 1)
    def _():
        lp = lamp_ref[...]
        lam = (jnp.exp(jnp.sum(lp[0:1] * lp[1:2], axis=-1, keepdims=True))
               - jnp.exp(jnp.sum(lp[2:3] * lp[3:4], axis=-1, keepdims=True)) + lam_init)
        for hh in range(2):
            o = acc_scr[2 * hh] / l_scr[2 * hh] - lam * (acc_scr[2 * hh + 1] / l_scr[2 * hh + 1])
            ms = jnp.mean(o * o, axis=0, keepdims=True)
            o_ref[0, hh * DIFF_V:(hh + 1) * DIFF_V, :] = (
                o * lax.rsqrt(ms + 1e-6) * g_ref[...] * (1.0 - lam_init)).astype(o_ref.dtype)


def _diff_attn(qa, ka, vat, lam_pack, g_sub, lam_init, tq, tk):
    B, S, _ = qa.shape
    kern = functools.partial(_diff_attn_kernel, lam_init)
    return pl.pallas_call(
        kern,
        grid=(B, DIFF_HEADS // 2, S // tq, S // tk),
        in_specs=[pl.BlockSpec((1, tq, LANES), lambda b, j, iq, ik: (b, iq, 0)),
                  pl.BlockSpec((1, tq, LANES), lambda b, j, iq, ik: (b, iq, 1)),
                  pl.BlockSpec((1, tk, LANES), lambda b, j, iq, ik: (b, ik, 0)),
                  pl.BlockSpec((1, tk, LANES), lambda b, j, iq, ik: (b, ik, 1)),
                  pl.BlockSpec((1, 2 * DIFF_V, tk), lambda b, j, iq, ik: (b, j, ik)),
                  pl.BlockSpec((4, LANES), lambda b, j, iq, ik: (0, 0)),
                  pl.BlockSpec((DIFF_V, 1), lambda b, j, iq, ik: (0, 0))],
        out_specs=pl.BlockSpec((1, 2 * DIFF_V, tq), lambda b, j, iq, ik: (b, j, iq)),
        out_shape=jax.ShapeDtypeStruct((B, DIFF_HEADS * DIFF_V, S), MXU_DTYPE),
        scratch_shapes=[pltpu.VMEM((4, tq, LANES), MXU_DTYPE),
                        pltpu.VMEM((4, 1, tq), F32),
                        pltpu.VMEM((4, 1, tq), F32),
                        pltpu.VMEM((4, DIFF_V, tq), F32)],
        compiler_params=_cparams(("parallel", "parallel", "parallel", "arbitrary")),
        name="diff_attn",
    )(qa, qa, ka, ka, vat, lam_pack, g_sub)


MLA_GROUP = 3


def _mla_attn_kernel(q_ref, k_ref, vt_ref, o_ref, m_scr, l_scr, acc_scr):
    ik = pl.program_id(3)

    @pl.when(ik == 0)
    def _():
        _init_softmax(m_scr, l_scr, acc_scr)

    chains = []
    for hh in range(MLA_GROUP):
        blk = slice(hh * LANES, (hh + 1) * LANES)
        chains.append((q_ref[0, :, blk], k_ref[0, :, blk], vt_ref[0, hh * MLA_V:(hh + 1) * MLA_V, :]))
    _softmax_steps(chains, m_scr, l_scr, acc_scr)

    @pl.when(ik == pl.num_programs(3) - 1)
    def _():
        for hh in range(MLA_GROUP):
            o_ref[0, hh * MLA_V:(hh + 1) * MLA_V, :] = (acc_scr[hh] / l_scr[hh]).astype(o_ref.dtype)


def _mla_attn(mq, mk, mvt, tq, tk):
    B, S, _ = mq.shape
    g = MLA_GROUP
    return pl.pallas_call(
        _mla_attn_kernel,
        grid=(B, MLA_HEADS // g, S // tq, S // tk),
        in_specs=[pl.BlockSpec((1, tq, g * LANES), lambda b, j, iq, ik: (b, iq, j)),
                  pl.BlockSpec((1, tk, g * LANES), lambda b, j, iq, ik: (b, ik, j)),
                  pl.BlockSpec((1, g * MLA_V, tk), lambda b, j, iq, ik: (b, j, ik))],
        out_specs=pl.BlockSpec((1, g * MLA_V, tq), lambda b, j, iq, ik: (b, j, iq)),
        out_shape=jax.ShapeDtypeStruct((B, MLA_HEADS * MLA_V, S), MXU_DTYPE),
        scratch_shapes=[pltpu.VMEM((g, 1, tq), F32),
                        pltpu.VMEM((g, 1, tq), F32),
                        pltpu.VMEM((g, MLA_V, tq), F32)],
        compiler_params=_cparams(("parallel", "parallel", "parallel", "arbitrary")),
        name="mla_attn",
    )(mq, mk, mvt)


DIL_SUB = 128
DIL_WIDTH = DIL_HEADS * DIL_DIM


DIL_DILS = tuple(d for _, d in DIL_PAIRS)
DIL_HALO = DIL_RADIUS * DIL_DILS[-1]
DIL_TILE = 2048


def _dil_kernel(seq, q_ref, kp_ref, kc_ref, kn_ref, vp_ref, vc_ref, vn_ref, o_ref,
                kbuf, vbuf, pat_o, pat_l, qg, kg, vg):
    i = pl.program_id(2)
    tt = q_ref.shape[2]
    H = DIL_HALO
    R = DIL_RADIUS
    kbuf[0:H] = kp_ref[0, 0]
    kbuf[H:H + tt] = kc_ref[0, 0]
    kbuf[H + tt:] = kn_ref[0, 0]
    vbuf[0:H] = vp_ref[0, 0]
    vbuf[H:H + tt] = vc_ref[0, 0]
    vbuf[H + tt:] = vn_ref[0, 0]
    sub = DIL_SUB
    win = sub + 2 * R
    nc = tt // sub
    cidx = lax.broadcasted_iota(jnp.int32, (nc, win, sub), 0)
    row_k = lax.broadcasted_iota(jnp.int32, (nc, win, sub), 1)
    col_q = lax.broadcasted_iota(jnp.int32, (nc, win, sub), 2)
    rel = row_k - col_q
    band = (rel >= 0) & (rel <= 2 * R)
    lane = _lane_iota((sub, LANES))
    for p, d in enumerate(DIL_DILS):
        n_sub = tt // d // sub
        length = seq // d
        q_starts = [c // n_sub + d * sub * (c % n_sub) for c in range(nc)]
        for c in range(nc):
            k_start = c // n_sub + d * (H // d + (c % n_sub) * sub - R)
            qf = q_ref[0, 0, pl.ds(q_starts[c], sub, stride=d), :]
            qg[c, :sub] = jnp.where(lane < DIL_DIM, qf, 0.0).astype(qg.dtype)
            qg[c, sub:] = jnp.where(lane >= DIL_DIM, qf, 0.0).astype(qg.dtype)
            kg[c] = kbuf[pl.ds(k_start, win, stride=d), :].astype(kg.dtype)
            vg[c] = vbuf[pl.ds(k_start, win, stride=d), :].T.astype(vg.dtype)
        u_of_c = cidx % n_sub if n_sub > 1 else jnp.zeros_like(cidx)
        kpos = i * (tt // d) + u_of_c * sub - R + row_k
        valid = band & (kpos >= 0) & (kpos < length)
        s = lax.dot_general(kg[...], qg[...], (((2,), (2,)), ((0,), (0,))),
                            preferred_element_type=F32)
        s = jnp.where(jnp.concatenate([valid, valid], axis=2), s, NEG)
        m = jnp.max(s, axis=1, keepdims=True)
        e = jnp.exp2(s - m)
        l = jnp.sum(e, axis=1, keepdims=True)
        ot = lax.dot_general(vg[...], e.astype(MXU_DTYPE), (((2,), (1,)), ((0,), (0,))),
                             preferred_element_type=F32) / l
        lse = m + jnp.log2(l)
        for c in range(nc):
            o_t = jnp.concatenate([ot[c, :DIL_DIM, :sub], ot[c, DIL_DIM:, sub:]], axis=0)
            l_t = jnp.concatenate([jnp.broadcast_to(lse[c, :, :sub], (DIL_DIM, sub)),
                                   jnp.broadcast_to(lse[c, :, sub:], (DIL_DIM, sub))], axis=0)
            pat_o[p, pl.ds(q_starts[c], sub, stride=d), :] = o_t.T
            pat_l[p, pl.ds(q_starts[c], sub, stride=d), :] = l_t.T

    l0, l1, l2 = pat_l[0], pat_l[1], pat_l[2]
    mx = jnp.maximum(jnp.maximum(l0, l1), l2)
    e0, e1, e2 = jnp.exp2(l0 - mx), jnp.exp2(l1 - mx), jnp.exp2(l2 - mx)
    den = e0 + e1 + e2
    o_ref[0] = ((e0 / den) * pat_o[0] + (e1 / den) * pat_o[1] + (e2 / den) * pat_o[2]).astype(o_ref.dtype)


def _dil_attn(qb, kb, vb):
    P, B, S, _ = qb.shape
    tt = _tile(S, DIL_TILE)
    H = DIL_HALO
    nh = S // H
    cur = pl.BlockSpec((1, 1, tt, LANES), lambda b, hp, i: (hp, b, i, 0))
    prv = pl.BlockSpec((1, 1, H, LANES), lambda b, hp, i: (hp, b, jnp.maximum(i * (tt // H) - 1, 0), 0))
    nxt = pl.BlockSpec((1, 1, H, LANES), lambda b, hp, i: (hp, b, jnp.minimum((i + 1) * (tt // H), nh - 1), 0))
    return pl.pallas_call(
        functools.partial(_dil_kernel, S),
        grid=(B, P, S // tt),
        in_specs=[cur, prv, cur, nxt, prv, cur, nxt],
        out_specs=pl.BlockSpec((1, tt, LANES), lambda b, hp, i: (b, i, hp)),
        out_shape=jax.ShapeDtypeStruct((B, S, P * LANES), MXU_DTYPE),
        scratch_shapes=[pltpu.VMEM((tt + 2 * H, LANES), F32),
                        pltpu.VMEM((tt + 2 * H, LANES), F32),
                        pltpu.VMEM((len(DIL_DILS), tt, LANES), F32),
                        pltpu.VMEM((len(DIL_DILS), tt, LANES), F32),
                        pltpu.VMEM((tt // DIL_SUB, 2 * DIL_SUB, LANES), MXU_DTYPE),
                        pltpu.VMEM((tt // DIL_SUB, DIL_SUB + 2 * DIL_RADIUS, LANES), MXU_DTYPE),
                        pltpu.VMEM((tt // DIL_SUB, LANES, DIL_SUB + 2 * DIL_RADIUS), MXU_DTYPE)],
        compiler_params=_cparams(("parallel", "parallel", "parallel")),
        name="dil_attn",
    )(qb, kb, kb, kb, vb, vb, vb)


def _partner(x, k, lane):
    up = pltpu.roll(x, LANES - k, 1)
    dn = pltpu.roll(x, k, 1)
    return jnp.where((lane & k) == 0, up, dn)


def _route(logits, bias, lane):
    valid = lane < N_EXPERTS
    lane_f = lane.astype(F32)
    score = 1.0 / (1.0 + jnp.exp(-logits))
    biased = jnp.where(valid, score + bias, NEG)
    p1 = _partner(biased, 1, lane)
    hi = jnp.maximum(biased, p1)
    lo = jnp.minimum(biased, p1)
    hi_p = _partner(hi, 2, lane)
    lo_p = _partner(lo, 2, lane)
    top1 = jnp.maximum(hi, hi_p)
    top2 = jnp.maximum(jnp.minimum(hi, hi_p), jnp.maximum(lo, lo_p))
    gscore = jnp.where(valid, top1 + top2, NEG)
    gidx = (lane // GROUP_SIZE).astype(F32)
    big = float(LANES)
    gmax = jnp.max(gscore, axis=-1, keepdims=True)
    gsel = jnp.min(jnp.where(gscore == gmax, gidx, big), axis=-1, keepdims=True)
    cand = jnp.where((gidx == gsel) & valid, biased, NEG)
    m1 = jnp.max(cand, axis=-1, keepdims=True)
    i1 = jnp.min(jnp.where(cand == m1, lane_f, big), axis=-1, keepdims=True)
    oh1 = lane_f == i1
    cand2 = jnp.where(oh1, NEG, cand)
    m2 = jnp.max(cand2, axis=-1, keepdims=True)
    i2 = jnp.min(jnp.where((cand2 == m2) & jnp.logical_not(oh1), lane_f, big), axis=-1, keepdims=True)
    oh2 = lane_f == i2
    s1 = jnp.sum(jnp.where(oh1, score, 0.0), axis=-1, keepdims=True)
    s2 = jnp.sum(jnp.where(oh2, score, 0.0), axis=-1, keepdims=True)
    den = s1 + s2
    return jnp.where(oh1, s1 / den, 0.0) + jnp.where(oh2, s2 / den, 0.0)


POST_PARTS = 2


def _post_kernel(x_ref, mod_ref, oa_ref, ob_ref, oc_ref, wo_ref, g_ref, b_ref, wrh_ref, wrl_ref, br_ref,
                 x1_ref, h2_ref, gate_ref):
    na = DIFF_HEADS * DIFF_V
    nb = na + DIL_WIDTH
    tn = (((0,), (0,)), ((), ()))
    g1 = mod_ref[0, 2:3, :]
    sh2 = mod_ref[0, 3:4, :]
    sc2 = mod_ref[0, 4:5, :]
    tm = x_ref.shape[1]
    parts = POST_PARTS if tm % (POST_PARTS * LANES) == 0 else 1
    for r in range(parts):
        rows = slice(r * (tm // parts), (r + 1) * (tm // parts))
        y = (lax.dot_general(oa_ref[0, :, rows], wo_ref[0:na], tn, preferred_element_type=F32)
             + jnp.dot(ob_ref[0, rows, :], wo_ref[na:nb], preferred_element_type=F32)
             + lax.dot_general(oc_ref[0, :, rows], wo_ref[nb:], tn, preferred_element_type=F32))
        x1 = _layer_norm(ALPHA * x_ref[0, rows, :] + g1 * y, g_ref[...], b_ref[...])
        x1_ref[0, rows, :] = x1
        h2 = x1 * (1.0 + sc2) + sh2
        h2_ref[0, rows, :] = h2.astype(h2_ref.dtype)
        h_hi = h2.astype(MXU_DTYPE)
        h_lo = (h2 - h_hi.astype(F32)).astype(MXU_DTYPE)
        logits = (jnp.dot(h_hi, wrh_ref[...], preferred_element_type=F32)
                  + jnp.dot(h_lo, wrh_ref[...], preferred_element_type=F32)
                  + jnp.dot(h_hi, wrl_ref[...], preferred_element_type=F32))
        gate_ref[0, rows, :] = _route(logits, br_ref[...], _lane_iota(logits.shape))


def _post(x, mod, oa, ob, oc, w_out, g, b, w_r_hi, w_r_lo, b_r, tm):
    B, S, D = x.shape
    row = lambda b_, i: (b_, i, 0)
    const2 = lambda b_, i: (0, 0)
    return pl.pallas_call(
        _post_kernel,
        grid=(B, S // tm),
        in_specs=[pl.BlockSpec((1, tm, D), row),
                  pl.BlockSpec((1, 6, D), lambda b_, i: (b_, 0, 0)),
                  pl.BlockSpec((1, oa.shape[1], tm), lambda b_, i: (b_, 0, i)),
                  pl.BlockSpec((1, tm, DIL_WIDTH), row),
                  pl.BlockSpec((1, oc.shape[1], tm), lambda b_, i: (b_, 0, i)),
                  pl.BlockSpec(w_out.shape, const2),
                  pl.BlockSpec((1, D), const2),
                  pl.BlockSpec((1, D), const2),
                  pl.BlockSpec(w_r_hi.shape, const2),
                  pl.BlockSpec(w_r_lo.shape, const2),
                  pl.BlockSpec((1, LANES), const2)],
        out_specs=[pl.BlockSpec((1, tm, D), row),
                   pl.BlockSpec((1, tm, D), row),
                   pl.BlockSpec((1, tm, LANES), row)],
        out_shape=[jax.ShapeDtypeStruct((B, S, D), F32),
                   jax.ShapeDtypeStruct((B, S, D), MXU_DTYPE),
                   jax.ShapeDtypeStruct((B, S, LANES), F32)],
        compiler_params=_cparams(("parallel", "parallel")),
        name="post_mix",
    )(x, mod, oa, ob, oc, w_out, g.reshape(1, D), b.reshape(1, D), w_r_hi, w_r_lo, b_r)


def _moe_kernel(ec, x1_ref, mod_ref, h2_ref, gate_ref, w1_ref, w3_ref, w2_ref, g_ref, b_ref,
                o_ref, acc_scr):
    c = pl.program_id(2)

    @pl.when(c == 0)
    def _():
        acc_scr[...] = jnp.zeros(acc_scr.shape, F32)

    h = h2_ref[0]
    gate = gate_ref[0]
    lane = _lane_iota(gate.shape)
    for k in range(ec):
        ge = jnp.sum(jnp.where(lane == c * ec + k, gate, 0.0), axis=-1, keepdims=True)
        a = jnp.dot(h, w1_ref[k], preferred_element_type=F32)
        bb = jnp.dot(h, w3_ref[k], preferred_element_type=F32)
        hid = (a * (1.0 / (1.0 + jnp.exp(-a)))) * bb * ge
        acc_scr[...] += jnp.dot(hid.astype(MXU_DTYPE), w2_ref[k], preferred_element_type=F32)

    @pl.when(c == pl.num_programs(2) - 1)
    def _():
        g2 = mod_ref[0, 5:6, :]
        o_ref[0] = _layer_norm(ALPHA * x1_ref[0] + g2 * acc_scr[...], g_ref[...], b_ref[...])


MOE_TM = 1024
MOE_EC = 2


def _moe(x1, mod, h2, gate, w1, w3, w2, g, b, tm, ec):
    B, S, D = x1.shape
    E = w1.shape[0]
    row = lambda b_, i, c: (b_, i, 0)
    const2 = lambda b_, i, c: (0, 0)
    return pl.pallas_call(
        functools.partial(_moe_kernel, ec),
        grid=(B, S // tm, E // ec),
        in_specs=[pl.BlockSpec((1, tm, D), row),
                  pl.BlockSpec((1, 6, D), lambda b_, i, c: (b_, 0, 0)),
                  pl.BlockSpec((1, tm, D), row),
                  pl.BlockSpec((1, tm, LANES), row),
                  pl.BlockSpec((ec, D, D_FF), lambda b_, i, c: (c, 0, 0)),
                  pl.BlockSpec((ec, D, D_FF), lambda b_, i, c: (c, 0, 0)),
                  pl.BlockSpec((ec, D_FF, D), lambda b_, i, c: (c, 0, 0)),
                  pl.BlockSpec((1, D), const2),
                  pl.BlockSpec((1, D), const2)],
        out_specs=pl.BlockSpec((1, tm, D), row),
        out_shape=jax.ShapeDtypeStruct((B, S, D), F32),
        scratch_shapes=[pltpu.VMEM((tm, D), F32)],
        compiler_params=_cparams(("parallel", "parallel", "arbitrary")),
        name="moe_ffn",
    )(x1, mod, h2, gate, w1, w3, w2, g.reshape(1, D), b.reshape(1, D))


def _rope_tables(seq):
    lane = np.arange(LANES)

    def tables(period, offset, rot, theta):
        half = rot // 2
        d = lane % period - offset
        first = (d >= 0) & (d < half)
        second = (d >= half) & (d < rot)
        fidx = np.where(first, d, np.where(second, d - half, 0))
        inv_freq = theta ** (-2.0 * jnp.arange(half, dtype=F32) / rot)
        ang = jnp.arange(seq, dtype=F32)[:, None] * inv_freq[None, :]
        cos_l = jnp.cos(ang)[:, fidx]
        sin_l = jnp.sin(ang)[:, fidx]
        cos = jnp.where((first | second)[None, :], cos_l, 1.0)
        s_up = jnp.where(first[None, :], -sin_l, 0.0)
        s_dn = jnp.where(second[None, :], sin_l, 0.0)
        return [cos, s_up, s_dn]

    tabs = (tables(DIFF_QK, 0, DIFF_QK // 4, ROPE_THETA)
            + tables(DIL_DIM, 0, DIL_DIM // 4, ROPE_THETA)
            + tables(LANES, KR_LANE, MLA_ROPE, MLA_THETA))
    return jnp.stack(tabs).astype(F32)


def _prep_layer(l, w_in, w_uq, w_ukv, w_out, w1, w3, w2):
    D = w_in.shape[1]
    wi = w_in[l]
    zeros = lambda n: jnp.zeros((D, n), wi.dtype)
    wi = jnp.concatenate([wi[:, :C_KR], zeros(KR_LANE), wi[:, C_KR:], zeros(LANES - KR_LANE - MLA_ROPE)], axis=1)
    uq = w_uq[l].reshape(MLA_Q_RANK, MLA_HEADS, MLA_NOPE + MLA_ROPE)
    uq = jnp.pad(uq, ((0, 0), (0, 0), (0, LANES - MLA_NOPE - MLA_ROPE))).reshape(MLA_Q_RANK, MLA_HEADS * LANES)
    ukv = w_ukv[l].reshape(MLA_KV_RANK, MLA_HEADS, MLA_NOPE + MLA_V)
    uk = jnp.pad(ukv[..., :MLA_NOPE], ((0, 0), (0, 0), (0, LANES - MLA_NOPE))).reshape(MLA_KV_RANK, MLA_HEADS * LANES)
    uv = ukv[..., MLA_NOPE:].reshape(MLA_KV_RANK, MLA_HEADS * MLA_V)
    c = lambda t: t.astype(MXU_DTYPE)
    return dict(w_in=c(wi), w_uq=c(uq), w_uk=c(uk), w_uv=c(uv), w_out=c(w_out[l]),
                w1=c(w1[l]), w3=c(w3[l]), w2=c(w2[l]))


def _tile(n, pref):
    return pref if n % pref == 0 else n


def _trunk(x, mods, tabs, layers, p):
    B, S, D = x.shape
    tm = _tile(S, 512)
    tq = _tile(S, ATTN_TQ)
    tk = _tile(S, ATTN_TK)
    x = _ln_in(x, p['ln_in_g'], p['ln_in_b'], tm)
    for l, lw in enumerate(layers):
        mod = mods[l]
        lam_init = 0.8 - 0.6 * math.exp(-0.3 * l)
        qa, ka, va, qb, kb, vb, mq, mk, mv = _proj(
            x, mod, tabs, lw['w_in'], p['g_cq'][l], p['g_ckv'][l], lw['w_uq'], lw['w_uk'], lw['w_uv'], tm)
        oa = _diff_attn(qa, ka, va, p['lam_pack'][l], p['g_sub'][l], lam_init, tq, tk)
        oc = _mla_attn(mq, mk, mv, tq, tk)
        ob = _dil_attn(qb, kb, vb)
        x1, h2, gate = _post(x, mod, oa, ob, oc, lw['w_out'], p['ln1_g'][l], p['ln1_b'][l],
                             p['w_r_hi'], p['w_r_lo'], p['b_r'], tm)
        x = _moe(x1, mod, h2, gate, lw['w1'], lw['w3'], lw['w2'], p['ln2_g'][l], p['ln2_b'][l],
                 _tile(S, MOE_TM), MOE_EC)
    return x


def kernel(x_prompt, x_sample, c_prompt, c_sample, ln_in_g, ln_in_b, w_router, b_router, w_ada, b_ada, w_in, lam_q1, lam_k1, lam_q2, lam_k2, g_subln, g_cq, g_ckv, w_uq, w_ukv, w_out, ln1_g, ln1_b, w1, w3, w2, ln2_g, ln2_b):
    depth = w_in.shape[0]
    D = x_prompt.shape[-1]
    bp, bs = c_prompt.shape[0], c_sample.shape[0]
    rows = -(-(bp + bs) // 8) * 8
    c_all = jnp.concatenate([c_prompt, c_sample, jnp.zeros((rows - bp - bs, D), F32)], axis=0)
    mods = _ada(c_all, w_ada, b_ada)
    mods_p = mods[:, :bp].reshape(depth, bp, 6, D)
    mods_s = mods[:, bp:bp + bs].reshape(depth, bs, 6, D)

    pad_l = lambda t: jnp.pad(t, ((0, 0), (0, LANES - t.shape[-1])))
    p = dict(
        ln_in_g=ln_in_g, ln_in_b=ln_in_b, g_cq=g_cq, g_ckv=g_ckv,
        ln1_g=ln1_g, ln1_b=ln1_b, ln2_g=ln2_g, ln2_b=ln2_b,
        lam_pack=jnp.stack([pad_l(lam_q1), pad_l(lam_k1), pad_l(lam_q2), pad_l(lam_k2)], axis=1),
        g_sub=g_subln.reshape(depth, DIFF_V, 1),
        b_r=pad_l(b_router.reshape(1, -1)),
    )
    w_r = pad_l(w_router)
    p['w_r_hi'] = w_r.astype(MXU_DTYPE)
    p['w_r_lo'] = (w_r - p['w_r_hi'].astype(F32)).astype(MXU_DTYPE)
    layers = [_prep_layer(l, w_in, w_uq, w_ukv, w_out, w1, w3, w2) for l in range(depth)]
    tabs = _rope_tables(max(x_prompt.shape[1], x_sample.shape[1]))
    y_prompt = _trunk(x_prompt, mods_p, tabs, layers, p)
    y_sample = _trunk(x_sample, mods_s, tabs, layers, p)
    return (y_prompt, y_sample)
```

```python
import functools
import math

import numpy as np
import jax
import jax.numpy as jnp
from jax import lax
from jax.experimental import pallas as pl
from jax.experimental.pallas import tpu as pltpu

F32 = jnp.float32
MXU_DTYPE = jnp.bfloat16

LANES = 128
D_MODEL = 1024
DEPTH = 4
DIFF_HEADS = 4
DIFF_QK = 32
DIFF_V = 64
DIL_HEADS = 6
DIL_DIM = 64
DIL_PAIRS = ((128, 1), (512, 4), (2048, 16))
DIL_RADIUS = 64
MLA_HEADS = 6
MLA_NOPE = 64
MLA_ROPE = 32
MLA_V = 64
MLA_Q_RANK = 256
MLA_KV_RANK = 128
MLA_THETA = 10000.0
ROPE_THETA = 500000.0
N_EXPERTS = 16
GROUP_SIZE = 4
D_FF = 256
ALPHA = (2 * DEPTH) ** 0.25
NEG = -1e30
LOG2E = math.log2(math.e)
ATTN_TQ = 512
ATTN_TK = 2048

C_Q1, C_Q2, C_K1, C_K2, C_VA = 0, 128, 256, 384, 512
C_QB, C_KB, C_VB = 768, 1152, 1536
C_CQ, C_CKV, C_KR, C_END = 1920, 2176, 2304, 2432
KR_LANE = 64

VMEM_LIMIT = 48 * 1024 * 1024


def _cparams(sem):
    return pltpu.CompilerParams(dimension_semantics=sem, vmem_limit_bytes=VMEM_LIMIT)


def _lane_iota(shape):
    return lax.broadcasted_iota(jnp.int32, shape, len(shape) - 1)


def _layer_norm(z, g, b):
    mu = jnp.mean(z, axis=-1, keepdims=True)
    zc = z - mu
    var = jnp.mean(zc * zc, axis=-1, keepdims=True)
    return zc * lax.rsqrt(var + 1e-5) * g + b


def _rope(t, tab_ref, typ, half):
    c = tab_ref[3 * typ]
    s_up = tab_ref[3 * typ + 1]
    s_dn = tab_ref[3 * typ + 2]
    return t * c + pltpu.roll(t, LANES - half, 1) * s_up + pltpu.roll(t, half, 1) * s_dn


def _ln_in_kernel(x_ref, g_ref, b_ref, o_ref):
    o_ref[0] = _layer_norm(x_ref[0], g_ref[...], b_ref[...])


def _ln_in(x, g, b, tm):
    B, S, D = x.shape
    return pl.pallas_call(
        _ln_in_kernel,
        grid=(B, S // tm),
        in_specs=[pl.BlockSpec((1, tm, D), lambda b, i: (b, i, 0)),
                  pl.BlockSpec((1, D), lambda b, i: (0, 0)),
                  pl.BlockSpec((1, D), lambda b, i: (0, 0))],
        out_specs=pl.BlockSpec((1, tm, D), lambda b, i: (b, i, 0)),
        out_shape=jax.ShapeDtypeStruct(x.shape, F32),
        compiler_params=_cparams(("parallel", "parallel")),
        name="ln_in",
    )(x, g.reshape(1, D), b.reshape(1, D))


def _ada_kernel(c_ref, w_ref, b_ref, o_ref):
    c = c_ref[...]
    cs = (c * (1.0 / (1.0 + jnp.exp(-c)))).astype(MXU_DTYPE)
    o_ref[0] = jnp.dot(cs, w_ref[0].astype(MXU_DTYPE), preferred_element_type=F32) + b_ref[0]


def _ada(c, w_ada, b_ada, tn=1536):
    R, D = c.shape
    L, _, N = w_ada.shape
    return pl.pallas_call(
        _ada_kernel,
        grid=(L, N // tn),
        in_specs=[pl.BlockSpec((R, D), lambda l, j: (0, 0)),
                  pl.BlockSpec((1, D, tn), lambda l, j: (l, 0, j)),
                  pl.BlockSpec((1, 1, tn), lambda l, j: (l, 0, j))],
        out_specs=pl.BlockSpec((1, R, tn), lambda l, j: (l, 0, j)),
        out_shape=jax.ShapeDtypeStruct((L, R, N), F32),
        compiler_params=_cparams(("parallel", "parallel")),
        name="ada",
    )(c, w_ada, b_ada.reshape(L, 1, N))


def _proj_kernel(x_ref, mod_ref, tab_ref, w_ref, gcq_ref, gckv_ref, wuq_ref, wuk_ref, wuv_ref,
                 qa_ref, ka_ref, va_ref, qb_ref, kb_ref, vb_ref, mq_ref, mk_ref, mv_ref):
    x = x_ref[0]
    sh1 = mod_ref[0, 0:1, :]
    sc1 = mod_ref[0, 1:2, :]
    h = (x * (1.0 + sc1) + sh1).astype(MXU_DTYPE)

    def mm(a, b):
        return jnp.dot(h, w_ref[:, a:b], preferred_element_type=F32)

    qa = mm(C_Q1, C_K1)
    ka = mm(C_K1, C_VA)
    sa = DIFF_QK ** -0.5 * LOG2E
    for j in range(2):
        blk = slice(j * LANES, (j + 1) * LANES)
        qa_ref[0, :, blk] = (_rope(qa[:, blk], tab_ref, 0, DIFF_QK // 8) * sa).astype(qa_ref.dtype)
        ka_ref[0, :, blk] = _rope(ka[:, blk], tab_ref, 0, DIFF_QK // 8).astype(ka_ref.dtype)
    va_ref[0] = mm(C_VA, C_QB).T.astype(va_ref.dtype)

    qb = mm(C_QB, C_KB)
    kb = mm(C_KB, C_VB)
    vb = mm(C_VB, C_CQ)
    sb = DIL_DIM ** -0.5 * LOG2E
    for j in range(DIL_HEADS // 2):
        blk = slice(j * LANES, (j + 1) * LANES)
        qb_ref[j, 0] = _rope(qb[:, blk], tab_ref, 1, DIL_DIM // 8) * sb
        kb_ref[j, 0] = _rope(kb[:, blk], tab_ref, 1, DIL_DIM // 8)
        vb_ref[j, 0] = vb[:, blk]

    def rms(t, g):
        return (t * lax.rsqrt(jnp.mean(t * t, axis=-1, keepdims=True) + 1e-6) * g).astype(MXU_DTYPE)

    cq = rms(mm(C_CQ, C_CKV), gcq_ref[...])
    ckv = rms(mm(C_CKV, C_KR), gckv_ref[...])
    kr = _rope(mm(C_KR, C_END), tab_ref, 2, MLA_ROPE // 2)
    q = jnp.dot(cq, wuq_ref[...], preferred_element_type=F32)
    kn = jnp.dot(ckv, wuk_ref[...], preferred_element_type=F32)
    sc = (MLA_NOPE + MLA_ROPE) ** -0.5 * LOG2E
    for j in range(MLA_HEADS):
        blk = slice(j * LANES, (j + 1) * LANES)
        mq_ref[0, :, blk] = (_rope(q[:, blk], tab_ref, 2, MLA_ROPE // 2) * sc).astype(mq_ref.dtype)
        mk_ref[0, :, blk] = (kn[:, blk] + kr).astype(mk_ref.dtype)
    mv_ref[0] = jnp.dot(ckv, wuv_ref[...], preferred_element_type=F32).T.astype(mv_ref.dtype)


def _proj(x, mod, tabs, w_in, g_cq, g_ckv, w_uq, w_uk, w_uv, tm):
    B, S, D = x.shape
    widths = (256, 256, 256, 384, 384, 384, 768, 768, 384)
    transposed = (2, 8)
    row = lambda b, i: (b, i, 0)
    col = lambda b, i: (b, 0, i)
    const2 = lambda b, i: (0, 0)
    out_specs = [pl.BlockSpec((1, w, tm), col) if n in transposed else pl.BlockSpec((1, tm, w), row)
                 for n, w in enumerate(widths)]
    out_shape = [jax.ShapeDtypeStruct((B, w, S) if n in transposed else (B, S, w), MXU_DTYPE)
                 for n, w in enumerate(widths)]
    for n in (3, 4, 5):
        out_specs[n] = pl.BlockSpec((DIL_HEADS // 2, 1, tm, LANES), lambda b, i: (0, b, i, 0))
        out_shape[n] = jax.ShapeDtypeStruct((DIL_HEADS // 2, B, S, LANES), F32)
    return pl.pallas_call(
        _proj_kernel,
        grid=(B, S // tm),
        in_specs=[pl.BlockSpec((1, tm, D), row),
                  pl.BlockSpec((1, 6, D), lambda b, i: (b, 0, 0)),
                  pl.BlockSpec((9, tm, LANES), lambda b, i: (0, i, 0)),
                  pl.BlockSpec(w_in.shape, const2),
                  pl.BlockSpec((1, MLA_Q_RANK), const2),
                  pl.BlockSpec((1, MLA_KV_RANK), const2),
                  pl.BlockSpec(w_uq.shape, const2),
                  pl.BlockSpec(w_uk.shape, const2),
                  pl.BlockSpec(w_uv.shape, const2)],
        out_specs=out_specs,
        out_shape=out_shape,
        compiler_params=_cparams(("parallel", "parallel")),
        name="proj_in",
    )(x, mod, tabs, w_in, g_cq.reshape(1, -1), g_ckv.reshape(1, -1), w_uq, w_uk, w_uv)


def _softmax_steps(chains, m_scr, l_scr, acc_scr):
    scores = [lax.dot_general(k, q, (((1,), (1,)), ((), ())), preferred_element_type=F32)
              for q, k, _ in chains]
    for u, (st, (_, _, vt)) in enumerate(zip(scores, chains)):
        m_prev = m_scr[u]
        m_new = jnp.maximum(m_prev, jnp.max(st, axis=0, keepdims=True))
        alpha = jnp.exp2(m_prev - m_new)
        p = jnp.exp2(st - m_new)
        l_scr[u] = alpha * l_scr[u] + jnp.sum(p, axis=0, keepdims=True)
        acc_scr[u] = alpha * acc_scr[u] + jnp.dot(vt, p.astype(MXU_DTYPE), preferred_element_type=F32)
        m_scr[u] = m_new


def _init_softmax(m_scr, l_scr, acc_scr):
    m_scr[...] = jnp.full(m_scr.shape, NEG, F32)
    l_scr[...] = jnp.zeros(l_scr.shape, F32)
    acc_scr[...] = jnp.zeros(acc_scr.shape, F32)


def _diff_attn_kernel(lam_init, q1_ref, q2_ref, k1_ref, k2_ref, vt_ref, lamp_ref, g_ref, o_ref,
                      q_scr, m_scr, l_scr, acc_scr):
    j = pl.program_id(1)
    ik = pl.program_id(3)

    @pl.when(ik == 0)
    def _():
        _init_softmax(m_scr, l_scr, acc_scr)
        lane = _lane_iota(q1_ref.shape[1:])
        for hh in range(2):
            h = 2 * j + hh
            msk = (lane >= h * DIFF_QK) & (lane < (h + 1) * DIFF_QK)
            q_scr[2 * hh] = jnp.where(msk, q1_ref[0].astype(F32), 0.0).astype(q_scr.dtype)
            q_scr[2 * hh + 1] = jnp.where(msk, q2_ref[0].astype(F32), 0.0).astype(q_scr.dtype)

    chains = []
    for hh in range(2):
        vt = vt_ref[0, hh * DIFF_V:(hh + 1) * DIFF_V, :]
        chains.append((q_scr[2 * hh], k1_ref[0], vt))
        chains.append((q_scr[2 * hh + 1], k2_ref[0], vt))
    _softmax_steps(chains, m_scr, l_scr, acc_scr)

    @pl.when(ik == pl.num_programs(3) - 1)
    def _():
        lp = lamp_ref[...]
        lam = (jnp.exp(jnp.sum(lp[0:1] * lp[1:2], axis=-1, keepdims=True))
               - jnp.exp(jnp.sum(lp[2:3] * lp[3:4], axis=-1, keepdims=True)) + lam_init)
        for hh in range(2):
            o = acc_scr[2 * hh] / l_scr[2 * hh] - lam * (acc_scr[2 * hh + 1] / l_scr[2 * hh + 1])
            ms = jnp.mean(o * o, axis=0, keepdims=True)
            o_ref[0, hh * DIFF_V:(hh + 1) * DIFF_V, :] = (
                o * lax.rsqrt(ms + 1e-6) * g_ref[...] * (1.0 - lam_init)).astype(o_ref.dtype)


def _diff_attn(qa, ka, vat, lam_pack, g_sub, lam_init, tq, tk):
    B, S, _ = qa.shape
    kern = functools.partial(_diff_attn_kernel, lam_init)
    return pl.pallas_call(
        kern,
        grid=(B, DIFF_HEADS // 2, S // tq, S // tk),
        in_specs=[pl.BlockSpec((1, tq, LANES), lambda b, j, iq, ik: (b, iq, 0)),
                  pl.BlockSpec((1, tq, LANES), lambda b, j, iq, ik: (b, iq, 1)),
                  pl.BlockSpec((1, tk, LANES), lambda b, j, iq, ik: (b, ik, 0)),
                  pl.BlockSpec((1, tk, LANES), lambda b, j, iq, ik: (b, ik, 1)),
                  pl.BlockSpec((1, 2 * DIFF_V, tk), lambda b, j, iq, ik: (b, j, ik)),
                  pl.BlockSpec((4, LANES), lambda b, j, iq, ik: (0, 0)),
                  pl.BlockSpec((DIFF_V, 1), lambda b, j, iq, ik: (0, 0))],
        out_specs=pl.BlockSpec((1, 2 * DIFF_V, tq), lambda b, j, iq, ik: (b, j, iq)),
        out_shape=jax.ShapeDtypeStruct((B, DIFF_HEADS * DIFF_V, S), MXU_DTYPE),
        scratch_shapes=[pltpu.VMEM((4, tq, LANES), MXU_DTYPE),
                        pltpu.VMEM((4, 1, tq), F32),
                        pltpu.VMEM((4, 1, tq), F32),
                        pltpu.VMEM((4, DIFF_V, tq), F32)],
        compiler_params=_cparams(("parallel", "parallel", "parallel", "arbitrary")),
        name="diff_attn",
    )(qa, qa, ka, ka, vat, lam_pack, g_sub)


MLA_GROUP = 3


def _mla_attn_kernel(q_ref, k_ref, vt_ref, o_ref, m_scr, l_scr, acc_scr):
    ik = pl.program_id(3)

    @pl.when(ik == 0)
    def _():
        _init_softmax(m_scr, l_scr, acc_scr)

    chains = []
    for hh in range(MLA_GROUP):
        blk = slice(hh * LANES, (hh + 1) * LANES)
        chains.append((q_ref[0, :, blk], k_ref[0, :, blk], vt_ref[0, hh * MLA_V:(hh + 1) * MLA_V, :]))
    _softmax_steps(chains, m_scr, l_scr, acc_scr)

    @pl.when(ik == pl.num_programs(3) - 1)
    def _():
        for hh in range(MLA_GROUP):
            o_ref[0, hh * MLA_V:(hh + 1) * MLA_V, :] = (acc_scr[hh] / l_scr[hh]).astype(o_ref.dtype)


def _mla_attn(mq, mk, mvt, tq, tk):
    B, S, _ = mq.shape
    g = MLA_GROUP
    return pl.pallas_call(
        _mla_attn_kernel,
        grid=(B, MLA_HEADS // g, S // tq, S // tk),
        in_specs=[pl.BlockSpec((1, tq, g * LANES), lambda b, j, iq, ik: (b, iq, j)),
                  pl.BlockSpec((1, tk, g * LANES), lambda b, j, iq, ik: (b, ik, j)),
                  pl.BlockSpec((1, g * MLA_V, tk), lambda b, j, iq, ik: (b, j, ik))],
        out_specs=pl.BlockSpec((1, g * MLA_V, tq), lambda b, j, iq, ik: (b, j, iq)),
        out_shape=jax.ShapeDtypeStruct((B, MLA_HEADS * MLA_V, S), MXU_DTYPE),
        scratch_shapes=[pltpu.VMEM((g, 1, tq), F32),
                        pltpu.VMEM((g, 1, tq), F32),
                        pltpu.VMEM((g, MLA_V, tq), F32)],
        compiler_params=_cparams(("parallel", "parallel", "parallel", "arbitrary")),
        name="mla_attn",
    )(mq, mk, mvt)


DIL_SUB = 128
DIL_WIDTH = DIL_HEADS * DIL_DIM


DIL_DILS = tuple(d for _, d in DIL_PAIRS)
DIL_HALO = DIL_RADIUS * DIL_DILS[-1]
DIL_TILE = 2048


def _dil_kernel(seq, q_ref, kp_ref, kc_ref, kn_ref, vp_ref, vc_ref, vn_ref, o_ref,
                kbuf, vbuf, pat_o, pat_l, qg, kg, vg):
    i = pl.program_id(2)
    tt = q_ref.shape[2]
    H = DIL_HALO
    R = DIL_RADIUS
    kbuf[0:H] = kp_ref[0, 0]
    kbuf[H:H + tt] = kc_ref[0, 0]
    kbuf[H + tt:] = kn_ref[0, 0]
    vbuf[0:H] = vp_ref[0, 0]
    vbuf[H:H + tt] = vc_ref[0, 0]
    vbuf[H + tt:] = vn_ref[0, 0]
    sub = DIL_SUB
    win = sub + 2 * R
    nc = tt // sub
    cidx = lax.broadcasted_iota(jnp.int32, (nc, win, sub), 0)
    row_k = lax.broadcasted_iota(jnp.int32, (nc, win, sub), 1)
    col_q = lax.broadcasted_iota(jnp.int32, (nc, win, sub), 2)
    rel = row_k - col_q
    band = (rel >= 0) & (rel <= 2 * R)
    lane = _lane_iota((sub, LANES))
    for p, d in enumerate(DIL_DILS):
        n_sub = tt // d // sub
        length = seq // d
        q_starts = [c // n_sub + d * sub * (c % n_sub) for c in range(nc)]
        for c in range(nc):
            k_start = c // n_sub + d * (H // d + (c % n_sub) * sub - R)
            qf = q_ref[0, 0, pl.ds(q_starts[c], sub, stride=d), :]
            qg[c, :sub] = jnp.where(lane < DIL_DIM, qf, 0.0).astype(qg.dtype)
            qg[c, sub:] = jnp.where(lane >= DIL_DIM, qf, 0.0).astype(qg.dtype)
            kg[c] = kbuf[pl.ds(k_start, win, stride=d), :].astype(kg.dtype)
            vg[c] = vbuf[pl.ds(k_start, win, stride=d), :].T.astype(vg.dtype)
        u_of_c = cidx % n_sub if n_sub > 1 else jnp.zeros_like(cidx)
        kpos = i * (tt // d) + u_of_c * sub - R + row_k
        valid = band & (kpos >= 0) & (kpos < length)
        s = lax.dot_general(kg[...], qg[...], (((2,), (2,)), ((0,), (0,))),
                            preferred_element_type=F32)
        s = jnp.where(jnp.concatenate([valid, valid], axis=2), s, NEG)
        m = jnp.max(s, axis=1, keepdims=True)
        e = jnp.exp2(s - m)
        l = jnp.sum(e, axis=1, keepdims=True)
        ot = lax.dot_general(vg[...], e.astype(MXU_DTYPE), (((2,), (1,)), ((0,), (0,))),
                             preferred_element_type=F32) / l
        lse = m + jnp.log2(l)
        for c in range(nc):
            o_t = jnp.concatenate([ot[c, :DIL_DIM, :sub], ot[c, DIL_DIM:, sub:]], axis=0)
            l_t = jnp.concatenate([jnp.broadcast_to(lse[c, :, :sub], (DIL_DIM, sub)),
                                   jnp.broadcast_to(lse[c, :, sub:], (DIL_DIM, sub))], axis=0)
            pat_o[p, pl.ds(q_starts[c], sub, stride=d), :] = o_t.T
            pat_l[p, pl.ds(q_starts[c], sub, stride=d), :] = l_t.T

    l0, l1, l2 = pat_l[0], pat_l[1], pat_l[2]
    mx = jnp.maximum(jnp.maximum(l0, l1), l2)
    e0, e1, e2 = jnp.exp2(l0 - mx), jnp.exp2(l1 - mx), jnp.exp2(l2 - mx)
    den = e0 + e1 + e2
    o_ref[0] = ((e0 / den) * pat_o[0] + (e1 / den) * pat_o[1] + (e2 / den) * pat_o[2]).astype(o_ref.dtype)


def _dil_attn(qb, kb, vb):
    P, B, S, _ = qb.shape
    tt = _tile(S, DIL_TILE)
    H = DIL_HALO
    nh = S // H
    cur = pl.BlockSpec((1, 1, tt, LANES), lambda b, hp, i: (hp, b, i, 0))
    prv = pl.BlockSpec((1, 1, H, LANES), lambda b, hp, i: (hp, b, jnp.maximum(i * (tt // H) - 1, 0), 0))
    nxt = pl.BlockSpec((1, 1, H, LANES), lambda b, hp, i: (hp, b, jnp.minimum((i + 1) * (tt // H), nh - 1), 0))
    return pl.pallas_call(
        functools.partial(_dil_kernel, S),
        grid=(B, P, S // tt),
        in_specs=[cur, prv, cur, nxt, prv, cur, nxt],
        out_specs=pl.BlockSpec((1, tt, LANES), lambda b, hp, i: (b, i, hp)),
        out_shape=jax.ShapeDtypeStruct((B, S, P * LANES), MXU_DTYPE),
        scratch_shapes=[pltpu.VMEM((tt + 2 * H, LANES), F32),
                        pltpu.VMEM((tt + 2 * H, LANES), F32),
                        pltpu.VMEM((len(DIL_DILS), tt, LANES), F32),
                        pltpu.VMEM((len(DIL_DILS), tt, LANES), F32),
                        pltpu.VMEM((tt // DIL_SUB, 2 * DIL_SUB, LANES), MXU_DTYPE),
                        pltpu.VMEM((tt // DIL_SUB, DIL_SUB + 2 * DIL_RADIUS, LANES), MXU_DTYPE),
                        pltpu.VMEM((tt // DIL_SUB, LANES, DIL_SUB + 2 * DIL_RADIUS), MXU_DTYPE)],
        compiler_params=_cparams(("parallel", "parallel", "parallel")),
        name="dil_attn",
    )(qb, kb, kb, kb, vb, vb, vb)


def _partner(x, k, lane):
    up = pltpu.roll(x, LANES - k, 1)
    dn = pltpu.roll(x, k, 1)
    return jnp.where((lane & k) == 0, up, dn)


GROUP_LANE = N_EXPERTS


def _route(logits, bias, lane):
    valid = lane < N_EXPERTS
    lane_f = lane.astype(F32)
    score = 1.0 / (1.0 + jnp.exp(-logits))
    biased = jnp.where(valid, score + bias, NEG)
    p1 = _partner(biased, 1, lane)
    hi = jnp.maximum(biased, p1)
    lo = jnp.minimum(biased, p1)
    hi_p = _partner(hi, 2, lane)
    lo_p = _partner(lo, 2, lane)
    top1 = jnp.maximum(hi, hi_p)
    top2 = jnp.maximum(jnp.minimum(hi, hi_p), jnp.maximum(lo, lo_p))
    gscore = jnp.where(valid, top1 + top2, NEG)
    gidx = (lane // GROUP_SIZE).astype(F32)
    big = float(LANES)
    gmax = jnp.max(gscore, axis=-1, keepdims=True)
    gsel = jnp.min(jnp.where(gscore == gmax, gidx, big), axis=-1, keepdims=True)
    cand = jnp.where((gidx == gsel) & valid, biased, NEG)
    m1 = jnp.max(cand, axis=-1, keepdims=True)
    i1 = jnp.min(jnp.where(cand == m1, lane_f, big), axis=-1, keepdims=True)
    oh1 = lane_f == i1
    cand2 = jnp.where(oh1, NEG, cand)
    m2 = jnp.max(cand2, axis=-1, keepdims=True)
    i2 = jnp.min(jnp.where((cand2 == m2) & jnp.logical_not(oh1), lane_f, big), axis=-1, keepdims=True)
    oh2 = lane_f == i2
    s1 = jnp.sum(jnp.where(oh1, score, 0.0), axis=-1, keepdims=True)
    s2 = jnp.sum(jnp.where(oh2, score, 0.0), axis=-1, keepdims=True)
    den = s1 + s2
    gate = jnp.where(oh1, s1 / den, 0.0) + jnp.where(oh2, s2 / den, 0.0)
    return jnp.where(lane == GROUP_LANE, gsel, gate)


POST_PARTS = 2


def _post_kernel(x_ref, mod_ref, oa_ref, ob_ref, oc_ref, wo_ref, g_ref, b_ref, wrh_ref, wrl_ref, br_ref,
                 x1_ref, h2_ref, gate_ref):
    na = DIFF_HEADS * DIFF_V
    nb = na + DIL_WIDTH
    tn = (((0,), (0,)), ((), ()))
    g1 = mod_ref[0, 2:3, :]
    sh2 = mod_ref[0, 3:4, :]
    sc2 = mod_ref[0, 4:5, :]
    tm = x_ref.shape[1]
    parts = POST_PARTS if tm % (POST_PARTS * LANES) == 0 else 1
    for r in range(parts):
        rows = slice(r * (tm // parts), (r + 1) * (tm // parts))
        y = (lax.dot_general(oa_ref[0, :, rows], wo_ref[0:na], tn, preferred_element_type=F32)
             + jnp.dot(ob_ref[0, rows, :], wo_ref[na:nb], preferred_element_type=F32)
             + lax.dot_general(oc_ref[0, :, rows], wo_ref[nb:], tn, preferred_element_type=F32))
        x1 = _layer_norm(ALPHA * x_ref[0, rows, :] + g1 * y, g_ref[...], b_ref[...])
        x1_ref[0, rows, :] = x1
        h2 = x1 * (1.0 + sc2) + sh2
        h2_ref[0, rows, :] = h2.astype(h2_ref.dtype)
        h_hi = h2.astype(MXU_DTYPE)
        h_lo = (h2 - h_hi.astype(F32)).astype(MXU_DTYPE)
        logits = (jnp.dot(h_hi, wrh_ref[...], preferred_element_type=F32)
                  + jnp.dot(h_lo, wrh_ref[...], preferred_element_type=F32)
                  + jnp.dot(h_hi, wrl_ref[...], preferred_element_type=F32))
        gate_ref[0, rows, :] = _route(logits, br_ref[...], _lane_iota(logits.shape))


def _post(x, mod, oa, ob, oc, w_out, g, b, w_r_hi, w_r_lo, b_r, tm):
    B, S, D = x.shape
    row = lambda b_, i: (b_, i, 0)
    const2 = lambda b_, i: (0, 0)
    return pl.pallas_call(
        _post_kernel,
        grid=(B, S // tm),
        in_specs=[pl.BlockSpec((1, tm, D), row),
                  pl.BlockSpec((1, 6, D), lambda b_, i: (b_, 0, 0)),
                  pl.BlockSpec((1, oa.shape[1], tm), lambda b_, i: (b_, 0, i)),
                  pl.BlockSpec((1, tm, DIL_WIDTH), row),
                  pl.BlockSpec((1, oc.shape[1], tm), lambda b_, i: (b_, 0, i)),
                  pl.BlockSpec(w_out.shape, const2),
                  pl.BlockSpec((1, D), const2),
                  pl.BlockSpec((1, D), const2),
                  pl.BlockSpec(w_r_hi.shape, const2),
                  pl.BlockSpec(w_r_lo.shape, const2),
                  pl.BlockSpec((1, LANES), const2)],
        out_specs=[pl.BlockSpec((1, tm, D), row),
                   pl.BlockSpec((1, tm, D), row),
                   pl.BlockSpec((1, tm, LANES), row)],
        out_shape=[jax.ShapeDtypeStruct((B, S, D), F32),
                   jax.ShapeDtypeStruct((B, S, D), MXU_DTYPE),
                   jax.ShapeDtypeStruct((B, S, LANES), F32)],
        compiler_params=_cparams(("parallel", "parallel")),
        name="post_mix",
    )(x, mod, oa, ob, oc, w_out, g.reshape(1, D), b.reshape(1, D), w_r_hi, w_r_lo, b_r)


MOE_TM = 1024
MOE_CHUNK = 256
N_GROUPS = N_EXPERTS // GROUP_SIZE


def _moe_kernel(x1_ref, mod_ref, h2_ref, gate_ref, w1_ref, w3_ref, w2_ref, g_ref, b_ref, o_ref,
                acc_scr, src_scr, sel_scr, rank_scr):
    g = pl.program_id(2)
    tm = h2_ref.shape[1]
    d = h2_ref.shape[2]
    C = MOE_CHUNK

    @pl.when(g == 0)
    def _():
        acc_scr[...] = jnp.zeros(acc_scr.shape, F32)
        gate = gate_ref[0]
        gate_hi = gate.astype(MXU_DTYPE)
        src_scr[:, 0:d] = h2_ref[0]
        src_scr[:, d:d + LANES] = gate_hi
        src_scr[:, d + LANES:] = (gate - gate_hi.astype(F32)).astype(MXU_DTYPE)
        gsel_t = gate.T[GROUP_LANE:GROUP_LANE + 1, :]
        grp = lax.broadcasted_iota(jnp.int32, sel_scr.shape, 0).astype(F32)
        sel = jnp.where(gsel_t == grp, 1.0, 0.0)
        r = lax.broadcasted_iota(jnp.int32, (tm, tm), 0)
        c = lax.broadcasted_iota(jnp.int32, (tm, tm), 1)
        before = jnp.where(r < c, 1.0, 0.0).astype(MXU_DTYPE)
        sel_scr[...] = sel
        rank_scr[...] = jnp.dot(sel.astype(MXU_DTYPE), before, preferred_element_type=F32)

    sel = sel_scr[pl.ds(g, 1), :]
    rank = rank_scr[pl.ds(g, 1), :]
    n_g = jnp.sum(sel).astype(jnp.int32)
    lane = _lane_iota((C, LANES))
    row = lax.broadcasted_iota(jnp.int32, (C, tm), 0).astype(F32)

    def chunk(ci, carry):
        base = (ci * C).astype(F32)
        onehot = jnp.where((rank - base == row) & (sel > 0.5), 1.0, 0.0).astype(MXU_DTYPE)
        picked = jnp.dot(onehot, src_scr[...], preferred_element_type=F32)
        xs = picked[:, 0:d].astype(MXU_DTYPE)
        gs = picked[:, d:d + LANES] + picked[:, d + LANES:]
        ys = jnp.zeros((C, d), F32)
        for e in range(GROUP_SIZE):
            ge = jnp.sum(jnp.where(lane == g * GROUP_SIZE + e, gs, 0.0), axis=-1, keepdims=True)
            a = jnp.dot(xs, w1_ref[e], preferred_element_type=F32)
            bb = jnp.dot(xs, w3_ref[e], preferred_element_type=F32)
            hid = (a * (1.0 / (1.0 + jnp.exp(-a)))) * bb * ge
            ys = ys + jnp.dot(hid.astype(MXU_DTYPE), w2_ref[e], preferred_element_type=F32)
        acc_scr[...] += lax.dot_general(onehot, ys.astype(MXU_DTYPE), (((0,), (0,)), ((), ())),
                                        preferred_element_type=F32)
        return carry

    lax.fori_loop(0, (n_g + C - 1) // C, chunk, 0)

    @pl.when(g == pl.num_programs(2) - 1)
    def _():
        g2 = mod_ref[0, 5:6, :]
        o_ref[0] = _layer_norm(ALPHA * x1_ref[0] + g2 * acc_scr[...], g_ref[...], b_ref[...])


def _moe(x1, mod, h2, gate, w1, w3, w2, g, b, tm):
    B, S, D = x1.shape
    gs = GROUP_SIZE
    row = lambda b_, i, c: (b_, i, 0)
    const2 = lambda b_, i, c: (0, 0)
    return pl.pallas_call(
        _moe_kernel,
        grid=(B, S // tm, N_GROUPS),
        in_specs=[pl.BlockSpec((1, tm, D), row),
                  pl.BlockSpec((1, 6, D), lambda b_, i, c: (b_, 0, 0)),
                  pl.BlockSpec((1, tm, D), row),
                  pl.BlockSpec((1, tm, LANES), row),
                  pl.BlockSpec((gs, D, D_FF), lambda b_, i, c: (c, 0, 0)),
                  pl.BlockSpec((gs, D, D_FF), lambda b_, i, c: (c, 0, 0)),
                  pl.BlockSpec((gs, D_FF, D), lambda b_, i, c: (c, 0, 0)),
                  pl.BlockSpec((1, D), const2),
                  pl.BlockSpec((1, D), const2)],
        out_specs=pl.BlockSpec((1, tm, D), row),
        out_shape=jax.ShapeDtypeStruct((B, S, D), F32),
        scratch_shapes=[pltpu.VMEM((tm, D), F32),
                        pltpu.VMEM((tm, D + 2 * LANES), MXU_DTYPE),
                        pltpu.VMEM((8, tm), F32),
                        pltpu.VMEM((8, tm), F32)],
        compiler_params=_cparams(("parallel", "parallel", "arbitrary")),
        name="moe_ffn",
    )(x1, mod, h2, gate, w1, w3, w2, g.reshape(1, D), b.reshape(1, D))


def _rope_tables(seq):
    lane = np.arange(LANES)

    def tables(period, offset, rot, theta):
        half = rot // 2
        d = lane % period - offset
        first = (d >= 0) & (d < half)
        second = (d >= half) & (d < rot)
        fidx = np.where(first, d, np.where(second, d - half, 0))
        inv_freq = theta ** (-2.0 * jnp.arange(half, dtype=F32) / rot)
        ang = jnp.arange(seq, dtype=F32)[:, None] * inv_freq[None, :]
        cos_l = jnp.cos(ang)[:, fidx]
        sin_l = jnp.sin(ang)[:, fidx]
        cos = jnp.where((first | second)[None, :], cos_l, 1.0)
        s_up = jnp.where(first[None, :], -sin_l, 0.0)
        s_dn = jnp.where(second[None, :], sin_l, 0.0)
        return [cos, s_up, s_dn]

    tabs = (tables(DIFF_QK, 0, DIFF_QK // 4, ROPE_THETA)
            + tables(DIL_DIM, 0, DIL_DIM // 4, ROPE_THETA)
            + tables(LANES, KR_LANE, MLA_ROPE, MLA_THETA))
    return jnp.stack(tabs).astype(F32)


def _prep_layer(l, w_in, w_uq, w_ukv, w_out, w1, w3, w2):
    D = w_in.shape[1]
    wi = w_in[l]
    zeros = lambda n: jnp.zeros((D, n), wi.dtype)
    wi = jnp.concatenate([wi[:, :C_KR], zeros(KR_LANE), wi[:, C_KR:], zeros(LANES - KR_LANE - MLA_ROPE)], axis=1)
    uq = w_uq[l].reshape(MLA_Q_RANK, MLA_HEADS, MLA_NOPE + MLA_ROPE)
    uq = jnp.pad(uq, ((0, 0), (0, 0), (0, LANES - MLA_NOPE - MLA_ROPE))).reshape(MLA_Q_RANK, MLA_HEADS * LANES)
    ukv = w_ukv[l].reshape(MLA_KV_RANK, MLA_HEADS, MLA_NOPE + MLA_V)
    uk = jnp.pad(ukv[..., :MLA_NOPE], ((0, 0), (0, 0), (0, LANES - MLA_NOPE))).reshape(MLA_KV_RANK, MLA_HEADS * LANES)
    uv = ukv[..., MLA_NOPE:].reshape(MLA_KV_RANK, MLA_HEADS * MLA_V)
    c = lambda t: t.astype(MXU_DTYPE)
    return dict(w_in=c(wi), w_uq=c(uq), w_uk=c(uk), w_uv=c(uv), w_out=c(w_out[l]),
                w1=c(w1[l]), w3=c(w3[l]), w2=c(w2[l]))


def _tile(n, pref):
    return pref if n % pref == 0 else n


def _trunk(x, mods, tabs, layers, p):
    B, S, D = x.shape
    tm = _tile(S, 512)
    tq = _tile(S, ATTN_TQ)
    tk = _tile(S, ATTN_TK)
    x = _ln_in(x, p['ln_in_g'], p['ln_in_b'], tm)
    for l, lw in enumerate(layers):
        mod = mods[l]
        lam_init = 0.8 - 0.6 * math.exp(-0.3 * l)
        qa, ka, va, qb, kb, vb, mq, mk, mv = _proj(
            x, mod, tabs, lw['w_in'], p['g_cq'][l], p['g_ckv'][l], lw['w_uq'], lw['w_uk'], lw['w_uv'], tm)
        oa = _diff_attn(qa, ka, va, p['lam_pack'][l], p['g_sub'][l], lam_init, tq, tk)
        oc = _mla_attn(mq, mk, mv, tq, tk)
        ob = _dil_attn(qb, kb, vb)
        x1, h2, gate = _post(x, mod, oa, ob, oc, lw['w_out'], p['ln1_g'][l], p['ln1_b'][l],
                             p['w_r_hi'], p['w_r_lo'], p['b_r'], tm)
        x = _moe(x1, mod, h2, gate, lw['w1'], lw['w3'], lw['w2'], p['ln2_g'][l], p['ln2_b'][l],
                 _tile(S, MOE_TM))
    return x


def kernel(x_prompt, x_sample, c_prompt, c_sample, ln_in_g, ln_in_b, w_router, b_router, w_ada, b_ada, w_in, lam_q1, lam_k1, lam_q2, lam_k2, g_subln, g_cq, g_ckv, w_uq, w_ukv, w_out, ln1_g, ln1_b, w1, w3, w2, ln2_g, ln2_b):
    depth = w_in.shape[0]
    D = x_prompt.shape[-1]
    bp, bs = c_prompt.shape[0], c_sample.shape[0]
    rows = -(-(bp + bs) // 8) * 8
    c_all = jnp.concatenate([c_prompt, c_sample, jnp.zeros((rows - bp - bs, D), F32)], axis=0)
    mods = _ada(c_all, w_ada, b_ada)
    mods_p = mods[:, :bp].reshape(depth, bp, 6, D)
    mods_s = mods[:, bp:bp + bs].reshape(depth, bs, 6, D)

    pad_l = lambda t: jnp.pad(t, ((0, 0), (0, LANES - t.shape[-1])))
    p = dict(
        ln_in_g=ln_in_g, ln_in_b=ln_in_b, g_cq=g_cq, g_ckv=g_ckv,
        ln1_g=ln1_g, ln1_b=ln1_b, ln2_g=ln2_g, ln2_b=ln2_b,
        lam_pack=jnp.stack([pad_l(lam_q1), pad_l(lam_k1), pad_l(lam_q2), pad_l(lam_k2)], axis=1),
        g_sub=g_subln.reshape(depth, DIFF_V, 1),
        b_r=pad_l(b_router.reshape(1, -1)),
    )
    w_r = pad_l(w_router)
    p['w_r_hi'] = w_r.astype(MXU_DTYPE)
    p['w_r_lo'] = (w_r - p['w_r_hi'].astype(F32)).astype(MXU_DTYPE)
    layers = [_prep_layer(l, w_in, w_uq, w_ukv, w_out, w1, w3, w2) for l in range(depth)]
    tabs = _rope_tables(max(x_prompt.shape[1], x_sample.shape[1]))
    y_prompt = _trunk(x_prompt, mods_p, tabs, layers, p)
    y_sample = _trunk(x_sample, mods_s, tabs, layers, p)
    return (y_prompt, y_sample)
```

```python
import functools
import math

import numpy as np
import jax
import jax.numpy as jnp
from jax import lax
from jax.experimental import pallas as pl
from jax.experimental.pallas import tpu as pltpu

F32 = jnp.float32
MXU_DTYPE = jnp.bfloat16

LANES = 128
D_MODEL = 1024
DEPTH = 4
DIFF_HEADS = 4
DIFF_QK = 32
DIFF_V = 64
DIL_HEADS = 6
DIL_DIM = 64
DIL_PAIRS = ((128, 1), (512, 4), (2048, 16))
DIL_RADIUS = 64
MLA_HEADS = 6
MLA_NOPE = 64
MLA_ROPE = 32
MLA_V = 64
MLA_Q_RANK = 256
MLA_KV_RANK = 128
MLA_THETA = 10000.0
ROPE_THETA = 500000.0
N_EXPERTS = 16
GROUP_SIZE = 4
D_FF = 256
ALPHA = (2 * DEPTH) ** 0.25
NEG = -1e30
LOG2E = math.log2(math.e)
ATTN_TQ = 512
ATTN_TK = 2048

C_Q1, C_Q2, C_K1, C_K2, C_VA = 0, 128, 256, 384, 512
C_QB, C_KB, C_VB = 768, 1152, 1536
C_CQ, C_CKV, C_KR, C_END = 1920, 2176, 2304, 2432
KR_LANE = 64

VMEM_LIMIT = 48 * 1024 * 1024


def _cparams(sem):
    return pltpu.CompilerParams(dimension_semantics=sem, vmem_limit_bytes=VMEM_LIMIT)


def _lane_iota(shape):
    return lax.broadcasted_iota(jnp.int32, shape, len(shape) - 1)


def _layer_norm(z, g, b):
    mu = jnp.mean(z, axis=-1, keepdims=True)
    zc = z - mu
    var = jnp.mean(zc * zc, axis=-1, keepdims=True)
    return zc * lax.rsqrt(var + 1e-5) * g + b


def _rope(t, tab_ref, typ, half, rows):
    c = tab_ref[3 * typ, rows, :]
    s_up = tab_ref[3 * typ + 1, rows, :]
    s_dn = tab_ref[3 * typ + 2, rows, :]
    return t * c + pltpu.roll(t, LANES - half, 1) * s_up + pltpu.roll(t, half, 1) * s_dn


def _ln_in_kernel(x_ref, g_ref, b_ref, o_ref):
    o_ref[0] = _layer_norm(x_ref[0], g_ref[...], b_ref[...])


def _ln_in(x, g, b, tm):
    B, S, D = x.shape
    return pl.pallas_call(
        _ln_in_kernel,
        grid=(B, S // tm),
        in_specs=[pl.BlockSpec((1, tm, D), lambda b, i: (b, i, 0)),
                  pl.BlockSpec((1, D), lambda b, i: (0, 0)),
                  pl.BlockSpec((1, D), lambda b, i: (0, 0))],
        out_specs=pl.BlockSpec((1, tm, D), lambda b, i: (b, i, 0)),
        out_shape=jax.ShapeDtypeStruct(x.shape, F32),
        compiler_params=_cparams(("parallel", "parallel")),
        name="ln_in",
    )(x, g.reshape(1, D), b.reshape(1, D))


def _ada_kernel(c_ref, w_ref, b_ref, o_ref):
    c = c_ref[...]
    cs = (c * (1.0 / (1.0 + jnp.exp(-c)))).astype(MXU_DTYPE)
    o_ref[0] = jnp.dot(cs, w_ref[0].astype(MXU_DTYPE), preferred_element_type=F32) + b_ref[0]


def _ada(c, w_ada, b_ada, tn=1536):
    R, D = c.shape
    L, _, N = w_ada.shape
    return pl.pallas_call(
        _ada_kernel,
        grid=(L, N // tn),
        in_specs=[pl.BlockSpec((R, D), lambda l, j: (0, 0)),
                  pl.BlockSpec((1, D, tn), lambda l, j: (l, 0, j)),
                  pl.BlockSpec((1, 1, tn), lambda l, j: (l, 0, j))],
        out_specs=pl.BlockSpec((1, R, tn), lambda l, j: (l, 0, j)),
        out_shape=jax.ShapeDtypeStruct((L, R, N), F32),
        compiler_params=_cparams(("parallel", "parallel")),
        name="ada",
    )(c, w_ada, b_ada.reshape(L, 1, N))


PROJ_PARTS = 2


def _proj_kernel(x_ref, mod_ref, tab_ref, w_ref, gcq_ref, gckv_ref, wuq_ref, wuk_ref, wuv_ref,
                 qa_ref, ka_ref, va_ref, qb_ref, kb_ref, vb_ref, mq_ref, mk_ref, mv_ref):
    sh1 = mod_ref[0, 0:1, :]
    sc1 = mod_ref[0, 1:2, :]
    tm = x_ref.shape[1]
    parts = PROJ_PARTS if tm % (PROJ_PARTS * LANES) == 0 else 1

    def rms(t, g):
        return (t * lax.rsqrt(jnp.mean(t * t, axis=-1, keepdims=True) + 1e-6) * g).astype(MXU_DTYPE)

    for r in range(parts):
        rows = slice(r * (tm // parts), (r + 1) * (tm // parts))
        h = (x_ref[0, rows, :] * (1.0 + sc1) + sh1).astype(MXU_DTYPE)

        def mm(a, b, h=h):
            return jnp.dot(h, w_ref[:, a:b], preferred_element_type=F32)

        rope = functools.partial(_rope, tab_ref=tab_ref, rows=rows)

        qa = mm(C_Q1, C_K1)
        ka = mm(C_K1, C_VA)
        sa = DIFF_QK ** -0.5 * LOG2E
        for j in range(2):
            blk = slice(j * LANES, (j + 1) * LANES)
            qa_ref[0, rows, blk] = (rope(qa[:, blk], typ=0, half=DIFF_QK // 8) * sa).astype(qa_ref.dtype)
            ka_ref[0, rows, blk] = rope(ka[:, blk], typ=0, half=DIFF_QK // 8).astype(ka_ref.dtype)
        va_ref[0, :, rows] = mm(C_VA, C_QB).T.astype(va_ref.dtype)

        qb = mm(C_QB, C_KB)
        kb = mm(C_KB, C_VB)
        vb = mm(C_VB, C_CQ)
        sb = DIL_DIM ** -0.5 * LOG2E
        for j in range(DIL_HEADS // 2):
            blk = slice(j * LANES, (j + 1) * LANES)
            qb_ref[j, 0, rows, :] = rope(qb[:, blk], typ=1, half=DIL_DIM // 8) * sb
            kb_ref[j, 0, rows, :] = rope(kb[:, blk], typ=1, half=DIL_DIM // 8)
            vb_ref[j, 0, rows, :] = vb[:, blk]

        cq = rms(mm(C_CQ, C_CKV), gcq_ref[...])
        ckv = rms(mm(C_CKV, C_KR), gckv_ref[...])
        kr = rope(mm(C_KR, C_END), typ=2, half=MLA_ROPE // 2)
        q = jnp.dot(cq, wuq_ref[...], preferred_element_type=F32)
        kn = jnp.dot(ckv, wuk_ref[...], preferred_element_type=F32)
        sc = (MLA_NOPE + MLA_ROPE) ** -0.5 * LOG2E
        for j in range(MLA_HEADS):
            blk = slice(j * LANES, (j + 1) * LANES)
            mq_ref[0, rows, blk] = (rope(q[:, blk], typ=2, half=MLA_ROPE // 2) * sc).astype(mq_ref.dtype)
            mk_ref[0, rows, blk] = (kn[:, blk] + kr).astype(mk_ref.dtype)
        mv_ref[0, :, rows] = jnp.dot(ckv, wuv_ref[...], preferred_element_type=F32).T.astype(mv_ref.dtype)


def _proj(x, mod, tabs, w_in, g_cq, g_ckv, w_uq, w_uk, w_uv, tm):
    B, S, D = x.shape
    widths = (256, 256, 256, 384, 384, 384, 768, 768, 384)
    transposed = (2, 8)
    row = lambda b, i: (b, i, 0)
    col = lambda b, i: (b, 0, i)
    const2 = lambda b, i: (0, 0)
    out_specs = [pl.BlockSpec((1, w, tm), col) if n in transposed else pl.BlockSpec((1, tm, w), row)
                 for n, w in enumerate(widths)]
    out_shape = [jax.ShapeDtypeStruct((B, w, S) if n in transposed else (B, S, w), MXU_DTYPE)
                 for n, w in enumerate(widths)]
    for n in (3, 4, 5):
        out_specs[n] = pl.BlockSpec((DIL_HEADS // 2, 1, tm, LANES), lambda b, i: (0, b, i, 0))
        out_shape[n] = jax.ShapeDtypeStruct((DIL_HEADS // 2, B, S, LANES), F32)
    return pl.pallas_call(
        _proj_kernel,
        grid=(B, S // tm),
        in_specs=[pl.BlockSpec((1, tm, D), row),
                  pl.BlockSpec((1, 6, D), lambda b, i: (b, 0, 0)),
                  pl.BlockSpec((9, tm, LANES), lambda b, i: (0, i, 0)),
                  pl.BlockSpec(w_in.shape, const2),
                  pl.BlockSpec((1, MLA_Q_RANK), const2),
                  pl.BlockSpec((1, MLA_KV_RANK), const2),
                  pl.BlockSpec(w_uq.shape, const2),
                  pl.BlockSpec(w_uk.shape, const2),
                  pl.BlockSpec(w_uv.shape, const2)],
        out_specs=out_specs,
        out_shape=out_shape,
        compiler_params=_cparams(("parallel", "parallel")),
        name="proj_in",
    )(x, mod, tabs, w_in, g_cq.reshape(1, -1), g_ckv.reshape(1, -1), w_uq, w_uk, w_uv)


def _softmax_steps(chains, m_scr, l_scr, acc_scr):
    scores = [lax.dot_general(k, q, (((1,), (1,)), ((), ())), preferred_element_type=F32)
              for q, k, _ in chains]
    for u, (st, (_, _, vt)) in enumerate(zip(scores, chains)):
        m_prev = m_scr[u]
        m_new = jnp.maximum(m_prev, jnp.max(st, axis=0, keepdims=True))
        alpha = jnp.exp2(m_prev - m_new)
        p = jnp.exp2(st - m_new)
        l_scr[u] = alpha * l_scr[u] + jnp.sum(p, axis=0, keepdims=True)
        acc_scr[u] = alpha * acc_scr[u] + jnp.dot(vt, p.astype(MXU_DTYPE), preferred_element_type=F32)
        m_scr[u] = m_new


def _init_softmax(m_scr, l_scr, acc_scr):
    m_scr[...] = jnp.full(m_scr.shape, NEG, F32)
    l_scr[...] = jnp.zeros(l_scr.shape, F32)
    acc_scr[...] = jnp.zeros(acc_scr.shape, F32)


def _diff_attn_kernel(lam_init, q1_ref, q2_ref, k1_ref, k2_ref, vt_ref, lamp_ref, g_ref, o_ref,
                      q_scr, m_scr, l_scr, acc_scr):
    j = pl.program_id(1)
    ik = pl.program_id(3)

    @pl.when(ik == 0)
    def _():
        _init_softmax(m_scr, l_scr, acc_scr)
        lane = _lane_iota(q1_ref.shape[1:])
        for hh in range(2):
            h = 2 * j + hh
            msk = (lane >= h * DIFF_QK) & (lane < (h + 1) * DIFF_QK)
            q_scr[2 * hh] = jnp.where(msk, q1_ref[0].astype(F32), 0.0).astype(q_scr.dtype)
            q_scr[2 * hh + 1] = jnp.where(msk, q2_ref[0].astype(F32), 0.0).astype(q_scr.dtype)

    chains = []
    for hh in range(2):
        vt = vt_ref[0, hh * DIFF_V:(hh + 1) * DIFF_V, :]
        chains.append((q_scr[2 * hh], k1_ref[0], vt))
        chains.append((q_scr[2 * hh + 1], k2_ref[0], vt))
    _softmax_steps(chains, m_scr, l_scr, acc_scr)

    @pl.when(ik == pl.num_programs(3) - 1)
    def _():
        lp = lamp_ref[...]
        lam = (jnp.exp(jnp.sum(lp[0:1] * lp[1:2], axis=-1, keepdims=True))
               - jnp.exp(jnp.sum(lp[2:3] * lp[3:4], axis=-1, keepdims=True)) + lam_init)
        for hh in range(2):
            o = acc_scr[2 * hh] / l_scr[2 * hh] - lam * (acc_scr[2 * hh + 1] / l_scr[2 * hh + 1])
            ms = jnp.mean(o * o, axis=0, keepdims=True)
            o_ref[0, hh * DIFF_V:(hh + 1) * DIFF_V, :] = (
                o * lax.rsqrt(ms + 1e-6) * g_ref[...] * (1.0 - lam_init)).astype(o_ref.dtype)


def _diff_attn(qa, ka, vat, lam_pack, g_sub, lam_init, tq, tk):
    B, S, _ = qa.shape
    kern = functools.partial(_diff_attn_kernel, lam_init)
    return pl.pallas_call(
        kern,
        grid=(B, DIFF_HEADS // 2, S // tq, S // tk),
        in_specs=[pl.BlockSpec((1, tq, LANES), lambda b, j, iq, ik: (b, iq, 0)),
                  pl.BlockSpec((1, tq, LANES), lambda b, j, iq, ik: (b, iq, 1)),
                  pl.BlockSpec((1, tk, LANES), lambda b, j, iq, ik: (b, ik, 0)),
                  pl.BlockSpec((1, tk, LANES), lambda b, j, iq, ik: (b, ik, 1)),
                  pl.BlockSpec((1, 2 * DIFF_V, tk), lambda b, j, iq, ik: (b, j, ik)),
                  pl.BlockSpec((4, LANES), lambda b, j, iq, ik: (0, 0)),
                  pl.BlockSpec((DIFF_V, 1), lambda b, j, iq, ik: (0, 0))],
        out_specs=pl.BlockSpec((1, 2 * DIFF_V, tq), lambda b, j, iq, ik: (b, j, iq)),
        out_shape=jax.ShapeDtypeStruct((B, DIFF_HEADS * DIFF_V, S), MXU_DTYPE),
        scratch_shapes=[pltpu.VMEM((4, tq, LANES), MXU_DTYPE),
                        pltpu.VMEM((4, 1, tq), F32),
                        pltpu.VMEM((4, 1, tq), F32),
                        pltpu.VMEM((4, DIFF_V, tq), F32)],
        compiler_params=_cparams(("parallel", "parallel", "parallel", "arbitrary")),
        name="diff_attn",
    )(qa, qa, ka, ka, vat, lam_pack, g_sub)


MLA_GROUP = 3


def _mla_attn_kernel(q_ref, k_ref, vt_ref, o_ref, m_scr, l_scr, acc_scr):
    ik = pl.program_id(3)

    @pl.when(ik == 0)
    def _():
        _init_softmax(m_scr, l_scr, acc_scr)

    chains = []
    for hh in range(MLA_GROUP):
        blk = slice(hh * LANES, (hh + 1) * LANES)
        chains.append((q_ref[0, :, blk], k_ref[0, :, blk], vt_ref[0, hh * MLA_V:(hh + 1) * MLA_V, :]))
    _softmax_steps(chains, m_scr, l_scr, acc_scr)

    @pl.when(ik == pl.num_programs(3) - 1)
    def _():
        for hh in range(MLA_GROUP):
            o_ref[0, hh * MLA_V:(hh + 1) * MLA_V, :] = (acc_scr[hh] / l_scr[hh]).astype(o_ref.dtype)


def _mla_attn(mq, mk, mvt, tq, tk):
    B, S, _ = mq.shape
    g = MLA_GROUP
    return pl.pallas_call(
        _mla_attn_kernel,
        grid=(B, MLA_HEADS // g, S // tq, S // tk),
        in_specs=[pl.BlockSpec((1, tq, g * LANES), lambda b, j, iq, ik: (b, iq, j)),
                  pl.BlockSpec((1, tk, g * LANES), lambda b, j, iq, ik: (b, ik, j)),
                  pl.BlockSpec((1, g * MLA_V, tk), lambda b, j, iq, ik: (b, j, ik))],
        out_specs=pl.BlockSpec((1, g * MLA_V, tq), lambda b, j, iq, ik: (b, j, iq)),
        out_shape=jax.ShapeDtypeStruct((B, MLA_HEADS * MLA_V, S), MXU_DTYPE),
        scratch_shapes=[pltpu.VMEM((g, 1, tq), F32),
                        pltpu.VMEM((g, 1, tq), F32),
                        pltpu.VMEM((g, MLA_V, tq), F32)],
        compiler_params=_cparams(("parallel", "parallel", "parallel", "arbitrary")),
        name="mla_attn",
    )(mq, mk, mvt)


DIL_SUB = 128
DIL_WIDTH = DIL_HEADS * DIL_DIM


DIL_DILS = tuple(d for _, d in DIL_PAIRS)
DIL_HALO = DIL_RADIUS * DIL_DILS[-1]
DIL_TILE = 2048


def _dil_kernel(seq, q_ref, kp_ref, kc_ref, kn_ref, vp_ref, vc_ref, vn_ref, o_ref,
                kbuf, vbuf, pat_o, pat_l, qg, kg, vg):
    i = pl.program_id(2)
    tt = q_ref.shape[2]
    H = DIL_HALO
    R = DIL_RADIUS
    kbuf[0:H] = kp_ref[0, 0]
    kbuf[H:H + tt] = kc_ref[0, 0]
    kbuf[H + tt:] = kn_ref[0, 0]
    vbuf[0:H] = vp_ref[0, 0]
    vbuf[H:H + tt] = vc_ref[0, 0]
    vbuf[H + tt:] = vn_ref[0, 0]
    sub = DIL_SUB
    win = sub + 2 * R
    nc = tt // sub
    cidx = lax.broadcasted_iota(jnp.int32, (nc, win, sub), 0)
    row_k = lax.broadcasted_iota(jnp.int32, (nc, win, sub), 1)
    col_q = lax.broadcasted_iota(jnp.int32, (nc, win, sub), 2)
    rel = row_k - col_q
    band = (rel >= 0) & (rel <= 2 * R)
    lane = _lane_iota((sub, LANES))
    for p, d in enumerate(DIL_DILS):
        n_sub = tt // d // sub
        length = seq // d
        q_starts = [c // n_sub + d * sub * (c % n_sub) for c in range(nc)]
        for c in range(nc):
            k_start = c // n_sub + d * (H // d + (c % n_sub) * sub - R)
            qf = q_ref[0, 0, pl.ds(q_starts[c], sub, stride=d), :]
            qg[c, :sub] = jnp.where(lane < DIL_DIM, qf, 0.0).astype(qg.dtype)
            qg[c, sub:] = jnp.where(lane >= DIL_DIM, qf, 0.0).astype(qg.dtype)
            kg[c] = kbuf[pl.ds(k_start, win, stride=d), :].astype(kg.dtype)
            vg[c] = vbuf[pl.ds(k_start, win, stride=d), :].T.astype(vg.dtype)
        u_of_c = cidx % n_sub if n_sub > 1 else jnp.zeros_like(cidx)
        kpos = i * (tt // d) + u_of_c * sub - R + row_k
        valid = band & (kpos >= 0) & (kpos < length)
        s = lax.dot_general(kg[...], qg[...], (((2,), (2,)), ((0,), (0,))),
                            preferred_element_type=F32)
        s = jnp.where(jnp.concatenate([valid, valid], axis=2), s, NEG)
        m = jnp.max(s, axis=1, keepdims=True)
        e = jnp.exp2(s - m)
        l = jnp.sum(e, axis=1, keepdims=True)
        ot = lax.dot_general(vg[...], e.astype(MXU_DTYPE), (((2,), (1,)), ((0,), (0,))),
                             preferred_element_type=F32) / l
        lse = m + jnp.log2(l)
        for c in range(nc):
            o_t = jnp.concatenate([ot[c, :DIL_DIM, :sub], ot[c, DIL_DIM:, sub:]], axis=0)
            l_t = jnp.concatenate([jnp.broadcast_to(lse[c, :, :sub], (DIL_DIM, sub)),
                                   jnp.broadcast_to(lse[c, :, sub:], (DIL_DIM, sub))], axis=0)
            pat_o[p, pl.ds(q_starts[c], sub, stride=d), :] = o_t.T
            pat_l[p, pl.ds(q_starts[c], sub, stride=d), :] = l_t.T

    l0, l1, l2 = pat_l[0], pat_l[1], pat_l[2]
    mx = jnp.maximum(jnp.maximum(l0, l1), l2)
    e0, e1, e2 = jnp.exp2(l0 - mx), jnp.exp2(l1 - mx), jnp.exp2(l2 - mx)
    den = e0 + e1 + e2
    o_ref[0] = ((e0 / den) * pat_o[0] + (e1 / den) * pat_o[1] + (e2 / den) * pat_o[2]).astype(o_ref.dtype)


def _dil_attn(qb, kb, vb):
    P, B, S, _ = qb.shape
    tt = _tile(S, DIL_TILE)
    H = DIL_HALO
    nh = S // H
    cur = pl.BlockSpec((1, 1, tt, LANES), lambda b, hp, i: (hp, b, i, 0))
    prv = pl.BlockSpec((1, 1, H, LANES), lambda b, hp, i: (hp, b, jnp.maximum(i * (tt // H) - 1, 0), 0))
    nxt = pl.BlockSpec((1, 1, H, LANES), lambda b, hp, i: (hp, b, jnp.minimum((i + 1) * (tt // H), nh - 1), 0))
    return pl.pallas_call(
        functools.partial(_dil_kernel, S),
        grid=(B, P, S // tt),
        in_specs=[cur, prv, cur, nxt, prv, cur, nxt],
        out_specs=pl.BlockSpec((1, tt, LANES), lambda b, hp, i: (b, i, hp)),
        out_shape=jax.ShapeDtypeStruct((B, S, P * LANES), MXU_DTYPE),
        scratch_shapes=[pltpu.VMEM((tt + 2 * H, LANES), F32),
                        pltpu.VMEM((tt + 2 * H, LANES), F32),
                        pltpu.VMEM((len(DIL_DILS), tt, LANES), F32),
                        pltpu.VMEM((len(DIL_DILS), tt, LANES), F32),
                        pltpu.VMEM((tt // DIL_SUB, 2 * DIL_SUB, LANES), MXU_DTYPE),
                        pltpu.VMEM((tt // DIL_SUB, DIL_SUB + 2 * DIL_RADIUS, LANES), MXU_DTYPE),
                        pltpu.VMEM((tt // DIL_SUB, LANES, DIL_SUB + 2 * DIL_RADIUS), MXU_DTYPE)],
        compiler_params=_cparams(("parallel", "parallel", "parallel")),
        name="dil_attn",
    )(qb, kb, kb, kb, vb, vb, vb)


def _partner(x, k, lane):
    up = pltpu.roll(x, LANES - k, 1)
    dn = pltpu.roll(x, k, 1)
    return jnp.where((lane & k) == 0, up, dn)


GROUP_LANE = N_EXPERTS


def _route(logits, bias, lane):
    valid = lane < N_EXPERTS
    lane_f = lane.astype(F32)
    score = 1.0 / (1.0 + jnp.exp(-logits))
    biased = jnp.where(valid, score + bias, NEG)
    p1 = _partner(biased, 1, lane)
    hi = jnp.maximum(biased, p1)
    lo = jnp.minimum(biased, p1)
    hi_p = _partner(hi, 2, lane)
    lo_p = _partner(lo, 2, lane)
    top1 = jnp.maximum(hi, hi_p)
    top2 = jnp.maximum(jnp.minimum(hi, hi_p), jnp.maximum(lo, lo_p))
    gscore = jnp.where(valid, top1 + top2, NEG)
    gidx = (lane // GROUP_SIZE).astype(F32)
    big = float(LANES)
    gmax = jnp.max(gscore, axis=-1, keepdims=True)
    gsel = jnp.min(jnp.where(gscore == gmax, gidx, big), axis=-1, keepdims=True)
    cand = jnp.where((gidx == gsel) & valid, biased, NEG)
    m1 = jnp.max(cand, axis=-1, keepdims=True)
    i1 = jnp.min(jnp.where(cand == m1, lane_f, big), axis=-1, keepdims=True)
    oh1 = lane_f == i1
    cand2 = jnp.where(oh1, NEG, cand)
    m2 = jnp.max(cand2, axis=-1, keepdims=True)
    i2 = jnp.min(jnp.where((cand2 == m2) & jnp.logical_not(oh1), lane_f, big), axis=-1, keepdims=True)
    oh2 = lane_f == i2
    s1 = jnp.sum(jnp.where(oh1, score, 0.0), axis=-1, keepdims=True)
    s2 = jnp.sum(jnp.where(oh2, score, 0.0), axis=-1, keepdims=True)
    den = s1 + s2
    gate = jnp.where(oh1, s1 / den, 0.0) + jnp.where(oh2, s2 / den, 0.0)
    return jnp.where(lane == GROUP_LANE, gsel, gate)


POST_PARTS = 2


def _post_kernel(x_ref, mod_ref, oa_ref, ob_ref, oc_ref, wo_ref, g_ref, b_ref, wrh_ref, wrl_ref, br_ref,
                 x1_ref, h2_ref, gate_ref):
    na = DIFF_HEADS * DIFF_V
    nb = na + DIL_WIDTH
    tn = (((0,), (0,)), ((), ()))
    g1 = mod_ref[0, 2:3, :]
    sh2 = mod_ref[0, 3:4, :]
    sc2 = mod_ref[0, 4:5, :]
    tm = x_ref.shape[1]
    parts = POST_PARTS if tm % (POST_PARTS * LANES) == 0 else 1
    for r in range(parts):
        rows = slice(r * (tm // parts), (r + 1) * (tm // parts))
        y = (lax.dot_general(oa_ref[0, :, rows], wo_ref[0:na], tn, preferred_element_type=F32)
             + jnp.dot(ob_ref[0, rows, :], wo_ref[na:nb], preferred_element_type=F32)
             + lax.dot_general(oc_ref[0, :, rows], wo_ref[nb:], tn, preferred_element_type=F32))
        x1 = _layer_norm(ALPHA * x_ref[0, rows, :] + g1 * y, g_ref[...], b_ref[...])
        x1_ref[0, rows, :] = x1
        h2 = x1 * (1.0 + sc2) + sh2
        h2_ref[0, rows, :] = h2.astype(h2_ref.dtype)
        h_hi = h2.astype(MXU_DTYPE)
        h_lo = (h2 - h_hi.astype(F32)).astype(MXU_DTYPE)
        logits = (jnp.dot(h_hi, wrh_ref[...], preferred_element_type=F32)
                  + jnp.dot(h_lo, wrh_ref[...], preferred_element_type=F32)
                  + jnp.dot(h_hi, wrl_ref[...], preferred_element_type=F32))
        gate_ref[0, rows, :] = _route(logits, br_ref[...], _lane_iota(logits.shape))


def _post(x, mod, oa, ob, oc, w_out, g, b, w_r_hi, w_r_lo, b_r, tm):
    B, S, D = x.shape
    row = lambda b_, i: (b_, i, 0)
    const2 = lambda b_, i: (0, 0)
    return pl.pallas_call(
        _post_kernel,
        grid=(B, S // tm),
        in_specs=[pl.BlockSpec((1, tm, D), row),
                  pl.BlockSpec((1, 6, D), lambda b_, i: (b_, 0, 0)),
                  pl.BlockSpec((1, oa.shape[1], tm), lambda b_, i: (b_, 0, i)),
                  pl.BlockSpec((1, tm, DIL_WIDTH), row),
                  pl.BlockSpec((1, oc.shape[1], tm), lambda b_, i: (b_, 0, i)),
                  pl.BlockSpec(w_out.shape, const2),
                  pl.BlockSpec((1, D), const2),
                  pl.BlockSpec((1, D), const2),
                  pl.BlockSpec(w_r_hi.shape, const2),
                  pl.BlockSpec(w_r_lo.shape, const2),
                  pl.BlockSpec((1, LANES), const2)],
        out_specs=[pl.BlockSpec((1, tm, D), row),
                   pl.BlockSpec((1, tm, D), row),
                   pl.BlockSpec((1, tm, LANES), row)],
        out_shape=[jax.ShapeDtypeStruct((B, S, D), F32),
                   jax.ShapeDtypeStruct((B, S, D), MXU_DTYPE),
                   jax.ShapeDtypeStruct((B, S, LANES), F32)],
        compiler_params=_cparams(("parallel", "parallel")),
        name="post_mix",
    )(x, mod, oa, ob, oc, w_out, g.reshape(1, D), b.reshape(1, D), w_r_hi, w_r_lo, b_r)


MOE_TM = 1024
MOE_CHUNK = 256
MOE_TAIL = 128
N_GROUPS = N_EXPERTS // GROUP_SIZE


def _moe_kernel(x1_ref, mod_ref, h2_ref, gate_ref, w1_ref, w3_ref, w2_ref, g_ref, b_ref, o_ref,
                acc_scr, src_scr, sel_scr, rank_scr):
    g = pl.program_id(2)
    tm = h2_ref.shape[1]
    d = h2_ref.shape[2]
    C = MOE_CHUNK

    @pl.when(g == 0)
    def _():
        acc_scr[...] = jnp.zeros(acc_scr.shape, F32)
        gate = gate_ref[0]
        gate_hi = gate.astype(MXU_DTYPE)
        src_scr[:, 0:d] = h2_ref[0]
        src_scr[:, d:d + LANES] = gate_hi
        src_scr[:, d + LANES:] = (gate - gate_hi.astype(F32)).astype(MXU_DTYPE)
        gsel_t = gate.T[GROUP_LANE:GROUP_LANE + 1, :]
        grp = lax.broadcasted_iota(jnp.int32, sel_scr.shape, 0).astype(F32)
        sel = jnp.where(gsel_t == grp, 1.0, 0.0)
        r = lax.broadcasted_iota(jnp.int32, (tm, tm), 0)
        c = lax.broadcasted_iota(jnp.int32, (tm, tm), 1)
        before = jnp.where(r < c, 1.0, 0.0).astype(MXU_DTYPE)
        sel_scr[...] = sel
        rank_scr[...] = jnp.dot(sel.astype(MXU_DTYPE), before, preferred_element_type=F32)

    sel = sel_scr[pl.ds(g, 1), :]
    rank = rank_scr[pl.ds(g, 1), :]
    n_g = jnp.sum(sel).astype(jnp.int32)

    def chunk(base, rows):
        lane = _lane_iota((rows, LANES))
        row = lax.broadcasted_iota(jnp.int32, (rows, tm), 0).astype(F32)
        onehot = jnp.where((rank - base.astype(F32) == row) & (sel > 0.5), 1.0, 0.0).astype(MXU_DTYPE)
        picked = jnp.dot(onehot, src_scr[...], preferred_element_type=F32)
        xs = picked[:, 0:d].astype(MXU_DTYPE)
        gs = picked[:, d:d + LANES] + picked[:, d + LANES:]
        ys = jnp.zeros((rows, d), F32)
        for e in range(GROUP_SIZE):
            ge = jnp.sum(jnp.where(lane == g * GROUP_SIZE + e, gs, 0.0), axis=-1, keepdims=True)
            a = jnp.dot(xs, w1_ref[e], preferred_element_type=F32)
            bb = jnp.dot(xs, w3_ref[e], preferred_element_type=F32)
            hid = (a * (1.0 / (1.0 + jnp.exp(-a)))) * bb * ge
            ys = ys + jnp.dot(hid.astype(MXU_DTYPE), w2_ref[e], preferred_element_type=F32)
        acc_scr[...] += lax.dot_general(onehot, ys.astype(MXU_DTYPE), (((0,), (0,)), ((), ())),
                                        preferred_element_type=F32)

    n_chunks = (n_g + C - 1) // C
    tail = n_g - (n_chunks - 1) * C
    merge = (n_chunks >= 2) & (tail <= MOE_TAIL)
    n_plain = jnp.where(merge, n_chunks - 2, n_chunks)

    def plain(ci, carry):
        chunk(ci * C, C)
        return carry

    lax.fori_loop(0, n_plain, plain, 0)

    @pl.when(merge)
    def _():
        chunk(n_plain * C, C + MOE_TAIL)

    @pl.when(g == pl.num_programs(2) - 1)
    def _():
        g2 = mod_ref[0, 5:6, :]
        o_ref[0] = _layer_norm(ALPHA * x1_ref[0] + g2 * acc_scr[...], g_ref[...], b_ref[...])


def _moe(x1, mod, h2, gate, w1, w3, w2, g, b, tm):
    B, S, D = x1.shape
    gs = GROUP_SIZE
    row = lambda b_, i, c: (b_, i, 0)
    const2 = lambda b_, i, c: (0, 0)
    return pl.pallas_call(
        _moe_kernel,
        grid=(B, S // tm, N_GROUPS),
        in_specs=[pl.BlockSpec((1, tm, D), row),
                  pl.BlockSpec((1, 6, D), lambda b_, i, c: (b_, 0, 0)),
                  pl.BlockSpec((1, tm, D), row),
                  pl.BlockSpec((1, tm, LANES), row),
                  pl.BlockSpec((gs, D, D_FF), lambda b_, i, c: (c, 0, 0)),
                  pl.BlockSpec((gs, D, D_FF), lambda b_, i, c: (c, 0, 0)),
                  pl.BlockSpec((gs, D_FF, D), lambda b_, i, c: (c, 0, 0)),
                  pl.BlockSpec((1, D), const2),
                  pl.BlockSpec((1, D), const2)],
        out_specs=pl.BlockSpec((1, tm, D), row),
        out_shape=jax.ShapeDtypeStruct((B, S, D), F32),
        scratch_shapes=[pltpu.VMEM((tm, D), F32),
                        pltpu.VMEM((tm, D + 2 * LANES), MXU_DTYPE),
                        pltpu.VMEM((8, tm), F32),
                        pltpu.VMEM((8, tm), F32)],
        compiler_params=_cparams(("parallel", "parallel", "arbitrary")),
        name="moe_ffn",
    )(x1, mod, h2, gate, w1, w3, w2, g.reshape(1, D), b.reshape(1, D))


def _rope_tables(seq):
    lane = np.arange(LANES)

    def tables(period, offset, rot, theta):
        half = rot // 2
        d = lane % period - offset
        first = (d >= 0) & (d < half)
        second = (d >= half) & (d < rot)
        fidx = np.where(first, d, np.where(second, d - half, 0))
        inv_freq = theta ** (-2.0 * jnp.arange(half, dtype=F32) / rot)
        ang = jnp.arange(seq, dtype=F32)[:, None] * inv_freq[None, :]
        cos_l = jnp.cos(ang)[:, fidx]
        sin_l = jnp.sin(ang)[:, fidx]
        cos = jnp.where((first | second)[None, :], cos_l, 1.0)
        s_up = jnp.where(first[None, :], -sin_l, 0.0)
        s_dn = jnp.where(second[None, :], sin_l, 0.0)
        return [cos, s_up, s_dn]

    tabs = (tables(DIFF_QK, 0, DIFF_QK // 4, ROPE_THETA)
            + tables(DIL_DIM, 0, DIL_DIM // 4, ROPE_THETA)
            + tables(LANES, KR_LANE, MLA_ROPE, MLA_THETA))
    return jnp.stack(tabs).astype(F32)


def _prep_layer(l, w_in, w_uq, w_ukv, w_out, w1, w3, w2):
    D = w_in.shape[1]
    wi = w_in[l]
    zeros = lambda n: jnp.zeros((D, n), wi.dtype)
    wi = jnp.concatenate([wi[:, :C_KR], zeros(KR_LANE), wi[:, C_KR:], zeros(LANES - KR_LANE - MLA_ROPE)], axis=1)
    uq = w_uq[l].reshape(MLA_Q_RANK, MLA_HEADS, MLA_NOPE + MLA_ROPE)
    uq = jnp.pad(uq, ((0, 0), (0, 0), (0, LANES - MLA_NOPE - MLA_ROPE))).reshape(MLA_Q_RANK, MLA_HEADS * LANES)
    ukv = w_ukv[l].reshape(MLA_KV_RANK, MLA_HEADS, MLA_NOPE + MLA_V)
    uk = jnp.pad(ukv[..., :MLA_NOPE], ((0, 0), (0, 0), (0, LANES - MLA_NOPE))).reshape(MLA_KV_RANK, MLA_HEADS * LANES)
    uv = ukv[..., MLA_NOPE:].reshape(MLA_KV_RANK, MLA_HEADS * MLA_V)
    c = lambda t: t.astype(MXU_DTYPE)
    return dict(w_in=c(wi), w_uq=c(uq), w_uk=c(uk), w_uv=c(uv), w_out=c(w_out[l]),
                w1=c(w1[l]), w3=c(w3[l]), w2=c(w2[l]))


def _tile(n, pref):
    return pref if n % pref == 0 else n


def _trunk(x, mods, tabs, layers, p):
    B, S, D = x.shape
    tm = _tile(S, 512)
    tq = _tile(S, ATTN_TQ)
    tk = _tile(S, ATTN_TK)
    x = _ln_in(x, p['ln_in_g'], p['ln_in_b'], tm)
    for l, lw in enumerate(layers):
        mod = mods[l]
        lam_init = 0.8 - 0.6 * math.exp(-0.3 * l)
        qa, ka, va, qb, kb, vb, mq, mk, mv = _proj(
            x, mod, tabs, lw['w_in'], p['g_cq'][l], p['g_ckv'][l], lw['w_uq'], lw['w_uk'], lw['w_uv'], tm)
        oa = _diff_attn(qa, ka, va, p['lam_pack'][l], p['g_sub'][l], lam_init, tq, tk)
        oc = _mla_attn(mq, mk, mv, tq, tk)
        ob = _dil_attn(qb, kb, vb)
        x1, h2, gate = _post(x, mod, oa, ob, oc, lw['w_out'], p['ln1_g'][l], p['ln1_b'][l],
                             p['w_r_hi'], p['w_r_lo'], p['b_r'], tm)
        x = _moe(x1, mod, h2, gate, lw['w1'], lw['w3'], lw['w2'], p['ln2_g'][l], p['ln2_b'][l],
                 _tile(S, MOE_TM))
    return x


def kernel(x_prompt, x_sample, c_prompt, c_sample, ln_in_g, ln_in_b, w_router, b_router, w_ada, b_ada, w_in, lam_q1, lam_k1, lam_q2, lam_k2, g_subln, g_cq, g_ckv, w_uq, w_ukv, w_out, ln1_g, ln1_b, w1, w3, w2, ln2_g, ln2_b):
    depth = w_in.shape[0]
    D = x_prompt.shape[-1]
    bp, bs = c_prompt.shape[0], c_sample.shape[0]
    rows = -(-(bp + bs) // 8) * 8
    c_all = jnp.concatenate([c_prompt, c_sample, jnp.zeros((rows - bp - bs, D), F32)], axis=0)
    mods = _ada(c_all, w_ada, b_ada)
    mods_p = mods[:, :bp].reshape(depth, bp, 6, D)
    mods_s = mods[:, bp:bp + bs].reshape(depth, bs, 6, D)

    pad_l = lambda t: jnp.pad(t, ((0, 0), (0, LANES - t.shape[-1])))
    p = dict(
        ln_in_g=ln_in_g, ln_in_b=ln_in_b, g_cq=g_cq, g_ckv=g_ckv,
        ln1_g=ln1_g, ln1_b=ln1_b, ln2_g=ln2_g, ln2_b=ln2_b,
        lam_pack=jnp.stack([pad_l(lam_q1), pad_l(lam_k1), pad_l(lam_q2), pad_l(lam_k2)], axis=1),
        g_sub=g_subln.reshape(depth, DIFF_V, 1),
        b_r=pad_l(b_router.reshape(1, -1)),
    )
    w_r = pad_l(w_router)
    p['w_r_hi'] = w_r.astype(MXU_DTYPE)
    p['w_r_lo'] = (w_r - p['w_r_hi'].astype(F32)).astype(MXU_DTYPE)
    layers = [_prep_layer(l, w_in, w_uq, w_ukv, w_out, w1, w3, w2) for l in range(depth)]
    tabs = _rope_tables(max(x_prompt.shape[1], x_sample.shape[1]))
    y_prompt = _trunk(x_prompt, mods_p, tabs, layers, p)
    y_sample = _trunk(x_sample, mods_s, tabs, layers, p)
    return (y_prompt, y_sample)
```

```python
import functools
import math

import numpy as np
import jax
import jax.numpy as jnp
from jax import lax
from jax.experimental import pallas as pl
from jax.experimental.pallas import tpu as pltpu

F32 = jnp.float32
MXU_DTYPE = jnp.bfloat16

LANES = 128
DEPTH = 4
DIFF_HEADS = 4
DIFF_QK = 32
DIFF_V = 64
DIL_HEADS = 6
DIL_DIM = 64
DIL_PAIRS = ((128, 1), (512, 4), (2048, 16))
DIL_RADIUS = 64
MLA_HEADS = 6
MLA_NOPE = 64
MLA_ROPE = 32
MLA_V = 64
MLA_Q_RANK = 256
MLA_KV_RANK = 128
MLA_THETA = 10000.0
ROPE_THETA = 500000.0
N_EXPERTS = 16
GROUP_SIZE = 4
D_FF = 256
ALPHA = (2 * DEPTH) ** 0.25
NEG = -1e30
LOG2E = math.log2(math.e)
ATTN_TQ = 512
ATTN_TK = 2048

C_Q1, C_Q2, C_K1, C_K2, C_VA = 0, 128, 256, 384, 512
C_QB, C_KB, C_VB = 768, 1152, 1536
C_CQ, C_CKV, C_KR, C_END = 1920, 2176, 2304, 2432
KR_LANE = 64

VMEM_LIMIT = 48 * 1024 * 1024


def _cparams(sem):
    return pltpu.CompilerParams(dimension_semantics=sem, vmem_limit_bytes=VMEM_LIMIT)


def _lane_iota(shape):
    return lax.broadcasted_iota(jnp.int32, shape, len(shape) - 1)


def _layer_norm(z, g, b):
    mu = jnp.mean(z, axis=-1, keepdims=True)
    zc = z - mu
    var = jnp.mean(zc * zc, axis=-1, keepdims=True)
    return zc * lax.rsqrt(var + 1e-5) * g + b


def _rope(t, tab_ref, typ, half, rows):
    c = tab_ref[3 * typ, rows, :]
    s_up = tab_ref[3 * typ + 1, rows, :]
    s_dn = tab_ref[3 * typ + 2, rows, :]
    return t * c + pltpu.roll(t, LANES - half, 1) * s_up + pltpu.roll(t, half, 1) * s_dn


def _ln_in_kernel(x_ref, g_ref, b_ref, o_ref):
    o_ref[0] = _layer_norm(x_ref[0], g_ref[...], b_ref[...])


def _ln_in(x, g, b, tm):
    B, S, D = x.shape
    return pl.pallas_call(
        _ln_in_kernel,
        grid=(B, S // tm),
        in_specs=[pl.BlockSpec((1, tm, D), lambda b, i: (b, i, 0)),
                  pl.BlockSpec((1, D), lambda b, i: (0, 0)),
                  pl.BlockSpec((1, D), lambda b, i: (0, 0))],
        out_specs=pl.BlockSpec((1, tm, D), lambda b, i: (b, i, 0)),
        out_shape=jax.ShapeDtypeStruct(x.shape, F32),
        compiler_params=_cparams(("parallel", "parallel")),
        name="ln_in",
    )(x, g.reshape(1, D), b.reshape(1, D))


def _ada_kernel(c_ref, w_ref, b_ref, o_ref):
    c = c_ref[...]
    cs = (c * (1.0 / (1.0 + jnp.exp(-c)))).astype(MXU_DTYPE)
    o_ref[0] = jnp.dot(cs, w_ref[0].astype(MXU_DTYPE), preferred_element_type=F32) + b_ref[0]


def _ada(c, w_ada, b_ada, tn=1536):
    R, D = c.shape
    L, _, N = w_ada.shape
    return pl.pallas_call(
        _ada_kernel,
        grid=(L, N // tn),
        in_specs=[pl.BlockSpec((R, D), lambda l, j: (0, 0)),
                  pl.BlockSpec((1, D, tn), lambda l, j: (l, 0, j)),
                  pl.BlockSpec((1, 1, tn), lambda l, j: (l, 0, j))],
        out_specs=pl.BlockSpec((1, R, tn), lambda l, j: (l, 0, j)),
        out_shape=jax.ShapeDtypeStruct((L, R, N), F32),
        compiler_params=_cparams(("parallel", "parallel")),
        name="ada",
    )(c, w_ada, b_ada.reshape(L, 1, N))


PROJ_PARTS = 2


def _proj_kernel(x_ref, mod_ref, tab_ref, w_ref, gcq_ref, gckv_ref, wuq_ref, wuk_ref, wuv_ref,
                 qa_ref, ka_ref, va_ref, qb_ref, kb_ref, vb_ref, mq_ref, mk_ref, mv_ref):
    sh1 = mod_ref[0, 0:1, :]
    sc1 = mod_ref[0, 1:2, :]
    tm = x_ref.shape[1]
    parts = PROJ_PARTS if tm % (PROJ_PARTS * LANES) == 0 else 1

    def rms(t, g):
        return (t * lax.rsqrt(jnp.mean(t * t, axis=-1, keepdims=True) + 1e-6) * g).astype(MXU_DTYPE)

    for r in range(parts):
        rows = slice(r * (tm // parts), (r + 1) * (tm // parts))
        h = (x_ref[0, rows, :] * (1.0 + sc1) + sh1).astype(MXU_DTYPE)

        def mm(a, b, h=h):
            return jnp.dot(h, w_ref[:, a:b], preferred_element_type=F32)

        rope = functools.partial(_rope, tab_ref=tab_ref, rows=rows)

        qa = mm(C_Q1, C_K1)
        ka = mm(C_K1, C_VA)
        sa = DIFF_QK ** -0.5 * LOG2E
        for j in range(2):
            blk = slice(j * LANES, (j + 1) * LANES)
            qa_ref[0, rows, blk] = (rope(qa[:, blk], typ=0, half=DIFF_QK // 8) * sa).astype(qa_ref.dtype)
            ka_ref[0, rows, blk] = rope(ka[:, blk], typ=0, half=DIFF_QK // 8).astype(ka_ref.dtype)
        va_ref[0, :, rows] = mm(C_VA, C_QB).T.astype(va_ref.dtype)

        qb = mm(C_QB, C_KB)
        kb = mm(C_KB, C_VB)
        vb = mm(C_VB, C_CQ)
        sb = DIL_DIM ** -0.5 * LOG2E
        for j in range(DIL_HEADS // 2):
            blk = slice(j * LANES, (j + 1) * LANES)
            qb_ref[j, 0, rows, :] = rope(qb[:, blk], typ=1, half=DIL_DIM // 8) * sb
            kb_ref[j, 0, rows, :] = rope(kb[:, blk], typ=1, half=DIL_DIM // 8)
            vb_ref[j, 0, rows, :] = vb[:, blk]

        cq = rms(mm(C_CQ, C_CKV), gcq_ref[...])
        ckv = rms(mm(C_CKV, C_KR), gckv_ref[...])
        kr = rope(mm(C_KR, C_END), typ=2, half=MLA_ROPE // 2)
        q = jnp.dot(cq, wuq_ref[...], preferred_element_type=F32)
        kn = jnp.dot(ckv, wuk_ref[...], preferred_element_type=F32)
        sc = (MLA_NOPE + MLA_ROPE) ** -0.5 * LOG2E
        for j in range(MLA_HEADS):
            blk = slice(j * LANES, (j + 1) * LANES)
            mq_ref[0, rows, blk] = (rope(q[:, blk], typ=2, half=MLA_ROPE // 2) * sc).astype(mq_ref.dtype)
            mk_ref[0, rows, blk] = (kn[:, blk] + kr).astype(mk_ref.dtype)
        mv_ref[0, :, rows] = jnp.dot(ckv, wuv_ref[...], preferred_element_type=F32).T.astype(mv_ref.dtype)


def _proj(x, mod, tabs, w_in, g_cq, g_ckv, w_uq, w_uk, w_uv, tm):
    B, S, D = x.shape
    widths = (256, 256, 256, 384, 384, 384, 768, 768, 384)
    transposed = (2, 8)
    row = lambda b, i: (b, i, 0)
    col = lambda b, i: (b, 0, i)
    const2 = lambda b, i: (0, 0)
    out_specs = [pl.BlockSpec((1, w, tm), col) if n in transposed else pl.BlockSpec((1, tm, w), row)
                 for n, w in enumerate(widths)]
    out_shape = [jax.ShapeDtypeStruct((B, w, S) if n in transposed else (B, S, w), MXU_DTYPE)
                 for n, w in enumerate(widths)]
    for n in (3, 4, 5):
        out_specs[n] = pl.BlockSpec((DIL_HEADS // 2, 1, tm, LANES), lambda b, i: (0, b, i, 0))
        out_shape[n] = jax.ShapeDtypeStruct((DIL_HEADS // 2, B, S, LANES), F32)
    return pl.pallas_call(
        _proj_kernel,
        grid=(B, S // tm),
        in_specs=[pl.BlockSpec((1, tm, D), row),
                  pl.BlockSpec((1, 6, D), lambda b, i: (b, 0, 0)),
                  pl.BlockSpec((9, tm, LANES), lambda b, i: (0, i, 0)),
                  pl.BlockSpec(w_in.shape, const2),
                  pl.BlockSpec((1, MLA_Q_RANK), const2),
                  pl.BlockSpec((1, MLA_KV_RANK), const2),
                  pl.BlockSpec(w_uq.shape, const2),
                  pl.BlockSpec(w_uk.shape, const2),
                  pl.BlockSpec(w_uv.shape, const2)],
        out_specs=out_specs,
        out_shape=out_shape,
        compiler_params=_cparams(("parallel", "parallel")),
        name="proj_in",
    )(x, mod, tabs, w_in, g_cq.reshape(1, -1), g_ckv.reshape(1, -1), w_uq, w_uk, w_uv)


def _softmax_steps(chains, m_scr, l_scr, acc_scr):
    scores = [lax.dot_general(k, q, (((1,), (1,)), ((), ())), preferred_element_type=F32)
              for q, k, _ in chains]
    for u, (st, (_, _, vt)) in enumerate(zip(scores, chains)):
        m_prev = m_scr[u]
        m_new = jnp.maximum(m_prev, jnp.max(st, axis=0, keepdims=True))
        alpha = jnp.exp2(m_prev - m_new)
        p = jnp.exp2(st - m_new)
        l_scr[u] = alpha * l_scr[u] + jnp.sum(p, axis=0, keepdims=True)
        acc_scr[u] = alpha * acc_scr[u] + jnp.dot(vt, p.astype(MXU_DTYPE), preferred_element_type=F32)
        m_scr[u] = m_new


def _init_softmax(m_scr, l_scr, acc_scr):
    m_scr[...] = jnp.full(m_scr.shape, NEG, F32)
    l_scr[...] = jnp.zeros(l_scr.shape, F32)
    acc_scr[...] = jnp.zeros(acc_scr.shape, F32)


def _diff_attn_kernel(lam_init, q1_ref, q2_ref, k1_ref, k2_ref, vt_ref, lamp_ref, g_ref, o_ref,
                      q_scr, m_scr, l_scr, acc_scr):
    j = pl.program_id(1)
    ik = pl.program_id(3)

    @pl.when(ik == 0)
    def _():
        _init_softmax(m_scr, l_scr, acc_scr)
        lane = _lane_iota(q1_ref.shape[1:])
        for hh in range(2):
            h = 2 * j + hh
            msk = (lane >= h * DIFF_QK) & (lane < (h + 1) * DIFF_QK)
            q_scr[2 * hh] = jnp.where(msk, q1_ref[0].astype(F32), 0.0).astype(q_scr.dtype)
            q_scr[2 * hh + 1] = jnp.where(msk, q2_ref[0].astype(F32), 0.0).astype(q_scr.dtype)

    chains = []
    for hh in range(2):
        vt = vt_ref[0, hh * DIFF_V:(hh + 1) * DIFF_V, :]
        chains.append((q_scr[2 * hh], k1_ref[0], vt))
        chains.append((q_scr[2 * hh + 1], k2_ref[0], vt))
    _softmax_steps(chains, m_scr, l_scr, acc_scr)

    @pl.when(ik == pl.num_programs(3) - 1)
    def _():
        lp = lamp_ref[...]
        lam = (jnp.exp(jnp.sum(lp[0:1] * lp[1:2], axis=-1, keepdims=True))
               - jnp.exp(jnp.sum(lp[2:3] * lp[3:4], axis=-1, keepdims=True)) + lam_init)
        for hh in range(2):
            o = acc_scr[2 * hh] / l_scr[2 * hh] - lam * (acc_scr[2 * hh + 1] / l_scr[2 * hh + 1])
            ms = jnp.mean(o * o, axis=0, keepdims=True)
            o_ref[0, hh * DIFF_V:(hh + 1) * DIFF_V, :] = (
                o * lax.rsqrt(ms + 1e-6) * g_ref[...] * (1.0 - lam_init)).astype(o_ref.dtype)


def _diff_attn(qa, ka, vat, lam_pack, g_sub, lam_init, tq, tk):
    B, S, _ = qa.shape
    kern = functools.partial(_diff_attn_kernel, lam_init)
    return pl.pallas_call(
        kern,
        grid=(B, DIFF_HEADS // 2, S // tq, S // tk),
        in_specs=[pl.BlockSpec((1, tq, LANES), lambda b, j, iq, ik: (b, iq, 0)),
                  pl.BlockSpec((1, tq, LANES), lambda b, j, iq, ik: (b, iq, 1)),
                  pl.BlockSpec((1, tk, LANES), lambda b, j, iq, ik: (b, ik, 0)),
                  pl.BlockSpec((1, tk, LANES), lambda b, j, iq, ik: (b, ik, 1)),
                  pl.BlockSpec((1, 2 * DIFF_V, tk), lambda b, j, iq, ik: (b, j, ik)),
                  pl.BlockSpec((4, LANES), lambda b, j, iq, ik: (0, 0)),
                  pl.BlockSpec((DIFF_V, 1), lambda b, j, iq, ik: (0, 0))],
        out_specs=pl.BlockSpec((1, 2 * DIFF_V, tq), lambda b, j, iq, ik: (b, j, iq)),
        out_shape=jax.ShapeDtypeStruct((B, DIFF_HEADS * DIFF_V, S), MXU_DTYPE),
        scratch_shapes=[pltpu.VMEM((4, tq, LANES), MXU_DTYPE),
                        pltpu.VMEM((4, 1, tq), F32),
                        pltpu.VMEM((4, 1, tq), F32),
                        pltpu.VMEM((4, DIFF_V, tq), F32)],
        compiler_params=_cparams(("parallel", "parallel", "parallel", "arbitrary")),
        name="diff_attn",
    )(qa, qa, ka, ka, vat, lam_pack, g_sub)


MLA_GROUP = 3


def _mla_attn_kernel(q_ref, k_ref, vt_ref, o_ref, m_scr, l_scr, acc_scr):
    ik = pl.program_id(3)

    @pl.when(ik == 0)
    def _():
        _init_softmax(m_scr, l_scr, acc_scr)

    chains = []
    for hh in range(MLA_GROUP):
        blk = slice(hh * LANES, (hh + 1) * LANES)
        chains.append((q_ref[0, :, blk], k_ref[0, :, blk], vt_ref[0, hh * MLA_V:(hh + 1) * MLA_V, :]))
    _softmax_steps(chains, m_scr, l_scr, acc_scr)

    @pl.when(ik == pl.num_programs(3) - 1)
    def _():
        for hh in range(MLA_GROUP):
            o_ref[0, hh * MLA_V:(hh + 1) * MLA_V, :] = (acc_scr[hh] / l_scr[hh]).astype(o_ref.dtype)


def _mla_attn(mq, mk, mvt, tq, tk):
    B, S, _ = mq.shape
    g = MLA_GROUP
    return pl.pallas_call(
        _mla_attn_kernel,
        grid=(B, MLA_HEADS // g, S // tq, S // tk),
        in_specs=[pl.BlockSpec((1, tq, g * LANES), lambda b, j, iq, ik: (b, iq, j)),
                  pl.BlockSpec((1, tk, g * LANES), lambda b, j, iq, ik: (b, ik, j)),
                  pl.BlockSpec((1, g * MLA_V, tk), lambda b, j, iq, ik: (b, j, ik))],
        out_specs=pl.BlockSpec((1, g * MLA_V, tq), lambda b, j, iq, ik: (b, j, iq)),
        out_shape=jax.ShapeDtypeStruct((B, MLA_HEADS * MLA_V, S), MXU_DTYPE),
        scratch_shapes=[pltpu.VMEM((g, 1, tq), F32),
                        pltpu.VMEM((g, 1, tq), F32),
                        pltpu.VMEM((g, MLA_V, tq), F32)],
        compiler_params=_cparams(("parallel", "parallel", "parallel", "arbitrary")),
        name="mla_attn",
    )(mq, mk, mvt)


DIL_SUB = 128
DIL_WIDTH = DIL_HEADS * DIL_DIM


DIL_DILS = tuple(d for _, d in DIL_PAIRS)
DIL_HALO = DIL_RADIUS * DIL_DILS[-1]
DIL_TILE = 2048
DIL_SPLIT = 4


def _dil_kernel(seq, q_ref, kp_ref, kc_ref, kn_ref, vp_ref, vc_ref, vn_ref, o_ref,
                kbuf, vbuf, pat_o, pat_l, qg, kg, vg, q4, k4, v4):
    i = pl.program_id(2)
    tt = q_ref.shape[2]
    H = DIL_HALO
    R = DIL_RADIUS
    kbuf[0:H] = kp_ref[0, 0]
    kbuf[H:H + tt] = kc_ref[0, 0]
    kbuf[H + tt:] = kn_ref[0, 0]
    vbuf[0:H] = vp_ref[0, 0]
    vbuf[H:H + tt] = vc_ref[0, 0]
    vbuf[H + tt:] = vn_ref[0, 0]
    for r4 in range(DIL_SPLIT):
        q4[r4] = q_ref[0, 0, pl.ds(r4, tt // DIL_SPLIT, stride=DIL_SPLIT), :]
        k4[r4] = kbuf[pl.ds(r4, (tt + 2 * H) // DIL_SPLIT, stride=DIL_SPLIT), :]
        v4[r4] = vbuf[pl.ds(r4, (tt + 2 * H) // DIL_SPLIT, stride=DIL_SPLIT), :]
    sub = DIL_SUB
    win = sub + 2 * R
    nc = tt // sub
    cidx = lax.broadcasted_iota(jnp.int32, (nc, win, sub), 0)
    row_k = lax.broadcasted_iota(jnp.int32, (nc, win, sub), 1)
    col_q = lax.broadcasted_iota(jnp.int32, (nc, win, sub), 2)
    rel = row_k - col_q
    band = (rel >= 0) & (rel <= 2 * R)
    lane = _lane_iota((sub, LANES))
    for p, d in enumerate(DIL_DILS):
        n_sub = tt // d // sub
        length = seq // d
        q_starts = [c // n_sub + d * sub * (c % n_sub) for c in range(nc)]
        for c in range(nc):
            r, u = c // n_sub, c % n_sub
            y0 = H // d + u * sub - R
            if d % DIL_SPLIT:
                qf = q_ref[0, 0, pl.ds(q_starts[c], sub, stride=d), :]
                kw = kbuf[pl.ds(r + d * y0, win, stride=d), :]
                vw = vbuf[pl.ds(r + d * y0, win, stride=d), :]
            else:
                e = d // DIL_SPLIT
                r4, m = r % DIL_SPLIT, r // DIL_SPLIT
                qf = q4[r4, pl.ds(m + e * sub * u, sub, stride=e), :]
                kw = k4[r4, pl.ds(m + e * y0, win, stride=e), :]
                vw = v4[r4, pl.ds(m + e * y0, win, stride=e), :]
            qg[c, :sub] = jnp.where(lane < DIL_DIM, qf, 0.0).astype(qg.dtype)
            qg[c, sub:] = jnp.where(lane >= DIL_DIM, qf, 0.0).astype(qg.dtype)
            kg[c] = kw.astype(kg.dtype)
            vg[c] = vw.T.astype(vg.dtype)
        u_of_c = cidx % n_sub if n_sub > 1 else jnp.zeros_like(cidx)
        kpos = i * (tt // d) + u_of_c * sub - R + row_k
        valid = band & (kpos >= 0) & (kpos < length)
        s = lax.dot_general(kg[...], qg[...], (((2,), (2,)), ((0,), (0,))),
                            preferred_element_type=F32)
        s = jnp.where(jnp.concatenate([valid, valid], axis=2), s, NEG)
        m = jnp.max(s, axis=1, keepdims=True)
        e = jnp.exp2(s - m)
        l = jnp.sum(e, axis=1, keepdims=True)
        ot = lax.dot_general(vg[...], e.astype(MXU_DTYPE), (((2,), (1,)), ((0,), (0,))),
                             preferred_element_type=F32) / l
        lse = m + jnp.log2(l)
        for c in range(nc):
            o_t = jnp.concatenate([ot[c, :DIL_DIM, :sub], ot[c, DIL_DIM:, sub:]], axis=0)
            l_t = jnp.concatenate([jnp.broadcast_to(lse[c, :, :sub], (DIL_DIM, sub)),
                                   jnp.broadcast_to(lse[c, :, sub:], (DIL_DIM, sub))], axis=0)
            pat_o[p, pl.ds(q_starts[c], sub, stride=d), :] = o_t.T
            pat_l[p, pl.ds(q_starts[c], sub, stride=d), :] = l_t.T

    l0, l1, l2 = pat_l[0], pat_l[1], pat_l[2]
    mx = jnp.maximum(jnp.maximum(l0, l1), l2)
    e0, e1, e2 = jnp.exp2(l0 - mx), jnp.exp2(l1 - mx), jnp.exp2(l2 - mx)
    den = e0 + e1 + e2
    o_ref[0] = ((e0 / den) * pat_o[0] + (e1 / den) * pat_o[1] + (e2 / den) * pat_o[2]).astype(o_ref.dtype)


def _dil_attn(qb, kb, vb):
    P, B, S, _ = qb.shape
    tt = _tile(S, DIL_TILE)
    H = DIL_HALO
    nh = S // H
    cur = pl.BlockSpec((1, 1, tt, LANES), lambda b, hp, i: (hp, b, i, 0))
    prv = pl.BlockSpec((1, 1, H, LANES), lambda b, hp, i: (hp, b, jnp.maximum(i * (tt // H) - 1, 0), 0))
    nxt = pl.BlockSpec((1, 1, H, LANES), lambda b, hp, i: (hp, b, jnp.minimum((i + 1) * (tt // H), nh - 1), 0))
    return pl.pallas_call(
        functools.partial(_dil_kernel, S),
        grid=(B, P, S // tt),
        in_specs=[cur, prv, cur, nxt, prv, cur, nxt],
        out_specs=pl.BlockSpec((1, tt, LANES), lambda b, hp, i: (b, i, hp)),
        out_shape=jax.ShapeDtypeStruct((B, S, P * LANES), MXU_DTYPE),
        scratch_shapes=[pltpu.VMEM((tt + 2 * H, LANES), F32),
                        pltpu.VMEM((tt + 2 * H, LANES), F32),
                        pltpu.VMEM((len(DIL_DILS), tt, LANES), F32),
                        pltpu.VMEM((len(DIL_DILS), tt, LANES), F32),
                        pltpu.VMEM((tt // DIL_SUB, 2 * DIL_SUB, LANES), MXU_DTYPE),
                        pltpu.VMEM((tt // DIL_SUB, DIL_SUB + 2 * DIL_RADIUS, LANES), MXU_DTYPE),
                        pltpu.VMEM((tt // DIL_SUB, LANES, DIL_SUB + 2 * DIL_RADIUS), MXU_DTYPE),
                        pltpu.VMEM((DIL_SPLIT, tt // DIL_SPLIT, LANES), F32),
                        pltpu.VMEM((DIL_SPLIT, (tt + 2 * H) // DIL_SPLIT, LANES), F32),
                        pltpu.VMEM((DIL_SPLIT, (tt + 2 * H) // DIL_SPLIT, LANES), F32)],
        compiler_params=_cparams(("parallel", "parallel", "parallel")),
        name="dil_attn",
    )(qb, kb, kb, kb, vb, vb, vb)


def _partner(x, k, lane):
    up = pltpu.roll(x, LANES - k, 1)
    dn = pltpu.roll(x, k, 1)
    return jnp.where((lane & k) == 0, up, dn)


GROUP_LANE = N_EXPERTS


def _route(logits, bias, lane):
    valid = lane < N_EXPERTS
    lane_f = lane.astype(F32)
    score = 1.0 / (1.0 + jnp.exp(-logits))
    biased = jnp.where(valid, score + bias, NEG)
    p1 = _partner(biased, 1, lane)
    hi = jnp.maximum(biased, p1)
    lo = jnp.minimum(biased, p1)
    hi_p = _partner(hi, 2, lane)
    lo_p = _partner(lo, 2, lane)
    top1 = jnp.maximum(hi, hi_p)
    top2 = jnp.maximum(jnp.minimum(hi, hi_p), jnp.maximum(lo, lo_p))
    gscore = jnp.where(valid, top1 + top2, NEG)
    gidx = (lane // GROUP_SIZE).astype(F32)
    big = float(LANES)
    gmax = jnp.max(gscore, axis=-1, keepdims=True)
    gsel = jnp.min(jnp.where(gscore == gmax, gidx, big), axis=-1, keepdims=True)
    cand = jnp.where((gidx == gsel) & valid, biased, NEG)
    m1 = jnp.max(cand, axis=-1, keepdims=True)
    i1 = jnp.min(jnp.where(cand == m1, lane_f, big), axis=-1, keepdims=True)
    oh1 = lane_f == i1
    cand2 = jnp.where(oh1, NEG, cand)
    m2 = jnp.max(cand2, axis=-1, keepdims=True)
    i2 = jnp.min(jnp.where((cand2 == m2) & jnp.logical_not(oh1), lane_f, big), axis=-1, keepdims=True)
    oh2 = lane_f == i2
    s1 = jnp.sum(jnp.where(oh1, score, 0.0), axis=-1, keepdims=True)
    s2 = jnp.sum(jnp.where(oh2, score, 0.0), axis=-1, keepdims=True)
    den = s1 + s2
    gate = jnp.where(oh1, s1 / den, 0.0) + jnp.where(oh2, s2 / den, 0.0)
    return jnp.where(lane == GROUP_LANE, gsel, gate)


POST_PARTS = 2


def _post_kernel(x_ref, mod_ref, oa_ref, ob_ref, oc_ref, wo_ref, g_ref, b_ref, wrh_ref, wrl_ref, br_ref,
                 x1_ref, h2_ref, gate_ref):
    na = DIFF_HEADS * DIFF_V
    nb = na + DIL_WIDTH
    tn = (((0,), (0,)), ((), ()))
    g1 = mod_ref[0, 2:3, :]
    sh2 = mod_ref[0, 3:4, :]
    sc2 = mod_ref[0, 4:5, :]
    tm = x_ref.shape[1]
    parts = POST_PARTS if tm % (POST_PARTS * LANES) == 0 else 1
    for r in range(parts):
        rows = slice(r * (tm // parts), (r + 1) * (tm // parts))
        y = (lax.dot_general(oa_ref[0, :, rows], wo_ref[0:na], tn, preferred_element_type=F32)
             + jnp.dot(ob_ref[0, rows, :], wo_ref[na:nb], preferred_element_type=F32)
             + lax.dot_general(oc_ref[0, :, rows], wo_ref[nb:], tn, preferred_element_type=F32))
        x1 = _layer_norm(ALPHA * x_ref[0, rows, :] + g1 * y, g_ref[...], b_ref[...])
        x1_ref[0, rows, :] = x1
        h2 = x1 * (1.0 + sc2) + sh2
        h2_ref[0, rows, :] = h2.astype(h2_ref.dtype)
        h_hi = h2.astype(MXU_DTYPE)
        h_lo = (h2 - h_hi.astype(F32)).astype(MXU_DTYPE)
        logits = (jnp.dot(h_hi, wrh_ref[...], preferred_element_type=F32)
                  + jnp.dot(h_lo, wrh_ref[...], preferred_element_type=F32)
                  + jnp.dot(h_hi, wrl_ref[...], preferred_element_type=F32))
        gate_ref[0, rows, :] = _route(logits, br_ref[...], _lane_iota(logits.shape))


def _post(x, mod, oa, ob, oc, w_out, g, b, w_r_hi, w_r_lo, b_r, tm):
    B, S, D = x.shape
    row = lambda b_, i: (b_, i, 0)
    const2 = lambda b_, i: (0, 0)
    return pl.pallas_call(
        _post_kernel,
        grid=(B, S // tm),
        in_specs=[pl.BlockSpec((1, tm, D), row),
                  pl.BlockSpec((1, 6, D), lambda b_, i: (b_, 0, 0)),
                  pl.BlockSpec((1, oa.shape[1], tm), lambda b_, i: (b_, 0, i)),
                  pl.BlockSpec((1, tm, DIL_WIDTH), row),
                  pl.BlockSpec((1, oc.shape[1], tm), lambda b_, i: (b_, 0, i)),
                  pl.BlockSpec(w_out.shape, const2),
                  pl.BlockSpec((1, D), const2),
                  pl.BlockSpec((1, D), const2),
                  pl.BlockSpec(w_r_hi.shape, const2),
                  pl.BlockSpec(w_r_lo.shape, const2),
                  pl.BlockSpec((1, LANES), const2)],
        out_specs=[pl.BlockSpec((1, tm, D), row),
                   pl.BlockSpec((1, tm, D), row),
                   pl.BlockSpec((1, tm, LANES), row)],
        out_shape=[jax.ShapeDtypeStruct((B, S, D), F32),
                   jax.ShapeDtypeStruct((B, S, D), MXU_DTYPE),
                   jax.ShapeDtypeStruct((B, S, LANES), F32)],
        compiler_params=_cparams(("parallel", "parallel")),
        name="post_mix",
    )(x, mod, oa, ob, oc, w_out, g.reshape(1, D), b.reshape(1, D), w_r_hi, w_r_lo, b_r)


MOE_TM = 1024
MOE_CHUNK = 256
MOE_TAIL = 64
N_GROUPS = N_EXPERTS // GROUP_SIZE


def _moe_kernel(x1_ref, mod_ref, h2_ref, gate_ref, w1_ref, w3_ref, w2_ref, g_ref, b_ref, o_ref,
                acc_scr, src_scr, sel_scr, rank_scr):
    g = pl.program_id(2)
    tm = h2_ref.shape[1]
    d = h2_ref.shape[2]
    C = MOE_CHUNK

    @pl.when(g == 0)
    def _():
        acc_scr[...] = jnp.zeros(acc_scr.shape, F32)
        gate = gate_ref[0]
        gate_hi = gate.astype(MXU_DTYPE)
        src_scr[:, 0:d] = h2_ref[0]
        src_scr[:, d:d + LANES] = gate_hi
        src_scr[:, d + LANES:] = (gate - gate_hi.astype(F32)).astype(MXU_DTYPE)
        gsel_t = gate.T[GROUP_LANE:GROUP_LANE + 1, :]
        grp = lax.broadcasted_iota(jnp.int32, sel_scr.shape, 0).astype(F32)
        sel = jnp.where(gsel_t == grp, 1.0, 0.0)
        r = lax.broadcasted_iota(jnp.int32, (tm, tm), 0)
        c = lax.broadcasted_iota(jnp.int32, (tm, tm), 1)
        before = jnp.where(r < c, 1.0, 0.0).astype(MXU_DTYPE)
        sel_scr[...] = sel
        rank_scr[...] = jnp.dot(sel.astype(MXU_DTYPE), before, preferred_element_type=F32)

    sel = sel_scr[pl.ds(g, 1), :]
    rank = rank_scr[pl.ds(g, 1), :]
    n_g = jnp.sum(sel).astype(jnp.int32)

    def chunk(base, rows):
        lane = _lane_iota((rows, LANES))
        row = lax.broadcasted_iota(jnp.int32, (rows, tm), 0).astype(F32)
        onehot = jnp.where((rank - base.astype(F32) == row) & (sel > 0.5), 1.0, 0.0).astype(MXU_DTYPE)
        picked = jnp.dot(onehot, src_scr[...], preferred_element_type=F32)
        xs = picked[:, 0:d].astype(MXU_DTYPE)
        gs = picked[:, d:d + LANES] + picked[:, d + LANES:]
        ys = jnp.zeros((rows, d), F32)
        for e in range(GROUP_SIZE):
            ge = jnp.sum(jnp.where(lane == g * GROUP_SIZE + e, gs, 0.0), axis=-1, keepdims=True)
            a = jnp.dot(xs, w1_ref[e], preferred_element_type=F32)
            bb = jnp.dot(xs, w3_ref[e], preferred_element_type=F32)
            hid = (a * (1.0 / (1.0 + jnp.exp(-a)))) * bb * ge
            ys = ys + jnp.dot(hid.astype(MXU_DTYPE), w2_ref[e], preferred_element_type=F32)
        acc_scr[...] += lax.dot_general(onehot, ys.astype(MXU_DTYPE), (((0,), (0,)), ((), ())),
                                        preferred_element_type=F32)

    n_chunks = (n_g + C - 1) // C
    tail = n_g - (n_chunks - 1) * C
    merge = (n_chunks >= 2) & (tail <= MOE_TAIL)
    n_plain = jnp.where(merge, n_chunks - 2, n_chunks)

    def plain(ci, carry):
        chunk(ci * C, C)
        return carry

    lax.fori_loop(0, n_plain, plain, 0)

    @pl.when(merge)
    def _():
        chunk(n_plain * C, C + MOE_TAIL)

    @pl.when(g == pl.num_programs(2) - 1)
    def _():
        g2 = mod_ref[0, 5:6, :]
        o_ref[0] = _layer_norm(ALPHA * x1_ref[0] + g2 * acc_scr[...], g_ref[...], b_ref[...])


def _moe(x1, mod, h2, gate, w1, w3, w2, g, b, tm):
    B, S, D = x1.shape
    gs = GROUP_SIZE
    row = lambda b_, i, c: (b_, i, 0)
    const2 = lambda b_, i, c: (0, 0)
    return pl.pallas_call(
        _moe_kernel,
        grid=(B, S // tm, N_GROUPS),
        in_specs=[pl.BlockSpec((1, tm, D), row),
                  pl.BlockSpec((1, 6, D), lambda b_, i, c: (b_, 0, 0)),
                  pl.BlockSpec((1, tm, D), row),
                  pl.BlockSpec((1, tm, LANES), row),
                  pl.BlockSpec((gs, D, D_FF), lambda b_, i, c: (c, 0, 0)),
                  pl.BlockSpec((gs, D, D_FF), lambda b_, i, c: (c, 0, 0)),
                  pl.BlockSpec((gs, D_FF, D), lambda b_, i, c: (c, 0, 0)),
                  pl.BlockSpec((1, D), const2),
                  pl.BlockSpec((1, D), const2)],
        out_specs=pl.BlockSpec((1, tm, D), row),
        out_shape=jax.ShapeDtypeStruct((B, S, D), F32),
        scratch_shapes=[pltpu.VMEM((tm, D), F32),
                        pltpu.VMEM((tm, D + 2 * LANES), MXU_DTYPE),
                        pltpu.VMEM((8, tm), F32),
                        pltpu.VMEM((8, tm), F32)],
        compiler_params=_cparams(("parallel", "parallel", "arbitrary")),
        name="moe_ffn",
    )(x1, mod, h2, gate, w1, w3, w2, g.reshape(1, D), b.reshape(1, D))


def _rope_tables(seq):
    lane = np.arange(LANES)

    def tables(period, offset, rot, theta):
        half = rot // 2
        d = lane % period - offset
        first = (d >= 0) & (d < half)
        second = (d >= half) & (d < rot)
        fidx = np.where(first, d, np.where(second, d - half, 0))
        inv_freq = theta ** (-2.0 * jnp.arange(half, dtype=F32) / rot)
        ang = jnp.arange(seq, dtype=F32)[:, None] * inv_freq[None, :]
        cos_l = jnp.cos(ang)[:, fidx]
        sin_l = jnp.sin(ang)[:, fidx]
        cos = jnp.where((first | second)[None, :], cos_l, 1.0)
        s_up = jnp.where(first[None, :], -sin_l, 0.0)
        s_dn = jnp.where(second[None, :], sin_l, 0.0)
        return [cos, s_up, s_dn]

    tabs = (tables(DIFF_QK, 0, DIFF_QK // 4, ROPE_THETA)
            + tables(DIL_DIM, 0, DIL_DIM // 4, ROPE_THETA)
            + tables(LANES, KR_LANE, MLA_ROPE, MLA_THETA))
    return jnp.stack(tabs).astype(F32)


def _prep_layer(l, w_in, w_uq, w_ukv, w_out, w1, w3, w2):
    D = w_in.shape[1]
    wi = w_in[l]
    zeros = lambda n: jnp.zeros((D, n), wi.dtype)
    wi = jnp.concatenate([wi[:, :C_KR], zeros(KR_LANE), wi[:, C_KR:], zeros(LANES - KR_LANE - MLA_ROPE)], axis=1)
    uq = w_uq[l].reshape(MLA_Q_RANK, MLA_HEADS, MLA_NOPE + MLA_ROPE)
    uq = jnp.pad(uq, ((0, 0), (0, 0), (0, LANES - MLA_NOPE - MLA_ROPE))).reshape(MLA_Q_RANK, MLA_HEADS * LANES)
    ukv = w_ukv[l].reshape(MLA_KV_RANK, MLA_HEADS, MLA_NOPE + MLA_V)
    uk = jnp.pad(ukv[..., :MLA_NOPE], ((0, 0), (0, 0), (0, LANES - MLA_NOPE))).reshape(MLA_KV_RANK, MLA_HEADS * LANES)
    uv = ukv[..., MLA_NOPE:].reshape(MLA_KV_RANK, MLA_HEADS * MLA_V)
    c = lambda t: t.astype(MXU_DTYPE)
    return dict(w_in=c(wi), w_uq=c(uq), w_uk=c(uk), w_uv=c(uv), w_out=c(w_out[l]),
                w1=c(w1[l]), w3=c(w3[l]), w2=c(w2[l]))


def _tile(n, pref):
    return pref if n % pref == 0 else n


def _trunk(x, mods, tabs, layers, p):
    B, S, D = x.shape
    tm = _tile(S, 512)
    tq = _tile(S, ATTN_TQ)
    tk = _tile(S, ATTN_TK)
    x = _ln_in(x, p['ln_in_g'], p['ln_in_b'], tm)
    for l, lw in enumerate(layers):
        mod = mods[l]
        lam_init = 0.8 - 0.6 * math.exp(-0.3 * l)
        qa, ka, va, qb, kb, vb, mq, mk, mv = _proj(
            x, mod, tabs, lw['w_in'], p['g_cq'][l], p['g_ckv'][l], lw['w_uq'], lw['w_uk'], lw['w_uv'], tm)
        oa = _diff_attn(qa, ka, va, p['lam_pack'][l], p['g_sub'][l], lam_init, tq, tk)
        oc = _mla_attn(mq, mk, mv, tq, tk)
        ob = _dil_attn(qb, kb, vb)
        x1, h2, gate = _post(x, mod, oa, ob, oc, lw['w_out'], p['ln1_g'][l], p['ln1_b'][l],
                             p['w_r_hi'], p['w_r_lo'], p['b_r'], tm)
        x = _moe(x1, mod, h2, gate, lw['w1'], lw['w3'], lw['w2'], p['ln2_g'][l], p['ln2_b'][l],
                 _tile(S, MOE_TM))
    return x


def kernel(x_prompt, x_sample, c_prompt, c_sample, ln_in_g, ln_in_b, w_router, b_router, w_ada, b_ada, w_in, lam_q1, lam_k1, lam_q2, lam_k2, g_subln, g_cq, g_ckv, w_uq, w_ukv, w_out, ln1_g, ln1_b, w1, w3, w2, ln2_g, ln2_b):
    depth = w_in.shape[0]
    D = x_prompt.shape[-1]
    bp, bs = c_prompt.shape[0], c_sample.shape[0]
    rows = -(-(bp + bs) // 8) * 8
    c_all = jnp.concatenate([c_prompt, c_sample, jnp.zeros((rows - bp - bs, D), F32)], axis=0)
    mods = _ada(c_all, w_ada, b_ada)
    mods_p = mods[:, :bp].reshape(depth, bp, 6, D)
    mods_s = mods[:, bp:bp + bs].reshape(depth, bs, 6, D)

    pad_l = lambda t: jnp.pad(t, ((0, 0), (0, LANES - t.shape[-1])))
    p = dict(
        ln_in_g=ln_in_g, ln_in_b=ln_in_b, g_cq=g_cq, g_ckv=g_ckv,
        ln1_g=ln1_g, ln1_b=ln1_b, ln2_g=ln2_g, ln2_b=ln2_b,
        lam_pack=jnp.stack([pad_l(lam_q1), pad_l(lam_k1), pad_l(lam_q2), pad_l(lam_k2)], axis=1),
        g_sub=g_subln.reshape(depth, DIFF_V, 1),
        b_r=pad_l(b_router.reshape(1, -1)),
    )
    w_r = pad_l(w_router)
    p['w_r_hi'] = w_r.astype(MXU_DTYPE)
    p['w_r_lo'] = (w_r - p['w_r_hi'].astype(F32)).astype(MXU_DTYPE)
    layers = [_prep_layer(l, w_in, w_uq, w_ukv, w_out, w1, w3, w2) for l in range(depth)]
    tabs = _rope_tables(max(x_prompt.shape[1], x_sample.shape[1]))
    y_prompt = _trunk(x_prompt, mods_p, tabs, layers, p)
    y_sample = _trunk(x_sample, mods_s, tabs, layers, p)
    return (y_prompt, y_sample)
```

```python
import functools
import math

import numpy as np
import jax
import jax.numpy as jnp
from jax import lax
from jax.experimental import pallas as pl
from jax.experimental.pallas import tpu as pltpu

F32 = jnp.float32
MXU_DTYPE = jnp.bfloat16

LANES = 128
DEPTH = 4
DIFF_HEADS = 4
DIFF_QK = 32
DIFF_V = 64
DIL_HEADS = 6
DIL_DIM = 64
DIL_PAIRS = ((128, 1), (512, 4), (2048, 16))
DIL_RADIUS = 64
MLA_HEADS = 6
MLA_NOPE = 64
MLA_ROPE = 32
MLA_V = 64
MLA_Q_RANK = 256
MLA_KV_RANK = 128
MLA_THETA = 10000.0
ROPE_THETA = 500000.0
N_EXPERTS = 16
GROUP_SIZE = 4
D_FF = 256
ALPHA = (2 * DEPTH) ** 0.25
NEG = -1e30
LOG2E = math.log2(math.e)
ATTN_TQ = 512
ATTN_TK = 2048

C_Q1, C_Q2, C_K1, C_K2, C_VA = 0, 128, 256, 384, 512
C_QB, C_KB, C_VB = 768, 1152, 1536
C_CQ, C_CKV, C_KR, C_END = 1920, 2176, 2304, 2432
KR_LANE = 64

VMEM_LIMIT = 48 * 1024 * 1024


def _cparams(sem):
    return pltpu.CompilerParams(dimension_semantics=sem, vmem_limit_bytes=VMEM_LIMIT)


def _lane_iota(shape):
    return lax.broadcasted_iota(jnp.int32, shape, len(shape) - 1)


def _layer_norm(z, g, b):
    mu = jnp.mean(z, axis=-1, keepdims=True)
    zc = z - mu
    var = jnp.mean(zc * zc, axis=-1, keepdims=True)
    return zc * lax.rsqrt(var + 1e-5) * g + b


def _rope(t, tab_ref, typ, half, rows):
    c = tab_ref[3 * typ, rows, :]
    s_up = tab_ref[3 * typ + 1, rows, :]
    s_dn = tab_ref[3 * typ + 2, rows, :]
    return t * c + pltpu.roll(t, LANES - half, 1) * s_up + pltpu.roll(t, half, 1) * s_dn


def _ln_in_kernel(x_ref, g_ref, b_ref, o_ref):
    o_ref[0] = _layer_norm(x_ref[0], g_ref[...], b_ref[...])


def _ln_in(x, g, b, tm):
    B, S, D = x.shape
    return pl.pallas_call(
        _ln_in_kernel,
        grid=(B, S // tm),
        in_specs=[pl.BlockSpec((1, tm, D), lambda b, i: (b, i, 0)),
                  pl.BlockSpec((1, D), lambda b, i: (0, 0)),
                  pl.BlockSpec((1, D), lambda b, i: (0, 0))],
        out_specs=pl.BlockSpec((1, tm, D), lambda b, i: (b, i, 0)),
        out_shape=jax.ShapeDtypeStruct(x.shape, F32),
        compiler_params=_cparams(("parallel", "parallel")),
        name="ln_in",
    )(x, g.reshape(1, D), b.reshape(1, D))


def _ada_kernel(c_ref, w_ref, b_ref, o_ref):
    c = c_ref[...]
    cs = (c * (1.0 / (1.0 + jnp.exp(-c)))).astype(MXU_DTYPE)
    o_ref[0] = jnp.dot(cs, w_ref[0].astype(MXU_DTYPE), preferred_element_type=F32) + b_ref[0]


def _ada(c, w_ada, b_ada, tn=1536):
    R, D = c.shape
    L, _, N = w_ada.shape
    return pl.pallas_call(
        _ada_kernel,
        grid=(L, N // tn),
        in_specs=[pl.BlockSpec((R, D), lambda l, j: (0, 0)),
                  pl.BlockSpec((1, D, tn), lambda l, j: (l, 0, j)),
                  pl.BlockSpec((1, 1, tn), lambda l, j: (l, 0, j))],
        out_specs=pl.BlockSpec((1, R, tn), lambda l, j: (l, 0, j)),
        out_shape=jax.ShapeDtypeStruct((L, R, N), F32),
        compiler_params=_cparams(("parallel", "parallel")),
        name="ada",
    )(c, w_ada, b_ada.reshape(L, 1, N))


PROJ_PARTS = 2
V_DIM = 64
V_ROWS = V_DIM + 16


def _store_values_t(ref, vt, heads, cols):
    n = vt.shape[1]
    for h in range(heads):
        ref[0, h * V_ROWS:h * V_ROWS + V_DIM, cols] = vt[h * V_DIM:(h + 1) * V_DIM].astype(ref.dtype)
        ref[0, h * V_ROWS + V_DIM:(h + 1) * V_ROWS, cols] = jnp.ones((V_ROWS - V_DIM, n), ref.dtype)


def _proj_kernel(x_ref, mod_ref, tab_ref, w_ref, gcq_ref, gckv_ref, wuq_ref, wuk_ref, wuv_ref,
                 qa_ref, ka_ref, va_ref, qb_ref, kb_ref, vb_ref, mq_ref, mk_ref, mv_ref):
    sh1 = mod_ref[0, 0:1, :]
    sc1 = mod_ref[0, 1:2, :]
    tm = x_ref.shape[1]
    parts = PROJ_PARTS if tm % (PROJ_PARTS * LANES) == 0 else 1

    def rms(t, g):
        return (t * lax.rsqrt(jnp.mean(t * t, axis=-1, keepdims=True) + 1e-6) * g).astype(MXU_DTYPE)

    for r in range(parts):
        rows = slice(r * (tm // parts), (r + 1) * (tm // parts))
        h = (x_ref[0, rows, :] * (1.0 + sc1) + sh1).astype(MXU_DTYPE)

        def mm(a, b, h=h):
            return jnp.dot(h, w_ref[:, a:b], preferred_element_type=F32)

        rope = functools.partial(_rope, tab_ref=tab_ref, rows=rows)

        qa = mm(C_Q1, C_K1)
        ka = mm(C_K1, C_VA)
        sa = DIFF_QK ** -0.5 * LOG2E
        for j in range(2):
            blk = slice(j * LANES, (j + 1) * LANES)
            qa_ref[0, rows, blk] = (rope(qa[:, blk], typ=0, half=DIFF_QK // 8) * sa).astype(qa_ref.dtype)
            ka_ref[0, rows, blk] = rope(ka[:, blk], typ=0, half=DIFF_QK // 8).astype(ka_ref.dtype)
        _store_values_t(va_ref, mm(C_VA, C_QB).T, DIFF_HEADS, rows)

        qb = mm(C_QB, C_KB)
        kb = mm(C_KB, C_VB)
        vb = mm(C_VB, C_CQ)
        sb = DIL_DIM ** -0.5 * LOG2E
        for j in range(DIL_HEADS // 2):
            blk = slice(j * LANES, (j + 1) * LANES)
            qb_ref[j, 0, rows, :] = rope(qb[:, blk], typ=1, half=DIL_DIM // 8) * sb
            kb_ref[j, 0, rows, :] = rope(kb[:, blk], typ=1, half=DIL_DIM // 8)
            vb_ref[j, 0, rows, :] = vb[:, blk]

        cq = rms(mm(C_CQ, C_CKV), gcq_ref[...])
        ckv = rms(mm(C_CKV, C_KR), gckv_ref[...])
        kr = rope(mm(C_KR, C_END), typ=2, half=MLA_ROPE // 2)
        q = jnp.dot(cq, wuq_ref[...], preferred_element_type=F32)
        kn = jnp.dot(ckv, wuk_ref[...], preferred_element_type=F32)
        sc = (MLA_NOPE + MLA_ROPE) ** -0.5 * LOG2E
        for j in range(MLA_HEADS):
            blk = slice(j * LANES, (j + 1) * LANES)
            mq_ref[0, rows, blk] = (rope(q[:, blk], typ=2, half=MLA_ROPE // 2) * sc).astype(mq_ref.dtype)
            mk_ref[0, rows, blk] = (kn[:, blk] + kr).astype(mk_ref.dtype)
        _store_values_t(mv_ref, jnp.dot(ckv, wuv_ref[...], preferred_element_type=F32).T, MLA_HEADS, rows)


def _proj(x, mod, tabs, w_in, g_cq, g_ckv, w_uq, w_uk, w_uv, tm):
    B, S, D = x.shape
    widths = (256, 256, DIFF_HEADS * V_ROWS, 384, 384, 384, 768, 768, MLA_HEADS * V_ROWS)
    transposed = (2, 8)
    row = lambda b, i: (b, i, 0)
    col = lambda b, i: (b, 0, i)
    const2 = lambda b, i: (0, 0)
    out_specs = [pl.BlockSpec((1, w, tm), col) if n in transposed else pl.BlockSpec((1, tm, w), row)
                 for n, w in enumerate(widths)]
    out_shape = [jax.ShapeDtypeStruct((B, w, S) if n in transposed else (B, S, w), MXU_DTYPE)
                 for n, w in enumerate(widths)]
    for n in (3, 4, 5):
        out_specs[n] = pl.BlockSpec((DIL_HEADS // 2, 1, tm, LANES), lambda b, i: (0, b, i, 0))
        out_shape[n] = jax.ShapeDtypeStruct((DIL_HEADS // 2, B, S, LANES), F32)
    return pl.pallas_call(
        _proj_kernel,
        grid=(B, S // tm),
        in_specs=[pl.BlockSpec((1, tm, D), row),
                  pl.BlockSpec((1, 6, D), lambda b, i: (b, 0, 0)),
                  pl.BlockSpec((9, tm, LANES), lambda b, i: (0, i, 0)),
                  pl.BlockSpec(w_in.shape, const2),
                  pl.BlockSpec((1, MLA_Q_RANK), const2),
                  pl.BlockSpec((1, MLA_KV_RANK), const2),
                  pl.BlockSpec(w_uq.shape, const2),
                  pl.BlockSpec(w_uk.shape, const2),
                  pl.BlockSpec(w_uv.shape, const2)],
        out_specs=out_specs,
        out_shape=out_shape,
        compiler_params=_cparams(("parallel", "parallel")),
        name="proj_in",
    )(x, mod, tabs, w_in, g_cq.reshape(1, -1), g_ckv.reshape(1, -1), w_uq, w_uk, w_uv)


def _softmax_steps(chains, m_scr, acc_scr):
    scores = [lax.dot_general(k, q, (((1,), (1,)), ((), ())), preferred_element_type=F32)
              for q, k, _ in chains]
    for u, (st, (_, _, vt)) in enumerate(zip(scores, chains)):
        m_prev = m_scr[u]
        m_new = jnp.maximum(m_prev, jnp.max(st, axis=0, keepdims=True))
        alpha = jnp.exp2(m_prev - m_new)
        p = jnp.exp2(st - m_new)
        acc_scr[u] = alpha * acc_scr[u] + jnp.dot(vt, p.astype(MXU_DTYPE), preferred_element_type=F32)
        m_scr[u] = m_new


def _init_softmax(m_scr, acc_scr):
    m_scr[...] = jnp.full(m_scr.shape, NEG, F32)
    acc_scr[...] = jnp.zeros(acc_scr.shape, F32)


def _softmax_out(acc_scr, u):
    return acc_scr[u, :V_DIM] / acc_scr[u, V_DIM:V_DIM + 1]


def _diff_attn_kernel(lam_init, q1_ref, q2_ref, k1_ref, k2_ref, vt_ref, lamp_ref, g_ref, o_ref,
                      q_scr, m_scr, acc_scr):
    j = pl.program_id(1)
    ik = pl.program_id(3)

    @pl.when(ik == 0)
    def _():
        _init_softmax(m_scr, acc_scr)
        lane = _lane_iota(q1_ref.shape[1:])
        for hh in range(2):
            h = 2 * j + hh
            msk = (lane >= h * DIFF_QK) & (lane < (h + 1) * DIFF_QK)
            q_scr[2 * hh] = jnp.where(msk, q1_ref[0].astype(F32), 0.0).astype(q_scr.dtype)
            q_scr[2 * hh + 1] = jnp.where(msk, q2_ref[0].astype(F32), 0.0).astype(q_scr.dtype)

    chains = []
    for hh in range(2):
        vt = vt_ref[0, hh * V_ROWS:(hh + 1) * V_ROWS, :]
        chains.append((q_scr[2 * hh], k1_ref[0], vt))
        chains.append((q_scr[2 * hh + 1], k2_ref[0], vt))
    _softmax_steps(chains, m_scr, acc_scr)

    @pl.when(ik == pl.num_programs(3) - 1)
    def _():
        lp = lamp_ref[...]
        lam = (jnp.exp(jnp.sum(lp[0:1] * lp[1:2], axis=-1, keepdims=True))
               - jnp.exp(jnp.sum(lp[2:3] * lp[3:4], axis=-1, keepdims=True)) + lam_init)
        for hh in range(2):
            o = _softmax_out(acc_scr, 2 * hh) - lam * _softmax_out(acc_scr, 2 * hh + 1)
            ms = jnp.mean(o * o, axis=0, keepdims=True)
            o_ref[0, hh * DIFF_V:(hh + 1) * DIFF_V, :] = (
                o * lax.rsqrt(ms + 1e-6) * g_ref[...] * (1.0 - lam_init)).astype(o_ref.dtype)


def _diff_attn(qa, ka, vat, lam_pack, g_sub, lam_init, tq, tk):
    B, S, _ = qa.shape
    kern = functools.partial(_diff_attn_kernel, lam_init)
    return pl.pallas_call(
        kern,
        grid=(B, DIFF_HEADS // 2, S // tq, S // tk),
        in_specs=[pl.BlockSpec((1, tq, LANES), lambda b, j, iq, ik: (b, iq, 0)),
                  pl.BlockSpec((1, tq, LANES), lambda b, j, iq, ik: (b, iq, 1)),
                  pl.BlockSpec((1, tk, LANES), lambda b, j, iq, ik: (b, ik, 0)),
                  pl.BlockSpec((1, tk, LANES), lambda b, j, iq, ik: (b, ik, 1)),
                  pl.BlockSpec((1, 2 * V_ROWS, tk), lambda b, j, iq, ik: (b, j, ik)),
                  pl.BlockSpec((4, LANES), lambda b, j, iq, ik: (0, 0)),
                  pl.BlockSpec((DIFF_V, 1), lambda b, j, iq, ik: (0, 0))],
        out_specs=pl.BlockSpec((1, 2 * DIFF_V, tq), lambda b, j, iq, ik: (b, j, iq)),
        out_shape=jax.ShapeDtypeStruct((B, DIFF_HEADS * DIFF_V, S), MXU_DTYPE),
        scratch_shapes=[pltpu.VMEM((4, tq, LANES), MXU_DTYPE),
                        pltpu.VMEM((4, 1, tq), F32),
                        pltpu.VMEM((4, V_ROWS, tq), F32)],
        compiler_params=_cparams(("parallel", "parallel", "parallel", "arbitrary")),
        name="diff_attn",
    )(qa, qa, ka, ka, vat, lam_pack, g_sub)


MLA_GROUP = 3


def _mla_attn_kernel(q_ref, k_ref, vt_ref, o_ref, m_scr, acc_scr):
    ik = pl.program_id(3)

    @pl.when(ik == 0)
    def _():
        _init_softmax(m_scr, acc_scr)

    chains = []
    for hh in range(MLA_GROUP):
        blk = slice(hh * LANES, (hh + 1) * LANES)
        chains.append((q_ref[0, :, blk], k_ref[0, :, blk], vt_ref[0, hh * V_ROWS:(hh + 1) * V_ROWS, :]))
    _softmax_steps(chains, m_scr, acc_scr)

    @pl.when(ik == pl.num_programs(3) - 1)
    def _():
        for hh in range(MLA_GROUP):
            o_ref[0, hh * MLA_V:(hh + 1) * MLA_V, :] = _softmax_out(acc_scr, hh).astype(o_ref.dtype)


def _mla_attn(mq, mk, mvt, tq, tk):
    B, S, _ = mq.shape
    g = MLA_GROUP
    return pl.pallas_call(
        _mla_attn_kernel,
        grid=(B, MLA_HEADS // g, S // tq, S // tk),
        in_specs=[pl.BlockSpec((1, tq, g * LANES), lambda b, j, iq, ik: (b, iq, j)),
                  pl.BlockSpec((1, tk, g * LANES), lambda b, j, iq, ik: (b, ik, j)),
                  pl.BlockSpec((1, g * V_ROWS, tk), lambda b, j, iq, ik: (b, j, ik))],
        out_specs=pl.BlockSpec((1, g * MLA_V, tq), lambda b, j, iq, ik: (b, j, iq)),
        out_shape=jax.ShapeDtypeStruct((B, MLA_HEADS * MLA_V, S), MXU_DTYPE),
        scratch_shapes=[pltpu.VMEM((g, 1, tq), F32),
                        pltpu.VMEM((g, V_ROWS, tq), F32)],
        compiler_params=_cparams(("parallel", "parallel", "parallel", "arbitrary")),
        name="mla_attn",
    )(mq, mk, mvt)


DIL_SUB = 128
DIL_WIDTH = DIL_HEADS * DIL_DIM


DIL_DILS = tuple(d for _, d in DIL_PAIRS)
DIL_HALO = DIL_RADIUS * DIL_DILS[-1]
DIL_TILE = 2048
DIL_SPLIT = 4


def _dil_kernel(seq, q_ref, kp_ref, kc_ref, kn_ref, vp_ref, vc_ref, vn_ref, o_ref,
                kbuf, vbuf, pat_o, pat_l, qg, kg, vg, q4, k4, v4):
    i = pl.program_id(2)
    tt = q_ref.shape[2]
    H = DIL_HALO
    R = DIL_RADIUS
    kbuf[0:H] = kp_ref[0, 0]
    kbuf[H:H + tt] = kc_ref[0, 0]
    kbuf[H + tt:] = kn_ref[0, 0]
    vbuf[0:H] = vp_ref[0, 0]
    vbuf[H:H + tt] = vc_ref[0, 0]
    vbuf[H + tt:] = vn_ref[0, 0]
    for r4 in range(DIL_SPLIT):
        q4[r4] = q_ref[0, 0, pl.ds(r4, tt // DIL_SPLIT, stride=DIL_SPLIT), :]
        k4[r4] = kbuf[pl.ds(r4, (tt + 2 * H) // DIL_SPLIT, stride=DIL_SPLIT), :]
        v4[r4] = vbuf[pl.ds(r4, (tt + 2 * H) // DIL_SPLIT, stride=DIL_SPLIT), :]
    sub = DIL_SUB
    win = sub + 2 * R
    nc = tt // sub
    cidx = lax.broadcasted_iota(jnp.int32, (nc, win, sub), 0)
    row_k = lax.broadcasted_iota(jnp.int32, (nc, win, sub), 1)
    col_q = lax.broadcasted_iota(jnp.int32, (nc, win, sub), 2)
    rel = row_k - col_q
    band = (rel >= 0) & (rel <= 2 * R)
    lane = _lane_iota((sub, LANES))
    for p, d in enumerate(DIL_DILS):
        n_sub = tt // d // sub
        length = seq // d
        q_starts = [c // n_sub + d * sub * (c % n_sub) for c in range(nc)]
        for c in range(nc):
            r, u = c // n_sub, c % n_sub
            y0 = H // d + u * sub - R
            if d % DIL_SPLIT:
                qf = q_ref[0, 0, pl.ds(q_starts[c], sub, stride=d), :]
                kw = kbuf[pl.ds(r + d * y0, win, stride=d), :]
                vw = vbuf[pl.ds(r + d * y0, win, stride=d), :]
            else:
                e = d // DIL_SPLIT
                r4, m = r % DIL_SPLIT, r // DIL_SPLIT
                qf = q4[r4, pl.ds(m + e * sub * u, sub, stride=e), :]
                kw = k4[r4, pl.ds(m + e * y0, win, stride=e), :]
                vw = v4[r4, pl.ds(m + e * y0, win, stride=e), :]
            qg[c, :sub] = jnp.where(lane < DIL_DIM, qf, 0.0).astype(qg.dtype)
            qg[c, sub:] = jnp.where(lane >= DIL_DIM, qf, 0.0).astype(qg.dtype)
            kg[c] = kw.astype(kg.dtype)
            vg[c] = vw.T.astype(vg.dtype)
        u_of_c = cidx % n_sub if n_sub > 1 else jnp.zeros_like(cidx)
        kpos = i * (tt // d) + u_of_c * sub - R + row_k
        valid = band & (kpos >= 0) & (kpos < length)
        s = lax.dot_general(kg[...], qg[...], (((2,), (2,)), ((0,), (0,))),
                            preferred_element_type=F32)
        s = jnp.where(jnp.concatenate([valid, valid], axis=2), s, NEG)
        m = jnp.max(s, axis=1, keepdims=True)
        e = jnp.exp2(s - m)
        l = jnp.sum(e, axis=1, keepdims=True)
        ot = lax.dot_general(vg[...], e.astype(MXU_DTYPE), (((2,), (1,)), ((0,), (0,))),
                             preferred_element_type=F32) / l
        lse = m + jnp.log2(l)
        for c in range(nc):
            o_t = jnp.concatenate([ot[c, :DIL_DIM, :sub], ot[c, DIL_DIM:, sub:]], axis=0)
            l_t = jnp.concatenate([jnp.broadcast_to(lse[c, :, :sub], (DIL_DIM, sub)),
                                   jnp.broadcast_to(lse[c, :, sub:], (DIL_DIM, sub))], axis=0)
            pat_o[p, pl.ds(q_starts[c], sub, stride=d), :] = o_t.T
            pat_l[p, pl.ds(q_starts[c], sub, stride=d), :] = l_t.T

    l0, l1, l2 = pat_l[0], pat_l[1], pat_l[2]
    mx = jnp.maximum(jnp.maximum(l0, l1), l2)
    e0, e1, e2 = jnp.exp2(l0 - mx), jnp.exp2(l1 - mx), jnp.exp2(l2 - mx)
    den = e0 + e1 + e2
    o_ref[0] = ((e0 / den) * pat_o[0] + (e1 / den) * pat_o[1] + (e2 / den) * pat_o[2]).astype(o_ref.dtype)


def _dil_attn(qb, kb, vb):
    P, B, S, _ = qb.shape
    tt = _tile(S, DIL_TILE)
    H = DIL_HALO
    nh = S // H
    cur = pl.BlockSpec((1, 1, tt, LANES), lambda b, hp, i: (hp, b, i, 0))
    prv = pl.BlockSpec((1, 1, H, LANES), lambda b, hp, i: (hp, b, jnp.maximum(i * (tt // H) - 1, 0), 0))
    nxt = pl.BlockSpec((1, 1, H, LANES), lambda b, hp, i: (hp, b, jnp.minimum((i + 1) * (tt // H), nh - 1), 0))
    return pl.pallas_call(
        functools.partial(_dil_kernel, S),
        grid=(B, P, S // tt),
        in_specs=[cur, prv, cur, nxt, prv, cur, nxt],
        out_specs=pl.BlockSpec((1, tt, LANES), lambda b, hp, i: (b, i, hp)),
        out_shape=jax.ShapeDtypeStruct((B, S, P * LANES), MXU_DTYPE),
        scratch_shapes=[pltpu.VMEM((tt + 2 * H, LANES), F32),
                        pltpu.VMEM((tt + 2 * H, LANES), F32),
                        pltpu.VMEM((len(DIL_DILS), tt, LANES), F32),
                        pltpu.VMEM((len(DIL_DILS), tt, LANES), F32),
                        pltpu.VMEM((tt // DIL_SUB, 2 * DIL_SUB, LANES), MXU_DTYPE),
                        pltpu.VMEM((tt // DIL_SUB, DIL_SUB + 2 * DIL_RADIUS, LANES), MXU_DTYPE),
                        pltpu.VMEM((tt // DIL_SUB, LANES, DIL_SUB + 2 * DIL_RADIUS), MXU_DTYPE),
                        pltpu.VMEM((DIL_SPLIT, tt // DIL_SPLIT, LANES), F32),
                        pltpu.VMEM((DIL_SPLIT, (tt + 2 * H) // DIL_SPLIT, LANES), F32),
                        pltpu.VMEM((DIL_SPLIT, (tt + 2 * H) // DIL_SPLIT, LANES), F32)],
        compiler_params=_cparams(("parallel", "parallel", "parallel")),
        name="dil_attn",
    )(qb, kb, kb, kb, vb, vb, vb)


def _partner(x, k, lane):
    up = pltpu.roll(x, LANES - k, 1)
    dn = pltpu.roll(x, k, 1)
    return jnp.where((lane & k) == 0, up, dn)


GROUP_LANE = N_EXPERTS


def _route(logits, bias, lane):
    valid = lane < N_EXPERTS
    lane_f = lane.astype(F32)
    score = 1.0 / (1.0 + jnp.exp(-logits))
    biased = jnp.where(valid, score + bias, NEG)
    p1 = _partner(biased, 1, lane)
    hi = jnp.maximum(biased, p1)
    lo = jnp.minimum(biased, p1)
    hi_p = _partner(hi, 2, lane)
    lo_p = _partner(lo, 2, lane)
    top1 = jnp.maximum(hi, hi_p)
    top2 = jnp.maximum(jnp.minimum(hi, hi_p), jnp.maximum(lo, lo_p))
    gscore = jnp.where(valid, top1 + top2, NEG)
    gidx = (lane // GROUP_SIZE).astype(F32)
    big = float(LANES)
    gmax = jnp.max(gscore, axis=-1, keepdims=True)
    gsel = jnp.min(jnp.where(gscore == gmax, gidx, big), axis=-1, keepdims=True)
    cand = jnp.where((gidx == gsel) & valid, biased, NEG)
    m1 = jnp.max(cand, axis=-1, keepdims=True)
    i1 = jnp.min(jnp.where(cand == m1, lane_f, big), axis=-1, keepdims=True)
    oh1 = lane_f == i1
    cand2 = jnp.where(oh1, NEG, cand)
    m2 = jnp.max(cand2, axis=-1, keepdims=True)
    i2 = jnp.min(jnp.where((cand2 == m2) & jnp.logical_not(oh1), lane_f, big), axis=-1, keepdims=True)
    oh2 = lane_f == i2
    s1 = jnp.sum(jnp.where(oh1, score, 0.0), axis=-1, keepdims=True)
    s2 = jnp.sum(jnp.where(oh2, score, 0.0), axis=-1, keepdims=True)
    den = s1 + s2
    gate = jnp.where(oh1, s1 / den, 0.0) + jnp.where(oh2, s2 / den, 0.0)
    return jnp.where(lane == GROUP_LANE, gsel, gate)


POST_PARTS = 2


def _post_kernel(x_ref, mod_ref, oa_ref, ob_ref, oc_ref, wo_ref, g_ref, b_ref, wrh_ref, wrl_ref, br_ref,
                 x1_ref, h2_ref, gate_ref):
    na = DIFF_HEADS * DIFF_V
    nb = na + DIL_WIDTH
    tn = (((0,), (0,)), ((), ()))
    g1 = mod_ref[0, 2:3, :]
    sh2 = mod_ref[0, 3:4, :]
    sc2 = mod_ref[0, 4:5, :]
    tm = x_ref.shape[1]
    parts = POST_PARTS if tm % (POST_PARTS * LANES) == 0 else 1
    for r in range(parts):
        rows = slice(r * (tm // parts), (r + 1) * (tm // parts))
        y = (lax.dot_general(oa_ref[0, :, rows], wo_ref[0:na], tn, preferred_element_type=F32)
             + jnp.dot(ob_ref[0, rows, :], wo_ref[na:nb], preferred_element_type=F32)
             + lax.dot_general(oc_ref[0, :, rows], wo_ref[nb:], tn, preferred_element_type=F32))
        x1 = _layer_norm(ALPHA * x_ref[0, rows, :] + g1 * y, g_ref[...], b_ref[...])
        x1_ref[0, rows, :] = x1
        h2 = x1 * (1.0 + sc2) + sh2
        h2_ref[0, rows, :] = h2.astype(h2_ref.dtype)
        h_hi = h2.astype(MXU_DTYPE)
        h_lo = (h2 - h_hi.astype(F32)).astype(MXU_DTYPE)
        logits = (jnp.dot(h_hi, wrh_ref[...], preferred_element_type=F32)
                  + jnp.dot(h_lo, wrh_ref[...], preferred_element_type=F32)
                  + jnp.dot(h_hi, wrl_ref[...], preferred_element_type=F32))
        gate_ref[0, rows, :] = _route(logits, br_ref[...], _lane_iota(logits.shape))


def _post(x, mod, oa, ob, oc, w_out, g, b, w_r_hi, w_r_lo, b_r, tm):
    B, S, D = x.shape
    row = lambda b_, i: (b_, i, 0)
    const2 = lambda b_, i: (0, 0)
    return pl.pallas_call(
        _post_kernel,
        grid=(B, S // tm),
        in_specs=[pl.BlockSpec((1, tm, D), row),
                  pl.BlockSpec((1, 6, D), lambda b_, i: (b_, 0, 0)),
                  pl.BlockSpec((1, oa.shape[1], tm), lambda b_, i: (b_, 0, i)),
                  pl.BlockSpec((1, tm, DIL_WIDTH), row),
                  pl.BlockSpec((1, oc.shape[1], tm), lambda b_, i: (b_, 0, i)),
                  pl.BlockSpec(w_out.shape, const2),
                  pl.BlockSpec((1, D), const2),
                  pl.BlockSpec((1, D), const2),
                  pl.BlockSpec(w_r_hi.shape, const2),
                  pl.BlockSpec(w_r_lo.shape, const2),
                  pl.BlockSpec((1, LANES), const2)],
        out_specs=[pl.BlockSpec((1, tm, D), row),
                   pl.BlockSpec((1, tm, D), row),
                   pl.BlockSpec((1, tm, LANES), row)],
        out_shape=[jax.ShapeDtypeStruct((B, S, D), F32),
                   jax.ShapeDtypeStruct((B, S, D), MXU_DTYPE),
                   jax.ShapeDtypeStruct((B, S, LANES), F32)],
        compiler_params=_cparams(("parallel", "parallel")),
        name="post_mix",
    )(x, mod, oa, ob, oc, w_out, g.reshape(1, D), b.reshape(1, D), w_r_hi, w_r_lo, b_r)


MOE_TM = 1024
MOE_CHUNK = 256
MOE_TAIL = 64
N_GROUPS = N_EXPERTS // GROUP_SIZE


def _moe_kernel(x1_ref, mod_ref, h2_ref, gate_ref, w1_ref, w3_ref, w2_ref, g_ref, b_ref, o_ref,
                acc_scr, src_scr, sel_scr, rank_scr):
    g = pl.program_id(2)
    tm = h2_ref.shape[1]
    d = h2_ref.shape[2]
    C = MOE_CHUNK

    @pl.when(g == 0)
    def _():
        acc_scr[...] = jnp.zeros(acc_scr.shape, F32)
        gate = gate_ref[0]
        gate_hi = gate.astype(MXU_DTYPE)
        src_scr[:, 0:d] = h2_ref[0]
        src_scr[:, d:d + LANES] = gate_hi
        src_scr[:, d + LANES:] = (gate - gate_hi.astype(F32)).astype(MXU_DTYPE)
        gsel_t = gate.T[GROUP_LANE:GROUP_LANE + 1, :]
        grp = lax.broadcasted_iota(jnp.int32, sel_scr.shape, 0).astype(F32)
        sel = jnp.where(gsel_t == grp, 1.0, 0.0)
        r = lax.broadcasted_iota(jnp.int32, (tm, tm), 0)
        c = lax.broadcasted_iota(jnp.int32, (tm, tm), 1)
        before = jnp.where(r < c, 1.0, 0.0).astype(MXU_DTYPE)
        sel_scr[...] = sel
        rank_scr[...] = jnp.dot(sel.astype(MXU_DTYPE), before, preferred_element_type=F32)

    sel = sel_scr[pl.ds(g, 1), :]
    rank = rank_scr[pl.ds(g, 1), :]
    n_g = jnp.sum(sel).astype(jnp.int32)

    def chunk(base, rows):
        lane = _lane_iota((rows, LANES))
        row = lax.broadcasted_iota(jnp.int32, (rows, tm), 0).astype(F32)
        onehot = jnp.where((rank - base.astype(F32) == row) & (sel > 0.5), 1.0, 0.0).astype(MXU_DTYPE)
        picked = jnp.dot(onehot, src_scr[...], preferred_element_type=F32)
        xs = picked[:, 0:d].astype(MXU_DTYPE)
        gs = picked[:, d:d + LANES] + picked[:, d + LANES:]
        ys = jnp.zeros((rows, d), F32)
        for e in range(GROUP_SIZE):
            ge = jnp.sum(jnp.where(lane == g * GROUP_SIZE + e, gs, 0.0), axis=-1, keepdims=True)
            a = jnp.dot(xs, w1_ref[e], preferred_element_type=F32)
            bb = jnp.dot(xs, w3_ref[e], preferred_element_type=F32)
            hid = (a * (1.0 / (1.0 + jnp.exp(-a)))) * bb * ge
            ys = ys + jnp.dot(hid.astype(MXU_DTYPE), w2_ref[e], preferred_element_type=F32)
        acc_scr[...] += lax.dot_general(onehot, ys.astype(MXU_DTYPE), (((0,), (0,)), ((), ())),
                                        preferred_element_type=F32)

    n_chunks = (n_g + C - 1) // C
    tail = n_g - (n_chunks - 1) * C
    merge = (n_chunks >= 2) & (tail <= MOE_TAIL)
    n_plain = jnp.where(merge, n_chunks - 2, n_chunks)

    def plain(ci, carry):
        chunk(ci * C, C)
        return carry

    lax.fori_loop(0, n_plain, plain, 0)

    @pl.when(merge)
    def _():
        chunk(n_plain * C, C + MOE_TAIL)

    @pl.when(g == pl.num_programs(2) - 1)
    def _():
        g2 = mod_ref[0, 5:6, :]
        o_ref[0] = _layer_norm(ALPHA * x1_ref[0] + g2 * acc_scr[...], g_ref[...], b_ref[...])


def _moe(x1, mod, h2, gate, w1, w3, w2, g, b, tm):
    B, S, D = x1.shape
    gs = GROUP_SIZE
    row = lambda b_, i, c: (b_, i, 0)
    const2 = lambda b_, i, c: (0, 0)
    return pl.pallas_call(
        _moe_kernel,
        grid=(B, S // tm, N_GROUPS),
        in_specs=[pl.BlockSpec((1, tm, D), row),
                  pl.BlockSpec((1, 6, D), lambda b_, i, c: (b_, 0, 0)),
                  pl.BlockSpec((1, tm, D), row),
                  pl.BlockSpec((1, tm, LANES), row),
                  pl.BlockSpec((gs, D, D_FF), lambda b_, i, c: (c, 0, 0)),
                  pl.BlockSpec((gs, D, D_FF), lambda b_, i, c: (c, 0, 0)),
                  pl.BlockSpec((gs, D_FF, D), lambda b_, i, c: (c, 0, 0)),
                  pl.BlockSpec((1, D), const2),
                  pl.BlockSpec((1, D), const2)],
        out_specs=pl.BlockSpec((1, tm, D), row),
        out_shape=jax.ShapeDtypeStruct((B, S, D), F32),
        scratch_shapes=[pltpu.VMEM((tm, D), F32),
                        pltpu.VMEM((tm, D + 2 * LANES), MXU_DTYPE),
                        pltpu.VMEM((8, tm), F32),
                        pltpu.VMEM((8, tm), F32)],
        compiler_params=_cparams(("parallel", "parallel", "arbitrary")),
        name="moe_ffn",
    )(x1, mod, h2, gate, w1, w3, w2, g.reshape(1, D), b.reshape(1, D))


def _rope_tables(seq):
    lane = np.arange(LANES)

    def tables(period, offset, rot, theta):
        half = rot // 2
        d = lane % period - offset
        first = (d >= 0) & (d < half)
        second = (d >= half) & (d < rot)
        fidx = np.where(first, d, np.where(second, d - half, 0))
        inv_freq = theta ** (-2.0 * jnp.arange(half, dtype=F32) / rot)
        ang = jnp.arange(seq, dtype=F32)[:, None] * inv_freq[None, :]
        cos_l = jnp.cos(ang)[:, fidx]
        sin_l = jnp.sin(ang)[:, fidx]
        cos = jnp.where((first | second)[None, :], cos_l, 1.0)
        s_up = jnp.where(first[None, :], -sin_l, 0.0)
        s_dn = jnp.where(second[None, :], sin_l, 0.0)
        return [cos, s_up, s_dn]

    tabs = (tables(DIFF_QK, 0, DIFF_QK // 4, ROPE_THETA)
            + tables(DIL_DIM, 0, DIL_DIM // 4, ROPE_THETA)
            + tables(LANES, KR_LANE, MLA_ROPE, MLA_THETA))
    return jnp.stack(tabs).astype(F32)


def _prep_layer(l, w_in, w_uq, w_ukv, w_out, w1, w3, w2):
    D = w_in.shape[1]
    wi = w_in[l]
    zeros = lambda n: jnp.zeros((D, n), wi.dtype)
    wi = jnp.concatenate([wi[:, :C_KR], zeros(KR_LANE), wi[:, C_KR:], zeros(LANES - KR_LANE - MLA_ROPE)], axis=1)
    uq = w_uq[l].reshape(MLA_Q_RANK, MLA_HEADS, MLA_NOPE + MLA_ROPE)
    uq = jnp.pad(uq, ((0, 0), (0, 0), (0, LANES - MLA_NOPE - MLA_ROPE))).reshape(MLA_Q_RANK, MLA_HEADS * LANES)
    ukv = w_ukv[l].reshape(MLA_KV_RANK, MLA_HEADS, MLA_NOPE + MLA_V)
    uk = jnp.pad(ukv[..., :MLA_NOPE], ((0, 0), (0, 0), (0, LANES - MLA_NOPE))).reshape(MLA_KV_RANK, MLA_HEADS * LANES)
    uv = ukv[..., MLA_NOPE:].reshape(MLA_KV_RANK, MLA_HEADS * MLA_V)
    c = lambda t: t.astype(MXU_DTYPE)
    return dict(w_in=c(wi), w_uq=c(uq), w_uk=c(uk), w_uv=c(uv), w_out=c(w_out[l]),
                w1=c(w1[l]), w3=c(w3[l]), w2=c(w2[l]))


def _tile(n, pref):
    return pref if n % pref == 0 else n


def _trunk(x, mods, tabs, layers, p):
    B, S, D = x.shape
    tm = _tile(S, 512)
    tq = _tile(S, ATTN_TQ)
    tk = _tile(S, ATTN_TK)
    x = _ln_in(x, p['ln_in_g'], p['ln_in_b'], tm)
    for l, lw in enumerate(layers):
        mod = mods[l]
        lam_init = 0.8 - 0.6 * math.exp(-0.3 * l)
        qa, ka, va, qb, kb, vb, mq, mk, mv = _proj(
            x, mod, tabs, lw['w_in'], p['g_cq'][l], p['g_ckv'][l], lw['w_uq'], lw['w_uk'], lw['w_uv'], tm)
        oa = _diff_attn(qa, ka, va, p['lam_pack'][l], p['g_sub'][l], lam_init, tq, tk)
        oc = _mla_attn(mq, mk, mv, tq, tk)
        ob = _dil_attn(qb, kb, vb)
        x1, h2, gate = _post(x, mod, oa, ob, oc, lw['w_out'], p['ln1_g'][l], p['ln1_b'][l],
                             p['w_r_hi'], p['w_r_lo'], p['b_r'], tm)
        x = _moe(x1, mod, h2, gate, lw['w1'], lw['w3'], lw['w2'], p['ln2_g'][l], p['ln2_b'][l],
                 _tile(S, MOE_TM))
    return x


def kernel(x_prompt, x_sample, c_prompt, c_sample, ln_in_g, ln_in_b, w_router, b_router, w_ada, b_ada, w_in, lam_q1, lam_k1, lam_q2, lam_k2, g_subln, g_cq, g_ckv, w_uq, w_ukv, w_out, ln1_g, ln1_b, w1, w3, w2, ln2_g, ln2_b):
    depth = w_in.shape[0]
    D = x_prompt.shape[-1]
    bp, bs = c_prompt.shape[0], c_sample.shape[0]
    rows = -(-(bp + bs) // 8) * 8
    c_all = jnp.concatenate([c_prompt, c_sample, jnp.zeros((rows - bp - bs, D), F32)], axis=0)
    mods = _ada(c_all, w_ada, b_ada)
    mods_p = mods[:, :bp].reshape(depth, bp, 6, D)
    mods_s = mods[:, bp:bp + bs].reshape(depth, bs, 6, D)

    pad_l = lambda t: jnp.pad(t, ((0, 0), (0, LANES - t.shape[-1])))
    p = dict(
        ln_in_g=ln_in_g, ln_in_b=ln_in_b, g_cq=g_cq, g_ckv=g_ckv,
        ln1_g=ln1_g, ln1_b=ln1_b, ln2_g=ln2_g, ln2_b=ln2_b,
        lam_pack=jnp.stack([pad_l(lam_q1), pad_l(lam_k1), pad_l(lam_q2), pad_l(lam_k2)], axis=1),
        g_sub=g_subln.reshape(depth, DIFF_V, 1),
        b_r=pad_l(b_router.reshape(1, -1)),
    )
    w_r = pad_l(w_router)
    p['w_r_hi'] = w_r.astype(MXU_DTYPE)
    p['w_r_lo'] = (w_r - p['w_r_hi'].astype(F32)).astype(MXU_DTYPE)
    layers = [_prep_layer(l, w_in, w_uq, w_ukv, w_out, w1, w3, w2) for l in range(depth)]
    tabs = _rope_tables(max(x_prompt.shape[1], x_sample.shape[1]))
    y_prompt = _trunk(x_prompt, mods_p, tabs, layers, p)
    y_sample = _trunk(x_sample, mods_s, tabs, layers, p)
    return (y_prompt, y_sample)
```

```python
import functools
import math

import numpy as np
import jax
import jax.numpy as jnp
from jax import lax
from jax.experimental import pallas as pl
from jax.experimental.pallas import tpu as pltpu

F32 = jnp.float32
MXU_DTYPE = jnp.bfloat16

LANES = 128
DEPTH = 4
DIFF_HEADS = 4
DIFF_QK = 32
DIFF_V = 64
DIL_HEADS = 6
DIL_DIM = 64
DIL_PAIRS = ((128, 1), (512, 4), (2048, 16))
DIL_RADIUS = 64
MLA_HEADS = 6
MLA_NOPE = 64
MLA_ROPE = 32
MLA_V = 64
MLA_Q_RANK = 256
MLA_KV_RANK = 128
MLA_THETA = 10000.0
ROPE_THETA = 500000.0
N_EXPERTS = 16
GROUP_SIZE = 4
D_FF = 256
ALPHA = (2 * DEPTH) ** 0.25
NEG = -1e30
LOG2E = math.log2(math.e)
ATTN_TQ = 512
ATTN_PARTS = 2
ATTN_TK = 2048

C_Q1, C_Q2, C_K1, C_K2, C_VA = 0, 128, 256, 384, 512
C_QB, C_KB, C_VB = 768, 1152, 1536
C_CQ, C_CKV, C_KR, C_END = 1920, 2176, 2304, 2432
KR_LANE = 64

VMEM_LIMIT = 48 * 1024 * 1024


def _cparams(sem):
    return pltpu.CompilerParams(dimension_semantics=sem, vmem_limit_bytes=VMEM_LIMIT)


def _lane_iota(shape):
    return lax.broadcasted_iota(jnp.int32, shape, len(shape) - 1)


def _layer_norm(z, g, b):
    mu = jnp.mean(z, axis=-1, keepdims=True)
    zc = z - mu
    var = jnp.mean(zc * zc, axis=-1, keepdims=True)
    return zc * lax.rsqrt(var + 1e-5) * g + b


def _rope(t, tab_ref, typ, half, rows):
    c = tab_ref[3 * typ, rows, :]
    s_up = tab_ref[3 * typ + 1, rows, :]
    s_dn = tab_ref[3 * typ + 2, rows, :]
    return t * c + pltpu.roll(t, LANES - half, 1) * s_up + pltpu.roll(t, half, 1) * s_dn


def _ln_in_kernel(x_ref, g_ref, b_ref, o_ref):
    o_ref[0] = _layer_norm(x_ref[0], g_ref[...], b_ref[...])


def _ln_in(x, g, b, tm):
    B, S, D = x.shape
    return pl.pallas_call(
        _ln_in_kernel,
        grid=(B, S // tm),
        in_specs=[pl.BlockSpec((1, tm, D), lambda b, i: (b, i, 0)),
                  pl.BlockSpec((1, D), lambda b, i: (0, 0)),
                  pl.BlockSpec((1, D), lambda b, i: (0, 0))],
        out_specs=pl.BlockSpec((1, tm, D), lambda b, i: (b, i, 0)),
        out_shape=jax.ShapeDtypeStruct(x.shape, F32),
        compiler_params=_cparams(("parallel", "parallel")),
        name="ln_in",
    )(x, g.reshape(1, D), b.reshape(1, D))


def _ada_kernel(c_ref, w_ref, b_ref, o_ref):
    c = c_ref[...]
    cs = (c * (1.0 / (1.0 + jnp.exp(-c)))).astype(MXU_DTYPE)
    o_ref[0] = jnp.dot(cs, w_ref[0].astype(MXU_DTYPE), preferred_element_type=F32) + b_ref[0]


def _ada(c, w_ada, b_ada, tn=1536):
    R, D = c.shape
    L, _, N = w_ada.shape
    return pl.pallas_call(
        _ada_kernel,
        grid=(L, N // tn),
        in_specs=[pl.BlockSpec((R, D), lambda l, j: (0, 0)),
                  pl.BlockSpec((1, D, tn), lambda l, j: (l, 0, j)),
                  pl.BlockSpec((1, 1, tn), lambda l, j: (l, 0, j))],
        out_specs=pl.BlockSpec((1, R, tn), lambda l, j: (l, 0, j)),
        out_shape=jax.ShapeDtypeStruct((L, R, N), F32),
        compiler_params=_cparams(("parallel", "parallel")),
        name="ada",
    )(c, w_ada, b_ada.reshape(L, 1, N))


PROJ_PARTS = 2
V_DIM = 64
V_ROWS = V_DIM + 16


def _store_values_t(ref, vt, heads, cols):
    n = vt.shape[1]
    for h in range(heads):
        ref[0, h * V_ROWS:h * V_ROWS + V_DIM, cols] = vt[h * V_DIM:(h + 1) * V_DIM].astype(ref.dtype)
        ref[0, h * V_ROWS + V_DIM:(h + 1) * V_ROWS, cols] = jnp.ones((V_ROWS - V_DIM, n), ref.dtype)


def _proj_kernel(x_ref, mod_ref, tab_ref, w_ref, gcq_ref, gckv_ref, wuq_ref, wuk_ref, wuv_ref,
                 qa_ref, ka_ref, va_ref, qb_ref, kb_ref, vb_ref, mq_ref, mk_ref, mv_ref):
    sh1 = mod_ref[0, 0:1, :]
    sc1 = mod_ref[0, 1:2, :]
    tm = x_ref.shape[1]
    parts = PROJ_PARTS if tm % (PROJ_PARTS * LANES) == 0 else 1

    def rms(t, g):
        return (t * lax.rsqrt(jnp.mean(t * t, axis=-1, keepdims=True) + 1e-6) * g).astype(MXU_DTYPE)

    for r in range(parts):
        rows = slice(r * (tm // parts), (r + 1) * (tm // parts))
        h = (x_ref[0, rows, :] * (1.0 + sc1) + sh1).astype(MXU_DTYPE)

        def mm(a, b, h=h):
            return jnp.dot(h, w_ref[:, a:b], preferred_element_type=F32)

        rope = functools.partial(_rope, tab_ref=tab_ref, rows=rows)

        qa = mm(C_Q1, C_K1)
        ka = mm(C_K1, C_VA)
        sa = DIFF_QK ** -0.5 * LOG2E
        for j in range(2):
            blk = slice(j * LANES, (j + 1) * LANES)
            qa_ref[0, rows, blk] = (rope(qa[:, blk], typ=0, half=DIFF_QK // 8) * sa).astype(qa_ref.dtype)
            ka_ref[0, rows, blk] = rope(ka[:, blk], typ=0, half=DIFF_QK // 8).astype(ka_ref.dtype)
        _store_values_t(va_ref, mm(C_VA, C_QB).T, DIFF_HEADS, rows)

        qb = mm(C_QB, C_KB)
        kb = mm(C_KB, C_VB)
        vb = mm(C_VB, C_CQ)
        sb = DIL_DIM ** -0.5 * LOG2E
        for j in range(DIL_HEADS // 2):
            blk = slice(j * LANES, (j + 1) * LANES)
            qb_ref[j, 0, rows, :] = rope(qb[:, blk], typ=1, half=DIL_DIM // 8) * sb
            kb_ref[j, 0, rows, :] = rope(kb[:, blk], typ=1, half=DIL_DIM // 8)
            vb_ref[j, 0, rows, :] = vb[:, blk]

        cq = rms(mm(C_CQ, C_CKV), gcq_ref[...])
        ckv = rms(mm(C_CKV, C_KR), gckv_ref[...])
        kr = rope(mm(C_KR, C_END), typ=2, half=MLA_ROPE // 2)
        q = jnp.dot(cq, wuq_ref[...], preferred_element_type=F32)
        kn = jnp.dot(ckv, wuk_ref[...], preferred_element_type=F32)
        sc = (MLA_NOPE + MLA_ROPE) ** -0.5 * LOG2E
        for j in range(MLA_HEADS):
            blk = slice(j * LANES, (j + 1) * LANES)
            mq_ref[0, rows, blk] = (rope(q[:, blk], typ=2, half=MLA_ROPE // 2) * sc).astype(mq_ref.dtype)
            mk_ref[0, rows, blk] = (kn[:, blk] + kr).astype(mk_ref.dtype)
        _store_values_t(mv_ref, jnp.dot(ckv, wuv_ref[...], preferred_element_type=F32).T, MLA_HEADS, rows)


def _proj(x, mod, tabs, w_in, g_cq, g_ckv, w_uq, w_uk, w_uv, tm):
    B, S, D = x.shape
    widths = (256, 256, DIFF_HEADS * V_ROWS, 384, 384, 384, 768, 768, MLA_HEADS * V_ROWS)
    transposed = (2, 8)
    row = lambda b, i: (b, i, 0)
    col = lambda b, i: (b, 0, i)
    const2 = lambda b, i: (0, 0)
    out_specs = [pl.BlockSpec((1, w, tm), col) if n in transposed else pl.BlockSpec((1, tm, w), row)
                 for n, w in enumerate(widths)]
    out_shape = [jax.ShapeDtypeStruct((B, w, S) if n in transposed else (B, S, w), MXU_DTYPE)
                 for n, w in enumerate(widths)]
    for n in (3, 4, 5):
        out_specs[n] = pl.BlockSpec((DIL_HEADS // 2, 1, tm, LANES), lambda b, i: (0, b, i, 0))
        out_shape[n] = jax.ShapeDtypeStruct((DIL_HEADS // 2, B, S, LANES), F32)
    return pl.pallas_call(
        _proj_kernel,
        grid=(B, S // tm),
        in_specs=[pl.BlockSpec((1, tm, D), row),
                  pl.BlockSpec((1, 6, D), lambda b, i: (b, 0, 0)),
                  pl.BlockSpec((9, tm, LANES), lambda b, i: (0, i, 0)),
                  pl.BlockSpec(w_in.shape, const2),
                  pl.BlockSpec((1, MLA_Q_RANK), const2),
                  pl.BlockSpec((1, MLA_KV_RANK), const2),
                  pl.BlockSpec(w_uq.shape, const2),
                  pl.BlockSpec(w_uk.shape, const2),
                  pl.BlockSpec(w_uv.shape, const2)],
        out_specs=out_specs,
        out_shape=out_shape,
        compiler_params=_cparams(("parallel", "parallel")),
        name="proj_in",
    )(x, mod, tabs, w_in, g_cq.reshape(1, -1), g_ckv.reshape(1, -1), w_uq, w_uk, w_uv)


def _softmax_steps(chains, m_scr, acc_scr, base=0):
    scores = [lax.dot_general(k, q, (((1,), (1,)), ((), ())), preferred_element_type=F32)
              for q, k, _ in chains]
    for u, (st, (_, _, vt)) in enumerate(zip(scores, chains), start=base):
        m_prev = m_scr[u]
        m_new = jnp.maximum(m_prev, jnp.max(st, axis=0, keepdims=True))
        alpha = jnp.exp2(m_prev - m_new)
        p = jnp.exp2(st - m_new)
        acc_scr[u] = alpha * acc_scr[u] + jnp.dot(vt, p.astype(MXU_DTYPE), preferred_element_type=F32)
        m_scr[u] = m_new


def _init_softmax(m_scr, acc_scr):
    m_scr[...] = jnp.full(m_scr.shape, NEG, F32)
    acc_scr[...] = jnp.zeros(acc_scr.shape, F32)


def _softmax_out(acc_scr, u):
    return acc_scr[u, :V_DIM] / acc_scr[u, V_DIM:V_DIM + 1]


def _diff_attn_kernel(lam_init, q1_ref, q2_ref, k1_ref, k2_ref, vt_ref, lamp_ref, g_ref, o_ref,
                      q_scr, m_scr, acc_scr):
    j = pl.program_id(1)
    ik = pl.program_id(3)

    t = m_scr.shape[2]
    parts = q1_ref.shape[1] // t

    @pl.when(ik == 0)
    def _():
        _init_softmax(m_scr, acc_scr)
        lane = _lane_iota((t, LANES))
        for r in range(parts):
            rows = slice(r * t, (r + 1) * t)
            for hh in range(2):
                h = 2 * j + hh
                msk = (lane >= h * DIFF_QK) & (lane < (h + 1) * DIFF_QK)
                q_scr[4 * r + 2 * hh] = jnp.where(msk, q1_ref[0, rows, :].astype(F32), 0.0).astype(q_scr.dtype)
                q_scr[4 * r + 2 * hh + 1] = jnp.where(msk, q2_ref[0, rows, :].astype(F32), 0.0).astype(q_scr.dtype)

    for r in range(parts):
        chains = []
        for hh in range(2):
            vt = vt_ref[0, hh * V_ROWS:(hh + 1) * V_ROWS, :]
            chains.append((q_scr[4 * r + 2 * hh], k1_ref[0], vt))
            chains.append((q_scr[4 * r + 2 * hh + 1], k2_ref[0], vt))
        _softmax_steps(chains, m_scr, acc_scr, 4 * r)

    @pl.when(ik == pl.num_programs(3) - 1)
    def _():
        lp = lamp_ref[...]
        lam = (jnp.exp(jnp.sum(lp[0:1] * lp[1:2], axis=-1, keepdims=True))
               - jnp.exp(jnp.sum(lp[2:3] * lp[3:4], axis=-1, keepdims=True)) + lam_init)
        for r in range(parts):
            for hh in range(2):
                u = 4 * r + 2 * hh
                o = _softmax_out(acc_scr, u) - lam * _softmax_out(acc_scr, u + 1)
                ms = jnp.mean(o * o, axis=0, keepdims=True)
                o_ref[0, hh * DIFF_V:(hh + 1) * DIFF_V, r * t:(r + 1) * t] = (
                    o * lax.rsqrt(ms + 1e-6) * g_ref[...] * (1.0 - lam_init)).astype(o_ref.dtype)


def _diff_attn(qa, ka, vat, lam_pack, g_sub, lam_init, tq, tk):
    B, S, _ = qa.shape
    parts = tq // ATTN_TQ if tq % ATTN_TQ == 0 else 1
    kern = functools.partial(_diff_attn_kernel, lam_init)
    return pl.pallas_call(
        kern,
        grid=(B, DIFF_HEADS // 2, S // tq, S // tk),
        in_specs=[pl.BlockSpec((1, tq, LANES), lambda b, j, iq, ik: (b, iq, 0)),
                  pl.BlockSpec((1, tq, LANES), lambda b, j, iq, ik: (b, iq, 1)),
                  pl.BlockSpec((1, tk, LANES), lambda b, j, iq, ik: (b, ik, 0)),
                  pl.BlockSpec((1, tk, LANES), lambda b, j, iq, ik: (b, ik, 1)),
                  pl.BlockSpec((1, 2 * V_ROWS, tk), lambda b, j, iq, ik: (b, j, ik)),
                  pl.BlockSpec((4, LANES), lambda b, j, iq, ik: (0, 0)),
                  pl.BlockSpec((DIFF_V, 1), lambda b, j, iq, ik: (0, 0))],
        out_specs=pl.BlockSpec((1, 2 * DIFF_V, tq), lambda b, j, iq, ik: (b, j, iq)),
        out_shape=jax.ShapeDtypeStruct((B, DIFF_HEADS * DIFF_V, S), MXU_DTYPE),
        scratch_shapes=[pltpu.VMEM((4 * parts, tq // parts, LANES), MXU_DTYPE),
                        pltpu.VMEM((4 * parts, 1, tq // parts), F32),
                        pltpu.VMEM((4 * parts, V_ROWS, tq // parts), F32)],
        compiler_params=_cparams(("parallel", "parallel", "parallel", "arbitrary")),
        name="diff_attn",
    )(qa, qa, ka, ka, vat, lam_pack, g_sub)


MLA_GROUP = 3


def _mla_attn_kernel(q_ref, k_ref, vt_ref, o_ref, m_scr, acc_scr):
    ik = pl.program_id(3)

    @pl.when(ik == 0)
    def _():
        _init_softmax(m_scr, acc_scr)

    t = m_scr.shape[2]
    parts = q_ref.shape[1] // t
    for r in range(parts):
        rows = slice(r * t, (r + 1) * t)
        chains = []
        for hh in range(MLA_GROUP):
            blk = slice(hh * LANES, (hh + 1) * LANES)
            chains.append((q_ref[0, rows, blk], k_ref[0, :, blk], vt_ref[0, hh * V_ROWS:(hh + 1) * V_ROWS, :]))
        _softmax_steps(chains, m_scr, acc_scr, r * MLA_GROUP)

    @pl.when(ik == pl.num_programs(3) - 1)
    def _():
        for r in range(parts):
            for hh in range(MLA_GROUP):
                o_ref[0, hh * MLA_V:(hh + 1) * MLA_V, r * t:(r + 1) * t] = _softmax_out(
                    acc_scr, r * MLA_GROUP + hh).astype(o_ref.dtype)


def _mla_attn(mq, mk, mvt, tq, tk):
    B, S, _ = mq.shape
    g = MLA_GROUP
    parts = tq // ATTN_TQ if tq % ATTN_TQ == 0 else 1
    return pl.pallas_call(
        _mla_attn_kernel,
        grid=(B, MLA_HEADS // g, S // tq, S // tk),
        in_specs=[pl.BlockSpec((1, tq, g * LANES), lambda b, j, iq, ik: (b, iq, j)),
                  pl.BlockSpec((1, tk, g * LANES), lambda b, j, iq, ik: (b, ik, j)),
                  pl.BlockSpec((1, g * V_ROWS, tk), lambda b, j, iq, ik: (b, j, ik))],
        out_specs=pl.BlockSpec((1, g * MLA_V, tq), lambda b, j, iq, ik: (b, j, iq)),
        out_shape=jax.ShapeDtypeStruct((B, MLA_HEADS * MLA_V, S), MXU_DTYPE),
        scratch_shapes=[pltpu.VMEM((g * parts, 1, tq // parts), F32),
                        pltpu.VMEM((g * parts, V_ROWS, tq // parts), F32)],
        compiler_params=_cparams(("parallel", "parallel", "parallel", "arbitrary")),
        name="mla_attn",
    )(mq, mk, mvt)


DIL_SUB = 128
DIL_WIDTH = DIL_HEADS * DIL_DIM


DIL_DILS = tuple(d for _, d in DIL_PAIRS)
DIL_HALO = DIL_RADIUS * DIL_DILS[-1]
DIL_TILE = 2048
DIL_SPLIT = 4


def _dil_kernel(seq, q_ref, kp_ref, kc_ref, kn_ref, vp_ref, vc_ref, vn_ref, o_ref,
                kbuf, vbuf, pat_o, pat_l, qg, kg, vg, q4, k4, v4):
    i = pl.program_id(2)
    tt = q_ref.shape[2]
    H = DIL_HALO
    R = DIL_RADIUS
    kbuf[0:H] = kp_ref[0, 0]
    kbuf[H:H + tt] = kc_ref[0, 0]
    kbuf[H + tt:] = kn_ref[0, 0]
    vbuf[0:H] = vp_ref[0, 0]
    vbuf[H:H + tt] = vc_ref[0, 0]
    vbuf[H + tt:] = vn_ref[0, 0]
    vg[:, LANES:, :] = jnp.ones((vg.shape[0], vg.shape[1] - LANES, vg.shape[2]), vg.dtype)
    for r4 in range(DIL_SPLIT):
        q4[r4] = q_ref[0, 0, pl.ds(r4, tt // DIL_SPLIT, stride=DIL_SPLIT), :]
        k4[r4] = kbuf[pl.ds(r4, (tt + 2 * H) // DIL_SPLIT, stride=DIL_SPLIT), :]
        v4[r4] = vbuf[pl.ds(r4, (tt + 2 * H) // DIL_SPLIT, stride=DIL_SPLIT), :]
    sub = DIL_SUB
    win = sub + 2 * R
    nc = tt // sub
    cidx = lax.broadcasted_iota(jnp.int32, (nc, win, sub), 0)
    row_k = lax.broadcasted_iota(jnp.int32, (nc, win, sub), 1)
    col_q = lax.broadcasted_iota(jnp.int32, (nc, win, sub), 2)
    rel = row_k - col_q
    band = (rel >= 0) & (rel <= 2 * R)
    lane = _lane_iota((sub, LANES))
    for p, d in enumerate(DIL_DILS):
        n_sub = tt // d // sub
        length = seq // d
        q_starts = [c // n_sub + d * sub * (c % n_sub) for c in range(nc)]
        for c in range(nc):
            r, u = c // n_sub, c % n_sub
            y0 = H // d + u * sub - R
            if d % DIL_SPLIT:
                qf = q_ref[0, 0, pl.ds(q_starts[c], sub, stride=d), :]
                kw = kbuf[pl.ds(r + d * y0, win, stride=d), :]
                vw = vbuf[pl.ds(r + d * y0, win, stride=d), :]
            else:
                e = d // DIL_SPLIT
                r4, m = r % DIL_SPLIT, r // DIL_SPLIT
                qf = q4[r4, pl.ds(m + e * sub * u, sub, stride=e), :]
                kw = k4[r4, pl.ds(m + e * y0, win, stride=e), :]
                vw = v4[r4, pl.ds(m + e * y0, win, stride=e), :]
            qg[c, :sub] = jnp.where(lane < DIL_DIM, qf, 0.0).astype(qg.dtype)
            qg[c, sub:] = jnp.where(lane >= DIL_DIM, qf, 0.0).astype(qg.dtype)
            kg[c] = kw.astype(kg.dtype)
            vg[c, :LANES] = vw.T.astype(vg.dtype)
        u_of_c = cidx % n_sub if n_sub > 1 else jnp.zeros_like(cidx)
        kpos = i * (tt // d) + u_of_c * sub - R + row_k
        valid = band & (kpos >= 0) & (kpos < length)
        s = lax.dot_general(kg[...], qg[...], (((2,), (2,)), ((0,), (0,))),
                            preferred_element_type=F32)
        s = jnp.where(jnp.concatenate([valid, valid], axis=2), s, NEG)
        m = jnp.max(s, axis=1, keepdims=True)
        e = jnp.exp2(s - m)
        ot = lax.dot_general(vg[...], e.astype(MXU_DTYPE), (((2,), (1,)), ((0,), (0,))),
                             preferred_element_type=F32)
        l = ot[:, LANES:LANES + 1, :]
        ot = ot[:, :LANES, :] / l
        lse = m + jnp.log2(l)
        for c in range(nc):
            o_t = jnp.concatenate([ot[c, :DIL_DIM, :sub], ot[c, DIL_DIM:, sub:]], axis=0)
            l_t = jnp.concatenate([jnp.broadcast_to(lse[c, :, :sub], (DIL_DIM, sub)),
                                   jnp.broadcast_to(lse[c, :, sub:], (DIL_DIM, sub))], axis=0)
            pat_o[p, pl.ds(q_starts[c], sub, stride=d), :] = o_t.T
            pat_l[p, pl.ds(q_starts[c], sub, stride=d), :] = l_t.T

    l0, l1, l2 = pat_l[0], pat_l[1], pat_l[2]
    mx = jnp.maximum(jnp.maximum(l0, l1), l2)
    e0, e1, e2 = jnp.exp2(l0 - mx), jnp.exp2(l1 - mx), jnp.exp2(l2 - mx)
    den = e0 + e1 + e2
    o_ref[0] = ((e0 / den) * pat_o[0] + (e1 / den) * pat_o[1] + (e2 / den) * pat_o[2]).astype(o_ref.dtype)


def _dil_attn(qb, kb, vb):
    P, B, S, _ = qb.shape
    tt = _tile(S, DIL_TILE)
    H = DIL_HALO
    nh = S // H
    cur = pl.BlockSpec((1, 1, tt, LANES), lambda b, hp, i: (hp, b, i, 0))
    prv = pl.BlockSpec((1, 1, H, LANES), lambda b, hp, i: (hp, b, jnp.maximum(i * (tt // H) - 1, 0), 0))
    nxt = pl.BlockSpec((1, 1, H, LANES), lambda b, hp, i: (hp, b, jnp.minimum((i + 1) * (tt // H), nh - 1), 0))
    return pl.pallas_call(
        functools.partial(_dil_kernel, S),
        grid=(B, P, S // tt),
        in_specs=[cur, prv, cur, nxt, prv, cur, nxt],
        out_specs=pl.BlockSpec((1, tt, LANES), lambda b, hp, i: (b, i, hp)),
        out_shape=jax.ShapeDtypeStruct((B, S, P * LANES), MXU_DTYPE),
        scratch_shapes=[pltpu.VMEM((tt + 2 * H, LANES), F32),
                        pltpu.VMEM((tt + 2 * H, LANES), F32),
                        pltpu.VMEM((len(DIL_DILS), tt, LANES), F32),
                        pltpu.VMEM((len(DIL_DILS), tt, LANES), F32),
                        pltpu.VMEM((tt // DIL_SUB, 2 * DIL_SUB, LANES), MXU_DTYPE),
                        pltpu.VMEM((tt // DIL_SUB, DIL_SUB + 2 * DIL_RADIUS, LANES), MXU_DTYPE),
                        pltpu.VMEM((tt // DIL_SUB, LANES + 16, DIL_SUB + 2 * DIL_RADIUS), MXU_DTYPE),
                        pltpu.VMEM((DIL_SPLIT, tt // DIL_SPLIT, LANES), F32),
                        pltpu.VMEM((DIL_SPLIT, (tt + 2 * H) // DIL_SPLIT, LANES), F32),
                        pltpu.VMEM((DIL_SPLIT, (tt + 2 * H) // DIL_SPLIT, LANES), F32)],
        compiler_params=_cparams(("parallel", "parallel", "parallel")),
        name="dil_attn",
    )(qb, kb, kb, kb, vb, vb, vb)


def _partner(x, k, lane):
    up = pltpu.roll(x, LANES - k, 1)
    dn = pltpu.roll(x, k, 1)
    return jnp.where((lane & k) == 0, up, dn)


GROUP_LANE = N_EXPERTS


def _route(logits, bias, lane):
    valid = lane < N_EXPERTS
    lane_f = lane.astype(F32)
    score = 1.0 / (1.0 + jnp.exp(-logits))
    biased = jnp.where(valid, score + bias, NEG)
    p1 = _partner(biased, 1, lane)
    hi = jnp.maximum(biased, p1)
    lo = jnp.minimum(biased, p1)
    hi_p = _partner(hi, 2, lane)
    lo_p = _partner(lo, 2, lane)
    top1 = jnp.maximum(hi, hi_p)
    top2 = jnp.maximum(jnp.minimum(hi, hi_p), jnp.maximum(lo, lo_p))
    gscore = jnp.where(valid, top1 + top2, NEG)
    gidx = (lane // GROUP_SIZE).astype(F32)
    big = float(LANES)
    gmax = jnp.max(gscore, axis=-1, keepdims=True)
    gsel = jnp.min(jnp.where(gscore == gmax, gidx, big), axis=-1, keepdims=True)
    cand = jnp.where((gidx == gsel) & valid, biased, NEG)
    m1 = jnp.max(cand, axis=-1, keepdims=True)
    i1 = jnp.min(jnp.where(cand == m1, lane_f, big), axis=-1, keepdims=True)
    oh1 = lane_f == i1
    cand2 = jnp.where(oh1, NEG, cand)
    m2 = jnp.max(cand2, axis=-1, keepdims=True)
    i2 = jnp.min(jnp.where((cand2 == m2) & jnp.logical_not(oh1), lane_f, big), axis=-1, keepdims=True)
    oh2 = lane_f == i2
    s1 = jnp.sum(jnp.where(oh1, score, 0.0), axis=-1, keepdims=True)
    s2 = jnp.sum(jnp.where(oh2, score, 0.0), axis=-1, keepdims=True)
    den = s1 + s2
    gate = jnp.where(oh1, s1 / den, 0.0) + jnp.where(oh2, s2 / den, 0.0)
    return jnp.where(lane == GROUP_LANE, gsel, gate)


POST_PARTS = 2


def _post_kernel(x_ref, mod_ref, oa_ref, ob_ref, oc_ref, wo_ref, g_ref, b_ref, wrh_ref, wrl_ref, br_ref,
                 x1_ref, h2_ref, gate_ref):
    na = DIFF_HEADS * DIFF_V
    nb = na + DIL_WIDTH
    tn = (((0,), (0,)), ((), ()))
    g1 = mod_ref[0, 2:3, :]
    sh2 = mod_ref[0, 3:4, :]
    sc2 = mod_ref[0, 4:5, :]
    tm = x_ref.shape[1]
    parts = POST_PARTS if tm % (POST_PARTS * LANES) == 0 else 1
    for r in range(parts):
        rows = slice(r * (tm // parts), (r + 1) * (tm // parts))
        y = (lax.dot_general(oa_ref[0, :, rows], wo_ref[0:na], tn, preferred_element_type=F32)
             + jnp.dot(ob_ref[0, rows, :], wo_ref[na:nb], preferred_element_type=F32)
             + lax.dot_general(oc_ref[0, :, rows], wo_ref[nb:], tn, preferred_element_type=F32))
        x1 = _layer_norm(ALPHA * x_ref[0, rows, :] + g1 * y, g_ref[...], b_ref[...])
        x1_ref[0, rows, :] = x1
        h2 = x1 * (1.0 + sc2) + sh2
        h2_ref[0, rows, :] = h2.astype(h2_ref.dtype)
        h_hi = h2.astype(MXU_DTYPE)
        h_lo = (h2 - h_hi.astype(F32)).astype(MXU_DTYPE)
        logits = (jnp.dot(h_hi, wrh_ref[...], preferred_element_type=F32)
                  + jnp.dot(h_lo, wrh_ref[...], preferred_element_type=F32)
                  + jnp.dot(h_hi, wrl_ref[...], preferred_element_type=F32))
        gate_ref[0, rows, :] = _route(logits, br_ref[...], _lane_iota(logits.shape))


def _post(x, mod, oa, ob, oc, w_out, g, b, w_r_hi, w_r_lo, b_r, tm):
    B, S, D = x.shape
    row = lambda b_, i: (b_, i, 0)
    const2 = lambda b_, i: (0, 0)
    return pl.pallas_call(
        _post_kernel,
        grid=(B, S // tm),
        in_specs=[pl.BlockSpec((1, tm, D), row),
                  pl.BlockSpec((1, 6, D), lambda b_, i: (b_, 0, 0)),
                  pl.BlockSpec((1, oa.shape[1], tm), lambda b_, i: (b_, 0, i)),
                  pl.BlockSpec((1, tm, DIL_WIDTH), row),
                  pl.BlockSpec((1, oc.shape[1], tm), lambda b_, i: (b_, 0, i)),
                  pl.BlockSpec(w_out.shape, const2),
                  pl.BlockSpec((1, D), const2),
                  pl.BlockSpec((1, D), const2),
                  pl.BlockSpec(w_r_hi.shape, const2),
                  pl.BlockSpec(w_r_lo.shape, const2),
                  pl.BlockSpec((1, LANES), const2)],
        out_specs=[pl.BlockSpec((1, tm, D), row),
                   pl.BlockSpec((1, tm, D), row),
                   pl.BlockSpec((1, tm, LANES), row)],
        out_shape=[jax.ShapeDtypeStruct((B, S, D), F32),
                   jax.ShapeDtypeStruct((B, S, D), MXU_DTYPE),
                   jax.ShapeDtypeStruct((B, S, LANES), F32)],
        compiler_params=_cparams(("parallel", "parallel")),
        name="post_mix",
    )(x, mod, oa, ob, oc, w_out, g.reshape(1, D), b.reshape(1, D), w_r_hi, w_r_lo, b_r)


MOE_TM = 1024
MOE_CHUNK = 256
MOE_TAIL = 64
N_GROUPS = N_EXPERTS // GROUP_SIZE


def _moe_kernel(x1_ref, mod_ref, h2_ref, gate_ref, w1_ref, w3_ref, w2_ref, g_ref, b_ref, o_ref,
                acc_scr, src_scr, sel_scr, rank_scr):
    g = pl.program_id(2)
    tm = h2_ref.shape[1]
    d = h2_ref.shape[2]
    C = MOE_CHUNK

    @pl.when(g == 0)
    def _():
        acc_scr[...] = jnp.zeros(acc_scr.shape, F32)
        gate = gate_ref[0]
        gate_hi = gate.astype(MXU_DTYPE)
        src_scr[:, 0:d] = h2_ref[0]
        src_scr[:, d:d + LANES] = gate_hi
        src_scr[:, d + LANES:] = (gate - gate_hi.astype(F32)).astype(MXU_DTYPE)
        gsel_t = gate.T[GROUP_LANE:GROUP_LANE + 1, :]
        grp = lax.broadcasted_iota(jnp.int32, sel_scr.shape, 0).astype(F32)
        sel = jnp.where(gsel_t == grp, 1.0, 0.0)
        r = lax.broadcasted_iota(jnp.int32, (tm, tm), 0)
        c = lax.broadcasted_iota(jnp.int32, (tm, tm), 1)
        before = jnp.where(r < c, 1.0, 0.0).astype(MXU_DTYPE)
        sel_scr[...] = sel
        rank_scr[...] = jnp.dot(sel.astype(MXU_DTYPE), before, preferred_element_type=F32)

    sel = sel_scr[pl.ds(g, 1), :]
    rank = rank_scr[pl.ds(g, 1), :]
    n_g = jnp.sum(sel).astype(jnp.int32)

    def chunk(base, rows):
        lane = _lane_iota((rows, LANES))
        row = lax.broadcasted_iota(jnp.int32, (rows, tm), 0).astype(F32)
        onehot = jnp.where((rank - base.astype(F32) == row) & (sel > 0.5), 1.0, 0.0).astype(MXU_DTYPE)
        picked = jnp.dot(onehot, src_scr[...], preferred_element_type=F32)
        xs = picked[:, 0:d].astype(MXU_DTYPE)
        gs = picked[:, d:d + LANES] + picked[:, d + LANES:]
        ys = jnp.zeros((rows, d), F32)
        for e in range(GROUP_SIZE):
            ge = jnp.sum(jnp.where(lane == g * GROUP_SIZE + e, gs, 0.0), axis=-1, keepdims=True)
            a = jnp.dot(xs, w1_ref[e], preferred_element_type=F32)
            bb = jnp.dot(xs, w3_ref[e], preferred_element_type=F32)
            hid = (a * (1.0 / (1.0 + jnp.exp(-a)))) * bb * ge
            ys = ys + jnp.dot(hid.astype(MXU_DTYPE), w2_ref[e], preferred_element_type=F32)
        acc_scr[...] += lax.dot_general(onehot, ys.astype(MXU_DTYPE), (((0,), (0,)), ((), ())),
                                        preferred_element_type=F32)

    n_chunks = (n_g + C - 1) // C
    tail = n_g - (n_chunks - 1) * C
    merge = (n_chunks >= 2) & (tail <= MOE_TAIL)
    n_plain = jnp.where(merge, n_chunks - 2, n_chunks)

    def plain(ci, carry):
        chunk(ci * C, C)
        return carry

    lax.fori_loop(0, n_plain, plain, 0)

    @pl.when(merge)
    def _():
        chunk(n_plain * C, C + MOE_TAIL)

    @pl.when(g == pl.num_programs(2) - 1)
    def _():
        g2 = mod_ref[0, 5:6, :]
        o_ref[0] = _layer_norm(ALPHA * x1_ref[0] + g2 * acc_scr[...], g_ref[...], b_ref[...])


def _moe(x1, mod, h2, gate, w1, w3, w2, g, b, tm):
    B, S, D = x1.shape
    gs = GROUP_SIZE
    row = lambda b_, i, c: (b_, i, 0)
    const2 = lambda b_, i, c: (0, 0)
    return pl.pallas_call(
        _moe_kernel,
        grid=(B, S // tm, N_GROUPS),
        in_specs=[pl.BlockSpec((1, tm, D), row),
                  pl.BlockSpec((1, 6, D), lambda b_, i, c: (b_, 0, 0)),
                  pl.BlockSpec((1, tm, D), row),
                  pl.BlockSpec((1, tm, LANES), row),
                  pl.BlockSpec((gs, D, D_FF), lambda b_, i, c: (c, 0, 0)),
                  pl.BlockSpec((gs, D, D_FF), lambda b_, i, c: (c, 0, 0)),
                  pl.BlockSpec((gs, D_FF, D), lambda b_, i, c: (c, 0, 0)),
                  pl.BlockSpec((1, D), const2),
                  pl.BlockSpec((1, D), const2)],
        out_specs=pl.BlockSpec((1, tm, D), row),
        out_shape=jax.ShapeDtypeStruct((B, S, D), F32),
        scratch_shapes=[pltpu.VMEM((tm, D), F32),
                        pltpu.VMEM((tm, D + 2 * LANES), MXU_DTYPE),
                        pltpu.VMEM((8, tm), F32),
                        pltpu.VMEM((8, tm), F32)],
        compiler_params=_cparams(("parallel", "parallel", "arbitrary")),
        name="moe_ffn",
    )(x1, mod, h2, gate, w1, w3, w2, g.reshape(1, D), b.reshape(1, D))


def _rope_tables(seq):
    lane = np.arange(LANES)

    def tables(period, offset, rot, theta):
        half = rot // 2
        d = lane % period - offset
        first = (d >= 0) & (d < half)
        second = (d >= half) & (d < rot)
        fidx = np.where(first, d, np.where(second, d - half, 0))
        inv_freq = theta ** (-2.0 * jnp.arange(half, dtype=F32) / rot)
        ang = jnp.arange(seq, dtype=F32)[:, None] * inv_freq[None, :]
        cos_l = jnp.cos(ang)[:, fidx]
        sin_l = jnp.sin(ang)[:, fidx]
        cos = jnp.where((first | second)[None, :], cos_l, 1.0)
        s_up = jnp.where(first[None, :], -sin_l, 0.0)
        s_dn = jnp.where(second[None, :], sin_l, 0.0)
        return [cos, s_up, s_dn]

    tabs = (tables(DIFF_QK, 0, DIFF_QK // 4, ROPE_THETA)
            + tables(DIL_DIM, 0, DIL_DIM // 4, ROPE_THETA)
            + tables(LANES, KR_LANE, MLA_ROPE, MLA_THETA))
    return jnp.stack(tabs).astype(F32)


def _prep_layer(l, w_in, w_uq, w_ukv, w_out, w1, w3, w2):
    D = w_in.shape[1]
    wi = w_in[l]
    zeros = lambda n: jnp.zeros((D, n), wi.dtype)
    wi = jnp.concatenate([wi[:, :C_KR], zeros(KR_LANE), wi[:, C_KR:], zeros(LANES - KR_LANE - MLA_ROPE)], axis=1)
    uq = w_uq[l].reshape(MLA_Q_RANK, MLA_HEADS, MLA_NOPE + MLA_ROPE)
    uq = jnp.pad(uq, ((0, 0), (0, 0), (0, LANES - MLA_NOPE - MLA_ROPE))).reshape(MLA_Q_RANK, MLA_HEADS * LANES)
    ukv = w_ukv[l].reshape(MLA_KV_RANK, MLA_HEADS, MLA_NOPE + MLA_V)
    uk = jnp.pad(ukv[..., :MLA_NOPE], ((0, 0), (0, 0), (0, LANES - MLA_NOPE))).reshape(MLA_KV_RANK, MLA_HEADS * LANES)
    uv = ukv[..., MLA_NOPE:].reshape(MLA_KV_RANK, MLA_HEADS * MLA_V)
    c = lambda t: t.astype(MXU_DTYPE)
    return dict(w_in=c(wi), w_uq=c(uq), w_uk=c(uk), w_uv=c(uv), w_out=c(w_out[l]),
                w1=c(w1[l]), w3=c(w3[l]), w2=c(w2[l]))


def _tile(n, pref):
    return pref if n % pref == 0 else n


def _trunk(x, mods, tabs, layers, p):
    B, S, D = x.shape
    tm = _tile(S, 512)
    tq = _tile(S, ATTN_TQ * ATTN_PARTS)
    tk = _tile(S, ATTN_TK)
    x = _ln_in(x, p['ln_in_g'], p['ln_in_b'], tm)
    for l, lw in enumerate(layers):
        mod = mods[l]
        lam_init = 0.8 - 0.6 * math.exp(-0.3 * l)
        qa, ka, va, qb, kb, vb, mq, mk, mv = _proj(
            x, mod, tabs, lw['w_in'], p['g_cq'][l], p['g_ckv'][l], lw['w_uq'], lw['w_uk'], lw['w_uv'], tm)
        oa = _diff_attn(qa, ka, va, p['lam_pack'][l], p['g_sub'][l], lam_init, tq, tk)
        oc = _mla_attn(mq, mk, mv, tq, tk)
        ob = _dil_attn(qb, kb, vb)
        x1, h2, gate = _post(x, mod, oa, ob, oc, lw['w_out'], p['ln1_g'][l], p['ln1_b'][l],
                             p['w_r_hi'], p['w_r_lo'], p['b_r'], tm)
        x = _moe(x1, mod, h2, gate, lw['w1'], lw['w3'], lw['w2'], p['ln2_g'][l], p['ln2_b'][l],
                 _tile(S, MOE_TM))
    return x


def kernel(x_prompt, x_sample, c_prompt, c_sample, ln_in_g, ln_in_b, w_router, b_router, w_ada, b_ada, w_in, lam_q1, lam_k1, lam_q2, lam_k2, g_subln, g_cq, g_ckv, w_uq, w_ukv, w_out, ln1_g, ln1_b, w1, w3, w2, ln2_g, ln2_b):
    depth = w_in.shape[0]
    D = x_prompt.shape[-1]
    bp, bs = c_prompt.shape[0], c_sample.shape[0]
    rows = -(-(bp + bs) // 8) * 8
    c_all = jnp.concatenate([c_prompt, c_sample, jnp.zeros((rows - bp - bs, D), F32)], axis=0)
    mods = _ada(c_all, w_ada, b_ada)
    mods_p = mods[:, :bp].reshape(depth, bp, 6, D)
    mods_s = mods[:, bp:bp + bs].reshape(depth, bs, 6, D)

    pad_l = lambda t: jnp.pad(t, ((0, 0), (0, LANES - t.shape[-1])))
    p = dict(
        ln_in_g=ln_in_g, ln_in_b=ln_in_b, g_cq=g_cq, g_ckv=g_ckv,
        ln1_g=ln1_g, ln1_b=ln1_b, ln2_g=ln2_g, ln2_b=ln2_b,
        lam_pack=jnp.stack([pad_l(lam_q1), pad_l(lam_k1), pad_l(lam_q2), pad_l(lam_k2)], axis=1),
        g_sub=g_subln.reshape(depth, DIFF_V, 1),
        b_r=pad_l(b_router.reshape(1, -1)),
    )
    w_r = pad_l(w_router)
    p['w_r_hi'] = w_r.astype(MXU_DTYPE)
    p['w_r_lo'] = (w_r - p['w_r_hi'].astype(F32)).astype(MXU_DTYPE)
    layers = [_prep_layer(l, w_in, w_uq, w_ukv, w_out, w1, w3, w2) for l in range(depth)]
    tabs = _rope_tables(max(x_prompt.shape[1], x_sample.shape[1]))
    y_prompt = _trunk(x_prompt, mods_p, tabs, layers, p)
    y_sample = _trunk(x_sample, mods_s, tabs, layers, p)
    return (y_prompt, y_sample)
```

```python
import functools
import math

import numpy as np
import jax
import jax.numpy as jnp
from jax import lax
from jax.experimental import pallas as pl
from jax.experimental.pallas import tpu as pltpu

F32 = jnp.float32
MXU_DTYPE = jnp.bfloat16

LANES = 128
DEPTH = 4
DIFF_HEADS = 4
DIFF_QK = 32
DIFF_V = 64
DIL_HEADS = 6
DIL_DIM = 64
DIL_PAIRS = ((128, 1), (512, 4), (2048, 16))
DIL_RADIUS = 64
MLA_HEADS = 6
MLA_NOPE = 64
MLA_ROPE = 32
MLA_V = 64
MLA_Q_RANK = 256
MLA_KV_RANK = 128
MLA_THETA = 10000.0
ROPE_THETA = 500000.0
N_EXPERTS = 16
GROUP_SIZE = 4
D_FF = 256
ALPHA = (2 * DEPTH) ** 0.25
NEG = -1e30
LOG2E = math.log2(math.e)
ATTN_TQ = 512
ATTN_PARTS = 4
SMALL_ROWS = 32
ATTN_TK = 2048

C_Q1, C_Q2, C_K1, C_K2, C_VA = 0, 128, 256, 384, 512
C_QB, C_KB, C_VB = 768, 1152, 1536
C_CQ, C_CKV, C_KR, C_END = 1920, 2176, 2304, 2432
KR_LANE = 64

VMEM_LIMIT = 48 * 1024 * 1024


def _cparams(sem):
    return pltpu.CompilerParams(dimension_semantics=sem, vmem_limit_bytes=VMEM_LIMIT)


def _lane_iota(shape):
    return lax.broadcasted_iota(jnp.int32, shape, len(shape) - 1)


def _layer_norm(z, g, b):
    mu = jnp.mean(z, axis=-1, keepdims=True)
    zc = z - mu
    var = jnp.mean(zc * zc, axis=-1, keepdims=True)
    return zc * lax.rsqrt(var + 1e-5) * g + b


def _rope(t, tab_ref, typ, half, rows):
    c = tab_ref[3 * typ, rows, :]
    s_up = tab_ref[3 * typ + 1, rows, :]
    s_dn = tab_ref[3 * typ + 2, rows, :]
    return t * c + pltpu.roll(t, LANES - half, 1) * s_up + pltpu.roll(t, half, 1) * s_dn


def _ln_in_kernel(x_ref, g_ref, b_ref, o_ref):
    o_ref[0] = _layer_norm(x_ref[0], g_ref[...], b_ref[...])


def _ln_in(x, g, b, tm):
    B, S, D = x.shape
    return pl.pallas_call(
        _ln_in_kernel,
        grid=(B, S // tm),
        in_specs=[pl.BlockSpec((1, tm, D), lambda b, i: (b, i, 0)),
                  pl.BlockSpec((1, D), lambda b, i: (0, 0)),
                  pl.BlockSpec((1, D), lambda b, i: (0, 0))],
        out_specs=pl.BlockSpec((1, tm, D), lambda b, i: (b, i, 0)),
        out_shape=jax.ShapeDtypeStruct(x.shape, F32),
        compiler_params=_cparams(("parallel", "parallel")),
        name="ln_in",
    )(x, g.reshape(1, D), b.reshape(1, D))


def _ada_kernel(c_ref, w_ref, b_ref, o_ref):
    c = c_ref[...]
    cs = (c * (1.0 / (1.0 + jnp.exp(-c)))).astype(MXU_DTYPE)
    o_ref[0] = jnp.dot(cs, w_ref[0].astype(MXU_DTYPE), preferred_element_type=F32) + b_ref[0]


def _ada(c, w_ada, b_ada, tn=1536):
    R, D = c.shape
    L, _, N = w_ada.shape
    return pl.pallas_call(
        _ada_kernel,
        grid=(L, N // tn),
        in_specs=[pl.BlockSpec((R, D), lambda l, j: (0, 0)),
                  pl.BlockSpec((1, D, tn), lambda l, j: (l, 0, j)),
                  pl.BlockSpec((1, 1, tn), lambda l, j: (l, 0, j))],
        out_specs=pl.BlockSpec((1, R, tn), lambda l, j: (l, 0, j)),
        out_shape=jax.ShapeDtypeStruct((L, R, N), F32),
        compiler_params=_cparams(("parallel", "parallel")),
        name="ada",
    )(c, w_ada, b_ada.reshape(L, 1, N))


PROJ_PARTS = 2
V_DIM = 64
V_ROWS = V_DIM + 16


def _store_values_t(ref, vt, heads, cols):
    n = vt.shape[1]
    for h in range(heads):
        ref[0, h * V_ROWS:h * V_ROWS + V_DIM, cols] = vt[h * V_DIM:(h + 1) * V_DIM].astype(ref.dtype)
        ref[0, h * V_ROWS + V_DIM:(h + 1) * V_ROWS, cols] = jnp.ones((V_ROWS - V_DIM, n), ref.dtype)


def _proj_kernel(x_ref, mod_ref, tab_ref, w_ref, gcq_ref, gckv_ref, wuq_ref, wuk_ref, wuv_ref,
                 qa_ref, ka_ref, va_ref, qb_ref, kb_ref, vb_ref, mq_ref, mk_ref, mv_ref):
    sh1 = mod_ref[0, 0:1, :]
    sc1 = mod_ref[0, 1:2, :]
    tm = x_ref.shape[1]
    parts = PROJ_PARTS if tm % (PROJ_PARTS * LANES) == 0 else 1

    def rms(t, g):
        return (t * lax.rsqrt(jnp.mean(t * t, axis=-1, keepdims=True) + 1e-6) * g).astype(MXU_DTYPE)

    for r in range(parts):
        rows = slice(r * (tm // parts), (r + 1) * (tm // parts))
        h = (x_ref[0, rows, :] * (1.0 + sc1) + sh1).astype(MXU_DTYPE)

        def mm(a, b, h=h):
            return jnp.dot(h, w_ref[:, a:b], preferred_element_type=F32)

        rope = functools.partial(_rope, tab_ref=tab_ref, rows=rows)

        qa = mm(C_Q1, C_K1)
        ka = mm(C_K1, C_VA)
        sa = DIFF_QK ** -0.5 * LOG2E
        for j in range(2):
            blk = slice(j * LANES, (j + 1) * LANES)
            qa_ref[0, rows, blk] = (rope(qa[:, blk], typ=0, half=DIFF_QK // 8) * sa).astype(qa_ref.dtype)
            ka_ref[0, rows, blk] = rope(ka[:, blk], typ=0, half=DIFF_QK // 8).astype(ka_ref.dtype)
        _store_values_t(va_ref, mm(C_VA, C_QB).T, DIFF_HEADS, rows)

        qb = mm(C_QB, C_KB)
        kb = mm(C_KB, C_VB)
        vb = mm(C_VB, C_CQ)
        sb = DIL_DIM ** -0.5 * LOG2E
        for j in range(DIL_HEADS // 2):
            blk = slice(j * LANES, (j + 1) * LANES)
            qb_ref[j, 0, rows, :] = rope(qb[:, blk], typ=1, half=DIL_DIM // 8) * sb
            kb_ref[j, 0, rows, :] = rope(kb[:, blk], typ=1, half=DIL_DIM // 8)
            vb_ref[j, 0, rows, :] = vb[:, blk]

        cq = rms(mm(C_CQ, C_CKV), gcq_ref[...])
        ckv = rms(mm(C_CKV, C_KR), gckv_ref[0:1, :])
        kr = rope(mm(C_KR, C_END), typ=2, half=MLA_ROPE // 2)
        q = jnp.dot(cq, wuq_ref[...], preferred_element_type=F32)
        kn = jnp.dot(ckv, wuk_ref[...], preferred_element_type=F32)
        sc = (MLA_NOPE + MLA_ROPE) ** -0.5 * LOG2E
        for j in range(MLA_HEADS):
            blk = slice(j * LANES, (j + 1) * LANES)
            mq_ref[0, rows, blk] = (rope(q[:, blk], typ=2, half=MLA_ROPE // 2) * sc).astype(mq_ref.dtype)
            mk_ref[0, rows, blk] = (kn[:, blk] + kr).astype(mk_ref.dtype)
        _store_values_t(mv_ref, jnp.dot(ckv, wuv_ref[...], preferred_element_type=F32).T, MLA_HEADS, rows)


def _proj(x, mod, tabs, w_in, g_cq, g_ckv, w_uq, w_uk, w_uv, tm):
    B, S, D = x.shape
    widths = (256, 256, DIFF_HEADS * V_ROWS, 384, 384, 384, 768, 768, MLA_HEADS * V_ROWS)
    transposed = (2, 8)
    row = lambda b, i: (b, i, 0)
    col = lambda b, i: (b, 0, i)
    const2 = lambda b, i: (0, 0)
    out_specs = [pl.BlockSpec((1, w, tm), col) if n in transposed else pl.BlockSpec((1, tm, w), row)
                 for n, w in enumerate(widths)]
    out_shape = [jax.ShapeDtypeStruct((B, w, S) if n in transposed else (B, S, w), MXU_DTYPE)
                 for n, w in enumerate(widths)]
    for n in (3, 4, 5):
        out_specs[n] = pl.BlockSpec((DIL_HEADS // 2, 1, tm, LANES), lambda b, i: (0, b, i, 0))
        out_shape[n] = jax.ShapeDtypeStruct((DIL_HEADS // 2, B, S, LANES), F32)
    return pl.pallas_call(
        _proj_kernel,
        grid=(B, S // tm),
        in_specs=[pl.BlockSpec((1, tm, D), row),
                  pl.BlockSpec((1, 6, D), lambda b, i: (b, 0, 0)),
                  pl.BlockSpec((9, tm, LANES), lambda b, i: (0, i, 0)),
                  pl.BlockSpec(w_in.shape, const2),
                  pl.BlockSpec((1, MLA_Q_RANK), const2),
                  pl.BlockSpec((SMALL_ROWS, MLA_KV_RANK), const2),
                  pl.BlockSpec(w_uq.shape, const2),
                  pl.BlockSpec(w_uk.shape, const2),
                  pl.BlockSpec(w_uv.shape, const2)],
        out_specs=out_specs,
        out_shape=out_shape,
        compiler_params=_cparams(("parallel", "parallel")),
        name="proj_in",
    )(x, mod, tabs, w_in, g_cq.reshape(1, -1), g_ckv, w_uq, w_uk, w_uv)


def _softmax_steps(chains, m_scr, acc_scr, base=0):
    scores = [lax.dot_general(k, q, (((1,), (1,)), ((), ())), preferred_element_type=F32)
              for q, k, _ in chains]
    for u, (st, (_, _, vt)) in enumerate(zip(scores, chains), start=base):
        m_prev = m_scr[u]
        m_new = jnp.maximum(m_prev, jnp.max(st, axis=0, keepdims=True))
        alpha = jnp.exp2(m_prev - m_new)
        p = jnp.exp2(st - m_new)
        acc_scr[u] = alpha * acc_scr[u] + jnp.dot(vt, p.astype(MXU_DTYPE), preferred_element_type=F32)
        m_scr[u] = m_new


def _init_softmax(m_scr, acc_scr):
    m_scr[...] = jnp.full(m_scr.shape, NEG, F32)
    acc_scr[...] = jnp.zeros(acc_scr.shape, F32)


def _softmax_out(acc_scr, u):
    return acc_scr[u, :V_DIM] / acc_scr[u, V_DIM:V_DIM + 1]


def _diff_attn_kernel(lam_init, q1_ref, q2_ref, k1_ref, k2_ref, vt_ref, lamp_ref, g_ref, o_ref,
                      q_scr, m_scr, acc_scr):
    j = pl.program_id(1)
    ik = pl.program_id(3)

    t = m_scr.shape[2]
    parts = q1_ref.shape[1] // t

    @pl.when(ik == 0)
    def _():
        _init_softmax(m_scr, acc_scr)
        lane = _lane_iota((t, LANES))
        for r in range(parts):
            rows = slice(r * t, (r + 1) * t)
            for hh in range(2):
                h = 2 * j + hh
                msk = (lane >= h * DIFF_QK) & (lane < (h + 1) * DIFF_QK)
                q_scr[4 * r + 2 * hh] = jnp.where(msk, q1_ref[0, rows, :].astype(F32), 0.0).astype(q_scr.dtype)
                q_scr[4 * r + 2 * hh + 1] = jnp.where(msk, q2_ref[0, rows, :].astype(F32), 0.0).astype(q_scr.dtype)

    for r in range(parts):
        chains = []
        for hh in range(2):
            vt = vt_ref[0, hh * V_ROWS:(hh + 1) * V_ROWS, :]
            chains.append((q_scr[4 * r + 2 * hh], k1_ref[0], vt))
            chains.append((q_scr[4 * r + 2 * hh + 1], k2_ref[0], vt))
        _softmax_steps(chains, m_scr, acc_scr, 4 * r)

    @pl.when(ik == pl.num_programs(3) - 1)
    def _():
        lp = lamp_ref[0:4, :]
        lam = (jnp.exp(jnp.sum(lp[0:1] * lp[1:2], axis=-1, keepdims=True))
               - jnp.exp(jnp.sum(lp[2:3] * lp[3:4], axis=-1, keepdims=True)) + lam_init)
        for r in range(parts):
            for hh in range(2):
                u = 4 * r + 2 * hh
                o = _softmax_out(acc_scr, u) - lam * _softmax_out(acc_scr, u + 1)
                ms = jnp.mean(o * o, axis=0, keepdims=True)
                o_ref[0, hh * DIFF_V:(hh + 1) * DIFF_V, r * t:(r + 1) * t] = (
                    o * lax.rsqrt(ms + 1e-6) * g_ref[...] * (1.0 - lam_init)).astype(o_ref.dtype)


def _diff_attn(qa, ka, vat, lam_pack, g_sub, lam_init, tq, tk):
    B, S, _ = qa.shape
    parts = tq // ATTN_TQ if tq % ATTN_TQ == 0 else 1
    kern = functools.partial(_diff_attn_kernel, lam_init)
    return pl.pallas_call(
        kern,
        grid=(B, DIFF_HEADS // 2, S // tq, S // tk),
        in_specs=[pl.BlockSpec((1, tq, LANES), lambda b, j, iq, ik: (b, iq, 0)),
                  pl.BlockSpec((1, tq, LANES), lambda b, j, iq, ik: (b, iq, 1)),
                  pl.BlockSpec((1, tk, LANES), lambda b, j, iq, ik: (b, ik, 0)),
                  pl.BlockSpec((1, tk, LANES), lambda b, j, iq, ik: (b, ik, 1)),
                  pl.BlockSpec((1, 2 * V_ROWS, tk), lambda b, j, iq, ik: (b, j, ik)),
                  pl.BlockSpec((SMALL_ROWS, LANES), lambda b, j, iq, ik: (0, 0)),
                  pl.BlockSpec((DIFF_V, 1), lambda b, j, iq, ik: (0, 0))],
        out_specs=pl.BlockSpec((1, 2 * DIFF_V, tq), lambda b, j, iq, ik: (b, j, iq)),
        out_shape=jax.ShapeDtypeStruct((B, DIFF_HEADS * DIFF_V, S), MXU_DTYPE),
        scratch_shapes=[pltpu.VMEM((4 * parts, tq // parts, LANES), MXU_DTYPE),
                        pltpu.VMEM((4 * parts, 1, tq // parts), F32),
                        pltpu.VMEM((4 * parts, V_ROWS, tq // parts), F32)],
        compiler_params=_cparams(("parallel", "parallel", "parallel", "arbitrary")),
        name="diff_attn",
    )(qa, qa, ka, ka, vat, lam_pack, g_sub)


MLA_GROUP = 3


def _mla_attn_kernel(q_ref, k_ref, vt_ref, o_ref, m_scr, acc_scr):
    ik = pl.program_id(3)

    @pl.when(ik == 0)
    def _():
        _init_softmax(m_scr, acc_scr)

    t = m_scr.shape[2]
    parts = q_ref.shape[1] // t
    for r in range(parts):
        rows = slice(r * t, (r + 1) * t)
        chains = []
        for hh in range(MLA_GROUP):
            blk = slice(hh * LANES, (hh + 1) * LANES)
            chains.append((q_ref[0, rows, blk], k_ref[0, :, blk], vt_ref[0, hh * V_ROWS:(hh + 1) * V_ROWS, :]))
        _softmax_steps(chains, m_scr, acc_scr, r * MLA_GROUP)

    @pl.when(ik == pl.num_programs(3) - 1)
    def _():
        for r in range(parts):
            for hh in range(MLA_GROUP):
                o_ref[0, hh * MLA_V:(hh + 1) * MLA_V, r * t:(r + 1) * t] = _softmax_out(
                    acc_scr, r * MLA_GROUP + hh).astype(o_ref.dtype)


def _mla_attn(mq, mk, mvt, tq, tk):
    B, S, _ = mq.shape
    g = MLA_GROUP
    parts = tq // ATTN_TQ if tq % ATTN_TQ == 0 else 1
    return pl.pallas_call(
        _mla_attn_kernel,
        grid=(B, MLA_HEADS // g, S // tq, S // tk),
        in_specs=[pl.BlockSpec((1, tq, g * LANES), lambda b, j, iq, ik: (b, iq, j)),
                  pl.BlockSpec((1, tk, g * LANES), lambda b, j, iq, ik: (b, ik, j)),
                  pl.BlockSpec((1, g * V_ROWS, tk), lambda b, j, iq, ik: (b, j, ik))],
        out_specs=pl.BlockSpec((1, g * MLA_V, tq), lambda b, j, iq, ik: (b, j, iq)),
        out_shape=jax.ShapeDtypeStruct((B, MLA_HEADS * MLA_V, S), MXU_DTYPE),
        scratch_shapes=[pltpu.VMEM((g * parts, 1, tq // parts), F32),
                        pltpu.VMEM((g * parts, V_ROWS, tq // parts), F32)],
        compiler_params=_cparams(("parallel", "parallel", "parallel", "arbitrary")),
        name="mla_attn",
    )(mq, mk, mvt)


DIL_SUB = 128
DIL_WIDTH = DIL_HEADS * DIL_DIM


DIL_DILS = tuple(d for _, d in DIL_PAIRS)
DIL_HALO = DIL_RADIUS * DIL_DILS[-1]
DIL_TILE = 2048
DIL_SPLIT = 4


def _dil_kernel(seq, q_ref, kp_ref, kc_ref, kn_ref, vp_ref, vc_ref, vn_ref, o_ref,
                kbuf, vbuf, pat_o, pat_l, qg, kg, vg, q4, k4, v4):
    i = pl.program_id(2)
    tt = q_ref.shape[2]
    H = DIL_HALO
    R = DIL_RADIUS
    kbuf[0:H] = kp_ref[0, 0]
    kbuf[H:H + tt] = kc_ref[0, 0]
    kbuf[H + tt:] = kn_ref[0, 0]
    vbuf[0:H] = vp_ref[0, 0]
    vbuf[H:H + tt] = vc_ref[0, 0]
    vbuf[H + tt:] = vn_ref[0, 0]
    vg[:, LANES:, :] = jnp.ones((vg.shape[0], vg.shape[1] - LANES, vg.shape[2]), vg.dtype)
    for r4 in range(DIL_SPLIT):
        q4[r4] = q_ref[0, 0, pl.ds(r4, tt // DIL_SPLIT, stride=DIL_SPLIT), :]
        k4[r4] = kbuf[pl.ds(r4, (tt + 2 * H) // DIL_SPLIT, stride=DIL_SPLIT), :]
        v4[r4] = vbuf[pl.ds(r4, (tt + 2 * H) // DIL_SPLIT, stride=DIL_SPLIT), :]
    sub = DIL_SUB
    win = sub + 2 * R
    nc = tt // sub
    cidx = lax.broadcasted_iota(jnp.int32, (nc, win, sub), 0)
    row_k = lax.broadcasted_iota(jnp.int32, (nc, win, sub), 1)
    col_q = lax.broadcasted_iota(jnp.int32, (nc, win, sub), 2)
    rel = row_k - col_q
    band = (rel >= 0) & (rel <= 2 * R)
    lane = _lane_iota((sub, LANES))
    for p, d in enumerate(DIL_DILS):
        n_sub = tt // d // sub
        length = seq // d
        q_starts = [c // n_sub + d * sub * (c % n_sub) for c in range(nc)]
        for c in range(nc):
            r, u = c // n_sub, c % n_sub
            y0 = H // d + u * sub - R
            if d % DIL_SPLIT:
                qf = q_ref[0, 0, pl.ds(q_starts[c], sub, stride=d), :]
                kw = kbuf[pl.ds(r + d * y0, win, stride=d), :]
                vw = vbuf[pl.ds(r + d * y0, win, stride=d), :]
            else:
                e = d // DIL_SPLIT
                r4, m = r % DIL_SPLIT, r // DIL_SPLIT
                qf = q4[r4, pl.ds(m + e * sub * u, sub, stride=e), :]
                kw = k4[r4, pl.ds(m + e * y0, win, stride=e), :]
                vw = v4[r4, pl.ds(m + e * y0, win, stride=e), :]
            qg[c, :sub] = jnp.where(lane < DIL_DIM, qf, 0.0).astype(qg.dtype)
            qg[c, sub:] = jnp.where(lane >= DIL_DIM, qf, 0.0).astype(qg.dtype)
            kg[c] = kw.astype(kg.dtype)
            vg[c, :LANES] = vw.T.astype(vg.dtype)
        u_of_c = cidx % n_sub if n_sub > 1 else jnp.zeros_like(cidx)
        kpos = i * (tt // d) + u_of_c * sub - R + row_k
        valid = band & (kpos >= 0) & (kpos < length)
        s = lax.dot_general(kg[...], qg[...], (((2,), (2,)), ((0,), (0,))),
                            preferred_element_type=F32)
        s = jnp.where(jnp.concatenate([valid, valid], axis=2), s, NEG)
        m = jnp.max(s, axis=1, keepdims=True)
        e = jnp.exp2(s - m)
        ot = lax.dot_general(vg[...], e.astype(MXU_DTYPE), (((2,), (1,)), ((0,), (0,))),
                             preferred_element_type=F32)
        l = ot[:, LANES:LANES + 1, :]
        ot = ot[:, :LANES, :] / l
        lse = m + jnp.log2(l)
        for c in range(nc):
            o_t = jnp.concatenate([ot[c, :DIL_DIM, :sub], ot[c, DIL_DIM:, sub:]], axis=0)
            l_t = jnp.concatenate([jnp.broadcast_to(lse[c, :, :sub], (DIL_DIM, sub)),
                                   jnp.broadcast_to(lse[c, :, sub:], (DIL_DIM, sub))], axis=0)
            pat_o[p, pl.ds(q_starts[c], sub, stride=d), :] = o_t.T
            pat_l[p, pl.ds(q_starts[c], sub, stride=d), :] = l_t.T

    l0, l1, l2 = pat_l[0], pat_l[1], pat_l[2]
    mx = jnp.maximum(jnp.maximum(l0, l1), l2)
    e0, e1, e2 = jnp.exp2(l0 - mx), jnp.exp2(l1 - mx), jnp.exp2(l2 - mx)
    den = e0 + e1 + e2
    o_ref[0] = ((e0 / den) * pat_o[0] + (e1 / den) * pat_o[1] + (e2 / den) * pat_o[2]).astype(o_ref.dtype)


def _dil_attn(qb, kb, vb):
    P, B, S, _ = qb.shape
    tt = _tile(S, DIL_TILE)
    H = DIL_HALO
    nh = S // H
    cur = pl.BlockSpec((1, 1, tt, LANES), lambda b, hp, i: (hp, b, i, 0))
    prv = pl.BlockSpec((1, 1, H, LANES), lambda b, hp, i: (hp, b, jnp.maximum(i * (tt // H) - 1, 0), 0))
    nxt = pl.BlockSpec((1, 1, H, LANES), lambda b, hp, i: (hp, b, jnp.minimum((i + 1) * (tt // H), nh - 1), 0))
    return pl.pallas_call(
        functools.partial(_dil_kernel, S),
        grid=(B, P, S // tt),
        in_specs=[cur, prv, cur, nxt, prv, cur, nxt],
        out_specs=pl.BlockSpec((1, tt, LANES), lambda b, hp, i: (b, i, hp)),
        out_shape=jax.ShapeDtypeStruct((B, S, P * LANES), MXU_DTYPE),
        scratch_shapes=[pltpu.VMEM((tt + 2 * H, LANES), F32),
                        pltpu.VMEM((tt + 2 * H, LANES), F32),
                        pltpu.VMEM((len(DIL_DILS), tt, LANES), F32),
                        pltpu.VMEM((len(DIL_DILS), tt, LANES), F32),
                        pltpu.VMEM((tt // DIL_SUB, 2 * DIL_SUB, LANES), MXU_DTYPE),
                        pltpu.VMEM((tt // DIL_SUB, DIL_SUB + 2 * DIL_RADIUS, LANES), MXU_DTYPE),
                        pltpu.VMEM((tt // DIL_SUB, LANES + 16, DIL_SUB + 2 * DIL_RADIUS), MXU_DTYPE),
                        pltpu.VMEM((DIL_SPLIT, tt // DIL_SPLIT, LANES), F32),
                        pltpu.VMEM((DIL_SPLIT, (tt + 2 * H) // DIL_SPLIT, LANES), F32),
                        pltpu.VMEM((DIL_SPLIT, (tt + 2 * H) // DIL_SPLIT, LANES), F32)],
        compiler_params=_cparams(("parallel", "parallel", "parallel")),
        name="dil_attn",
    )(qb, kb, kb, kb, vb, vb, vb)


def _partner(x, k, lane):
    up = pltpu.roll(x, LANES - k, 1)
    dn = pltpu.roll(x, k, 1)
    return jnp.where((lane & k) == 0, up, dn)


GROUP_LANE = N_EXPERTS


def _route(logits, bias, lane):
    valid = lane < N_EXPERTS
    lane_f = lane.astype(F32)
    score = 1.0 / (1.0 + jnp.exp(-logits))
    biased = jnp.where(valid, score + bias, NEG)
    p1 = _partner(biased, 1, lane)
    hi = jnp.maximum(biased, p1)
    lo = jnp.minimum(biased, p1)
    hi_p = _partner(hi, 2, lane)
    lo_p = _partner(lo, 2, lane)
    top1 = jnp.maximum(hi, hi_p)
    top2 = jnp.maximum(jnp.minimum(hi, hi_p), jnp.maximum(lo, lo_p))
    gscore = jnp.where(valid, top1 + top2, NEG)
    gidx = (lane // GROUP_SIZE).astype(F32)
    big = float(LANES)
    gmax = jnp.max(gscore, axis=-1, keepdims=True)
    gsel = jnp.min(jnp.where(gscore == gmax, gidx, big), axis=-1, keepdims=True)
    cand = jnp.where((gidx == gsel) & valid, biased, NEG)
    m1 = jnp.max(cand, axis=-1, keepdims=True)
    i1 = jnp.min(jnp.where(cand == m1, lane_f, big), axis=-1, keepdims=True)
    oh1 = lane_f == i1
    cand2 = jnp.where(oh1, NEG, cand)
    m2 = jnp.max(cand2, axis=-1, keepdims=True)
    i2 = jnp.min(jnp.where((cand2 == m2) & jnp.logical_not(oh1), lane_f, big), axis=-1, keepdims=True)
    oh2 = lane_f == i2
    s1 = jnp.sum(jnp.where(oh1, score, 0.0), axis=-1, keepdims=True)
    s2 = jnp.sum(jnp.where(oh2, score, 0.0), axis=-1, keepdims=True)
    den = s1 + s2
    gate = jnp.where(oh1, s1 / den, 0.0) + jnp.where(oh2, s2 / den, 0.0)
    return jnp.where(lane == GROUP_LANE, gsel, gate)


POST_PARTS = 2


def _post_kernel(x_ref, mod_ref, oa_ref, ob_ref, oc_ref, wo_ref, g_ref, b_ref, wrh_ref, wrl_ref, br_ref,
                 x1_ref, h2_ref, gate_ref):
    na = DIFF_HEADS * DIFF_V
    nb = na + DIL_WIDTH
    tn = (((0,), (0,)), ((), ()))
    g1 = mod_ref[0, 2:3, :]
    sh2 = mod_ref[0, 3:4, :]
    sc2 = mod_ref[0, 4:5, :]
    tm = x_ref.shape[1]
    parts = POST_PARTS if tm % (POST_PARTS * LANES) == 0 else 1
    for r in range(parts):
        rows = slice(r * (tm // parts), (r + 1) * (tm // parts))
        y = (lax.dot_general(oa_ref[0, :, rows], wo_ref[0:na], tn, preferred_element_type=F32)
             + jnp.dot(ob_ref[0, rows, :], wo_ref[na:nb], preferred_element_type=F32)
             + lax.dot_general(oc_ref[0, :, rows], wo_ref[nb:], tn, preferred_element_type=F32))
        x1 = _layer_norm(ALPHA * x_ref[0, rows, :] + g1 * y, g_ref[...], b_ref[...])
        x1_ref[0, rows, :] = x1
        h2 = x1 * (1.0 + sc2) + sh2
        h2_ref[0, rows, :] = h2.astype(h2_ref.dtype)
        h_hi = h2.astype(MXU_DTYPE)
        h_lo = (h2 - h_hi.astype(F32)).astype(MXU_DTYPE)
        logits = (jnp.dot(h_hi, wrh_ref[...], preferred_element_type=F32)
                  + jnp.dot(h_lo, wrh_ref[...], preferred_element_type=F32)
                  + jnp.dot(h_hi, wrl_ref[...], preferred_element_type=F32))
        gate_ref[0, rows, :] = _route(logits, br_ref[0:1, :], _lane_iota(logits.shape))


def _post(x, mod, oa, ob, oc, w_out, g, b, w_r_hi, w_r_lo, b_r, tm):
    B, S, D = x.shape
    row = lambda b_, i: (b_, i, 0)
    const2 = lambda b_, i: (0, 0)
    return pl.pallas_call(
        _post_kernel,
        grid=(B, S // tm),
        in_specs=[pl.BlockSpec((1, tm, D), row),
                  pl.BlockSpec((1, 6, D), lambda b_, i: (b_, 0, 0)),
                  pl.BlockSpec((1, oa.shape[1], tm), lambda b_, i: (b_, 0, i)),
                  pl.BlockSpec((1, tm, DIL_WIDTH), row),
                  pl.BlockSpec((1, oc.shape[1], tm), lambda b_, i: (b_, 0, i)),
                  pl.BlockSpec(w_out.shape, const2),
                  pl.BlockSpec((1, D), const2),
                  pl.BlockSpec((1, D), const2),
                  pl.BlockSpec(w_r_hi.shape, const2),
                  pl.BlockSpec(w_r_lo.shape, const2),
                  pl.BlockSpec((SMALL_ROWS, LANES), const2)],
        out_specs=[pl.BlockSpec((1, tm, D), row),
                   pl.BlockSpec((1, tm, D), row),
                   pl.BlockSpec((1, tm, LANES), row)],
        out_shape=[jax.ShapeDtypeStruct((B, S, D), F32),
                   jax.ShapeDtypeStruct((B, S, D), MXU_DTYPE),
                   jax.ShapeDtypeStruct((B, S, LANES), F32)],
        compiler_params=_cparams(("parallel", "parallel")),
        name="post_mix",
    )(x, mod, oa, ob, oc, w_out, g.reshape(1, D), b.reshape(1, D), w_r_hi, w_r_lo, b_r)


MOE_TM = 1024
MOE_CHUNK = 256
MOE_TAIL = 64
N_GROUPS = N_EXPERTS // GROUP_SIZE


def _moe_kernel(x1_ref, mod_ref, h2_ref, gate_ref, w1_ref, w3_ref, w2_ref, g_ref, b_ref, o_ref,
                acc_scr, src_scr, sel_scr, rank_scr):
    g = pl.program_id(2)
    tm = h2_ref.shape[1]
    d = h2_ref.shape[2]
    C = MOE_CHUNK

    @pl.when(g == 0)
    def _():
        acc_scr[...] = jnp.zeros(acc_scr.shape, F32)
        gate = gate_ref[0]
        gate_hi = gate.astype(MXU_DTYPE)
        src_scr[:, 0:d] = h2_ref[0]
        src_scr[:, d:d + LANES] = gate_hi
        src_scr[:, d + LANES:] = (gate - gate_hi.astype(F32)).astype(MXU_DTYPE)
        gsel_t = gate.T[GROUP_LANE:GROUP_LANE + 1, :]
        grp = lax.broadcasted_iota(jnp.int32, sel_scr.shape, 0).astype(F32)
        sel = jnp.where(gsel_t == grp, 1.0, 0.0)
        r = lax.broadcasted_iota(jnp.int32, (tm, tm), 0)
        c = lax.broadcasted_iota(jnp.int32, (tm, tm), 1)
        before = jnp.where(r < c, 1.0, 0.0).astype(MXU_DTYPE)
        sel_scr[...] = sel
        rank_scr[...] = jnp.dot(sel.astype(MXU_DTYPE), before, preferred_element_type=F32)

    sel = sel_scr[pl.ds(g, 1), :]
    rank = rank_scr[pl.ds(g, 1), :]
    n_g = jnp.sum(sel).astype(jnp.int32)

    def chunk(base, rows):
        lane = _lane_iota((rows, LANES))
        row = lax.broadcasted_iota(jnp.int32, (rows, tm), 0).astype(F32)
        onehot = jnp.where((rank - base.astype(F32) == row) & (sel > 0.5), 1.0, 0.0).astype(MXU_DTYPE)
        picked = jnp.dot(onehot, src_scr[...], preferred_element_type=F32)
        xs = picked[:, 0:d].astype(MXU_DTYPE)
        gs = picked[:, d:d + LANES] + picked[:, d + LANES:]
        ys = jnp.zeros((rows, d), F32)
        for e in range(GROUP_SIZE):
            ge = jnp.sum(jnp.where(lane == g * GROUP_SIZE + e, gs, 0.0), axis=-1, keepdims=True)
            a = jnp.dot(xs, w1_ref[e], preferred_element_type=F32)
            bb = jnp.dot(xs, w3_ref[e], preferred_element_type=F32)
            hid = (a * (1.0 / (1.0 + jnp.exp(-a)))) * bb * ge
            ys = ys + jnp.dot(hid.astype(MXU_DTYPE), w2_ref[e], preferred_element_type=F32)
        acc_scr[...] += lax.dot_general(onehot, ys.astype(MXU_DTYPE), (((0,), (0,)), ((), ())),
                                        preferred_element_type=F32)

    n_chunks = (n_g + C - 1) // C
    tail = n_g - (n_chunks - 1) * C
    merge = (n_chunks >= 2) & (tail <= MOE_TAIL)
    n_plain = jnp.where(merge, n_chunks - 2, n_chunks)

    def plain(ci, carry):
        chunk(ci * C, C)
        return carry

    lax.fori_loop(0, n_plain, plain, 0)

    @pl.when(merge)
    def _():
        chunk(n_plain * C, C + MOE_TAIL)

    @pl.when(g == pl.num_programs(2) - 1)
    def _():
        g2 = mod_ref[0, 5:6, :]
        o_ref[0] = _layer_norm(ALPHA * x1_ref[0] + g2 * acc_scr[...], g_ref[...], b_ref[...])


def _moe(x1, mod, h2, gate, w1, w3, w2, g, b, tm):
    B, S, D = x1.shape
    gs = GROUP_SIZE
    row = lambda b_, i, c: (b_, i, 0)
    const2 = lambda b_, i, c: (0, 0)
    return pl.pallas_call(
        _moe_kernel,
        grid=(B, S // tm, N_GROUPS),
        in_specs=[pl.BlockSpec((1, tm, D), row),
                  pl.BlockSpec((1, 6, D), lambda b_, i, c: (b_, 0, 0)),
                  pl.BlockSpec((1, tm, D), row),
                  pl.BlockSpec((1, tm, LANES), row),
                  pl.BlockSpec((gs, D, D_FF), lambda b_, i, c: (c, 0, 0)),
                  pl.BlockSpec((gs, D, D_FF), lambda b_, i, c: (c, 0, 0)),
                  pl.BlockSpec((gs, D_FF, D), lambda b_, i, c: (c, 0, 0)),
                  pl.BlockSpec((1, D), const2),
                  pl.BlockSpec((1, D), const2)],
        out_specs=pl.BlockSpec((1, tm, D), row),
        out_shape=jax.ShapeDtypeStruct((B, S, D), F32),
        scratch_shapes=[pltpu.VMEM((tm, D), F32),
                        pltpu.VMEM((tm, D + 2 * LANES), MXU_DTYPE),
                        pltpu.VMEM((8, tm), F32),
                        pltpu.VMEM((8, tm), F32)],
        compiler_params=_cparams(("parallel", "parallel", "arbitrary")),
        name="moe_ffn",
    )(x1, mod, h2, gate, w1, w3, w2, g.reshape(1, D), b.reshape(1, D))


def _rope_tables(seq):
    lane = np.arange(LANES)

    def tables(period, offset, rot, theta):
        half = rot // 2
        d = lane % period - offset
        first = (d >= 0) & (d < half)
        second = (d >= half) & (d < rot)
        fidx = np.where(first, d, np.where(second, d - half, 0))
        inv_freq = theta ** (-2.0 * jnp.arange(half, dtype=F32) / rot)
        ang = jnp.arange(seq, dtype=F32)[:, None] * inv_freq[None, :]
        cos_l = jnp.cos(ang)[:, fidx]
        sin_l = jnp.sin(ang)[:, fidx]
        cos = jnp.where((first | second)[None, :], cos_l, 1.0)
        s_up = jnp.where(first[None, :], -sin_l, 0.0)
        s_dn = jnp.where(second[None, :], sin_l, 0.0)
        return [cos, s_up, s_dn]

    tabs = (tables(DIFF_QK, 0, DIFF_QK // 4, ROPE_THETA)
            + tables(DIL_DIM, 0, DIL_DIM // 4, ROPE_THETA)
            + tables(LANES, KR_LANE, MLA_ROPE, MLA_THETA))
    return jnp.stack(tabs).astype(F32)


def _prep_layer(l, w_in, w_uq, w_ukv, w_out, w1, w3, w2):
    D = w_in.shape[1]
    wi = w_in[l]
    zeros = lambda n: jnp.zeros((D, n), wi.dtype)
    wi = jnp.concatenate([wi[:, :C_KR], zeros(KR_LANE), wi[:, C_KR:], zeros(LANES - KR_LANE - MLA_ROPE)], axis=1)
    uq = w_uq[l].reshape(MLA_Q_RANK, MLA_HEADS, MLA_NOPE + MLA_ROPE)
    uq = jnp.pad(uq, ((0, 0), (0, 0), (0, LANES - MLA_NOPE - MLA_ROPE))).reshape(MLA_Q_RANK, MLA_HEADS * LANES)
    ukv = w_ukv[l].reshape(MLA_KV_RANK, MLA_HEADS, MLA_NOPE + MLA_V)
    uk = jnp.pad(ukv[..., :MLA_NOPE], ((0, 0), (0, 0), (0, LANES - MLA_NOPE))).reshape(MLA_KV_RANK, MLA_HEADS * LANES)
    uv = ukv[..., MLA_NOPE:].reshape(MLA_KV_RANK, MLA_HEADS * MLA_V)
    c = lambda t: t.astype(MXU_DTYPE)
    return dict(w_in=c(wi), w_uq=c(uq), w_uk=c(uk), w_uv=c(uv), w_out=c(w_out[l]),
                w1=c(w1[l]), w3=c(w3[l]), w2=c(w2[l]))


def _tile(n, pref):
    return pref if n % pref == 0 else n


def _trunk(x, mods, tabs, layers, p):
    B, S, D = x.shape
    tm = _tile(S, 512)
    tq = _tile(S, ATTN_TQ * ATTN_PARTS)
    tk = _tile(S, ATTN_TK)
    x = _ln_in(x, p['ln_in_g'], p['ln_in_b'], tm)
    for l, lw in enumerate(layers):
        mod = mods[l]
        lam_init = 0.8 - 0.6 * math.exp(-0.3 * l)
        qa, ka, va, qb, kb, vb, mq, mk, mv = _proj(
            x, mod, tabs, lw['w_in'], p['g_cq'][l], p['g_ckv'][l], lw['w_uq'], lw['w_uk'], lw['w_uv'], tm)
        oa = _diff_attn(qa, ka, va, p['lam_pack'][l], p['g_sub'][l], lam_init, tq, tk)
        oc = _mla_attn(mq, mk, mv, tq, tk)
        ob = _dil_attn(qb, kb, vb)
        x1, h2, gate = _post(x, mod, oa, ob, oc, lw['w_out'], p['ln1_g'][l], p['ln1_b'][l],
                             p['w_r_hi'], p['w_r_lo'], p['b_r'], tm)
        x = _moe(x1, mod, h2, gate, lw['w1'], lw['w3'], lw['w2'], p['ln2_g'][l], p['ln2_b'][l],
                 _tile(S, MOE_TM))
    return x


def kernel(x_prompt, x_sample, c_prompt, c_sample, ln_in_g, ln_in_b, w_router, b_router, w_ada, b_ada, w_in, lam_q1, lam_k1, lam_q2, lam_k2, g_subln, g_cq, g_ckv, w_uq, w_ukv, w_out, ln1_g, ln1_b, w1, w3, w2, ln2_g, ln2_b):
    depth = w_in.shape[0]
    D = x_prompt.shape[-1]
    bp, bs = c_prompt.shape[0], c_sample.shape[0]
    rows = -(-(bp + bs) // 8) * 8
    c_all = jnp.concatenate([c_prompt, c_sample, jnp.zeros((rows - bp - bs, D), F32)], axis=0)
    mods = _ada(c_all, w_ada, b_ada)
    mods_p = mods[:, :bp].reshape(depth, bp, 6, D)
    mods_s = mods[:, bp:bp + bs].reshape(depth, bs, 6, D)

    pad_l = lambda t: jnp.pad(t, ((0, 0), (0, LANES - t.shape[-1])))
    pad_r = lambda t: jnp.pad(t, ((0, 0),) * (t.ndim - 2) + ((0, SMALL_ROWS - t.shape[-2]), (0, 0)))
    p = dict(
        ln_in_g=ln_in_g, ln_in_b=ln_in_b, g_cq=g_cq, g_ckv=pad_r(g_ckv[:, None, :]),
        ln1_g=ln1_g, ln1_b=ln1_b, ln2_g=ln2_g, ln2_b=ln2_b,
        lam_pack=pad_r(jnp.stack([pad_l(lam_q1), pad_l(lam_k1), pad_l(lam_q2), pad_l(lam_k2)], axis=1)),
        g_sub=g_subln.reshape(depth, DIFF_V, 1),
        b_r=pad_r(pad_l(b_router.reshape(1, -1))),
    )
    w_r = pad_l(w_router)
    p['w_r_hi'] = w_r.astype(MXU_DTYPE)
    p['w_r_lo'] = (w_r - p['w_r_hi'].astype(F32)).astype(MXU_DTYPE)
    layers = [_prep_layer(l, w_in, w_uq, w_ukv, w_out, w1, w3, w2) for l in range(depth)]
    tabs = _rope_tables(max(x_prompt.shape[1], x_sample.shape[1]))
    y_prompt = _trunk(x_prompt, mods_p, tabs, layers, p)
    y_sample = _trunk(x_sample, mods_s, tabs, layers, p)
    return (y_prompt, y_sample)
```

```python
import functools
import math

import numpy as np
import jax
import jax.numpy as jnp
from jax import lax
from jax.experimental import pallas as pl
from jax.experimental.pallas import tpu as pltpu

F32 = jnp.float32
MXU_DTYPE = jnp.bfloat16

LANES = 128
DEPTH = 4
DIFF_HEADS = 4
DIFF_QK = 32
DIFF_V = 64
DIL_HEADS = 6
DIL_DIM = 64
DIL_PAIRS = ((128, 1), (512, 4), (2048, 16))
DIL_RADIUS = 64
MLA_HEADS = 6
MLA_NOPE = 64
MLA_ROPE = 32
MLA_V = 64
MLA_Q_RANK = 256
MLA_KV_RANK = 128
MLA_THETA = 10000.0
ROPE_THETA = 500000.0
N_EXPERTS = 16
GROUP_SIZE = 4
D_FF = 256
ALPHA = (2 * DEPTH) ** 0.25
NEG = -1e30
LOG2E = math.log2(math.e)
ATTN_TQ = 512
ATTN_PARTS = 4
SMALL_ROWS = 32
ATTN_TK = 2048

C_Q1, C_Q2, C_K1, C_K2, C_VA = 0, 128, 256, 384, 512
C_QB, C_KB, C_VB = 768, 1152, 1536
C_CQ, C_CKV, C_KR, C_END = 1920, 2176, 2304, 2432
KR_LANE = 64

VMEM_LIMIT = 48 * 1024 * 1024


def _cparams(sem):
    return pltpu.CompilerParams(dimension_semantics=sem, vmem_limit_bytes=VMEM_LIMIT)


def _lane_iota(shape):
    return lax.broadcasted_iota(jnp.int32, shape, len(shape) - 1)


def _layer_norm(z, g, b):
    mu = jnp.mean(z, axis=-1, keepdims=True)
    zc = z - mu
    var = jnp.mean(zc * zc, axis=-1, keepdims=True)
    return zc * lax.rsqrt(var + 1e-5) * g + b


def _rope(t, tab_ref, typ, half, rows):
    c = tab_ref[3 * typ, rows, :]
    s_up = tab_ref[3 * typ + 1, rows, :]
    s_dn = tab_ref[3 * typ + 2, rows, :]
    return t * c + pltpu.roll(t, LANES - half, 1) * s_up + pltpu.roll(t, half, 1) * s_dn


def _ln_in_kernel(x_ref, g_ref, b_ref, o_ref):
    o_ref[0] = _layer_norm(x_ref[0], g_ref[...], b_ref[...])


def _ln_in(x, g, b, tm):
    B, S, D = x.shape
    return pl.pallas_call(
        _ln_in_kernel,
        grid=(B, S // tm),
        in_specs=[pl.BlockSpec((1, tm, D), lambda b, i: (b, i, 0)),
                  pl.BlockSpec((1, D), lambda b, i: (0, 0)),
                  pl.BlockSpec((1, D), lambda b, i: (0, 0))],
        out_specs=pl.BlockSpec((1, tm, D), lambda b, i: (b, i, 0)),
        out_shape=jax.ShapeDtypeStruct(x.shape, F32),
        compiler_params=_cparams(("parallel", "parallel")),
        name="ln_in",
    )(x, g.reshape(1, D), b.reshape(1, D))


def _ada_kernel(c_ref, w_ref, b_ref, o_ref):
    c = c_ref[...]
    cs = (c * (1.0 / (1.0 + jnp.exp(-c)))).astype(MXU_DTYPE)
    o_ref[0] = jnp.dot(cs, w_ref[0].astype(MXU_DTYPE), preferred_element_type=F32) + b_ref[0]


def _ada(c, w_ada, b_ada, tn=1536):
    R, D = c.shape
    L, _, N = w_ada.shape
    return pl.pallas_call(
        _ada_kernel,
        grid=(L, N // tn),
        in_specs=[pl.BlockSpec((R, D), lambda l, j: (0, 0)),
                  pl.BlockSpec((1, D, tn), lambda l, j: (l, 0, j)),
                  pl.BlockSpec((1, 1, tn), lambda l, j: (l, 0, j))],
        out_specs=pl.BlockSpec((1, R, tn), lambda l, j: (l, 0, j)),
        out_shape=jax.ShapeDtypeStruct((L, R, N), F32),
        compiler_params=_cparams(("parallel", "parallel")),
        name="ada",
    )(c, w_ada, b_ada.reshape(L, 1, N))


PROJ_PARTS = 2
V_DIM = 64
V_ROWS = V_DIM + 16


def _store_values_t(ref, vt, heads, cols):
    n = vt.shape[1]
    for h in range(heads):
        ref[0, h * V_ROWS:h * V_ROWS + V_DIM, cols] = vt[h * V_DIM:(h + 1) * V_DIM].astype(ref.dtype)
        ref[0, h * V_ROWS + V_DIM:(h + 1) * V_ROWS, cols] = jnp.ones((V_ROWS - V_DIM, n), ref.dtype)


def _proj_kernel(x_ref, mod_ref, tab_ref, w_ref, gcq_ref, gckv_ref, wuq_ref, wuk_ref, wuv_ref,
                 qa_ref, ka_ref, va_ref, qb_ref, kb_ref, vb_ref, mq_ref, mk_ref, mv_ref):
    sh1 = mod_ref[0, 0:1, :]
    sc1 = mod_ref[0, 1:2, :]
    tm = x_ref.shape[1]
    parts = PROJ_PARTS if tm % (PROJ_PARTS * LANES) == 0 else 1

    def rms(t, g):
        return (t * lax.rsqrt(jnp.mean(t * t, axis=-1, keepdims=True) + 1e-6) * g).astype(MXU_DTYPE)

    for r in range(parts):
        rows = slice(r * (tm // parts), (r + 1) * (tm // parts))
        h = (x_ref[0, rows, :] * (1.0 + sc1) + sh1).astype(MXU_DTYPE)

        def mm(a, b, h=h):
            return jnp.dot(h, w_ref[:, a:b], preferred_element_type=F32)

        rope = functools.partial(_rope, tab_ref=tab_ref, rows=rows)

        qa = mm(C_Q1, C_K1)
        ka = mm(C_K1, C_VA)
        sa = DIFF_QK ** -0.5 * LOG2E
        for j in range(2):
            blk = slice(j * LANES, (j + 1) * LANES)
            qa_ref[0, rows, blk] = (rope(qa[:, blk], typ=0, half=DIFF_QK // 8) * sa).astype(qa_ref.dtype)
            ka_ref[0, rows, blk] = rope(ka[:, blk], typ=0, half=DIFF_QK // 8).astype(ka_ref.dtype)
        _store_values_t(va_ref, mm(C_VA, C_QB).T, DIFF_HEADS, rows)

        qb = mm(C_QB, C_KB)
        kb = mm(C_KB, C_VB)
        vb = mm(C_VB, C_CQ)
        sb = DIL_DIM ** -0.5 * LOG2E
        for j in range(DIL_HEADS // 2):
            blk = slice(j * LANES, (j + 1) * LANES)
            qb_ref[j, 0, rows, :] = (rope(qb[:, blk], typ=1, half=DIL_DIM // 8) * sb).astype(qb_ref.dtype)
            kb_ref[j, 0, rows, :] = rope(kb[:, blk], typ=1, half=DIL_DIM // 8).astype(kb_ref.dtype)
            vb_ref[j, 0, rows, :] = vb[:, blk].astype(vb_ref.dtype)

        cq = rms(mm(C_CQ, C_CKV), gcq_ref[...])
        ckv = rms(mm(C_CKV, C_KR), gckv_ref[0:1, :])
        kr = rope(mm(C_KR, C_END), typ=2, half=MLA_ROPE // 2)
        q = jnp.dot(cq, wuq_ref[...], preferred_element_type=F32)
        kn = jnp.dot(ckv, wuk_ref[...], preferred_element_type=F32)
        sc = (MLA_NOPE + MLA_ROPE) ** -0.5 * LOG2E
        for j in range(MLA_HEADS):
            blk = slice(j * LANES, (j + 1) * LANES)
            mq_ref[0, rows, blk] = (rope(q[:, blk], typ=2, half=MLA_ROPE // 2) * sc).astype(mq_ref.dtype)
            mk_ref[0, rows, blk] = (kn[:, blk] + kr).astype(mk_ref.dtype)
        _store_values_t(mv_ref, jnp.dot(ckv, wuv_ref[...], preferred_element_type=F32).T, MLA_HEADS, rows)


def _proj(x, mod, tabs, w_in, g_cq, g_ckv, w_uq, w_uk, w_uv, tm):
    B, S, D = x.shape
    widths = (256, 256, DIFF_HEADS * V_ROWS, 384, 384, 384, 768, 768, MLA_HEADS * V_ROWS)
    transposed = (2, 8)
    row = lambda b, i: (b, i, 0)
    col = lambda b, i: (b, 0, i)
    const2 = lambda b, i: (0, 0)
    out_specs = [pl.BlockSpec((1, w, tm), col) if n in transposed else pl.BlockSpec((1, tm, w), row)
                 for n, w in enumerate(widths)]
    out_shape = [jax.ShapeDtypeStruct((B, w, S) if n in transposed else (B, S, w), MXU_DTYPE)
                 for n, w in enumerate(widths)]
    for n in (3, 4, 5):
        out_specs[n] = pl.BlockSpec((DIL_HEADS // 2, 1, tm, LANES), lambda b, i: (0, b, i, 0))
        out_shape[n] = jax.ShapeDtypeStruct((DIL_HEADS // 2, B, S, LANES), MXU_DTYPE)
    return pl.pallas_call(
        _proj_kernel,
        grid=(B, S // tm),
        in_specs=[pl.BlockSpec((1, tm, D), row),
                  pl.BlockSpec((1, 6, D), lambda b, i: (b, 0, 0)),
                  pl.BlockSpec((9, tm, LANES), lambda b, i: (0, i, 0)),
                  pl.BlockSpec(w_in.shape, const2),
                  pl.BlockSpec((1, MLA_Q_RANK), const2),
                  pl.BlockSpec((SMALL_ROWS, MLA_KV_RANK), const2),
                  pl.BlockSpec(w_uq.shape, const2),
                  pl.BlockSpec(w_uk.shape, const2),
                  pl.BlockSpec(w_uv.shape, const2)],
        out_specs=out_specs,
        out_shape=out_shape,
        compiler_params=_cparams(("parallel", "parallel")),
        name="proj_in",
    )(x, mod, tabs, w_in, g_cq.reshape(1, -1), g_ckv, w_uq, w_uk, w_uv)


def _softmax_steps(chains, m_scr, acc_scr, base=0):
    scores = [lax.dot_general(k, q, (((1,), (1,)), ((), ())), preferred_element_type=F32)
              for q, k, _ in chains]
    for u, (st, (_, _, vt)) in enumerate(zip(scores, chains), start=base):
        m_prev = m_scr[u]
        m_new = jnp.maximum(m_prev, jnp.max(st, axis=0, keepdims=True))
        alpha = jnp.exp2(m_prev - m_new)
        p = jnp.exp2(st - m_new)
        acc_scr[u] = alpha * acc_scr[u] + jnp.dot(vt, p.astype(MXU_DTYPE), preferred_element_type=F32)
        m_scr[u] = m_new


def _init_softmax(m_scr, acc_scr):
    m_scr[...] = jnp.full(m_scr.shape, NEG, F32)
    acc_scr[...] = jnp.zeros(acc_scr.shape, F32)


def _softmax_out(acc_scr, u):
    return acc_scr[u, :V_DIM] / acc_scr[u, V_DIM:V_DIM + 1]


def _diff_attn_kernel(lam_init, q1_ref, q2_ref, k1_ref, k2_ref, vt_ref, lamp_ref, g_ref, o_ref,
                      q_scr, m_scr, acc_scr):
    j = pl.program_id(1)
    ik = pl.program_id(3)

    t = m_scr.shape[2]
    parts = q1_ref.shape[1] // t

    @pl.when(ik == 0)
    def _():
        _init_softmax(m_scr, acc_scr)
        lane = _lane_iota((t, LANES))
        for r in range(parts):
            rows = slice(r * t, (r + 1) * t)
            for hh in range(2):
                h = 2 * j + hh
                msk = (lane >= h * DIFF_QK) & (lane < (h + 1) * DIFF_QK)
                q_scr[4 * r + 2 * hh] = jnp.where(msk, q1_ref[0, rows, :].astype(F32), 0.0).astype(q_scr.dtype)
                q_scr[4 * r + 2 * hh + 1] = jnp.where(msk, q2_ref[0, rows, :].astype(F32), 0.0).astype(q_scr.dtype)

    for r in range(parts):
        chains = []
        for hh in range(2):
            vt = vt_ref[0, hh * V_ROWS:(hh + 1) * V_ROWS, :]
            chains.append((q_scr[4 * r + 2 * hh], k1_ref[0], vt))
            chains.append((q_scr[4 * r + 2 * hh + 1], k2_ref[0], vt))
        _softmax_steps(chains, m_scr, acc_scr, 4 * r)

    @pl.when(ik == pl.num_programs(3) - 1)
    def _():
        lp = lamp_ref[0:4, :]
        lam = (jnp.exp(jnp.sum(lp[0:1] * lp[1:2], axis=-1, keepdims=True))
               - jnp.exp(jnp.sum(lp[2:3] * lp[3:4], axis=-1, keepdims=True)) + lam_init)
        for r in range(parts):
            for hh in range(2):
                u = 4 * r + 2 * hh
                o = _softmax_out(acc_scr, u) - lam * _softmax_out(acc_scr, u + 1)
                ms = jnp.mean(o * o, axis=0, keepdims=True)
                o_ref[0, hh * DIFF_V:(hh + 1) * DIFF_V, r * t:(r + 1) * t] = (
                    o * lax.rsqrt(ms + 1e-6) * g_ref[...] * (1.0 - lam_init)).astype(o_ref.dtype)


def _diff_attn(qa, ka, vat, lam_pack, g_sub, lam_init, tq, tk):
    B, S, _ = qa.shape
    parts = tq // ATTN_TQ if tq % ATTN_TQ == 0 else 1
    kern = functools.partial(_diff_attn_kernel, lam_init)
    return pl.pallas_call(
        kern,
        grid=(B, DIFF_HEADS // 2, S // tq, S // tk),
        in_specs=[pl.BlockSpec((1, tq, LANES), lambda b, j, iq, ik: (b, iq, 0)),
                  pl.BlockSpec((1, tq, LANES), lambda b, j, iq, ik: (b, iq, 1)),
                  pl.BlockSpec((1, tk, LANES), lambda b, j, iq, ik: (b, ik, 0)),
                  pl.BlockSpec((1, tk, LANES), lambda b, j, iq, ik: (b, ik, 1)),
                  pl.BlockSpec((1, 2 * V_ROWS, tk), lambda b, j, iq, ik: (b, j, ik)),
                  pl.BlockSpec((SMALL_ROWS, LANES), lambda b, j, iq, ik: (0, 0)),
                  pl.BlockSpec((DIFF_V, 1), lambda b, j, iq, ik: (0, 0))],
        out_specs=pl.BlockSpec((1, 2 * DIFF_V, tq), lambda b, j, iq, ik: (b, j, iq)),
        out_shape=jax.ShapeDtypeStruct((B, DIFF_HEADS * DIFF_V, S), MXU_DTYPE),
        scratch_shapes=[pltpu.VMEM((4 * parts, tq // parts, LANES), MXU_DTYPE),
                        pltpu.VMEM((4 * parts, 1, tq // parts), F32),
                        pltpu.VMEM((4 * parts, V_ROWS, tq // parts), F32)],
        compiler_params=_cparams(("parallel", "parallel", "parallel", "arbitrary")),
        name="diff_attn",
    )(qa, qa, ka, ka, vat, lam_pack, g_sub)


MLA_GROUP = 3


def _mla_attn_kernel(q_ref, k_ref, vt_ref, o_ref, m_scr, acc_scr):
    ik = pl.program_id(3)

    @pl.when(ik == 0)
    def _():
        _init_softmax(m_scr, acc_scr)

    t = m_scr.shape[2]
    parts = q_ref.shape[1] // t
    for r in range(parts):
        rows = slice(r * t, (r + 1) * t)
        chains = []
        for hh in range(MLA_GROUP):
            blk = slice(hh * LANES, (hh + 1) * LANES)
            chains.append((q_ref[0, rows, blk], k_ref[0, :, blk], vt_ref[0, hh * V_ROWS:(hh + 1) * V_ROWS, :]))
        _softmax_steps(chains, m_scr, acc_scr, r * MLA_GROUP)

    @pl.when(ik == pl.num_programs(3) - 1)
    def _():
        for r in range(parts):
            for hh in range(MLA_GROUP):
                o_ref[0, hh * MLA_V:(hh + 1) * MLA_V, r * t:(r + 1) * t] = _softmax_out(
                    acc_scr, r * MLA_GROUP + hh).astype(o_ref.dtype)


def _mla_attn(mq, mk, mvt, tq, tk):
    B, S, _ = mq.shape
    g = MLA_GROUP
    parts = tq // ATTN_TQ if tq % ATTN_TQ == 0 else 1
    return pl.pallas_call(
        _mla_attn_kernel,
        grid=(B, MLA_HEADS // g, S // tq, S // tk),
        in_specs=[pl.BlockSpec((1, tq, g * LANES), lambda b, j, iq, ik: (b, iq, j)),
                  pl.BlockSpec((1, tk, g * LANES), lambda b, j, iq, ik: (b, ik, j)),
                  pl.BlockSpec((1, g * V_ROWS, tk), lambda b, j, iq, ik: (b, j, ik))],
        out_specs=pl.BlockSpec((1, g * MLA_V, tq), lambda b, j, iq, ik: (b, j, iq)),
        out_shape=jax.ShapeDtypeStruct((B, MLA_HEADS * MLA_V, S), MXU_DTYPE),
        scratch_shapes=[pltpu.VMEM((g * parts, 1, tq // parts), F32),
                        pltpu.VMEM((g * parts, V_ROWS, tq // parts), F32)],
        compiler_params=_cparams(("parallel", "parallel", "parallel", "arbitrary")),
        name="mla_attn",
    )(mq, mk, mvt)


DIL_SUB = 128
DIL_WIDTH = DIL_HEADS * DIL_DIM


DIL_DILS = tuple(d for _, d in DIL_PAIRS)
DIL_HALO = DIL_RADIUS * DIL_DILS[-1]
DIL_TILE = 2048
DIL_SPLIT = 4


def _dil_kernel(seq, q_ref, kp_ref, kc_ref, kn_ref, vp_ref, vc_ref, vn_ref, o_ref,
                qbuf, kbuf, vbuf, pat_o, pat_l, qg, kg, vg, q4, k4, v4):
    i = pl.program_id(2)
    tt = q_ref.shape[2]
    H = DIL_HALO
    R = DIL_RADIUS
    qbuf[...] = q_ref[0, 0].astype(F32)
    kbuf[0:H] = kp_ref[0, 0].astype(F32)
    kbuf[H:H + tt] = kc_ref[0, 0].astype(F32)
    kbuf[H + tt:] = kn_ref[0, 0].astype(F32)
    vbuf[0:H] = vp_ref[0, 0].astype(F32)
    vbuf[H:H + tt] = vc_ref[0, 0].astype(F32)
    vbuf[H + tt:] = vn_ref[0, 0].astype(F32)
    vg[:, LANES:, :] = jnp.ones((vg.shape[0], vg.shape[1] - LANES, vg.shape[2]), vg.dtype)
    for r4 in range(DIL_SPLIT):
        q4[r4] = qbuf[pl.ds(r4, tt // DIL_SPLIT, stride=DIL_SPLIT), :]
        k4[r4] = kbuf[pl.ds(r4, (tt + 2 * H) // DIL_SPLIT, stride=DIL_SPLIT), :]
        v4[r4] = vbuf[pl.ds(r4, (tt + 2 * H) // DIL_SPLIT, stride=DIL_SPLIT), :]
    sub = DIL_SUB
    win = sub + 2 * R
    nc = tt // sub
    cidx = lax.broadcasted_iota(jnp.int32, (nc, win, sub), 0)
    row_k = lax.broadcasted_iota(jnp.int32, (nc, win, sub), 1)
    col_q = lax.broadcasted_iota(jnp.int32, (nc, win, sub), 2)
    rel = row_k - col_q
    band = (rel >= 0) & (rel <= 2 * R)
    lane = _lane_iota((sub, LANES))
    for p, d in enumerate(DIL_DILS):
        n_sub = tt // d // sub
        length = seq // d
        q_starts = [c // n_sub + d * sub * (c % n_sub) for c in range(nc)]
        for c in range(nc):
            r, u = c // n_sub, c % n_sub
            y0 = H // d + u * sub - R
            if d % DIL_SPLIT:
                qf = qbuf[pl.ds(q_starts[c], sub, stride=d), :]
                kw = kbuf[pl.ds(r + d * y0, win, stride=d), :]
                vw = vbuf[pl.ds(r + d * y0, win, stride=d), :]
            else:
                e = d // DIL_SPLIT
                r4, m = r % DIL_SPLIT, r // DIL_SPLIT
                qf = q4[r4, pl.ds(m + e * sub * u, sub, stride=e), :]
                kw = k4[r4, pl.ds(m + e * y0, win, stride=e), :]
                vw = v4[r4, pl.ds(m + e * y0, win, stride=e), :]
            qg[c, :sub] = jnp.where(lane < DIL_DIM, qf, 0.0).astype(qg.dtype)
            qg[c, sub:] = jnp.where(lane >= DIL_DIM, qf, 0.0).astype(qg.dtype)
            kg[c] = kw.astype(kg.dtype)
            vg[c, :LANES] = vw.T.astype(vg.dtype)
        u_of_c = cidx % n_sub if n_sub > 1 else jnp.zeros_like(cidx)
        kpos = i * (tt // d) + u_of_c * sub - R + row_k
        valid = band & (kpos >= 0) & (kpos < length)
        s = lax.dot_general(kg[...], qg[...], (((2,), (2,)), ((0,), (0,))),
                            preferred_element_type=F32)
        s = jnp.where(jnp.concatenate([valid, valid], axis=2), s, NEG)
        m = jnp.max(s, axis=1, keepdims=True)
        e = jnp.exp2(s - m)
        ot = lax.dot_general(vg[...], e.astype(MXU_DTYPE), (((2,), (1,)), ((0,), (0,))),
                             preferred_element_type=F32)
        l = ot[:, LANES:LANES + 1, :]
        ot = ot[:, :LANES, :] / l
        lse = m + jnp.log2(l)
        for c in range(nc):
            o_t = jnp.concatenate([ot[c, :DIL_DIM, :sub], ot[c, DIL_DIM:, sub:]], axis=0)
            l_t = jnp.concatenate([jnp.broadcast_to(lse[c, :, :sub], (DIL_DIM, sub)),
                                   jnp.broadcast_to(lse[c, :, sub:], (DIL_DIM, sub))], axis=0)
            pat_o[p, pl.ds(q_starts[c], sub, stride=d), :] = o_t.T
            pat_l[p, pl.ds(q_starts[c], sub, stride=d), :] = l_t.T

    l0, l1, l2 = pat_l[0], pat_l[1], pat_l[2]
    mx = jnp.maximum(jnp.maximum(l0, l1), l2)
    e0, e1, e2 = jnp.exp2(l0 - mx), jnp.exp2(l1 - mx), jnp.exp2(l2 - mx)
    den = e0 + e1 + e2
    o_ref[0] = ((e0 / den) * pat_o[0] + (e1 / den) * pat_o[1] + (e2 / den) * pat_o[2]).astype(o_ref.dtype)


def _dil_attn(qb, kb, vb):
    P, B, S, _ = qb.shape
    tt = _tile(S, DIL_TILE)
    H = DIL_HALO
    nh = S // H
    cur = pl.BlockSpec((1, 1, tt, LANES), lambda b, hp, i: (hp, b, i, 0))
    prv = pl.BlockSpec((1, 1, H, LANES), lambda b, hp, i: (hp, b, jnp.maximum(i * (tt // H) - 1, 0), 0))
    nxt = pl.BlockSpec((1, 1, H, LANES), lambda b, hp, i: (hp, b, jnp.minimum((i + 1) * (tt // H), nh - 1), 0))
    return pl.pallas_call(
        functools.partial(_dil_kernel, S),
        grid=(B, P, S // tt),
        in_specs=[cur, prv, cur, nxt, prv, cur, nxt],
        out_specs=pl.BlockSpec((1, tt, LANES), lambda b, hp, i: (b, i, hp)),
        out_shape=jax.ShapeDtypeStruct((B, S, P * LANES), MXU_DTYPE),
        scratch_shapes=[pltpu.VMEM((tt, LANES), F32),
                        pltpu.VMEM((tt + 2 * H, LANES), F32),
                        pltpu.VMEM((tt + 2 * H, LANES), F32),
                        pltpu.VMEM((len(DIL_DILS), tt, LANES), F32),
                        pltpu.VMEM((len(DIL_DILS), tt, LANES), F32),
                        pltpu.VMEM((tt // DIL_SUB, 2 * DIL_SUB, LANES), MXU_DTYPE),
                        pltpu.VMEM((tt // DIL_SUB, DIL_SUB + 2 * DIL_RADIUS, LANES), MXU_DTYPE),
                        pltpu.VMEM((tt // DIL_SUB, LANES + 16, DIL_SUB + 2 * DIL_RADIUS), MXU_DTYPE),
                        pltpu.VMEM((DIL_SPLIT, tt // DIL_SPLIT, LANES), F32),
                        pltpu.VMEM((DIL_SPLIT, (tt + 2 * H) // DIL_SPLIT, LANES), F32),
                        pltpu.VMEM((DIL_SPLIT, (tt + 2 * H) // DIL_SPLIT, LANES), F32)],
        compiler_params=_cparams(("parallel", "parallel", "parallel")),
        name="dil_attn",
    )(qb, kb, kb, kb, vb, vb, vb)


def _partner(x, k, lane):
    up = pltpu.roll(x, LANES - k, 1)
    dn = pltpu.roll(x, k, 1)
    return jnp.where((lane & k) == 0, up, dn)


GROUP_LANE = N_EXPERTS


def _route(logits, bias, lane):
    valid = lane < N_EXPERTS
    lane_f = lane.astype(F32)
    score = 1.0 / (1.0 + jnp.exp(-logits))
    biased = jnp.where(valid, score + bias, NEG)
    p1 = _partner(biased, 1, lane)
    hi = jnp.maximum(biased, p1)
    lo = jnp.minimum(biased, p1)
    hi_p = _partner(hi, 2, lane)
    lo_p = _partner(lo, 2, lane)
    top1 = jnp.maximum(hi, hi_p)
    top2 = jnp.maximum(jnp.minimum(hi, hi_p), jnp.maximum(lo, lo_p))
    gscore = jnp.where(valid, top1 + top2, NEG)
    gidx = (lane // GROUP_SIZE).astype(F32)
    big = float(LANES)
    gmax = jnp.max(gscore, axis=-1, keepdims=True)
    gsel = jnp.min(jnp.where(gscore == gmax, gidx, big), axis=-1, keepdims=True)
    cand = jnp.where((gidx == gsel) & valid, biased, NEG)
    m1 = jnp.max(cand, axis=-1, keepdims=True)
    i1 = jnp.min(jnp.where(cand == m1, lane_f, big), axis=-1, keepdims=True)
    oh1 = lane_f == i1
    cand2 = jnp.where(oh1, NEG, cand)
    m2 = jnp.max(cand2, axis=-1, keepdims=True)
    i2 = jnp.min(jnp.where((cand2 == m2) & jnp.logical_not(oh1), lane_f, big), axis=-1, keepdims=True)
    oh2 = lane_f == i2
    s1 = jnp.sum(jnp.where(oh1, score, 0.0), axis=-1, keepdims=True)
    s2 = jnp.sum(jnp.where(oh2, score, 0.0), axis=-1, keepdims=True)
    den = s1 + s2
    gate = jnp.where(oh1, s1 / den, 0.0) + jnp.where(oh2, s2 / den, 0.0)
    return jnp.where(lane == GROUP_LANE, gsel, gate)


POST_PARTS = 2


def _post_kernel(x_ref, mod_ref, oa_ref, ob_ref, oc_ref, wo_ref, g_ref, b_ref, wrh_ref, wrl_ref, br_ref,
                 x1_ref, h2_ref, gate_ref):
    na = DIFF_HEADS * DIFF_V
    nb = na + DIL_WIDTH
    tn = (((0,), (0,)), ((), ()))
    g1 = mod_ref[0, 2:3, :]
    sh2 = mod_ref[0, 3:4, :]
    sc2 = mod_ref[0, 4:5, :]
    tm = x_ref.shape[1]
    parts = POST_PARTS if tm % (POST_PARTS * LANES) == 0 else 1
    for r in range(parts):
        rows = slice(r * (tm // parts), (r + 1) * (tm // parts))
        y = (lax.dot_general(oa_ref[0, :, rows], wo_ref[0:na], tn, preferred_element_type=F32)
             + jnp.dot(ob_ref[0, rows, :], wo_ref[na:nb], preferred_element_type=F32)
             + lax.dot_general(oc_ref[0, :, rows], wo_ref[nb:], tn, preferred_element_type=F32))
        x1 = _layer_norm(ALPHA * x_ref[0, rows, :] + g1 * y, g_ref[...], b_ref[...])
        x1_ref[0, rows, :] = x1
        h2 = x1 * (1.0 + sc2) + sh2
        h2_ref[0, rows, :] = h2.astype(h2_ref.dtype)
        h_hi = h2.astype(MXU_DTYPE)
        h_lo = (h2 - h_hi.astype(F32)).astype(MXU_DTYPE)
        logits = (jnp.dot(h_hi, wrh_ref[...], preferred_element_type=F32)
                  + jnp.dot(h_lo, wrh_ref[...], preferred_element_type=F32)
                  + jnp.dot(h_hi, wrl_ref[...], preferred_element_type=F32))
        gate_ref[0, rows, :] = _route(logits, br_ref[0:1, :], _lane_iota(logits.shape))


def _post(x, mod, oa, ob, oc, w_out, g, b, w_r_hi, w_r_lo, b_r, tm):
    B, S, D = x.shape
    row = lambda b_, i: (b_, i, 0)
    const2 = lambda b_, i: (0, 0)
    return pl.pallas_call(
        _post_kernel,
        grid=(B, S // tm),
        in_specs=[pl.BlockSpec((1, tm, D), row),
                  pl.BlockSpec((1, 6, D), lambda b_, i: (b_, 0, 0)),
                  pl.BlockSpec((1, oa.shape[1], tm), lambda b_, i: (b_, 0, i)),
                  pl.BlockSpec((1, tm, DIL_WIDTH), row),
                  pl.BlockSpec((1, oc.shape[1], tm), lambda b_, i: (b_, 0, i)),
                  pl.BlockSpec(w_out.shape, const2),
                  pl.BlockSpec((1, D), const2),
                  pl.BlockSpec((1, D), const2),
                  pl.BlockSpec(w_r_hi.shape, const2),
                  pl.BlockSpec(w_r_lo.shape, const2),
                  pl.BlockSpec((SMALL_ROWS, LANES), const2)],
        out_specs=[pl.BlockSpec((1, tm, D), row),
                   pl.BlockSpec((1, tm, D), row),
                   pl.BlockSpec((1, tm, LANES), row)],
        out_shape=[jax.ShapeDtypeStruct((B, S, D), F32),
                   jax.ShapeDtypeStruct((B, S, D), MXU_DTYPE),
                   jax.ShapeDtypeStruct((B, S, LANES), F32)],
        compiler_params=_cparams(("parallel", "parallel")),
        name="post_mix",
    )(x, mod, oa, ob, oc, w_out, g.reshape(1, D), b.reshape(1, D), w_r_hi, w_r_lo, b_r)


MOE_TM = 1024
MOE_CHUNK = 256
MOE_TAIL = 64
N_GROUPS = N_EXPERTS // GROUP_SIZE


def _moe_kernel(x1_ref, mod_ref, h2_ref, gate_ref, w1_ref, w3_ref, w2_ref, g_ref, b_ref, o_ref,
                acc_scr, src_scr, sel_scr, rank_scr):
    g = pl.program_id(2)
    tm = h2_ref.shape[1]
    d = h2_ref.shape[2]
    C = MOE_CHUNK

    @pl.when(g == 0)
    def _():
        acc_scr[...] = jnp.zeros(acc_scr.shape, F32)
        gate = gate_ref[0]
        gate_hi = gate.astype(MXU_DTYPE)
        src_scr[:, 0:d] = h2_ref[0]
        src_scr[:, d:d + LANES] = gate_hi
        src_scr[:, d + LANES:] = (gate - gate_hi.astype(F32)).astype(MXU_DTYPE)
        gsel_t = gate.T[GROUP_LANE:GROUP_LANE + 1, :]
        grp = lax.broadcasted_iota(jnp.int32, sel_scr.shape, 0).astype(F32)
        sel = jnp.where(gsel_t == grp, 1.0, 0.0)
        r = lax.broadcasted_iota(jnp.int32, (tm, tm), 0)
        c = lax.broadcasted_iota(jnp.int32, (tm, tm), 1)
        before = jnp.where(r < c, 1.0, 0.0).astype(MXU_DTYPE)
        sel_scr[...] = sel
        rank_scr[...] = jnp.dot(sel.astype(MXU_DTYPE), before, preferred_element_type=F32)

    sel = sel_scr[pl.ds(g, 1), :]
    rank = rank_scr[pl.ds(g, 1), :]
    n_g = jnp.sum(sel).astype(jnp.int32)

    def chunk(base, rows):
        lane = _lane_iota((rows, LANES))
        row = lax.broadcasted_iota(jnp.int32, (rows, tm), 0).astype(F32)
        onehot = jnp.where((rank - base.astype(F32) == row) & (sel > 0.5), 1.0, 0.0).astype(MXU_DTYPE)
        picked = jnp.dot(onehot, src_scr[...], preferred_element_type=F32)
        xs = picked[:, 0:d].astype(MXU_DTYPE)
        gs = picked[:, d:d + LANES] + picked[:, d + LANES:]
        ys = jnp.zeros((rows, d), F32)
        for e in range(GROUP_SIZE):
            ge = jnp.sum(jnp.where(lane == g * GROUP_SIZE + e, gs, 0.0), axis=-1, keepdims=True)
            a = jnp.dot(xs, w1_ref[e], preferred_element_type=F32)
            bb = jnp.dot(xs, w3_ref[e], preferred_element_type=F32)
            hid = (a * (1.0 / (1.0 + jnp.exp(-a)))) * bb * ge
            ys = ys + jnp.dot(hid.astype(MXU_DTYPE), w2_ref[e], preferred_element_type=F32)
        acc_scr[...] += lax.dot_general(onehot, ys.astype(MXU_DTYPE), (((0,), (0,)), ((), ())),
                                        preferred_element_type=F32)

    n_chunks = (n_g + C - 1) // C
    tail = n_g - (n_chunks - 1) * C
    merge = (n_chunks >= 2) & (tail <= MOE_TAIL)
    n_plain = jnp.where(merge, n_chunks - 2, n_chunks)

    def plain(ci, carry):
        chunk(ci * C, C)
        return carry

    lax.fori_loop(0, n_plain, plain, 0)

    @pl.when(merge)
    def _():
        chunk(n_plain * C, C + MOE_TAIL)

    @pl.when(g == pl.num_programs(2) - 1)
    def _():
        g2 = mod_ref[0, 5:6, :]
        o_ref[0] = _layer_norm(ALPHA * x1_ref[0] + g2 * acc_scr[...], g_ref[...], b_ref[...])


def _moe(x1, mod, h2, gate, w1, w3, w2, g, b, tm):
    B, S, D = x1.shape
    gs = GROUP_SIZE
    row = lambda b_, i, c: (b_, i, 0)
    const2 = lambda b_, i, c: (0, 0)
    return pl.pallas_call(
        _moe_kernel,
        grid=(B, S // tm, N_GROUPS),
        in_specs=[pl.BlockSpec((1, tm, D), row),
                  pl.BlockSpec((1, 6, D), lambda b_, i, c: (b_, 0, 0)),
                  pl.BlockSpec((1, tm, D), row),
                  pl.BlockSpec((1, tm, LANES), row),
                  pl.BlockSpec((gs, D, D_FF), lambda b_, i, c: (c, 0, 0)),
                  pl.BlockSpec((gs, D, D_FF), lambda b_, i, c: (c, 0, 0)),
                  pl.BlockSpec((gs, D_FF, D), lambda b_, i, c: (c, 0, 0)),
                  pl.BlockSpec((1, D), const2),
                  pl.BlockSpec((1, D), const2)],
        out_specs=pl.BlockSpec((1, tm, D), row),
        out_shape=jax.ShapeDtypeStruct((B, S, D), F32),
        scratch_shapes=[pltpu.VMEM((tm, D), F32),
                        pltpu.VMEM((tm, D + 2 * LANES), MXU_DTYPE),
                        pltpu.VMEM((8, tm), F32),
                        pltpu.VMEM((8, tm), F32)],
        compiler_params=_cparams(("parallel", "parallel", "arbitrary")),
        name="moe_ffn",
    )(x1, mod, h2, gate, w1, w3, w2, g.reshape(1, D), b.reshape(1, D))


def _rope_tables(seq):
    lane = np.arange(LANES)

    def tables(period, offset, rot, theta):
        half = rot // 2
        d = lane % period - offset
        first = (d >= 0) & (d < half)
        second = (d >= half) & (d < rot)
        fidx = np.where(first, d, np.where(second, d - half, 0))
        inv_freq = theta ** (-2.0 * jnp.arange(half, dtype=F32) / rot)
        ang = jnp.arange(seq, dtype=F32)[:, None] * inv_freq[None, :]
        cos_l = jnp.cos(ang)[:, fidx]
        sin_l = jnp.sin(ang)[:, fidx]
        cos = jnp.where((first | second)[None, :], cos_l, 1.0)
        s_up = jnp.where(first[None, :], -sin_l, 0.0)
        s_dn = jnp.where(second[None, :], sin_l, 0.0)
        return [cos, s_up, s_dn]

    tabs = (tables(DIFF_QK, 0, DIFF_QK // 4, ROPE_THETA)
            + tables(DIL_DIM, 0, DIL_DIM // 4, ROPE_THETA)
            + tables(LANES, KR_LANE, MLA_ROPE, MLA_THETA))
    return jnp.stack(tabs).astype(F32)


def _prep_layer(l, w_in, w_uq, w_ukv, w_out, w1, w3, w2):
    D = w_in.shape[1]
    wi = w_in[l]
    zeros = lambda n: jnp.zeros((D, n), wi.dtype)
    wi = jnp.concatenate([wi[:, :C_KR], zeros(KR_LANE), wi[:, C_KR:], zeros(LANES - KR_LANE - MLA_ROPE)], axis=1)
    uq = w_uq[l].reshape(MLA_Q_RANK, MLA_HEADS, MLA_NOPE + MLA_ROPE)
    uq = jnp.pad(uq, ((0, 0), (0, 0), (0, LANES - MLA_NOPE - MLA_ROPE))).reshape(MLA_Q_RANK, MLA_HEADS * LANES)
    ukv = w_ukv[l].reshape(MLA_KV_RANK, MLA_HEADS, MLA_NOPE + MLA_V)
    uk = jnp.pad(ukv[..., :MLA_NOPE], ((0, 0), (0, 0), (0, LANES - MLA_NOPE))).reshape(MLA_KV_RANK, MLA_HEADS * LANES)
    uv = ukv[..., MLA_NOPE:].reshape(MLA_KV_RANK, MLA_HEADS * MLA_V)
    c = lambda t: t.astype(MXU_DTYPE)
    return dict(w_in=c(wi), w_uq=c(uq), w_uk=c(uk), w_uv=c(uv), w_out=c(w_out[l]),
                w1=c(w1[l]), w3=c(w3[l]), w2=c(w2[l]))


def _tile(n, pref):
    return pref if n % pref == 0 else n


def _trunk(x, mods, tabs, layers, p):
    B, S, D = x.shape
    tm = _tile(S, 512)
    tq = _tile(S, ATTN_TQ * ATTN_PARTS)
    tk = _tile(S, ATTN_TK)
    x = _ln_in(x, p['ln_in_g'], p['ln_in_b'], tm)
    for l, lw in enumerate(layers):
        mod = mods[l]
        lam_init = 0.8 - 0.6 * math.exp(-0.3 * l)
        qa, ka, va, qb, kb, vb, mq, mk, mv = _proj(
            x, mod, tabs, lw['w_in'], p['g_cq'][l], p['g_ckv'][l], lw['w_uq'], lw['w_uk'], lw['w_uv'], tm)
        oa = _diff_attn(qa, ka, va, p['lam_pack'][l], p['g_sub'][l], lam_init, tq, tk)
        oc = _mla_attn(mq, mk, mv, tq, tk)
        ob = _dil_attn(qb, kb, vb)
        x1, h2, gate = _post(x, mod, oa, ob, oc, lw['w_out'], p['ln1_g'][l], p['ln1_b'][l],
                             p['w_r_hi'], p['w_r_lo'], p['b_r'], tm)
        x = _moe(x1, mod, h2, gate, lw['w1'], lw['w3'], lw['w2'], p['ln2_g'][l], p['ln2_b'][l],
                 _tile(S, MOE_TM))
    return x


def kernel(x_prompt, x_sample, c_prompt, c_sample, ln_in_g, ln_in_b, w_router, b_router, w_ada, b_ada, w_in, lam_q1, lam_k1, lam_q2, lam_k2, g_subln, g_cq, g_ckv, w_uq, w_ukv, w_out, ln1_g, ln1_b, w1, w3, w2, ln2_g, ln2_b):
    depth = w_in.shape[0]
    D = x_prompt.shape[-1]
    bp, bs = c_prompt.shape[0], c_sample.shape[0]
    rows = -(-(bp + bs) // 8) * 8
    c_all = jnp.concatenate([c_prompt, c_sample, jnp.zeros((rows - bp - bs, D), F32)], axis=0)
    mods = _ada(c_all, w_ada, b_ada)
    mods_p = mods[:, :bp].reshape(depth, bp, 6, D)
    mods_s = mods[:, bp:bp + bs].reshape(depth, bs, 6, D)

    pad_l = lambda t: jnp.pad(t, ((0, 0), (0, LANES - t.shape[-1])))
    pad_r = lambda t: jnp.pad(t, ((0, 0),) * (t.ndim - 2) + ((0, SMALL_ROWS - t.shape[-2]), (0, 0)))
    p = dict(
        ln_in_g=ln_in_g, ln_in_b=ln_in_b, g_cq=g_cq, g_ckv=pad_r(g_ckv[:, None, :]),
        ln1_g=ln1_g, ln1_b=ln1_b, ln2_g=ln2_g, ln2_b=ln2_b,
        lam_pack=pad_r(jnp.stack([pad_l(lam_q1), pad_l(lam_k1), pad_l(lam_q2), pad_l(lam_k2)], axis=1)),
        g_sub=g_subln.reshape(depth, DIFF_V, 1),
        b_r=pad_r(pad_l(b_router.reshape(1, -1))),
    )
    w_r = pad_l(w_router)
    p['w_r_hi'] = w_r.astype(MXU_DTYPE)
    p['w_r_lo'] = (w_r - p['w_r_hi'].astype(F32)).astype(MXU_DTYPE)
    layers = [_prep_layer(l, w_in, w_uq, w_ukv, w_out, w1, w3, w2) for l in range(depth)]
    tabs = _rope_tables(max(x_prompt.shape[1], x_sample.shape[1]))
    y_prompt = _trunk(x_prompt, mods_p, tabs, layers, p)
    y_sample = _trunk(x_sample, mods_s, tabs, layers, p)
    return (y_prompt, y_sample)
```

```python
import functools
import math

import numpy as np
import jax
import jax.numpy as jnp
from jax import lax
from jax.experimental import pallas as pl
from jax.experimental.pallas import tpu as pltpu

F32 = jnp.float32
MXU_DTYPE = jnp.bfloat16

LANES = 128
DEPTH = 4
DIFF_HEADS = 4
DIFF_QK = 32
DIFF_V = 64
DIL_HEADS = 6
DIL_DIM = 64
DIL_PAIRS = ((128, 1), (512, 4), (2048, 16))
DIL_RADIUS = 64
MLA_HEADS = 6
MLA_NOPE = 64
MLA_ROPE = 32
MLA_V = 64
MLA_Q_RANK = 256
MLA_KV_RANK = 128
MLA_THETA = 10000.0
ROPE_THETA = 500000.0
N_EXPERTS = 16
GROUP_SIZE = 4
D_FF = 256
ALPHA = (2 * DEPTH) ** 0.25
NEG = -1e30
LOG2E = math.log2(math.e)
ATTN_TQ = 512
ATTN_PARTS = 4
SMALL_ROWS = 32
ATTN_TK = 2048

C_Q1, C_Q2, C_K1, C_K2, C_VA = 0, 128, 256, 384, 512
C_QB, C_KB, C_VB = 768, 1152, 1536
C_CQ, C_CKV, C_KR, C_END = 1920, 2176, 2304, 2432
KR_LANE = 64

VMEM_LIMIT = 48 * 1024 * 1024


def _cparams(sem):
    return pltpu.CompilerParams(dimension_semantics=sem, vmem_limit_bytes=VMEM_LIMIT)


def _lane_iota(shape):
    return lax.broadcasted_iota(jnp.int32, shape, len(shape) - 1)


def _layer_norm(z, g, b):
    mu = jnp.mean(z, axis=-1, keepdims=True)
    zc = z - mu
    var = jnp.mean(zc * zc, axis=-1, keepdims=True)
    return zc * lax.rsqrt(var + 1e-5) * g + b


def _rope(t, tab_ref, typ, half, rows):
    c = tab_ref[3 * typ, rows, :]
    s_up = tab_ref[3 * typ + 1, rows, :]
    s_dn = tab_ref[3 * typ + 2, rows, :]
    return t * c + pltpu.roll(t, LANES - half, 1) * s_up + pltpu.roll(t, half, 1) * s_dn


def _ln_in_kernel(x_ref, g_ref, b_ref, o_ref):
    o_ref[0] = _layer_norm(x_ref[0], g_ref[...], b_ref[...])


def _ln_in(x, g, b, tm):
    B, S, D = x.shape
    return pl.pallas_call(
        _ln_in_kernel,
        grid=(B, S // tm),
        in_specs=[pl.BlockSpec((1, tm, D), lambda b, i: (b, i, 0)),
                  pl.BlockSpec((1, D), lambda b, i: (0, 0)),
                  pl.BlockSpec((1, D), lambda b, i: (0, 0))],
        out_specs=pl.BlockSpec((1, tm, D), lambda b, i: (b, i, 0)),
        out_shape=jax.ShapeDtypeStruct(x.shape, F32),
        compiler_params=_cparams(("parallel", "parallel")),
        name="ln_in",
    )(x, g.reshape(1, D), b.reshape(1, D))


def _ada_kernel(c_ref, w_ref, b_ref, o_ref):
    c = c_ref[...]
    cs = (c * (1.0 / (1.0 + jnp.exp(-c)))).astype(MXU_DTYPE)
    o_ref[0] = jnp.dot(cs, w_ref[0].astype(MXU_DTYPE), preferred_element_type=F32) + b_ref[0]


def _ada(c, w_ada, b_ada, tn=1536):
    R, D = c.shape
    L, _, N = w_ada.shape
    return pl.pallas_call(
        _ada_kernel,
        grid=(L, N // tn),
        in_specs=[pl.BlockSpec((R, D), lambda l, j: (0, 0)),
                  pl.BlockSpec((1, D, tn), lambda l, j: (l, 0, j)),
                  pl.BlockSpec((1, 1, tn), lambda l, j: (l, 0, j))],
        out_specs=pl.BlockSpec((1, R, tn), lambda l, j: (l, 0, j)),
        out_shape=jax.ShapeDtypeStruct((L, R, N), F32),
        compiler_params=_cparams(("parallel", "parallel")),
        name="ada",
    )(c, w_ada, b_ada.reshape(L, 1, N))


PROJ_PARTS = 2
V_DIM = 64
V_ROWS = V_DIM + 16


def _store_values_t(ref, vt, heads, cols):
    n = vt.shape[1]
    for h in range(heads):
        ref[0, h * V_ROWS:h * V_ROWS + V_DIM, cols] = vt[h * V_DIM:(h + 1) * V_DIM].astype(ref.dtype)
        ref[0, h * V_ROWS + V_DIM:(h + 1) * V_ROWS, cols] = jnp.ones((V_ROWS - V_DIM, n), ref.dtype)


def _proj_kernel(x_ref, mod_ref, tab_ref, w_ref, gcq_ref, gckv_ref, wuq_ref, wuk_ref, wuv_ref,
                 qa_ref, ka_ref, va_ref, qb_ref, kb_ref, vb_ref, mq_ref, mk_ref, mv_ref):
    sh1 = mod_ref[0, 0:1, :]
    sc1 = mod_ref[0, 1:2, :]
    tm = x_ref.shape[1]
    parts = PROJ_PARTS if tm % (PROJ_PARTS * LANES) == 0 else 1

    def rms(t, g):
        return (t * lax.rsqrt(jnp.mean(t * t, axis=-1, keepdims=True) + 1e-6) * g).astype(MXU_DTYPE)

    for r in range(parts):
        rows = slice(r * (tm // parts), (r + 1) * (tm // parts))
        h = (x_ref[0, rows, :] * (1.0 + sc1) + sh1).astype(MXU_DTYPE)

        def mm(a, b, h=h):
            return jnp.dot(h, w_ref[:, a:b], preferred_element_type=F32)

        rope = functools.partial(_rope, tab_ref=tab_ref, rows=rows)

        qa = mm(C_Q1, C_K1)
        ka = mm(C_K1, C_VA)
        sa = DIFF_QK ** -0.5 * LOG2E
        for j in range(2):
            blk = slice(j * LANES, (j + 1) * LANES)
            qa_ref[0, rows, blk] = (rope(qa[:, blk], typ=0, half=DIFF_QK // 8) * sa).astype(qa_ref.dtype)
            ka_ref[0, rows, blk] = rope(ka[:, blk], typ=0, half=DIFF_QK // 8).astype(ka_ref.dtype)
        _store_values_t(va_ref, mm(C_VA, C_QB).T, DIFF_HEADS, rows)

        qb = mm(C_QB, C_KB)
        kb = mm(C_KB, C_VB)
        vb = mm(C_VB, C_CQ)
        sb = DIL_DIM ** -0.5 * LOG2E
        for j in range(DIL_HEADS // 2):
            blk = slice(j * LANES, (j + 1) * LANES)
            qb_ref[j, 0, rows, :] = (rope(qb[:, blk], typ=1, half=DIL_DIM // 8) * sb).astype(qb_ref.dtype)
            kb_ref[j, 0, rows, :] = rope(kb[:, blk], typ=1, half=DIL_DIM // 8).astype(kb_ref.dtype)
            vb_ref[j, 0, rows, :] = vb[:, blk].astype(vb_ref.dtype)

        cq = rms(mm(C_CQ, C_CKV), gcq_ref[...])
        ckv = rms(mm(C_CKV, C_KR), gckv_ref[0:1, :])
        kr = rope(mm(C_KR, C_END), typ=2, half=MLA_ROPE // 2)
        q = jnp.dot(cq, wuq_ref[...], preferred_element_type=F32)
        kn = jnp.dot(ckv, wuk_ref[...], preferred_element_type=F32)
        sc = (MLA_NOPE + MLA_ROPE) ** -0.5 * LOG2E
        for j in range(MLA_HEADS):
            blk = slice(j * LANES, (j + 1) * LANES)
            mq_ref[0, rows, blk] = (rope(q[:, blk], typ=2, half=MLA_ROPE // 2) * sc).astype(mq_ref.dtype)
            mk_ref[0, rows, blk] = (kn[:, blk] + kr).astype(mk_ref.dtype)
        _store_values_t(mv_ref, jnp.dot(ckv, wuv_ref[...], preferred_element_type=F32).T, MLA_HEADS, rows)


def _proj(x, mod, tabs, w_in, g_cq, g_ckv, w_uq, w_uk, w_uv, tm):
    B, S, D = x.shape
    widths = (256, 256, DIFF_HEADS * V_ROWS, 384, 384, 384, 768, 768, MLA_HEADS * V_ROWS)
    transposed = (2, 8)
    row = lambda b, i: (b, i, 0)
    col = lambda b, i: (b, 0, i)
    const2 = lambda b, i: (0, 0)
    out_specs = [pl.BlockSpec((1, w, tm), col) if n in transposed else pl.BlockSpec((1, tm, w), row)
                 for n, w in enumerate(widths)]
    out_shape = [jax.ShapeDtypeStruct((B, w, S) if n in transposed else (B, S, w), MXU_DTYPE)
                 for n, w in enumerate(widths)]
    for n in (3, 4, 5):
        out_specs[n] = pl.BlockSpec((DIL_HEADS // 2, 1, tm, LANES), lambda b, i: (0, b, i, 0))
        out_shape[n] = jax.ShapeDtypeStruct((DIL_HEADS // 2, B, S, LANES), MXU_DTYPE)
    return pl.pallas_call(
        _proj_kernel,
        grid=(B, S // tm),
        in_specs=[pl.BlockSpec((1, tm, D), row),
                  pl.BlockSpec((1, 6, D), lambda b, i: (b, 0, 0)),
                  pl.BlockSpec((9, tm, LANES), lambda b, i: (0, i, 0)),
                  pl.BlockSpec(w_in.shape, const2),
                  pl.BlockSpec((1, MLA_Q_RANK), const2),
                  pl.BlockSpec((SMALL_ROWS, MLA_KV_RANK), const2),
                  pl.BlockSpec(w_uq.shape, const2),
                  pl.BlockSpec(w_uk.shape, const2),
                  pl.BlockSpec(w_uv.shape, const2)],
        out_specs=out_specs,
        out_shape=out_shape,
        compiler_params=_cparams(("parallel", "parallel")),
        name="proj_in",
    )(x, mod, tabs, w_in, g_cq.reshape(1, -1), g_ckv, w_uq, w_uk, w_uv)


def _softmax_steps(chains, m_scr, acc_scr, base=0):
    scores = [lax.dot_general(k, q, (((1,), (1,)), ((), ())), preferred_element_type=F32)
              for q, k, _ in chains]
    for u, (st, (_, _, vt)) in enumerate(zip(scores, chains), start=base):
        m_prev = m_scr[u]
        m_new = jnp.maximum(m_prev, jnp.max(st, axis=0, keepdims=True))
        alpha = jnp.exp2(m_prev - m_new)
        p = jnp.exp2(st - m_new)
        acc_scr[u] = alpha * acc_scr[u] + jnp.dot(vt, p.astype(MXU_DTYPE), preferred_element_type=F32)
        m_scr[u] = m_new


def _init_softmax(m_scr, acc_scr):
    m_scr[...] = jnp.full(m_scr.shape, NEG, F32)
    acc_scr[...] = jnp.zeros(acc_scr.shape, F32)


def _softmax_out(acc_scr, u):
    return acc_scr[u, :V_DIM] / acc_scr[u, V_DIM:V_DIM + 1]


def _diff_attn_kernel(lam_init, q1_ref, q2_ref, k1_ref, k2_ref, vt_ref, lamp_ref, g_ref, o_ref,
                      q_scr, m_scr, acc_scr):
    j = pl.program_id(1)
    ik = pl.program_id(3)

    t = m_scr.shape[2]
    parts = q1_ref.shape[1] // t

    @pl.when(ik == 0)
    def _():
        _init_softmax(m_scr, acc_scr)
        lane = _lane_iota((t, LANES))
        for r in range(parts):
            rows = slice(r * t, (r + 1) * t)
            for hh in range(2):
                h = 2 * j + hh
                msk = (lane >= h * DIFF_QK) & (lane < (h + 1) * DIFF_QK)
                q_scr[4 * r + 2 * hh] = jnp.where(msk, q1_ref[0, rows, :].astype(F32), 0.0).astype(q_scr.dtype)
                q_scr[4 * r + 2 * hh + 1] = jnp.where(msk, q2_ref[0, rows, :].astype(F32), 0.0).astype(q_scr.dtype)

    for r in range(parts):
        chains = []
        for hh in range(2):
            vt = vt_ref[0, hh * V_ROWS:(hh + 1) * V_ROWS, :]
            chains.append((q_scr[4 * r + 2 * hh], k1_ref[0], vt))
            chains.append((q_scr[4 * r + 2 * hh + 1], k2_ref[0], vt))
        _softmax_steps(chains, m_scr, acc_scr, 4 * r)

    @pl.when(ik == pl.num_programs(3) - 1)
    def _():
        lp = lamp_ref[0:4, :]
        lam = (jnp.exp(jnp.sum(lp[0:1] * lp[1:2], axis=-1, keepdims=True))
               - jnp.exp(jnp.sum(lp[2:3] * lp[3:4], axis=-1, keepdims=True)) + lam_init)
        for r in range(parts):
            for hh in range(2):
                u = 4 * r + 2 * hh
                o = _softmax_out(acc_scr, u) - lam * _softmax_out(acc_scr, u + 1)
                ms = jnp.mean(o * o, axis=0, keepdims=True)
                o_ref[0, hh * DIFF_V:(hh + 1) * DIFF_V, r * t:(r + 1) * t] = (
                    o * lax.rsqrt(ms + 1e-6) * g_ref[...] * (1.0 - lam_init)).astype(o_ref.dtype)


def _diff_attn(qa, ka, vat, lam_pack, g_sub, lam_init, tq, tk):
    B, S, _ = qa.shape
    parts = tq // ATTN_TQ if tq % ATTN_TQ == 0 else 1
    kern = functools.partial(_diff_attn_kernel, lam_init)
    return pl.pallas_call(
        kern,
        grid=(B, DIFF_HEADS // 2, S // tq, S // tk),
        in_specs=[pl.BlockSpec((1, tq, LANES), lambda b, j, iq, ik: (b, iq, 0)),
                  pl.BlockSpec((1, tq, LANES), lambda b, j, iq, ik: (b, iq, 1)),
                  pl.BlockSpec((1, tk, LANES), lambda b, j, iq, ik: (b, ik, 0)),
                  pl.BlockSpec((1, tk, LANES), lambda b, j, iq, ik: (b, ik, 1)),
                  pl.BlockSpec((1, 2 * V_ROWS, tk), lambda b, j, iq, ik: (b, j, ik)),
                  pl.BlockSpec((SMALL_ROWS, LANES), lambda b, j, iq, ik: (0, 0)),
                  pl.BlockSpec((DIFF_V, 1), lambda b, j, iq, ik: (0, 0))],
        out_specs=pl.BlockSpec((1, 2 * DIFF_V, tq), lambda b, j, iq, ik: (b, j, iq)),
        out_shape=jax.ShapeDtypeStruct((B, DIFF_HEADS * DIFF_V, S), MXU_DTYPE),
        scratch_shapes=[pltpu.VMEM((4 * parts, tq // parts, LANES), MXU_DTYPE),
                        pltpu.VMEM((4 * parts, 1, tq // parts), F32),
                        pltpu.VMEM((4 * parts, V_ROWS, tq // parts), F32)],
        compiler_params=_cparams(("parallel", "parallel", "parallel", "arbitrary")),
        name="diff_attn",
    )(qa, qa, ka, ka, vat, lam_pack, g_sub)


MLA_GROUP = 6


def _mla_attn_kernel(q_ref, k_ref, vt_ref, o_ref, m_scr, acc_scr):
    ik = pl.program_id(3)

    @pl.when(ik == 0)
    def _():
        _init_softmax(m_scr, acc_scr)

    t = m_scr.shape[2]
    parts = q_ref.shape[1] // t
    for r in range(parts):
        rows = slice(r * t, (r + 1) * t)
        chains = []
        for hh in range(MLA_GROUP):
            blk = slice(hh * LANES, (hh + 1) * LANES)
            chains.append((q_ref[0, rows, blk], k_ref[0, :, blk], vt_ref[0, hh * V_ROWS:(hh + 1) * V_ROWS, :]))
        _softmax_steps(chains, m_scr, acc_scr, r * MLA_GROUP)

    @pl.when(ik == pl.num_programs(3) - 1)
    def _():
        for r in range(parts):
            for hh in range(MLA_GROUP):
                o_ref[0, hh * MLA_V:(hh + 1) * MLA_V, r * t:(r + 1) * t] = _softmax_out(
                    acc_scr, r * MLA_GROUP + hh).astype(o_ref.dtype)


def _mla_attn(mq, mk, mvt, tq, tk):
    B, S, _ = mq.shape
    g = MLA_GROUP
    parts = tq // ATTN_TQ if tq % ATTN_TQ == 0 else 1
    return pl.pallas_call(
        _mla_attn_kernel,
        grid=(B, MLA_HEADS // g, S // tq, S // tk),
        in_specs=[pl.BlockSpec((1, tq, g * LANES), lambda b, j, iq, ik: (b, iq, j)),
                  pl.BlockSpec((1, tk, g * LANES), lambda b, j, iq, ik: (b, ik, j)),
                  pl.BlockSpec((1, g * V_ROWS, tk), lambda b, j, iq, ik: (b, j, ik))],
        out_specs=pl.BlockSpec((1, g * MLA_V, tq), lambda b, j, iq, ik: (b, j, iq)),
        out_shape=jax.ShapeDtypeStruct((B, MLA_HEADS * MLA_V, S), MXU_DTYPE),
        scratch_shapes=[pltpu.VMEM((g * parts, 1, tq // parts), F32),
                        pltpu.VMEM((g * parts, V_ROWS, tq // parts), F32)],
        compiler_params=_cparams(("parallel", "parallel", "parallel", "arbitrary")),
        name="mla_attn",
    )(mq, mk, mvt)


DIL_SUB = 128
DIL_WIDTH = DIL_HEADS * DIL_DIM


DIL_DILS = tuple(d for _, d in DIL_PAIRS)
DIL_HALO = DIL_RADIUS * DIL_DILS[-1]
DIL_TILE = 2048
DIL_SPLIT = 4


def _dil_kernel(seq, q_ref, kp_ref, kc_ref, kn_ref, vp_ref, vc_ref, vn_ref, o_ref,
                qbuf, kbuf, vbuf, pat_o, pat_l, qg, kg, vg, q4, k4, v4):
    i = pl.program_id(2)
    tt = q_ref.shape[2]
    H = DIL_HALO
    R = DIL_RADIUS
    qbuf[...] = q_ref[0, 0].astype(F32)
    kbuf[0:H] = kp_ref[0, 0].astype(F32)
    kbuf[H:H + tt] = kc_ref[0, 0].astype(F32)
    kbuf[H + tt:] = kn_ref[0, 0].astype(F32)
    vbuf[0:H] = vp_ref[0, 0].astype(F32)
    vbuf[H:H + tt] = vc_ref[0, 0].astype(F32)
    vbuf[H + tt:] = vn_ref[0, 0].astype(F32)
    vg[:, LANES:, :] = jnp.ones((vg.shape[0], vg.shape[1] - LANES, vg.shape[2]), vg.dtype)
    for r4 in range(DIL_SPLIT):
        q4[r4] = qbuf[pl.ds(r4, tt // DIL_SPLIT, stride=DIL_SPLIT), :]
        k4[r4] = kbuf[pl.ds(r4, (tt + 2 * H) // DIL_SPLIT, stride=DIL_SPLIT), :]
        v4[r4] = vbuf[pl.ds(r4, (tt + 2 * H) // DIL_SPLIT, stride=DIL_SPLIT), :]
    sub = DIL_SUB
    win = sub + 2 * R
    nc = tt // sub
    cidx = lax.broadcasted_iota(jnp.int32, (nc, win, sub), 0)
    row_k = lax.broadcasted_iota(jnp.int32, (nc, win, sub), 1)
    col_q = lax.broadcasted_iota(jnp.int32, (nc, win, sub), 2)
    rel = row_k - col_q
    band = (rel >= 0) & (rel <= 2 * R)
    lane = _lane_iota((sub, LANES))
    for p, d in enumerate(DIL_DILS):
        n_sub = tt // d // sub
        length = seq // d
        q_starts = [c // n_sub + d * sub * (c % n_sub) for c in range(nc)]
        for c in range(nc):
            r, u = c // n_sub, c % n_sub
            y0 = H // d + u * sub - R
            if d % DIL_SPLIT:
                qf = qbuf[pl.ds(q_starts[c], sub, stride=d), :]
                kw = kbuf[pl.ds(r + d * y0, win, stride=d), :]
                vw = vbuf[pl.ds(r + d * y0, win, stride=d), :]
            else:
                e = d // DIL_SPLIT
                r4, m = r % DIL_SPLIT, r // DIL_SPLIT
                qf = q4[r4, pl.ds(m + e * sub * u, sub, stride=e), :]
                kw = k4[r4, pl.ds(m + e * y0, win, stride=e), :]
                vw = v4[r4, pl.ds(m + e * y0, win, stride=e), :]
            qg[c, :sub] = jnp.where(lane < DIL_DIM, qf, 0.0).astype(qg.dtype)
            qg[c, sub:] = jnp.where(lane >= DIL_DIM, qf, 0.0).astype(qg.dtype)
            kg[c] = kw.astype(kg.dtype)
            vg[c, :LANES] = vw.T.astype(vg.dtype)
        u_of_c = cidx % n_sub if n_sub > 1 else jnp.zeros_like(cidx)
        kpos = i * (tt // d) + u_of_c * sub - R + row_k
        valid = band & (kpos >= 0) & (kpos < length)
        s = lax.dot_general(kg[...], qg[...], (((2,), (2,)), ((0,), (0,))),
                            preferred_element_type=F32)
        s = jnp.where(jnp.concatenate([valid, valid], axis=2), s, NEG)
        m = jnp.max(s, axis=1, keepdims=True)
        e = jnp.exp2(s - m)
        ot = lax.dot_general(vg[...], e.astype(MXU_DTYPE), (((2,), (1,)), ((0,), (0,))),
                             preferred_element_type=F32)
        l = ot[:, LANES:LANES + 1, :]
        ot = ot[:, :LANES, :] / l
        lse = m + jnp.log2(l)
        for c in range(nc):
            o_t = jnp.concatenate([ot[c, :DIL_DIM, :sub], ot[c, DIL_DIM:, sub:]], axis=0)
            l_t = jnp.concatenate([jnp.broadcast_to(lse[c, :, :sub], (DIL_DIM, sub)),
                                   jnp.broadcast_to(lse[c, :, sub:], (DIL_DIM, sub))], axis=0)
            pat_o[p, pl.ds(q_starts[c], sub, stride=d), :] = o_t.T
            pat_l[p, pl.ds(q_starts[c], sub, stride=d), :] = l_t.T

    l0, l1, l2 = pat_l[0], pat_l[1], pat_l[2]
    mx = jnp.maximum(jnp.maximum(l0, l1), l2)
    e0, e1, e2 = jnp.exp2(l0 - mx), jnp.exp2(l1 - mx), jnp.exp2(l2 - mx)
    den = e0 + e1 + e2
    o_ref[0] = ((e0 / den) * pat_o[0] + (e1 / den) * pat_o[1] + (e2 / den) * pat_o[2]).astype(o_ref.dtype)


def _dil_attn(qb, kb, vb):
    P, B, S, _ = qb.shape
    tt = _tile(S, DIL_TILE)
    H = DIL_HALO
    nh = S // H
    cur = pl.BlockSpec((1, 1, tt, LANES), lambda b, hp, i: (hp, b, i, 0))
    prv = pl.BlockSpec((1, 1, H, LANES), lambda b, hp, i: (hp, b, jnp.maximum(i * (tt // H) - 1, 0), 0))
    nxt = pl.BlockSpec((1, 1, H, LANES), lambda b, hp, i: (hp, b, jnp.minimum((i + 1) * (tt // H), nh - 1), 0))
    return pl.pallas_call(
        functools.partial(_dil_kernel, S),
        grid=(B, P, S // tt),
        in_specs=[cur, prv, cur, nxt, prv, cur, nxt],
        out_specs=pl.BlockSpec((1, tt, LANES), lambda b, hp, i: (b, i, hp)),
        out_shape=jax.ShapeDtypeStruct((B, S, P * LANES), MXU_DTYPE),
        scratch_shapes=[pltpu.VMEM((tt, LANES), F32),
                        pltpu.VMEM((tt + 2 * H, LANES), F32),
                        pltpu.VMEM((tt + 2 * H, LANES), F32),
                        pltpu.VMEM((len(DIL_DILS), tt, LANES), F32),
                        pltpu.VMEM((len(DIL_DILS), tt, LANES), F32),
                        pltpu.VMEM((tt // DIL_SUB, 2 * DIL_SUB, LANES), MXU_DTYPE),
                        pltpu.VMEM((tt // DIL_SUB, DIL_SUB + 2 * DIL_RADIUS, LANES), MXU_DTYPE),
                        pltpu.VMEM((tt // DIL_SUB, LANES + 16, DIL_SUB + 2 * DIL_RADIUS), MXU_DTYPE),
                        pltpu.VMEM((DIL_SPLIT, tt // DIL_SPLIT, LANES), F32),
                        pltpu.VMEM((DIL_SPLIT, (tt + 2 * H) // DIL_SPLIT, LANES), F32),
                        pltpu.VMEM((DIL_SPLIT, (tt + 2 * H) // DIL_SPLIT, LANES), F32)],
        compiler_params=_cparams(("parallel", "parallel", "parallel")),
        name="dil_attn",
    )(qb, kb, kb, kb, vb, vb, vb)


def _partner(x, k, lane):
    up = pltpu.roll(x, LANES - k, 1)
    dn = pltpu.roll(x, k, 1)
    return jnp.where((lane & k) == 0, up, dn)


GROUP_LANE = N_EXPERTS


def _route(logits, bias, lane):
    valid = lane < N_EXPERTS
    lane_f = lane.astype(F32)
    score = 1.0 / (1.0 + jnp.exp(-logits))
    biased = jnp.where(valid, score + bias, NEG)
    p1 = _partner(biased, 1, lane)
    hi = jnp.maximum(biased, p1)
    lo = jnp.minimum(biased, p1)
    hi_p = _partner(hi, 2, lane)
    lo_p = _partner(lo, 2, lane)
    top1 = jnp.maximum(hi, hi_p)
    top2 = jnp.maximum(jnp.minimum(hi, hi_p), jnp.maximum(lo, lo_p))
    gscore = jnp.where(valid, top1 + top2, NEG)
    gidx = (lane // GROUP_SIZE).astype(F32)
    big = float(LANES)
    gmax = jnp.max(gscore, axis=-1, keepdims=True)
    gsel = jnp.min(jnp.where(gscore == gmax, gidx, big), axis=-1, keepdims=True)
    cand = jnp.where((gidx == gsel) & valid, biased, NEG)
    m1 = jnp.max(cand, axis=-1, keepdims=True)
    i1 = jnp.min(jnp.where(cand == m1, lane_f, big), axis=-1, keepdims=True)
    oh1 = lane_f == i1
    cand2 = jnp.where(oh1, NEG, cand)
    m2 = jnp.max(cand2, axis=-1, keepdims=True)
    i2 = jnp.min(jnp.where((cand2 == m2) & jnp.logical_not(oh1), lane_f, big), axis=-1, keepdims=True)
    oh2 = lane_f == i2
    s1 = jnp.sum(jnp.where(oh1, score, 0.0), axis=-1, keepdims=True)
    s2 = jnp.sum(jnp.where(oh2, score, 0.0), axis=-1, keepdims=True)
    den = s1 + s2
    gate = jnp.where(oh1, s1 / den, 0.0) + jnp.where(oh2, s2 / den, 0.0)
    return jnp.where(lane == GROUP_LANE, gsel, gate)


POST_PARTS = 2


def _post_kernel(x_ref, mod_ref, oa_ref, ob_ref, oc_ref, wo_ref, g_ref, b_ref, wrh_ref, wrl_ref, br_ref,
                 x1_ref, h2_ref, gate_ref):
    na = DIFF_HEADS * DIFF_V
    nb = na + DIL_WIDTH
    tn = (((0,), (0,)), ((), ()))
    g1 = mod_ref[0, 2:3, :]
    sh2 = mod_ref[0, 3:4, :]
    sc2 = mod_ref[0, 4:5, :]
    tm = x_ref.shape[1]
    parts = POST_PARTS if tm % (POST_PARTS * LANES) == 0 else 1
    for r in range(parts):
        rows = slice(r * (tm // parts), (r + 1) * (tm // parts))
        y = (lax.dot_general(oa_ref[0, :, rows], wo_ref[0:na], tn, preferred_element_type=F32)
             + jnp.dot(ob_ref[0, rows, :], wo_ref[na:nb], preferred_element_type=F32)
             + lax.dot_general(oc_ref[0, :, rows], wo_ref[nb:], tn, preferred_element_type=F32))
        x1 = _layer_norm(ALPHA * x_ref[0, rows, :] + g1 * y, g_ref[...], b_ref[...])
        x1_ref[0, rows, :] = x1
        h2 = x1 * (1.0 + sc2) + sh2
        h2_ref[0, rows, :] = h2.astype(h2_ref.dtype)
        h_hi = h2.astype(MXU_DTYPE)
        h_lo = (h2 - h_hi.astype(F32)).astype(MXU_DTYPE)
        logits = (jnp.dot(h_hi, wrh_ref[...], preferred_element_type=F32)
                  + jnp.dot(h_lo, wrh_ref[...], preferred_element_type=F32)
                  + jnp.dot(h_hi, wrl_ref[...], preferred_element_type=F32))
        gate_ref[0, rows, :] = _route(logits, br_ref[0:1, :], _lane_iota(logits.shape))


def _post(x, mod, oa, ob, oc, w_out, g, b, w_r_hi, w_r_lo, b_r, tm):
    B, S, D = x.shape
    row = lambda b_, i: (b_, i, 0)
    const2 = lambda b_, i: (0, 0)
    return pl.pallas_call(
        _post_kernel,
        grid=(B, S // tm),
        in_specs=[pl.BlockSpec((1, tm, D), row),
                  pl.BlockSpec((1, 6, D), lambda b_, i: (b_, 0, 0)),
                  pl.BlockSpec((1, oa.shape[1], tm), lambda b_, i: (b_, 0, i)),
                  pl.BlockSpec((1, tm, DIL_WIDTH), row),
                  pl.BlockSpec((1, oc.shape[1], tm), lambda b_, i: (b_, 0, i)),
                  pl.BlockSpec(w_out.shape, const2),
                  pl.BlockSpec((1, D), const2),
                  pl.BlockSpec((1, D), const2),
                  pl.BlockSpec(w_r_hi.shape, const2),
                  pl.BlockSpec(w_r_lo.shape, const2),
                  pl.BlockSpec((SMALL_ROWS, LANES), const2)],
        out_specs=[pl.BlockSpec((1, tm, D), row),
                   pl.BlockSpec((1, tm, D), row),
                   pl.BlockSpec((1, tm, LANES), row)],
        out_shape=[jax.ShapeDtypeStruct((B, S, D), F32),
                   jax.ShapeDtypeStruct((B, S, D), MXU_DTYPE),
                   jax.ShapeDtypeStruct((B, S, LANES), F32)],
        compiler_params=_cparams(("parallel", "parallel")),
        name="post_mix",
    )(x, mod, oa, ob, oc, w_out, g.reshape(1, D), b.reshape(1, D), w_r_hi, w_r_lo, b_r)


MOE_TM = 1024
MOE_CHUNK = 256
MOE_TAIL = 64
N_GROUPS = N_EXPERTS // GROUP_SIZE


def _moe_kernel(x1_ref, mod_ref, h2_ref, gate_ref, w1_ref, w3_ref, w2_ref, g_ref, b_ref, o_ref,
                acc_scr, src_scr, sel_scr, rank_scr):
    g = pl.program_id(2)
    tm = h2_ref.shape[1]
    d = h2_ref.shape[2]
    C = MOE_CHUNK

    @pl.when(g == 0)
    def _():
        acc_scr[...] = jnp.zeros(acc_scr.shape, F32)
        gate = gate_ref[0]
        gate_hi = gate.astype(MXU_DTYPE)
        src_scr[:, 0:d] = h2_ref[0]
        src_scr[:, d:d + LANES] = gate_hi
        src_scr[:, d + LANES:] = (gate - gate_hi.astype(F32)).astype(MXU_DTYPE)
        gsel_t = gate.T[GROUP_LANE:GROUP_LANE + 1, :]
        grp = lax.broadcasted_iota(jnp.int32, sel_scr.shape, 0).astype(F32)
        sel = jnp.where(gsel_t == grp, 1.0, 0.0)
        r = lax.broadcasted_iota(jnp.int32, (tm, tm), 0)
        c = lax.broadcasted_iota(jnp.int32, (tm, tm), 1)
        before = jnp.where(r < c, 1.0, 0.0).astype(MXU_DTYPE)
        sel_scr[...] = sel
        rank_scr[...] = jnp.dot(sel.astype(MXU_DTYPE), before, preferred_element_type=F32)

    sel = sel_scr[pl.ds(g, 1), :]
    rank = rank_scr[pl.ds(g, 1), :]
    n_g = jnp.sum(sel).astype(jnp.int32)

    def chunk(base, rows):
        lane = _lane_iota((rows, LANES))
        row = lax.broadcasted_iota(jnp.int32, (rows, tm), 0).astype(F32)
        onehot = jnp.where((rank - base.astype(F32) == row) & (sel > 0.5), 1.0, 0.0).astype(MXU_DTYPE)
        picked = jnp.dot(onehot, src_scr[...], preferred_element_type=F32)
        xs = picked[:, 0:d].astype(MXU_DTYPE)
        gs = picked[:, d:d + LANES] + picked[:, d + LANES:]
        ys = jnp.zeros((rows, d), F32)
        for e in range(GROUP_SIZE):
            ge = jnp.sum(jnp.where(lane == g * GROUP_SIZE + e, gs, 0.0), axis=-1, keepdims=True)
            a = jnp.dot(xs, w1_ref[e], preferred_element_type=F32)
            bb = jnp.dot(xs, w3_ref[e], preferred_element_type=F32)
            hid = (a * (1.0 / (1.0 + jnp.exp(-a)))) * bb * ge
            ys = ys + jnp.dot(hid.astype(MXU_DTYPE), w2_ref[e], preferred_element_type=F32)
        acc_scr[...] += lax.dot_general(onehot, ys.astype(MXU_DTYPE), (((0,), (0,)), ((), ())),
                                        preferred_element_type=F32)

    n_chunks = (n_g + C - 1) // C
    tail = n_g - (n_chunks - 1) * C
    merge = (n_chunks >= 2) & (tail <= MOE_TAIL)
    n_plain = jnp.where(merge, n_chunks - 2, n_chunks)

    def plain(ci, carry):
        chunk(ci * C, C)
        return carry

    lax.fori_loop(0, n_plain, plain, 0)

    @pl.when(merge)
    def _():
        chunk(n_plain * C, C + MOE_TAIL)

    @pl.when(g == pl.num_programs(2) - 1)
    def _():
        g2 = mod_ref[0, 5:6, :]
        o_ref[0] = _layer_norm(ALPHA * x1_ref[0] + g2 * acc_scr[...], g_ref[...], b_ref[...])


def _moe(x1, mod, h2, gate, w1, w3, w2, g, b, tm):
    B, S, D = x1.shape
    gs = GROUP_SIZE
    row = lambda b_, i, c: (b_, i, 0)
    const2 = lambda b_, i, c: (0, 0)
    return pl.pallas_call(
        _moe_kernel,
        grid=(B, S // tm, N_GROUPS),
        in_specs=[pl.BlockSpec((1, tm, D), row),
                  pl.BlockSpec((1, 6, D), lambda b_, i, c: (b_, 0, 0)),
                  pl.BlockSpec((1, tm, D), row),
                  pl.BlockSpec((1, tm, LANES), row),
                  pl.BlockSpec((gs, D, D_FF), lambda b_, i, c: (c, 0, 0)),
                  pl.BlockSpec((gs, D, D_FF), lambda b_, i, c: (c, 0, 0)),
                  pl.BlockSpec((gs, D_FF, D), lambda b_, i, c: (c, 0, 0)),
                  pl.BlockSpec((1, D), const2),
                  pl.BlockSpec((1, D), const2)],
        out_specs=pl.BlockSpec((1, tm, D), row),
        out_shape=jax.ShapeDtypeStruct((B, S, D), F32),
        scratch_shapes=[pltpu.VMEM((tm, D), F32),
                        pltpu.VMEM((tm, D + 2 * LANES), MXU_DTYPE),
                        pltpu.VMEM((8, tm), F32),
                        pltpu.VMEM((8, tm), F32)],
        compiler_params=_cparams(("parallel", "parallel", "arbitrary")),
        name="moe_ffn",
    )(x1, mod, h2, gate, w1, w3, w2, g.reshape(1, D), b.reshape(1, D))


def _rope_tables(seq):
    lane = np.arange(LANES)

    def tables(period, offset, rot, theta):
        half = rot // 2
        d = lane % period - offset
        first = (d >= 0) & (d < half)
        second = (d >= half) & (d < rot)
        fidx = np.where(first, d, np.where(second, d - half, 0))
        inv_freq = theta ** (-2.0 * jnp.arange(half, dtype=F32) / rot)
        ang = jnp.arange(seq, dtype=F32)[:, None] * inv_freq[None, :]
        cos_l = jnp.cos(ang)[:, fidx]
        sin_l = jnp.sin(ang)[:, fidx]
        cos = jnp.where((first | second)[None, :], cos_l, 1.0)
        s_up = jnp.where(first[None, :], -sin_l, 0.0)
        s_dn = jnp.where(second[None, :], sin_l, 0.0)
        return [cos, s_up, s_dn]

    tabs = (tables(DIFF_QK, 0, DIFF_QK // 4, ROPE_THETA)
            + tables(DIL_DIM, 0, DIL_DIM // 4, ROPE_THETA)
            + tables(LANES, KR_LANE, MLA_ROPE, MLA_THETA))
    return jnp.stack(tabs).astype(F32)


def _prep_layer(l, w_in, w_uq, w_ukv, w_out, w1, w3, w2):
    D = w_in.shape[1]
    wi = w_in[l]
    zeros = lambda n: jnp.zeros((D, n), wi.dtype)
    wi = jnp.concatenate([wi[:, :C_KR], zeros(KR_LANE), wi[:, C_KR:], zeros(LANES - KR_LANE - MLA_ROPE)], axis=1)
    uq = w_uq[l].reshape(MLA_Q_RANK, MLA_HEADS, MLA_NOPE + MLA_ROPE)
    uq = jnp.pad(uq, ((0, 0), (0, 0), (0, LANES - MLA_NOPE - MLA_ROPE))).reshape(MLA_Q_RANK, MLA_HEADS * LANES)
    ukv = w_ukv[l].reshape(MLA_KV_RANK, MLA_HEADS, MLA_NOPE + MLA_V)
    uk = jnp.pad(ukv[..., :MLA_NOPE], ((0, 0), (0, 0), (0, LANES - MLA_NOPE))).reshape(MLA_KV_RANK, MLA_HEADS * LANES)
    uv = ukv[..., MLA_NOPE:].reshape(MLA_KV_RANK, MLA_HEADS * MLA_V)
    c = lambda t: t.astype(MXU_DTYPE)
    return dict(w_in=c(wi), w_uq=c(uq), w_uk=c(uk), w_uv=c(uv), w_out=c(w_out[l]),
                w1=c(w1[l]), w3=c(w3[l]), w2=c(w2[l]))


def _tile(n, pref):
    return pref if n % pref == 0 else n


TOKEN_TM = 512


def _trunk(x, mods, tabs, layers, p):
    B, S, D = x.shape
    assert S % (DIL_DILS[-1] * DIL_SUB) == 0 and S % DIL_HALO == 0, S
    assert D == layers[0]['w_out'].shape[1], (D, layers[0]['w_out'].shape)
    tm = _tile(S, TOKEN_TM)
    tq = _tile(S, ATTN_TQ * ATTN_PARTS)
    tk = _tile(S, ATTN_TK)
    x = _ln_in(x, p['ln_in_g'], p['ln_in_b'], tm)
    for l, lw in enumerate(layers):
        mod = mods[l]
        lam_init = 0.8 - 0.6 * math.exp(-0.3 * l)
        qa, ka, va, qb, kb, vb, mq, mk, mv = _proj(
            x, mod, tabs, lw['w_in'], p['g_cq'][l], p['g_ckv'][l], lw['w_uq'], lw['w_uk'], lw['w_uv'], tm)
        oa = _diff_attn(qa, ka, va, p['lam_pack'][l], p['g_sub'][l], lam_init, tq, tk)
        oc = _mla_attn(mq, mk, mv, tq, tk)
        ob = _dil_attn(qb, kb, vb)
        x1, h2, gate = _post(x, mod, oa, ob, oc, lw['w_out'], p['ln1_g'][l], p['ln1_b'][l],
                             p['w_r_hi'], p['w_r_lo'], p['b_r'], tm)
        x = _moe(x1, mod, h2, gate, lw['w1'], lw['w3'], lw['w2'], p['ln2_g'][l], p['ln2_b'][l],
                 _tile(S, MOE_TM))
    return x


def kernel(x_prompt, x_sample, c_prompt, c_sample, ln_in_g, ln_in_b, w_router, b_router, w_ada, b_ada, w_in, lam_q1, lam_k1, lam_q2, lam_k2, g_subln, g_cq, g_ckv, w_uq, w_ukv, w_out, ln1_g, ln1_b, w1, w3, w2, ln2_g, ln2_b):
    depth = w_in.shape[0]
    D = x_prompt.shape[-1]
    bp, bs = c_prompt.shape[0], c_sample.shape[0]
    rows = -(-(bp + bs) // 8) * 8
    c_all = jnp.concatenate([c_prompt, c_sample, jnp.zeros((rows - bp - bs, D), F32)], axis=0)
    mods = _ada(c_all, w_ada, b_ada)
    mods_p = mods[:, :bp].reshape(depth, bp, 6, D)
    mods_s = mods[:, bp:bp + bs].reshape(depth, bs, 6, D)

    pad_l = lambda t: jnp.pad(t, ((0, 0), (0, LANES - t.shape[-1])))
    pad_r = lambda t: jnp.pad(t, ((0, 0),) * (t.ndim - 2) + ((0, SMALL_ROWS - t.shape[-2]), (0, 0)))
    p = dict(
        ln_in_g=ln_in_g, ln_in_b=ln_in_b, g_cq=g_cq, g_ckv=pad_r(g_ckv[:, None, :]),
        ln1_g=ln1_g, ln1_b=ln1_b, ln2_g=ln2_g, ln2_b=ln2_b,
        lam_pack=pad_r(jnp.stack([pad_l(lam_q1), pad_l(lam_k1), pad_l(lam_q2), pad_l(lam_k2)], axis=1)),
        g_sub=g_subln.reshape(depth, DIFF_V, 1),
        b_r=pad_r(pad_l(b_router.reshape(1, -1))),
    )
    w_r = pad_l(w_router)
    p['w_r_hi'] = w_r.astype(MXU_DTYPE)
    p['w_r_lo'] = (w_r - p['w_r_hi'].astype(F32)).astype(MXU_DTYPE)
    layers = [_prep_layer(l, w_in, w_uq, w_ukv, w_out, w1, w3, w2) for l in range(depth)]
    tabs = _rope_tables(max(x_prompt.shape[1], x_sample.shape[1]))
    y_prompt = _trunk(x_prompt, mods_p, tabs, layers, p)
    y_sample = _trunk(x_sample, mods_s, tabs, layers, p)
    return (y_prompt, y_sample)
```

```python
import functools
import math

import numpy as np
import jax
import jax.numpy as jnp
from jax import lax
from jax.experimental import pallas as pl
from jax.experimental.pallas import tpu as pltpu

F32 = jnp.float32
MXU_DTYPE = jnp.bfloat16

LANES = 128
DEPTH = 4
DIFF_HEADS = 4
DIFF_QK = 32
DIFF_V = 64
DIL_HEADS = 6
DIL_DIM = 64
DIL_PAIRS = ((128, 1), (512, 4), (2048, 16))
DIL_RADIUS = 64
MLA_HEADS = 6
MLA_NOPE = 64
MLA_ROPE = 32
MLA_V = 64
MLA_Q_RANK = 256
MLA_KV_RANK = 128
MLA_THETA = 10000.0
ROPE_THETA = 500000.0
N_EXPERTS = 16
GROUP_SIZE = 4
D_FF = 256
ALPHA = (2 * DEPTH) ** 0.25
NEG = -1e30
LOG2E = math.log2(math.e)
ATTN_TQ = 512
ATTN_PARTS = 4
SMALL_ROWS = 32
ATTN_TK = 2048

C_Q1, C_Q2, C_K1, C_K2, C_VA = 0, 128, 256, 384, 512
C_QB, C_KB, C_VB = 768, 1152, 1536
C_CQ, C_CKV, C_KR, C_END = 1920, 2176, 2304, 2432
KR_LANE = 64

VMEM_LIMIT = 48 * 1024 * 1024


def _cparams(sem):
    return pltpu.CompilerParams(dimension_semantics=sem, vmem_limit_bytes=VMEM_LIMIT)


def _lane_iota(shape):
    return lax.broadcasted_iota(jnp.int32, shape, len(shape) - 1)


def _layer_norm(z, g, b):
    mu = jnp.mean(z, axis=-1, keepdims=True)
    zc = z - mu
    var = jnp.mean(zc * zc, axis=-1, keepdims=True)
    return zc * lax.rsqrt(var + 1e-5) * g + b


def _rope(t, tab_ref, typ, half, rows):
    c = tab_ref[3 * typ, rows, :]
    s_up = tab_ref[3 * typ + 1, rows, :]
    s_dn = tab_ref[3 * typ + 2, rows, :]
    return t * c + pltpu.roll(t, LANES - half, 1) * s_up + pltpu.roll(t, half, 1) * s_dn


def _ln_in_kernel(x_ref, g_ref, b_ref, o_ref):
    o_ref[0] = _layer_norm(x_ref[0], g_ref[...], b_ref[...])


def _ln_in(x, g, b, tm):
    B, S, D = x.shape
    return pl.pallas_call(
        _ln_in_kernel,
        grid=(B, S // tm),
        in_specs=[pl.BlockSpec((1, tm, D), lambda b, i: (b, i, 0)),
                  pl.BlockSpec((1, D), lambda b, i: (0, 0)),
                  pl.BlockSpec((1, D), lambda b, i: (0, 0))],
        out_specs=pl.BlockSpec((1, tm, D), lambda b, i: (b, i, 0)),
        out_shape=jax.ShapeDtypeStruct(x.shape, F32),
        compiler_params=_cparams(("parallel", "parallel")),
        name="ln_in",
    )(x, g.reshape(1, D), b.reshape(1, D))


def _ada_kernel(c_ref, w_ref, b_ref, o_ref):
    c = c_ref[...]
    cs = (c * (1.0 / (1.0 + jnp.exp(-c)))).astype(MXU_DTYPE)
    o_ref[0] = jnp.dot(cs, w_ref[0].astype(MXU_DTYPE), preferred_element_type=F32) + b_ref[0]


def _ada(c, w_ada, b_ada, tn=1536):
    R, D = c.shape
    L, _, N = w_ada.shape
    return pl.pallas_call(
        _ada_kernel,
        grid=(L, N // tn),
        in_specs=[pl.BlockSpec((R, D), lambda l, j: (0, 0)),
                  pl.BlockSpec((1, D, tn), lambda l, j: (l, 0, j)),
                  pl.BlockSpec((1, 1, tn), lambda l, j: (l, 0, j))],
        out_specs=pl.BlockSpec((1, R, tn), lambda l, j: (l, 0, j)),
        out_shape=jax.ShapeDtypeStruct((L, R, N), F32),
        compiler_params=_cparams(("parallel", "parallel")),
        name="ada",
    )(c, w_ada, b_ada.reshape(L, 1, N))


PROJ_PARTS = 2
V_DIM = 64
V_ROWS = V_DIM + 16


def _store_values_t(ref, vt, heads, cols):
    n = vt.shape[1]
    for h in range(heads):
        ref[0, h * V_ROWS:h * V_ROWS + V_DIM, cols] = vt[h * V_DIM:(h + 1) * V_DIM].astype(ref.dtype)
        ref[0, h * V_ROWS + V_DIM:(h + 1) * V_ROWS, cols] = jnp.ones((V_ROWS - V_DIM, n), ref.dtype)


def _proj_kernel(x_ref, mod_ref, tab_ref, w_ref, gcq_ref, gckv_ref, wuq_ref, wuk_ref, wuv_ref,
                 qa_ref, ka_ref, va_ref, qb_ref, kb_ref, vb_ref, mq_ref, mk_ref, mv_ref):
    sh1 = mod_ref[0, 0:1, :]
    sc1 = mod_ref[0, 1:2, :]
    tm = x_ref.shape[1]
    parts = PROJ_PARTS if tm % (PROJ_PARTS * LANES) == 0 else 1

    def rms(t, g):
        return (t * lax.rsqrt(jnp.mean(t * t, axis=-1, keepdims=True) + 1e-6) * g).astype(MXU_DTYPE)

    for r in range(parts):
        rows = slice(r * (tm // parts), (r + 1) * (tm // parts))
        h = (x_ref[0, rows, :] * (1.0 + sc1) + sh1).astype(MXU_DTYPE)

        def mm(a, b, h=h):
            return jnp.dot(h, w_ref[:, a:b], preferred_element_type=F32)

        rope = functools.partial(_rope, tab_ref=tab_ref, rows=rows)

        qa = mm(C_Q1, C_K1)
        ka = mm(C_K1, C_VA)
        sa = DIFF_QK ** -0.5 * LOG2E
        for j in range(2):
            blk = slice(j * LANES, (j + 1) * LANES)
            qa_ref[0, rows, blk] = (rope(qa[:, blk], typ=0, half=DIFF_QK // 8) * sa).astype(qa_ref.dtype)
            ka_ref[0, rows, blk] = rope(ka[:, blk], typ=0, half=DIFF_QK // 8).astype(ka_ref.dtype)
        _store_values_t(va_ref, mm(C_VA, C_QB).T, DIFF_HEADS, rows)

        qb = mm(C_QB, C_KB)
        kb = mm(C_KB, C_VB)
        vb = mm(C_VB, C_CQ)
        sb = DIL_DIM ** -0.5 * LOG2E
        for j in range(DIL_HEADS // 2):
            blk = slice(j * LANES, (j + 1) * LANES)
            qb_ref[j, 0, rows, :] = (rope(qb[:, blk], typ=1, half=DIL_DIM // 8) * sb).astype(qb_ref.dtype)
            kb_ref[j, 0, rows, :] = rope(kb[:, blk], typ=1, half=DIL_DIM // 8).astype(kb_ref.dtype)
            vb_ref[j, 0, rows, :] = vb[:, blk].astype(vb_ref.dtype)

        cq = rms(mm(C_CQ, C_CKV), gcq_ref[...])
        ckv = rms(mm(C_CKV, C_KR), gckv_ref[0:1, :])
        kr = rope(mm(C_KR, C_END), typ=2, half=MLA_ROPE // 2)
        q = jnp.dot(cq, wuq_ref[...], preferred_element_type=F32)
        kn = jnp.dot(ckv, wuk_ref[...], preferred_element_type=F32)
        sc = (MLA_NOPE + MLA_ROPE) ** -0.5 * LOG2E
        for j in range(MLA_HEADS):
            blk = slice(j * LANES, (j + 1) * LANES)
            mq_ref[0, rows, blk] = (rope(q[:, blk], typ=2, half=MLA_ROPE // 2) * sc).astype(mq_ref.dtype)
            mk_ref[0, rows, blk] = (kn[:, blk] + kr).astype(mk_ref.dtype)
        _store_values_t(mv_ref, jnp.dot(ckv, wuv_ref[...], preferred_element_type=F32).T, MLA_HEADS, rows)


def _proj(x, mod, tabs, w_in, g_cq, g_ckv, w_uq, w_uk, w_uv, tm):
    B, S, D = x.shape
    widths = (256, 256, DIFF_HEADS * V_ROWS, 384, 384, 384, 768, 768, MLA_HEADS * V_ROWS)
    transposed = (2, 8)
    row = lambda b, i: (b, i, 0)
    col = lambda b, i: (b, 0, i)
    const2 = lambda b, i: (0, 0)
    out_specs = [pl.BlockSpec((1, w, tm), col) if n in transposed else pl.BlockSpec((1, tm, w), row)
                 for n, w in enumerate(widths)]
    out_shape = [jax.ShapeDtypeStruct((B, w, S) if n in transposed else (B, S, w), MXU_DTYPE)
                 for n, w in enumerate(widths)]
    for n in (3, 4, 5):
        out_specs[n] = pl.BlockSpec((DIL_HEADS // 2, 1, tm, LANES), lambda b, i: (0, b, i, 0))
        out_shape[n] = jax.ShapeDtypeStruct((DIL_HEADS // 2, B, S, LANES), MXU_DTYPE)
    return pl.pallas_call(
        _proj_kernel,
        grid=(B, S // tm),
        in_specs=[pl.BlockSpec((1, tm, D), row),
                  pl.BlockSpec((1, 6, D), lambda b, i: (b, 0, 0)),
                  pl.BlockSpec((9, tm, LANES), lambda b, i: (0, i, 0)),
                  pl.BlockSpec(w_in.shape, const2),
                  pl.BlockSpec((1, MLA_Q_RANK), const2),
                  pl.BlockSpec((SMALL_ROWS, MLA_KV_RANK), const2),
                  pl.BlockSpec(w_uq.shape, const2),
                  pl.BlockSpec(w_uk.shape, const2),
                  pl.BlockSpec(w_uv.shape, const2)],
        out_specs=out_specs,
        out_shape=out_shape,
        compiler_params=_cparams(("parallel", "parallel")),
        name="proj_in",
    )(x, mod, tabs, w_in, g_cq.reshape(1, -1), g_ckv, w_uq, w_uk, w_uv)


REF_ROWS = 128
EXP_HEADROOM = 64.0


def _scores(chains):
    return [lax.dot_general(k, q, (((1,), (1,)), ((), ())), preferred_element_type=F32)
            for q, k, _ in chains]


def _softmax_steps_exact(chains, m_scr, acc_scr, base):
    for u, (st, (_, _, vt)) in enumerate(zip(_scores(chains), chains), start=base):
        m_prev = m_scr[u]
        m_new = jnp.maximum(m_prev, jnp.max(st, axis=0, keepdims=True))
        alpha = jnp.exp2(m_prev - m_new)
        p = jnp.exp2(st - m_new)
        acc_scr[u] = alpha * acc_scr[u] + jnp.dot(vt, p.astype(MXU_DTYPE), preferred_element_type=F32)
        m_scr[u] = m_new


def _softmax_steps(chains, m_scr, acc_scr, base=0):
    new_state = []
    worst = None
    for u, (st, (_, _, vt)) in enumerate(zip(_scores(chains), chains), start=base):
        m_prev = m_scr[u]
        m_ref = jnp.maximum(m_prev, jnp.max(st[:REF_ROWS], axis=0, keepdims=True))
        p = jnp.exp2(st - m_ref)
        m_new = jnp.maximum(m_prev, jnp.max(st, axis=0, keepdims=True))
        acc = jnp.exp2(m_ref - m_new) * (
            jnp.exp2(m_prev - m_ref) * acc_scr[u] + jnp.dot(vt, p.astype(MXU_DTYPE), preferred_element_type=F32))
        new_state.append((u, m_new, acc))
        gap = m_new - m_ref
        worst = gap if worst is None else jnp.maximum(worst, gap)
    safe = jnp.max(worst) <= EXP_HEADROOM

    @pl.when(safe)
    def _():
        for u, m_new, acc in new_state:
            m_scr[u] = m_new
            acc_scr[u] = acc

    @pl.when(jnp.logical_not(safe))
    def _():
        _softmax_steps_exact(chains, m_scr, acc_scr, base)


def _init_softmax(m_scr, acc_scr):
    m_scr[...] = jnp.full(m_scr.shape, NEG, F32)
    acc_scr[...] = jnp.zeros(acc_scr.shape, F32)


def _softmax_out(acc_scr, u):
    return acc_scr[u, :V_DIM] / acc_scr[u, V_DIM:V_DIM + 1]


def _diff_attn_kernel(lam_init, q1_ref, q2_ref, k1_ref, k2_ref, vt_ref, lamp_ref, g_ref, o_ref,
                      q_scr, m_scr, acc_scr):
    j = pl.program_id(1)
    ik = pl.program_id(3)

    t = m_scr.shape[2]
    parts = q1_ref.shape[1] // t

    @pl.when(ik == 0)
    def _():
        _init_softmax(m_scr, acc_scr)
        lane = _lane_iota((t, LANES))
        for r in range(parts):
            rows = slice(r * t, (r + 1) * t)
            for hh in range(2):
                h = 2 * j + hh
                msk = (lane >= h * DIFF_QK) & (lane < (h + 1) * DIFF_QK)
                q_scr[4 * r + 2 * hh] = jnp.where(msk, q1_ref[0, rows, :].astype(F32), 0.0).astype(q_scr.dtype)
                q_scr[4 * r + 2 * hh + 1] = jnp.where(msk, q2_ref[0, rows, :].astype(F32), 0.0).astype(q_scr.dtype)

    for r in range(parts):
        chains = []
        for hh in range(2):
            vt = vt_ref[0, hh * V_ROWS:(hh + 1) * V_ROWS, :]
            chains.append((q_scr[4 * r + 2 * hh], k1_ref[0], vt))
            chains.append((q_scr[4 * r + 2 * hh + 1], k2_ref[0], vt))
        _softmax_steps(chains, m_scr, acc_scr, 4 * r)

    @pl.when(ik == pl.num_programs(3) - 1)
    def _():
        lp = lamp_ref[0:4, :]
        lam = (jnp.exp(jnp.sum(lp[0:1] * lp[1:2], axis=-1, keepdims=True))
               - jnp.exp(jnp.sum(lp[2:3] * lp[3:4], axis=-1, keepdims=True)) + lam_init)
        for r in range(parts):
            for hh in range(2):
                u = 4 * r + 2 * hh
                o = _softmax_out(acc_scr, u) - lam * _softmax_out(acc_scr, u + 1)
                ms = jnp.mean(o * o, axis=0, keepdims=True)
                o_ref[0, hh * DIFF_V:(hh + 1) * DIFF_V, r * t:(r + 1) * t] = (
                    o * lax.rsqrt(ms + 1e-6) * g_ref[...] * (1.0 - lam_init)).astype(o_ref.dtype)


def _diff_attn(qa, ka, vat, lam_pack, g_sub, lam_init, tq, tk):
    B, S, _ = qa.shape
    parts = tq // ATTN_TQ if tq % ATTN_TQ == 0 else 1
    kern = functools.partial(_diff_attn_kernel, lam_init)
    return pl.pallas_call(
        kern,
        grid=(B, DIFF_HEADS // 2, S // tq, S // tk),
        in_specs=[pl.BlockSpec((1, tq, LANES), lambda b, j, iq, ik: (b, iq, 0)),
                  pl.BlockSpec((1, tq, LANES), lambda b, j, iq, ik: (b, iq, 1)),
                  pl.BlockSpec((1, tk, LANES), lambda b, j, iq, ik: (b, ik, 0)),
                  pl.BlockSpec((1, tk, LANES), lambda b, j, iq, ik: (b, ik, 1)),
                  pl.BlockSpec((1, 2 * V_ROWS, tk), lambda b, j, iq, ik: (b, j, ik)),
                  pl.BlockSpec((SMALL_ROWS, LANES), lambda b, j, iq, ik: (0, 0)),
                  pl.BlockSpec((DIFF_V, 1), lambda b, j, iq, ik: (0, 0))],
        out_specs=pl.BlockSpec((1, 2 * DIFF_V, tq), lambda b, j, iq, ik: (b, j, iq)),
        out_shape=jax.ShapeDtypeStruct((B, DIFF_HEADS * DIFF_V, S), MXU_DTYPE),
        scratch_shapes=[pltpu.VMEM((4 * parts, tq // parts, LANES), MXU_DTYPE),
                        pltpu.VMEM((4 * parts, 1, tq // parts), F32),
                        pltpu.VMEM((4 * parts, V_ROWS, tq // parts), F32)],
        compiler_params=_cparams(("parallel", "parallel", "parallel", "arbitrary")),
        name="diff_attn",
    )(qa, qa, ka, ka, vat, lam_pack, g_sub)


MLA_GROUP = 3


def _mla_attn_kernel(q_ref, k_ref, vt_ref, o_ref, m_scr, acc_scr):
    ik = pl.program_id(3)

    @pl.when(ik == 0)
    def _():
        _init_softmax(m_scr, acc_scr)

    t = m_scr.shape[2]
    parts = q_ref.shape[1] // t
    for r in range(parts):
        rows = slice(r * t, (r + 1) * t)
        chains = []
        for hh in range(MLA_GROUP):
            blk = slice(hh * LANES, (hh + 1) * LANES)
            chains.append((q_ref[0, rows, blk], k_ref[0, :, blk], vt_ref[0, hh * V_ROWS:(hh + 1) * V_ROWS, :]))
        _softmax_steps(chains, m_scr, acc_scr, r * MLA_GROUP)

    @pl.when(ik == pl.num_programs(3) - 1)
    def _():
        for r in range(parts):
            for hh in range(MLA_GROUP):
                o_ref[0, hh * MLA_V:(hh + 1) * MLA_V, r * t:(r + 1) * t] = _softmax_out(
                    acc_scr, r * MLA_GROUP + hh).astype(o_ref.dtype)


def _mla_attn(mq, mk, mvt, tq, tk):
    B, S, _ = mq.shape
    g = MLA_GROUP
    parts = tq // ATTN_TQ if tq % ATTN_TQ == 0 else 1
    return pl.pallas_call(
        _mla_attn_kernel,
        grid=(B, MLA_HEADS // g, S // tq, S // tk),
        in_specs=[pl.BlockSpec((1, tq, g * LANES), lambda b, j, iq, ik: (b, iq, j)),
                  pl.BlockSpec((1, tk, g * LANES), lambda b, j, iq, ik: (b, ik, j)),
                  pl.BlockSpec((1, g * V_ROWS, tk), lambda b, j, iq, ik: (b, j, ik))],
        out_specs=pl.BlockSpec((1, g * MLA_V, tq), lambda b, j, iq, ik: (b, j, iq)),
        out_shape=jax.ShapeDtypeStruct((B, MLA_HEADS * MLA_V, S), MXU_DTYPE),
        scratch_shapes=[pltpu.VMEM((g * parts, 1, tq // parts), F32),
                        pltpu.VMEM((g * parts, V_ROWS, tq // parts), F32)],
        compiler_params=_cparams(("parallel", "parallel", "parallel", "arbitrary")),
        name="mla_attn",
    )(mq, mk, mvt)


DIL_SUB = 128
DIL_WIDTH = DIL_HEADS * DIL_DIM


DIL_DILS = tuple(d for _, d in DIL_PAIRS)
DIL_HALO = DIL_RADIUS * DIL_DILS[-1]
DIL_TILE = 2048
DIL_SPLIT = 4


def _dil_kernel(seq, q_ref, kp_ref, kc_ref, kn_ref, vp_ref, vc_ref, vn_ref, o_ref,
                qbuf, kbuf, vbuf, pat_o, pat_l, qg, kg, vg, q4, k4, v4):
    i = pl.program_id(2)
    tt = q_ref.shape[2]
    H = DIL_HALO
    R = DIL_RADIUS
    qbuf[...] = q_ref[0, 0].astype(F32)
    kbuf[0:H] = kp_ref[0, 0].astype(F32)
    kbuf[H:H + tt] = kc_ref[0, 0].astype(F32)
    kbuf[H + tt:] = kn_ref[0, 0].astype(F32)
    vbuf[0:H] = vp_ref[0, 0].astype(F32)
    vbuf[H:H + tt] = vc_ref[0, 0].astype(F32)
    vbuf[H + tt:] = vn_ref[0, 0].astype(F32)
    vg[:, LANES:, :] = jnp.ones((vg.shape[0], vg.shape[1] - LANES, vg.shape[2]), vg.dtype)
    for r4 in range(DIL_SPLIT):
        q4[r4] = qbuf[pl.ds(r4, tt // DIL_SPLIT, stride=DIL_SPLIT), :]
        k4[r4] = kbuf[pl.ds(r4, (tt + 2 * H) // DIL_SPLIT, stride=DIL_SPLIT), :]
        v4[r4] = vbuf[pl.ds(r4, (tt + 2 * H) // DIL_SPLIT, stride=DIL_SPLIT), :]
    sub = DIL_SUB
    win = sub + 2 * R
    nc = tt // sub
    cidx = lax.broadcasted_iota(jnp.int32, (nc, win, sub), 0)
    row_k = lax.broadcasted_iota(jnp.int32, (nc, win, sub), 1)
    col_q = lax.broadcasted_iota(jnp.int32, (nc, win, sub), 2)
    rel = row_k - col_q
    band = (rel >= 0) & (rel <= 2 * R)
    lane = _lane_iota((sub, LANES))
    for p, d in enumerate(DIL_DILS):
        n_sub = tt // d // sub
        length = seq // d
        q_starts = [c // n_sub + d * sub * (c % n_sub) for c in range(nc)]
        for c in range(nc):
            r, u = c // n_sub, c % n_sub
            y0 = H // d + u * sub - R
            if d % DIL_SPLIT:
                qf = qbuf[pl.ds(q_starts[c], sub, stride=d), :]
                kw = kbuf[pl.ds(r + d * y0, win, stride=d), :]
                vw = vbuf[pl.ds(r + d * y0, win, stride=d), :]
            else:
                e = d // DIL_SPLIT
                r4, m = r % DIL_SPLIT, r // DIL_SPLIT
                qf = q4[r4, pl.ds(m + e * sub * u, sub, stride=e), :]
                kw = k4[r4, pl.ds(m + e * y0, win, stride=e), :]
                vw = v4[r4, pl.ds(m + e * y0, win, stride=e), :]
            qg[c, :sub] = jnp.where(lane < DIL_DIM, qf, 0.0).astype(qg.dtype)
            qg[c, sub:] = jnp.where(lane >= DIL_DIM, qf, 0.0).astype(qg.dtype)
            kg[c] = kw.astype(kg.dtype)
            vg[c, :LANES] = vw.T.astype(vg.dtype)
        u_of_c = cidx % n_sub if n_sub > 1 else jnp.zeros_like(cidx)
        kpos = i * (tt // d) + u_of_c * sub - R + row_k
        valid = band & (kpos >= 0) & (kpos < length)
        s = lax.dot_general(kg[...], qg[...], (((2,), (2,)), ((0,), (0,))),
                            preferred_element_type=F32)
        s = jnp.where(jnp.concatenate([valid, valid], axis=2), s, NEG)
        m = jnp.max(s, axis=1, keepdims=True)
        e = jnp.exp2(s - m)
        ot = lax.dot_general(vg[...], e.astype(MXU_DTYPE), (((2,), (1,)), ((0,), (0,))),
                             preferred_element_type=F32)
        l = ot[:, LANES:LANES + 1, :]
        ot = ot[:, :LANES, :] / l
        lse = m + jnp.log2(l)
        for c in range(nc):
            o_t = jnp.concatenate([ot[c, :DIL_DIM, :sub], ot[c, DIL_DIM:, sub:]], axis=0)
            l_t = jnp.concatenate([jnp.broadcast_to(lse[c, :, :sub], (DIL_DIM, sub)),
                                   jnp.broadcast_to(lse[c, :, sub:], (DIL_DIM, sub))], axis=0)
            pat_o[p, pl.ds(q_starts[c], sub, stride=d), :] = o_t.T
            pat_l[p, pl.ds(q_starts[c], sub, stride=d), :] = l_t.T

    l0, l1, l2 = pat_l[0], pat_l[1], pat_l[2]
    mx = jnp.maximum(jnp.maximum(l0, l1), l2)
    e0, e1, e2 = jnp.exp2(l0 - mx), jnp.exp2(l1 - mx), jnp.exp2(l2 - mx)
    den = e0 + e1 + e2
    o_ref[0] = ((e0 / den) * pat_o[0] + (e1 / den) * pat_o[1] + (e2 / den) * pat_o[2]).astype(o_ref.dtype)


def _dil_attn(qb, kb, vb):
    P, B, S, _ = qb.shape
    tt = _tile(S, DIL_TILE)
    H = DIL_HALO
    nh = S // H
    cur = pl.BlockSpec((1, 1, tt, LANES), lambda b, hp, i: (hp, b, i, 0))
    prv = pl.BlockSpec((1, 1, H, LANES), lambda b, hp, i: (hp, b, jnp.maximum(i * (tt // H) - 1, 0), 0))
    nxt = pl.BlockSpec((1, 1, H, LANES), lambda b, hp, i: (hp, b, jnp.minimum((i + 1) * (tt // H), nh - 1), 0))
    return pl.pallas_call(
        functools.partial(_dil_kernel, S),
        grid=(B, P, S // tt),
        in_specs=[cur, prv, cur, nxt, prv, cur, nxt],
        out_specs=pl.BlockSpec((1, tt, LANES), lambda b, hp, i: (b, i, hp)),
        out_shape=jax.ShapeDtypeStruct((B, S, P * LANES), MXU_DTYPE),
        scratch_shapes=[pltpu.VMEM((tt, LANES), F32),
                        pltpu.VMEM((tt + 2 * H, LANES), F32),
                        pltpu.VMEM((tt + 2 * H, LANES), F32),
                        pltpu.VMEM((len(DIL_DILS), tt, LANES), F32),
                        pltpu.VMEM((len(DIL_DILS), tt, LANES), F32),
                        pltpu.VMEM((tt // DIL_SUB, 2 * DIL_SUB, LANES), MXU_DTYPE),
                        pltpu.VMEM((tt // DIL_SUB, DIL_SUB + 2 * DIL_RADIUS, LANES), MXU_DTYPE),
                        pltpu.VMEM((tt // DIL_SUB, LANES + 16, DIL_SUB + 2 * DIL_RADIUS), MXU_DTYPE),
                        pltpu.VMEM((DIL_SPLIT, tt // DIL_SPLIT, LANES), F32),
                        pltpu.VMEM((DIL_SPLIT, (tt + 2 * H) // DIL_SPLIT, LANES), F32),
                        pltpu.VMEM((DIL_SPLIT, (tt + 2 * H) // DIL_SPLIT, LANES), F32)],
        compiler_params=_cparams(("parallel", "parallel", "parallel")),
        name="dil_attn",
    )(qb, kb, kb, kb, vb, vb, vb)


def _partner(x, k, lane):
    up = pltpu.roll(x, LANES - k, 1)
    dn = pltpu.roll(x, k, 1)
    return jnp.where((lane & k) == 0, up, dn)


GROUP_LANE = N_EXPERTS


def _route(logits, bias, lane):
    valid = lane < N_EXPERTS
    lane_f = lane.astype(F32)
    score = 1.0 / (1.0 + jnp.exp(-logits))
    biased = jnp.where(valid, score + bias, NEG)
    p1 = _partner(biased, 1, lane)
    hi = jnp.maximum(biased, p1)
    lo = jnp.minimum(biased, p1)
    hi_p = _partner(hi, 2, lane)
    lo_p = _partner(lo, 2, lane)
    top1 = jnp.maximum(hi, hi_p)
    top2 = jnp.maximum(jnp.minimum(hi, hi_p), jnp.maximum(lo, lo_p))
    gscore = jnp.where(valid, top1 + top2, NEG)
    gidx = (lane // GROUP_SIZE).astype(F32)
    big = float(LANES)
    gmax = jnp.max(gscore, axis=-1, keepdims=True)
    gsel = jnp.min(jnp.where(gscore == gmax, gidx, big), axis=-1, keepdims=True)
    cand = jnp.where((gidx == gsel) & valid, biased, NEG)
    m1 = jnp.max(cand, axis=-1, keepdims=True)
    i1 = jnp.min(jnp.where(cand == m1, lane_f, big), axis=-1, keepdims=True)
    oh1 = lane_f == i1
    cand2 = jnp.where(oh1, NEG, cand)
    m2 = jnp.max(cand2, axis=-1, keepdims=True)
    i2 = jnp.min(jnp.where((cand2 == m2) & jnp.logical_not(oh1), lane_f, big), axis=-1, keepdims=True)
    oh2 = lane_f == i2
    s1 = jnp.sum(jnp.where(oh1, score, 0.0), axis=-1, keepdims=True)
    s2 = jnp.sum(jnp.where(oh2, score, 0.0), axis=-1, keepdims=True)
    den = s1 + s2
    gate = jnp.where(oh1, s1 / den, 0.0) + jnp.where(oh2, s2 / den, 0.0)
    return jnp.where(lane == GROUP_LANE, gsel, gate)


POST_PARTS = 2


def _post_kernel(x_ref, mod_ref, oa_ref, ob_ref, oc_ref, wo_ref, g_ref, b_ref, wrh_ref, wrl_ref, br_ref,
                 x1_ref, h2_ref, gate_ref):
    na = DIFF_HEADS * DIFF_V
    nb = na + DIL_WIDTH
    tn = (((0,), (0,)), ((), ()))
    g1 = mod_ref[0, 2:3, :]
    sh2 = mod_ref[0, 3:4, :]
    sc2 = mod_ref[0, 4:5, :]
    tm = x_ref.shape[1]
    parts = POST_PARTS if tm % (POST_PARTS * LANES) == 0 else 1
    for r in range(parts):
        rows = slice(r * (tm // parts), (r + 1) * (tm // parts))
        y = (lax.dot_general(oa_ref[0, :, rows], wo_ref[0:na], tn, preferred_element_type=F32)
             + jnp.dot(ob_ref[0, rows, :], wo_ref[na:nb], preferred_element_type=F32)
             + lax.dot_general(oc_ref[0, :, rows], wo_ref[nb:], tn, preferred_element_type=F32))
        x1 = _layer_norm(ALPHA * x_ref[0, rows, :] + g1 * y, g_ref[...], b_ref[...])
        x1_ref[0, rows, :] = x1
        h2 = x1 * (1.0 + sc2) + sh2
        h2_ref[0, rows, :] = h2.astype(h2_ref.dtype)
        h_hi = h2.astype(MXU_DTYPE)
        h_lo = (h2 - h_hi.astype(F32)).astype(MXU_DTYPE)
        logits = (jnp.dot(h_hi, wrh_ref[...], preferred_element_type=F32)
                  + jnp.dot(h_lo, wrh_ref[...], preferred_element_type=F32)
                  + jnp.dot(h_hi, wrl_ref[...], preferred_element_type=F32))
        gate_ref[0, rows, :] = _route(logits, br_ref[0:1, :], _lane_iota(logits.shape))


def _post(x, mod, oa, ob, oc, w_out, g, b, w_r_hi, w_r_lo, b_r, tm):
    B, S, D = x.shape
    row = lambda b_, i: (b_, i, 0)
    const2 = lambda b_, i: (0, 0)
    return pl.pallas_call(
        _post_kernel,
        grid=(B, S // tm),
        in_specs=[pl.BlockSpec((1, tm, D), row),
                  pl.BlockSpec((1, 6, D), lambda b_, i: (b_, 0, 0)),
                  pl.BlockSpec((1, oa.shape[1], tm), lambda b_, i: (b_, 0, i)),
                  pl.BlockSpec((1, tm, DIL_WIDTH), row),
                  pl.BlockSpec((1, oc.shape[1], tm), lambda b_, i: (b_, 0, i)),
                  pl.BlockSpec(w_out.shape, const2),
                  pl.BlockSpec((1, D), const2),
                  pl.BlockSpec((1, D), const2),
                  pl.BlockSpec(w_r_hi.shape, const2),
                  pl.BlockSpec(w_r_lo.shape, const2),
                  pl.BlockSpec((SMALL_ROWS, LANES), const2)],
        out_specs=[pl.BlockSpec((1, tm, D), row),
                   pl.BlockSpec((1, tm, D), row),
                   pl.BlockSpec((1, tm, LANES), row)],
        out_shape=[jax.ShapeDtypeStruct((B, S, D), F32),
                   jax.ShapeDtypeStruct((B, S, D), MXU_DTYPE),
                   jax.ShapeDtypeStruct((B, S, LANES), F32)],
        compiler_params=_cparams(("parallel", "parallel")),
        name="post_mix",
    )(x, mod, oa, ob, oc, w_out, g.reshape(1, D), b.reshape(1, D), w_r_hi, w_r_lo, b_r)


MOE_TM = 1024
MOE_CHUNK = 256
MOE_TAIL = 64
N_GROUPS = N_EXPERTS // GROUP_SIZE


def _moe_kernel(x1_ref, mod_ref, h2_ref, gate_ref, w1_ref, w3_ref, w2_ref, g_ref, b_ref, o_ref,
                acc_scr, src_scr, sel_scr, rank_scr):
    g = pl.program_id(2)
    tm = h2_ref.shape[1]
    d = h2_ref.shape[2]
    C = MOE_CHUNK

    @pl.when(g == 0)
    def _():
        acc_scr[...] = jnp.zeros(acc_scr.shape, F32)
        gate = gate_ref[0]
        gate_hi = gate.astype(MXU_DTYPE)
        src_scr[:, 0:d] = h2_ref[0]
        src_scr[:, d:d + LANES] = gate_hi
        src_scr[:, d + LANES:] = (gate - gate_hi.astype(F32)).astype(MXU_DTYPE)
        gsel_t = gate.T[GROUP_LANE:GROUP_LANE + 1, :]
        grp = lax.broadcasted_iota(jnp.int32, sel_scr.shape, 0).astype(F32)
        sel = jnp.where(gsel_t == grp, 1.0, 0.0)
        r = lax.broadcasted_iota(jnp.int32, (tm, tm), 0)
        c = lax.broadcasted_iota(jnp.int32, (tm, tm), 1)
        before = jnp.where(r < c, 1.0, 0.0).astype(MXU_DTYPE)
        sel_scr[...] = sel
        rank_scr[...] = jnp.dot(sel.astype(MXU_DTYPE), before, preferred_element_type=F32)

    sel = sel_scr[pl.ds(g, 1), :]
    rank = rank_scr[pl.ds(g, 1), :]
    n_g = jnp.sum(sel).astype(jnp.int32)

    def chunk(base, rows):
        lane = _lane_iota((rows, LANES))
        row = lax.broadcasted_iota(jnp.int32, (rows, tm), 0).astype(F32)
        onehot = jnp.where((rank - base.astype(F32) == row) & (sel > 0.5), 1.0, 0.0).astype(MXU_DTYPE)
        picked = jnp.dot(onehot, src_scr[...], preferred_element_type=F32)
        xs = picked[:, 0:d].astype(MXU_DTYPE)
        gs = picked[:, d:d + LANES] + picked[:, d + LANES:]
        ys = jnp.zeros((rows, d), F32)
        for e in range(GROUP_SIZE):
            ge = jnp.sum(jnp.where(lane == g * GROUP_SIZE + e, gs, 0.0), axis=-1, keepdims=True)
            a = jnp.dot(xs, w1_ref[e], preferred_element_type=F32)
            bb = jnp.dot(xs, w3_ref[e], preferred_element_type=F32)
            hid = (a * (1.0 / (1.0 + jnp.exp(-a)))) * bb * ge
            ys = ys + jnp.dot(hid.astype(MXU_DTYPE), w2_ref[e], preferred_element_type=F32)
        acc_scr[...] += lax.dot_general(onehot, ys.astype(MXU_DTYPE), (((0,), (0,)), ((), ())),
                                        preferred_element_type=F32)

    n_chunks = (n_g + C - 1) // C
    tail = n_g - (n_chunks - 1) * C
    merge = (n_chunks >= 2) & (tail <= MOE_TAIL)
    n_plain = jnp.where(merge, n_chunks - 2, n_chunks)

    def plain(ci, carry):
        chunk(ci * C, C)
        return carry

    lax.fori_loop(0, n_plain, plain, 0)

    @pl.when(merge)
    def _():
        chunk(n_plain * C, C + MOE_TAIL)

    @pl.when(g == pl.num_programs(2) - 1)
    def _():
        g2 = mod_ref[0, 5:6, :]
        o_ref[0] = _layer_norm(ALPHA * x1_ref[0] + g2 * acc_scr[...], g_ref[...], b_ref[...])


def _moe(x1, mod, h2, gate, w1, w3, w2, g, b, tm):
    B, S, D = x1.shape
    gs = GROUP_SIZE
    row = lambda b_, i, c: (b_, i, 0)
    const2 = lambda b_, i, c: (0, 0)
    return pl.pallas_call(
        _moe_kernel,
        grid=(B, S // tm, N_GROUPS),
        in_specs=[pl.BlockSpec((1, tm, D), row),
                  pl.BlockSpec((1, 6, D), lambda b_, i, c: (b_, 0, 0)),
                  pl.BlockSpec((1, tm, D), row),
                  pl.BlockSpec((1, tm, LANES), row),
                  pl.BlockSpec((gs, D, D_FF), lambda b_, i, c: (c, 0, 0)),
                  pl.BlockSpec((gs, D, D_FF), lambda b_, i, c: (c, 0, 0)),
                  pl.BlockSpec((gs, D_FF, D), lambda b_, i, c: (c, 0, 0)),
                  pl.BlockSpec((1, D), const2),
                  pl.BlockSpec((1, D), const2)],
        out_specs=pl.BlockSpec((1, tm, D), row),
        out_shape=jax.ShapeDtypeStruct((B, S, D), F32),
        scratch_shapes=[pltpu.VMEM((tm, D), F32),
                        pltpu.VMEM((tm, D + 2 * LANES), MXU_DTYPE),
                        pltpu.VMEM((8, tm), F32),
                        pltpu.VMEM((8, tm), F32)],
        compiler_params=_cparams(("parallel", "parallel", "arbitrary")),
        name="moe_ffn",
    )(x1, mod, h2, gate, w1, w3, w2, g.reshape(1, D), b.reshape(1, D))


def _rope_tables(seq):
    lane = np.arange(LANES)

    def tables(period, offset, rot, theta):
        half = rot // 2
        d = lane % period - offset
        first = (d >= 0) & (d < half)
        second = (d >= half) & (d < rot)
        fidx = np.where(first, d, np.where(second, d - half, 0))
        inv_freq = theta ** (-2.0 * jnp.arange(half, dtype=F32) / rot)
        ang = jnp.arange(seq, dtype=F32)[:, None] * inv_freq[None, :]
        cos_l = jnp.cos(ang)[:, fidx]
        sin_l = jnp.sin(ang)[:, fidx]
        cos = jnp.where((first | second)[None, :], cos_l, 1.0)
        s_up = jnp.where(first[None, :], -sin_l, 0.0)
        s_dn = jnp.where(second[None, :], sin_l, 0.0)
        return [cos, s_up, s_dn]

    tabs = (tables(DIFF_QK, 0, DIFF_QK // 4, ROPE_THETA)
            + tables(DIL_DIM, 0, DIL_DIM // 4, ROPE_THETA)
            + tables(LANES, KR_LANE, MLA_ROPE, MLA_THETA))
    return jnp.stack(tabs).astype(F32)


def _prep_layer(l, w_in, w_uq, w_ukv, w_out, w1, w3, w2):
    D = w_in.shape[1]
    wi = w_in[l]
    zeros = lambda n: jnp.zeros((D, n), wi.dtype)
    wi = jnp.concatenate([wi[:, :C_KR], zeros(KR_LANE), wi[:, C_KR:], zeros(LANES - KR_LANE - MLA_ROPE)], axis=1)
    uq = w_uq[l].reshape(MLA_Q_RANK, MLA_HEADS, MLA_NOPE + MLA_ROPE)
    uq = jnp.pad(uq, ((0, 0), (0, 0), (0, LANES - MLA_NOPE - MLA_ROPE))).reshape(MLA_Q_RANK, MLA_HEADS * LANES)
    ukv = w_ukv[l].reshape(MLA_KV_RANK, MLA_HEADS, MLA_NOPE + MLA_V)
    uk = jnp.pad(ukv[..., :MLA_NOPE], ((0, 0), (0, 0), (0, LANES - MLA_NOPE))).reshape(MLA_KV_RANK, MLA_HEADS * LANES)
    uv = ukv[..., MLA_NOPE:].reshape(MLA_KV_RANK, MLA_HEADS * MLA_V)
    c = lambda t: t.astype(MXU_DTYPE)
    return dict(w_in=c(wi), w_uq=c(uq), w_uk=c(uk), w_uv=c(uv), w_out=c(w_out[l]),
                w1=c(w1[l]), w3=c(w3[l]), w2=c(w2[l]))


def _tile(n, pref):
    return pref if n % pref == 0 else n


TOKEN_TM = 512


def _trunk(x, mods, tabs, layers, p):
    B, S, D = x.shape
    assert S % (DIL_DILS[-1] * DIL_SUB) == 0 and S % DIL_HALO == 0, S
    assert D == layers[0]['w_out'].shape[1], (D, layers[0]['w_out'].shape)
    tm = _tile(S, TOKEN_TM)
    tq = _tile(S, ATTN_TQ * ATTN_PARTS)
    tk = _tile(S, ATTN_TK)
    x = _ln_in(x, p['ln_in_g'], p['ln_in_b'], tm)
    for l, lw in enumerate(layers):
        mod = mods[l]
        lam_init = 0.8 - 0.6 * math.exp(-0.3 * l)
        qa, ka, va, qb, kb, vb, mq, mk, mv = _proj(
            x, mod, tabs, lw['w_in'], p['g_cq'][l], p['g_ckv'][l], lw['w_uq'], lw['w_uk'], lw['w_uv'], tm)
        oa = _diff_attn(qa, ka, va, p['lam_pack'][l], p['g_sub'][l], lam_init, tq, tk)
        oc = _mla_attn(mq, mk, mv, tq, tk)
        ob = _dil_attn(qb, kb, vb)
        x1, h2, gate = _post(x, mod, oa, ob, oc, lw['w_out'], p['ln1_g'][l], p['ln1_b'][l],
                             p['w_r_hi'], p['w_r_lo'], p['b_r'], tm)
        x = _moe(x1, mod, h2, gate, lw['w1'], lw['w3'], lw['w2'], p['ln2_g'][l], p['ln2_b'][l],
                 _tile(S, MOE_TM))
    return x


def kernel(x_prompt, x_sample, c_prompt, c_sample, ln_in_g, ln_in_b, w_router, b_router, w_ada, b_ada, w_in, lam_q1, lam_k1, lam_q2, lam_k2, g_subln, g_cq, g_ckv, w_uq, w_ukv, w_out, ln1_g, ln1_b, w1, w3, w2, ln2_g, ln2_b):
    depth = w_in.shape[0]
    D = x_prompt.shape[-1]
    bp, bs = c_prompt.shape[0], c_sample.shape[0]
    rows = -(-(bp + bs) // 8) * 8
    c_all = jnp.concatenate([c_prompt, c_sample, jnp.zeros((rows - bp - bs, D), F32)], axis=0)
    mods = _ada(c_all, w_ada, b_ada)
    mods_p = mods[:, :bp].reshape(depth, bp, 6, D)
    mods_s = mods[:, bp:bp + bs].reshape(depth, bs, 6, D)

    pad_l = lambda t: jnp.pad(t, ((0, 0), (0, LANES - t.shape[-1])))
    pad_r = lambda t: jnp.pad(t, ((0, 0),) * (t.ndim - 2) + ((0, SMALL_ROWS - t.shape[-2]), (0, 0)))
    p = dict(
        ln_in_g=ln_in_g, ln_in_b=ln_in_b, g_cq=g_cq, g_ckv=pad_r(g_ckv[:, None, :]),
        ln1_g=ln1_g, ln1_b=ln1_b, ln2_g=ln2_g, ln2_b=ln2_b,
        lam_pack=pad_r(jnp.stack([pad_l(lam_q1), pad_l(lam_k1), pad_l(lam_q2), pad_l(lam_k2)], axis=1)),
        g_sub=g_subln.reshape(depth, DIFF_V, 1),
        b_r=pad_r(pad_l(b_router.reshape(1, -1))),
    )
    w_r = pad_l(w_router)
    p['w_r_hi'] = w_r.astype(MXU_DTYPE)
    p['w_r_lo'] = (w_r - p['w_r_hi'].astype(F32)).astype(MXU_DTYPE)
    layers = [_prep_layer(l, w_in, w_uq, w_ukv, w_out, w1, w3, w2) for l in range(depth)]
    tabs = _rope_tables(max(x_prompt.shape[1], x_sample.shape[1]))
    y_prompt = _trunk(x_prompt, mods_p, tabs, layers, p)
    y_sample = _trunk(x_sample, mods_s, tabs, layers, p)
    return (y_prompt, y_sample)
```

```python
import functools
import math

import numpy as np
import jax
import jax.numpy as jnp
from jax import lax
from jax.experimental import pallas as pl
from jax.experimental.pallas import tpu as pltpu

F32 = jnp.float32
MXU_DTYPE = jnp.bfloat16

LANES = 128
DEPTH = 4
DIFF_HEADS = 4
DIFF_QK = 32
DIFF_V = 64
DIL_HEADS = 6
DIL_DIM = 64
DIL_PAIRS = ((128, 1), (512, 4), (2048, 16))
DIL_RADIUS = 64
MLA_HEADS = 6
MLA_NOPE = 64
MLA_ROPE = 32
MLA_V = 64
MLA_Q_RANK = 256
MLA_KV_RANK = 128
MLA_THETA = 10000.0
ROPE_THETA = 500000.0
N_EXPERTS = 16
GROUP_SIZE = 4
D_FF = 256
ALPHA = (2 * DEPTH) ** 0.25
NEG = -1e30
LOG2E = math.log2(math.e)
ATTN_TQ = 512
ATTN_PARTS = 4
SMALL_ROWS = 32
ATTN_TK = 2048

C_Q1, C_Q2, C_K1, C_K2, C_VA = 0, 128, 256, 384, 512
C_QB, C_KB, C_VB = 768, 1152, 1536
C_CQ, C_CKV, C_KR, C_END = 1920, 2176, 2304, 2432
KR_LANE = 64

VMEM_LIMIT = 48 * 1024 * 1024


def _cparams(sem):
    return pltpu.CompilerParams(dimension_semantics=sem, vmem_limit_bytes=VMEM_LIMIT)


def _lane_iota(shape):
    return lax.broadcasted_iota(jnp.int32, shape, len(shape) - 1)


def _layer_norm(z, g, b):
    mu = jnp.mean(z, axis=-1, keepdims=True)
    zc = z - mu
    var = jnp.mean(zc * zc, axis=-1, keepdims=True)
    return zc * lax.rsqrt(var + 1e-5) * g + b


def _rope(t, tab_ref, typ, half, rows):
    c = tab_ref[3 * typ, rows, :]
    s_up = tab_ref[3 * typ + 1, rows, :]
    s_dn = tab_ref[3 * typ + 2, rows, :]
    return t * c + pltpu.roll(t, LANES - half, 1) * s_up + pltpu.roll(t, half, 1) * s_dn


def _ada_kernel(c_ref, w_ref, b_ref, o_ref):
    c = c_ref[...]
    cs = (c * (1.0 / (1.0 + jnp.exp(-c)))).astype(MXU_DTYPE)
    o_ref[0] = jnp.dot(cs, w_ref[0].astype(MXU_DTYPE), preferred_element_type=F32) + b_ref[0]


def _ada(c, w_ada, b_ada, tn=1536):
    R, D = c.shape
    L, _, N = w_ada.shape
    return pl.pallas_call(
        _ada_kernel,
        grid=(L, N // tn),
        in_specs=[pl.BlockSpec((R, D), lambda l, j: (0, 0)),
                  pl.BlockSpec((1, D, tn), lambda l, j: (l, 0, j)),
                  pl.BlockSpec((1, 1, tn), lambda l, j: (l, 0, j))],
        out_specs=pl.BlockSpec((1, R, tn), lambda l, j: (l, 0, j)),
        out_shape=jax.ShapeDtypeStruct((L, R, N), F32),
        compiler_params=_cparams(("parallel", "parallel")),
        name="ada",
    )(c, w_ada, b_ada.reshape(L, 1, N))


PROJ_PARTS = 2
V_DIM = 64
V_ROWS = V_DIM + 16


def _store_values_t(ref, vt, heads, cols):
    n = vt.shape[1]
    for h in range(heads):
        ref[0, h * V_ROWS:h * V_ROWS + V_DIM, cols] = vt[h * V_DIM:(h + 1) * V_DIM].astype(ref.dtype)
        ref[0, h * V_ROWS + V_DIM:(h + 1) * V_ROWS, cols] = jnp.ones((V_ROWS - V_DIM, n), ref.dtype)


def _proj_kernel(first, x_ref, mod_ref, tab_ref, w_ref, gcq_ref, gckv_ref, wuq_ref, wuk_ref, wuv_ref, *refs):
    if first:
        lng_ref, lnb_ref, xln_ref = refs[:3]
        refs = refs[3:]
    qa_ref, ka_ref, va_ref, qb_ref, kb_ref, vb_ref, mq_ref, mk_ref, mv_ref = refs
    sh1 = mod_ref[0, 0:1, :]
    sc1 = mod_ref[0, 1:2, :]
    tm = x_ref.shape[1]
    parts = PROJ_PARTS if tm % (PROJ_PARTS * LANES) == 0 else 1

    def rms(t, g):
        return (t * lax.rsqrt(jnp.mean(t * t, axis=-1, keepdims=True) + 1e-6) * g).astype(MXU_DTYPE)

    for r in range(parts):
        rows = slice(r * (tm // parts), (r + 1) * (tm // parts))
        x = x_ref[0, rows, :]
        if first:
            x = _layer_norm(x, lng_ref[...], lnb_ref[...])
            xln_ref[0, rows, :] = x
        h = (x * (1.0 + sc1) + sh1).astype(MXU_DTYPE)

        def mm(a, b, h=h):
            return jnp.dot(h, w_ref[:, a:b], preferred_element_type=F32)

        rope = functools.partial(_rope, tab_ref=tab_ref, rows=rows)

        qa = mm(C_Q1, C_K1)
        ka = mm(C_K1, C_VA)
        sa = DIFF_QK ** -0.5 * LOG2E
        for j in range(2):
            blk = slice(j * LANES, (j + 1) * LANES)
            qa_ref[0, rows, blk] = (rope(qa[:, blk], typ=0, half=DIFF_QK // 8) * sa).astype(qa_ref.dtype)
            ka_ref[0, rows, blk] = rope(ka[:, blk], typ=0, half=DIFF_QK // 8).astype(ka_ref.dtype)
        _store_values_t(va_ref, mm(C_VA, C_QB).T, DIFF_HEADS, rows)

        qb = mm(C_QB, C_KB)
        kb = mm(C_KB, C_VB)
        vb = mm(C_VB, C_CQ)
        sb = DIL_DIM ** -0.5 * LOG2E
        for j in range(DIL_HEADS // 2):
            blk = slice(j * LANES, (j + 1) * LANES)
            qb_ref[j, 0, rows, :] = (rope(qb[:, blk], typ=1, half=DIL_DIM // 8) * sb).astype(qb_ref.dtype)
            kb_ref[j, 0, rows, :] = rope(kb[:, blk], typ=1, half=DIL_DIM // 8).astype(kb_ref.dtype)
            vb_ref[j, 0, rows, :] = vb[:, blk].astype(vb_ref.dtype)

        cq = rms(mm(C_CQ, C_CKV), gcq_ref[...])
        ckv = rms(mm(C_CKV, C_KR), gckv_ref[0:1, :])
        kr = rope(mm(C_KR, C_END), typ=2, half=MLA_ROPE // 2)
        q = jnp.dot(cq, wuq_ref[...], preferred_element_type=F32)
        kn = jnp.dot(ckv, wuk_ref[...], preferred_element_type=F32)
        sc = (MLA_NOPE + MLA_ROPE) ** -0.5 * LOG2E
        for j in range(MLA_HEADS):
            blk = slice(j * LANES, (j + 1) * LANES)
            mq_ref[0, rows, blk] = (rope(q[:, blk], typ=2, half=MLA_ROPE // 2) * sc).astype(mq_ref.dtype)
            mk_ref[0, rows, blk] = (kn[:, blk] + kr).astype(mk_ref.dtype)
        _store_values_t(mv_ref, jnp.dot(ckv, wuv_ref[...], preferred_element_type=F32).T, MLA_HEADS, rows)


def _proj(x, mod, tabs, w_in, g_cq, g_ckv, w_uq, w_uk, w_uv, tm, ln_in=None):
    B, S, D = x.shape
    widths = (256, 256, DIFF_HEADS * V_ROWS, 384, 384, 384, 768, 768, MLA_HEADS * V_ROWS)
    transposed = (2, 8)
    row = lambda b, i: (b, i, 0)
    col = lambda b, i: (b, 0, i)
    const2 = lambda b, i: (0, 0)
    out_specs = [pl.BlockSpec((1, w, tm), col) if n in transposed else pl.BlockSpec((1, tm, w), row)
                 for n, w in enumerate(widths)]
    out_shape = [jax.ShapeDtypeStruct((B, w, S) if n in transposed else (B, S, w), MXU_DTYPE)
                 for n, w in enumerate(widths)]
    for n in (3, 4, 5):
        out_specs[n] = pl.BlockSpec((DIL_HEADS // 2, 1, tm, LANES), lambda b, i: (0, b, i, 0))
        out_shape[n] = jax.ShapeDtypeStruct((DIL_HEADS // 2, B, S, LANES), MXU_DTYPE)
    in_specs = [pl.BlockSpec((1, tm, D), row),
                pl.BlockSpec((1, 6, D), lambda b, i: (b, 0, 0)),
                pl.BlockSpec((9, tm, LANES), lambda b, i: (0, i, 0)),
                pl.BlockSpec(w_in.shape, const2),
                pl.BlockSpec((1, MLA_Q_RANK), const2),
                pl.BlockSpec((SMALL_ROWS, MLA_KV_RANK), const2),
                pl.BlockSpec(w_uq.shape, const2),
                pl.BlockSpec(w_uk.shape, const2),
                pl.BlockSpec(w_uv.shape, const2)]
    args = [x, mod, tabs, w_in, g_cq.reshape(1, -1), g_ckv, w_uq, w_uk, w_uv]
    first = ln_in is not None
    if first:
        in_specs += [pl.BlockSpec((1, D), const2), pl.BlockSpec((1, D), const2)]
        args += [ln_in[0].reshape(1, D), ln_in[1].reshape(1, D)]
        out_specs = [pl.BlockSpec((1, tm, D), row)] + out_specs
        out_shape = [jax.ShapeDtypeStruct((B, S, D), F32)] + out_shape
    return pl.pallas_call(
        functools.partial(_proj_kernel, first),
        grid=(B, S // tm),
        in_specs=in_specs,
        out_specs=out_specs,
        out_shape=out_shape,
        compiler_params=_cparams(("parallel", "parallel")),
        name="proj_in",
    )(*args)


REF_ROWS = 128
EXP_HEADROOM = 64.0


def _scores(chains):
    return [lax.dot_general(k, q, (((1,), (1,)), ((), ())), preferred_element_type=F32)
            for q, k, _ in chains]


def _softmax_steps_exact(chains, m_scr, acc_scr, base):
    for u, (st, (_, _, vt)) in enumerate(zip(_scores(chains), chains), start=base):
        m_prev = m_scr[u]
        m_new = jnp.maximum(m_prev, jnp.max(st, axis=0, keepdims=True))
        alpha = jnp.exp2(m_prev - m_new)
        p = jnp.exp2(st - m_new)
        acc_scr[u] = alpha * acc_scr[u] + jnp.dot(vt, p.astype(MXU_DTYPE), preferred_element_type=F32)
        m_scr[u] = m_new


def _softmax_steps(chains, m_scr, acc_scr, base=0):
    new_state = []
    worst = None
    for u, (st, (_, _, vt)) in enumerate(zip(_scores(chains), chains), start=base):
        m_prev = m_scr[u]
        m_ref = jnp.maximum(m_prev, jnp.max(st[:REF_ROWS], axis=0, keepdims=True))
        p = jnp.exp2(st - m_ref)
        m_new = jnp.maximum(m_prev, jnp.max(st, axis=0, keepdims=True))
        acc = jnp.exp2(m_ref - m_new) * (
            jnp.exp2(m_prev - m_ref) * acc_scr[u] + jnp.dot(vt, p.astype(MXU_DTYPE), preferred_element_type=F32))
        new_state.append((u, m_new, acc))
        gap = m_new - m_ref
        worst = gap if worst is None else jnp.maximum(worst, gap)
    safe = jnp.max(worst) <= EXP_HEADROOM

    @pl.when(safe)
    def _():
        for u, m_new, acc in new_state:
            m_scr[u] = m_new
            acc_scr[u] = acc

    @pl.when(jnp.logical_not(safe))
    def _():
        _softmax_steps_exact(chains, m_scr, acc_scr, base)


def _init_softmax(m_scr, acc_scr):
    m_scr[...] = jnp.full(m_scr.shape, NEG, F32)
    acc_scr[...] = jnp.zeros(acc_scr.shape, F32)


def _softmax_out(acc_scr, u):
    return acc_scr[u, :V_DIM] / acc_scr[u, V_DIM:V_DIM + 1]


def _diff_attn_kernel(lam_init, q1_ref, q2_ref, k1_ref, k2_ref, vt_ref, lamp_ref, g_ref, o_ref,
                      q_scr, m_scr, acc_scr):
    j = pl.program_id(1)
    ik = pl.program_id(3)

    t = m_scr.shape[2]
    parts = q1_ref.shape[1] // t

    @pl.when(ik == 0)
    def _():
        _init_softmax(m_scr, acc_scr)
        lane = _lane_iota((t, LANES))
        for r in range(parts):
            rows = slice(r * t, (r + 1) * t)
            for hh in range(2):
                h = 2 * j + hh
                msk = (lane >= h * DIFF_QK) & (lane < (h + 1) * DIFF_QK)
                q_scr[4 * r + 2 * hh] = jnp.where(msk, q1_ref[0, rows, :].astype(F32), 0.0).astype(q_scr.dtype)
                q_scr[4 * r + 2 * hh + 1] = jnp.where(msk, q2_ref[0, rows, :].astype(F32), 0.0).astype(q_scr.dtype)

    for r in range(parts):
        chains = []
        for hh in range(2):
            vt = vt_ref[0, hh * V_ROWS:(hh + 1) * V_ROWS, :]
            chains.append((q_scr[4 * r + 2 * hh], k1_ref[0], vt))
            chains.append((q_scr[4 * r + 2 * hh + 1], k2_ref[0], vt))
        _softmax_steps(chains, m_scr, acc_scr, 4 * r)

    @pl.when(ik == pl.num_programs(3) - 1)
    def _():
        lp = lamp_ref[0:4, :]
        lam = (jnp.exp(jnp.sum(lp[0:1] * lp[1:2], axis=-1, keepdims=True))
               - jnp.exp(jnp.sum(lp[2:3] * lp[3:4], axis=-1, keepdims=True)) + lam_init)
        for r in range(parts):
            for hh in range(2):
                u = 4 * r + 2 * hh
                o = _softmax_out(acc_scr, u) - lam * _softmax_out(acc_scr, u + 1)
                ms = jnp.mean(o * o, axis=0, keepdims=True)
                o_ref[0, hh * DIFF_V:(hh + 1) * DIFF_V, r * t:(r + 1) * t] = (
                    o * lax.rsqrt(ms + 1e-6) * g_ref[...] * (1.0 - lam_init)).astype(o_ref.dtype)


def _diff_attn(qa, ka, vat, lam_pack, g_sub, lam_init, tq, tk):
    B, S, _ = qa.shape
    parts = tq // ATTN_TQ if tq % ATTN_TQ == 0 else 1
    kern = functools.partial(_diff_attn_kernel, lam_init)
    return pl.pallas_call(
        kern,
        grid=(B, DIFF_HEADS // 2, S // tq, S // tk),
        in_specs=[pl.BlockSpec((1, tq, LANES), lambda b, j, iq, ik: (b, iq, 0)),
                  pl.BlockSpec((1, tq, LANES), lambda b, j, iq, ik: (b, iq, 1)),
                  pl.BlockSpec((1, tk, LANES), lambda b, j, iq, ik: (b, ik, 0)),
                  pl.BlockSpec((1, tk, LANES), lambda b, j, iq, ik: (b, ik, 1)),
                  pl.BlockSpec((1, 2 * V_ROWS, tk), lambda b, j, iq, ik: (b, j, ik)),
                  pl.BlockSpec((SMALL_ROWS, LANES), lambda b, j, iq, ik: (0, 0)),
                  pl.BlockSpec((DIFF_V, 1), lambda b, j, iq, ik: (0, 0))],
        out_specs=pl.BlockSpec((1, 2 * DIFF_V, tq), lambda b, j, iq, ik: (b, j, iq)),
        out_shape=jax.ShapeDtypeStruct((B, DIFF_HEADS * DIFF_V, S), MXU_DTYPE),
        scratch_shapes=[pltpu.VMEM((4 * parts, tq // parts, LANES), MXU_DTYPE),
                        pltpu.VMEM((4 * parts, 1, tq // parts), F32),
                        pltpu.VMEM((4 * parts, V_ROWS, tq // parts), F32)],
        compiler_params=_cparams(("parallel", "parallel", "parallel", "arbitrary")),
        name="diff_attn",
    )(qa, qa, ka, ka, vat, lam_pack, g_sub)


MLA_GROUP = 3


def _mla_attn_kernel(q_ref, k_ref, vt_ref, o_ref, m_scr, acc_scr):
    ik = pl.program_id(3)

    @pl.when(ik == 0)
    def _():
        _init_softmax(m_scr, acc_scr)

    t = m_scr.shape[2]
    parts = q_ref.shape[1] // t
    for r in range(parts):
        rows = slice(r * t, (r + 1) * t)
        chains = []
        for hh in range(MLA_GROUP):
            blk = slice(hh * LANES, (hh + 1) * LANES)
            chains.append((q_ref[0, rows, blk], k_ref[0, :, blk], vt_ref[0, hh * V_ROWS:(hh + 1) * V_ROWS, :]))
        _softmax_steps(chains, m_scr, acc_scr, r * MLA_GROUP)

    @pl.when(ik == pl.num_programs(3) - 1)
    def _():
        for r in range(parts):
            for hh in range(MLA_GROUP):
                o_ref[0, hh * MLA_V:(hh + 1) * MLA_V, r * t:(r + 1) * t] = _softmax_out(
                    acc_scr, r * MLA_GROUP + hh).astype(o_ref.dtype)


def _mla_attn(mq, mk, mvt, tq, tk):
    B, S, _ = mq.shape
    g = MLA_GROUP
    parts = tq // ATTN_TQ if tq % ATTN_TQ == 0 else 1
    return pl.pallas_call(
        _mla_attn_kernel,
        grid=(B, MLA_HEADS // g, S // tq, S // tk),
        in_specs=[pl.BlockSpec((1, tq, g * LANES), lambda b, j, iq, ik: (b, iq, j)),
                  pl.BlockSpec((1, tk, g * LANES), lambda b, j, iq, ik: (b, ik, j)),
                  pl.BlockSpec((1, g * V_ROWS, tk), lambda b, j, iq, ik: (b, j, ik))],
        out_specs=pl.BlockSpec((1, g * MLA_V, tq), lambda b, j, iq, ik: (b, j, iq)),
        out_shape=jax.ShapeDtypeStruct((B, MLA_HEADS * MLA_V, S), MXU_DTYPE),
        scratch_shapes=[pltpu.VMEM((g * parts, 1, tq // parts), F32),
                        pltpu.VMEM((g * parts, V_ROWS, tq // parts), F32)],
        compiler_params=_cparams(("parallel", "parallel", "parallel", "arbitrary")),
        name="mla_attn",
    )(mq, mk, mvt)


DIL_SUB = 128
DIL_WIDTH = DIL_HEADS * DIL_DIM


DIL_DILS = tuple(d for _, d in DIL_PAIRS)
DIL_HALO = DIL_RADIUS * DIL_DILS[-1]
DIL_TILE = 2048
DIL_SPLIT = 4


def _dil_kernel(seq, q_ref, kp_ref, kc_ref, kn_ref, vp_ref, vc_ref, vn_ref, o_ref,
                qbuf, kbuf, vbuf, pat_o, pat_l, qg, kg, vg, q4, k4, v4):
    i = pl.program_id(2)
    tt = q_ref.shape[2]
    H = DIL_HALO
    R = DIL_RADIUS
    qbuf[...] = q_ref[0, 0].astype(F32)
    kbuf[0:H] = kp_ref[0, 0].astype(F32)
    kbuf[H:H + tt] = kc_ref[0, 0].astype(F32)
    kbuf[H + tt:] = kn_ref[0, 0].astype(F32)
    vbuf[0:H] = vp_ref[0, 0].astype(F32)
    vbuf[H:H + tt] = vc_ref[0, 0].astype(F32)
    vbuf[H + tt:] = vn_ref[0, 0].astype(F32)
    vg[:, LANES:, :] = jnp.ones((vg.shape[0], vg.shape[1] - LANES, vg.shape[2]), vg.dtype)
    for r4 in range(DIL_SPLIT):
        q4[r4] = qbuf[pl.ds(r4, tt // DIL_SPLIT, stride=DIL_SPLIT), :]
        k4[r4] = kbuf[pl.ds(r4, (tt + 2 * H) // DIL_SPLIT, stride=DIL_SPLIT), :]
        v4[r4] = vbuf[pl.ds(r4, (tt + 2 * H) // DIL_SPLIT, stride=DIL_SPLIT), :]
    sub = DIL_SUB
    win = sub + 2 * R
    nc = tt // sub
    cidx = lax.broadcasted_iota(jnp.int32, (nc, win, sub), 0)
    row_k = lax.broadcasted_iota(jnp.int32, (nc, win, sub), 1)
    col_q = lax.broadcasted_iota(jnp.int32, (nc, win, sub), 2)
    rel = row_k - col_q
    band = (rel >= 0) & (rel <= 2 * R)
    lane = _lane_iota((sub, LANES))
    for p, d in enumerate(DIL_DILS):
        n_sub = tt // d // sub
        length = seq // d
        q_starts = [c // n_sub + d * sub * (c % n_sub) for c in range(nc)]
        for c in range(nc):
            r, u = c // n_sub, c % n_sub
            y0 = H // d + u * sub - R
            if d % DIL_SPLIT:
                qf = qbuf[pl.ds(q_starts[c], sub, stride=d), :]
                kw = kbuf[pl.ds(r + d * y0, win, stride=d), :]
                vw = vbuf[pl.ds(r + d * y0, win, stride=d), :]
            else:
                e = d // DIL_SPLIT
                r4, m = r % DIL_SPLIT, r // DIL_SPLIT
                qf = q4[r4, pl.ds(m + e * sub * u, sub, stride=e), :]
                kw = k4[r4, pl.ds(m + e * y0, win, stride=e), :]
                vw = v4[r4, pl.ds(m + e * y0, win, stride=e), :]
            qg[c, :sub] = jnp.where(lane < DIL_DIM, qf, 0.0).astype(qg.dtype)
            qg[c, sub:] = jnp.where(lane >= DIL_DIM, qf, 0.0).astype(qg.dtype)
            kg[c] = kw.astype(kg.dtype)
            vg[c, :LANES] = vw.T.astype(vg.dtype)
        u_of_c = cidx % n_sub if n_sub > 1 else jnp.zeros_like(cidx)
        kpos = i * (tt // d) + u_of_c * sub - R + row_k
        valid = band & (kpos >= 0) & (kpos < length)
        s = lax.dot_general(kg[...], qg[...], (((2,), (2,)), ((0,), (0,))),
                            preferred_element_type=F32)
        s = jnp.where(jnp.concatenate([valid, valid], axis=2), s, NEG)
        m = jnp.max(s, axis=1, keepdims=True)
        e = jnp.exp2(s - m)
        ot = lax.dot_general(vg[...], e.astype(MXU_DTYPE), (((2,), (1,)), ((0,), (0,))),
                             preferred_element_type=F32)
        l = ot[:, LANES:LANES + 1, :]
        ot = ot[:, :LANES, :] / l
        lse = m + jnp.log2(l)
        for c in range(nc):
            o_t = jnp.concatenate([ot[c, :DIL_DIM, :sub], ot[c, DIL_DIM:, sub:]], axis=0)
            l_t = jnp.concatenate([jnp.broadcast_to(lse[c, :, :sub], (DIL_DIM, sub)),
                                   jnp.broadcast_to(lse[c, :, sub:], (DIL_DIM, sub))], axis=0)
            pat_o[p, pl.ds(q_starts[c], sub, stride=d), :] = o_t.T
            pat_l[p, pl.ds(q_starts[c], sub, stride=d), :] = l_t.T

    l0, l1, l2 = pat_l[0], pat_l[1], pat_l[2]
    mx = jnp.maximum(jnp.maximum(l0, l1), l2)
    e0, e1, e2 = jnp.exp2(l0 - mx), jnp.exp2(l1 - mx), jnp.exp2(l2 - mx)
    den = e0 + e1 + e2
    o_ref[0] = ((e0 / den) * pat_o[0] + (e1 / den) * pat_o[1] + (e2 / den) * pat_o[2]).astype(o_ref.dtype)


def _dil_attn(qb, kb, vb):
    P, B, S, _ = qb.shape
    tt = _tile(S, DIL_TILE)
    H = DIL_HALO
    nh = S // H
    cur = pl.BlockSpec((1, 1, tt, LANES), lambda b, hp, i: (hp, b, i, 0))
    prv = pl.BlockSpec((1, 1, H, LANES), lambda b, hp, i: (hp, b, jnp.maximum(i * (tt // H) - 1, 0), 0))
    nxt = pl.BlockSpec((1, 1, H, LANES), lambda b, hp, i: (hp, b, jnp.minimum((i + 1) * (tt // H), nh - 1), 0))
    return pl.pallas_call(
        functools.partial(_dil_kernel, S),
        grid=(B, P, S // tt),
        in_specs=[cur, prv, cur, nxt, prv, cur, nxt],
        out_specs=pl.BlockSpec((1, tt, LANES), lambda b, hp, i: (b, i, hp)),
        out_shape=jax.ShapeDtypeStruct((B, S, P * LANES), MXU_DTYPE),
        scratch_shapes=[pltpu.VMEM((tt, LANES), F32),
                        pltpu.VMEM((tt + 2 * H, LANES), F32),
                        pltpu.VMEM((tt + 2 * H, LANES), F32),
                        pltpu.VMEM((len(DIL_DILS), tt, LANES), F32),
                        pltpu.VMEM((len(DIL_DILS), tt, LANES), F32),
                        pltpu.VMEM((tt // DIL_SUB, 2 * DIL_SUB, LANES), MXU_DTYPE),
                        pltpu.VMEM((tt // DIL_SUB, DIL_SUB + 2 * DIL_RADIUS, LANES), MXU_DTYPE),
                        pltpu.VMEM((tt // DIL_SUB, LANES + 16, DIL_SUB + 2 * DIL_RADIUS), MXU_DTYPE),
                        pltpu.VMEM((DIL_SPLIT, tt // DIL_SPLIT, LANES), F32),
                        pltpu.VMEM((DIL_SPLIT, (tt + 2 * H) // DIL_SPLIT, LANES), F32),
                        pltpu.VMEM((DIL_SPLIT, (tt + 2 * H) // DIL_SPLIT, LANES), F32)],
        compiler_params=_cparams(("parallel", "parallel", "parallel")),
        name="dil_attn",
    )(qb, kb, kb, kb, vb, vb, vb)


def _partner(x, k, lane):
    up = pltpu.roll(x, LANES - k, 1)
    dn = pltpu.roll(x, k, 1)
    return jnp.where((lane & k) == 0, up, dn)


GROUP_LANE = N_EXPERTS


def _route(logits, bias, lane):
    valid = lane < N_EXPERTS
    lane_f = lane.astype(F32)
    score = 1.0 / (1.0 + jnp.exp(-logits))
    biased = jnp.where(valid, score + bias, NEG)
    p1 = _partner(biased, 1, lane)
    hi = jnp.maximum(biased, p1)
    lo = jnp.minimum(biased, p1)
    hi_p = _partner(hi, 2, lane)
    lo_p = _partner(lo, 2, lane)
    top1 = jnp.maximum(hi, hi_p)
    top2 = jnp.maximum(jnp.minimum(hi, hi_p), jnp.maximum(lo, lo_p))
    gscore = jnp.where(valid, top1 + top2, NEG)
    gidx = (lane // GROUP_SIZE).astype(F32)
    big = float(LANES)
    gmax = jnp.max(gscore, axis=-1, keepdims=True)
    gsel = jnp.min(jnp.where(gscore == gmax, gidx, big), axis=-1, keepdims=True)
    cand = jnp.where((gidx == gsel) & valid, biased, NEG)
    m1 = jnp.max(cand, axis=-1, keepdims=True)
    i1 = jnp.min(jnp.where(cand == m1, lane_f, big), axis=-1, keepdims=True)
    oh1 = lane_f == i1
    cand2 = jnp.where(oh1, NEG, cand)
    m2 = jnp.max(cand2, axis=-1, keepdims=True)
    i2 = jnp.min(jnp.where((cand2 == m2) & jnp.logical_not(oh1), lane_f, big), axis=-1, keepdims=True)
    oh2 = lane_f == i2
    s1 = jnp.sum(jnp.where(oh1, score, 0.0), axis=-1, keepdims=True)
    s2 = jnp.sum(jnp.where(oh2, score, 0.0), axis=-1, keepdims=True)
    den = s1 + s2
    gate = jnp.where(oh1, s1 / den, 0.0) + jnp.where(oh2, s2 / den, 0.0)
    return jnp.where(lane == GROUP_LANE, gsel, gate)


POST_PARTS = 2


def _post_kernel(x_ref, mod_ref, oa_ref, ob_ref, oc_ref, wo_ref, g_ref, b_ref, wrh_ref, wrl_ref, br_ref,
                 x1_ref, h2_ref, gate_ref):
    na = DIFF_HEADS * DIFF_V
    nb = na + DIL_WIDTH
    tn = (((0,), (0,)), ((), ()))
    g1 = mod_ref[0, 2:3, :]
    sh2 = mod_ref[0, 3:4, :]
    sc2 = mod_ref[0, 4:5, :]
    tm = x_ref.shape[1]
    parts = POST_PARTS if tm % (POST_PARTS * LANES) == 0 else 1
    for r in range(parts):
        rows = slice(r * (tm // parts), (r + 1) * (tm // parts))
        y = (lax.dot_general(oa_ref[0, :, rows], wo_ref[0:na], tn, preferred_element_type=F32)
             + jnp.dot(ob_ref[0, rows, :], wo_ref[na:nb], preferred_element_type=F32)
             + lax.dot_general(oc_ref[0, :, rows], wo_ref[nb:], tn, preferred_element_type=F32))
        x1 = _layer_norm(ALPHA * x_ref[0, rows, :] + g1 * y, g_ref[...], b_ref[...])
        x1_ref[0, rows, :] = x1
        h2 = x1 * (1.0 + sc2) + sh2
        h2_ref[0, rows, :] = h2.astype(h2_ref.dtype)
        h_hi = h2.astype(MXU_DTYPE)
        h_lo = (h2 - h_hi.astype(F32)).astype(MXU_DTYPE)
        logits = (jnp.dot(h_hi, wrh_ref[...], preferred_element_type=F32)
                  + jnp.dot(h_lo, wrh_ref[...], preferred_element_type=F32)
                  + jnp.dot(h_hi, wrl_ref[...], preferred_element_type=F32))
        gate_ref[0, rows, :] = _route(logits, br_ref[0:1, :], _lane_iota(logits.shape))


def _post(x, mod, oa, ob, oc, w_out, g, b, w_r_hi, w_r_lo, b_r, tm):
    B, S, D = x.shape
    row = lambda b_, i: (b_, i, 0)
    const2 = lambda b_, i: (0, 0)
    return pl.pallas_call(
        _post_kernel,
        grid=(B, S // tm),
        in_specs=[pl.BlockSpec((1, tm, D), row),
                  pl.BlockSpec((1, 6, D), lambda b_, i: (b_, 0, 0)),
                  pl.BlockSpec((1, oa.shape[1], tm), lambda b_, i: (b_, 0, i)),
                  pl.BlockSpec((1, tm, DIL_WIDTH), row),
                  pl.BlockSpec((1, oc.shape[1], tm), lambda b_, i: (b_, 0, i)),
                  pl.BlockSpec(w_out.shape, const2),
                  pl.BlockSpec((1, D), const2),
                  pl.BlockSpec((1, D), const2),
                  pl.BlockSpec(w_r_hi.shape, const2),
                  pl.BlockSpec(w_r_lo.shape, const2),
                  pl.BlockSpec((SMALL_ROWS, LANES), const2)],
        out_specs=[pl.BlockSpec((1, tm, D), row),
                   pl.BlockSpec((1, tm, D), row),
                   pl.BlockSpec((1, tm, LANES), row)],
        out_shape=[jax.ShapeDtypeStruct((B, S, D), F32),
                   jax.ShapeDtypeStruct((B, S, D), MXU_DTYPE),
                   jax.ShapeDtypeStruct((B, S, LANES), F32)],
        compiler_params=_cparams(("parallel", "parallel")),
        name="post_mix",
    )(x, mod, oa, ob, oc, w_out, g.reshape(1, D), b.reshape(1, D), w_r_hi, w_r_lo, b_r)


MOE_TM = 1024
MOE_CHUNK = 256
MOE_TAIL = 64
N_GROUPS = N_EXPERTS // GROUP_SIZE


def _moe_kernel(x1_ref, mod_ref, h2_ref, gate_ref, w1_ref, w3_ref, w2_ref, g_ref, b_ref, o_ref,
                acc_scr, src_scr, sel_scr, rank_scr):
    g = pl.program_id(2)
    tm = h2_ref.shape[1]
    d = h2_ref.shape[2]
    C = MOE_CHUNK

    @pl.when(g == 0)
    def _():
        acc_scr[...] = jnp.zeros(acc_scr.shape, F32)
        gate = gate_ref[0]
        gate_hi = gate.astype(MXU_DTYPE)
        src_scr[:, 0:d] = h2_ref[0]
        src_scr[:, d:d + LANES] = gate_hi
        src_scr[:, d + LANES:] = (gate - gate_hi.astype(F32)).astype(MXU_DTYPE)
        gsel_t = gate.T[GROUP_LANE:GROUP_LANE + 1, :]
        grp = lax.broadcasted_iota(jnp.int32, sel_scr.shape, 0).astype(F32)
        sel = jnp.where(gsel_t == grp, 1.0, 0.0)
        r = lax.broadcasted_iota(jnp.int32, (tm, tm), 0)
        c = lax.broadcasted_iota(jnp.int32, (tm, tm), 1)
        before = jnp.where(r < c, 1.0, 0.0).astype(MXU_DTYPE)
        sel_scr[...] = sel
        rank_scr[...] = jnp.dot(sel.astype(MXU_DTYPE), before, preferred_element_type=F32)

    sel = sel_scr[pl.ds(g, 1), :]
    rank = rank_scr[pl.ds(g, 1), :]
    n_g = jnp.sum(sel).astype(jnp.int32)

    def chunk(base, rows):
        lane = _lane_iota((rows, LANES))
        row = lax.broadcasted_iota(jnp.int32, (rows, tm), 0).astype(F32)
        onehot = jnp.where((rank - base.astype(F32) == row) & (sel > 0.5), 1.0, 0.0).astype(MXU_DTYPE)
        picked = jnp.dot(onehot, src_scr[...], preferred_element_type=F32)
        xs = picked[:, 0:d].astype(MXU_DTYPE)
        gs = picked[:, d:d + LANES] + picked[:, d + LANES:]
        ys = jnp.zeros((rows, d), F32)
        for e in range(GROUP_SIZE):
            ge = jnp.sum(jnp.where(lane == g * GROUP_SIZE + e, gs, 0.0), axis=-1, keepdims=True)
            a = jnp.dot(xs, w1_ref[e], preferred_element_type=F32)
            bb = jnp.dot(xs, w3_ref[e], preferred_element_type=F32)
            hid = (a * (1.0 / (1.0 + jnp.exp(-a)))) * bb * ge
            ys = ys + jnp.dot(hid.astype(MXU_DTYPE), w2_ref[e], preferred_element_type=F32)
        acc_scr[...] += lax.dot_general(onehot, ys.astype(MXU_DTYPE), (((0,), (0,)), ((), ())),
                                        preferred_element_type=F32)

    n_chunks = (n_g + C - 1) // C
    tail = n_g - (n_chunks - 1) * C
    merge = (n_chunks >= 2) & (tail <= MOE_TAIL)
    n_plain = jnp.where(merge, n_chunks - 2, n_chunks)

    def plain(ci, carry):
        chunk(ci * C, C)
        return carry

    lax.fori_loop(0, n_plain, plain, 0)

    @pl.when(merge)
    def _():
        chunk(n_plain * C, C + MOE_TAIL)

    @pl.when(g == pl.num_programs(2) - 1)
    def _():
        g2 = mod_ref[0, 5:6, :]
        o_ref[0] = _layer_norm(ALPHA * x1_ref[0] + g2 * acc_scr[...], g_ref[...], b_ref[...])


def _moe(x1, mod, h2, gate, w1, w3, w2, g, b, tm):
    B, S, D = x1.shape
    gs = GROUP_SIZE
    row = lambda b_, i, c: (b_, i, 0)
    const2 = lambda b_, i, c: (0, 0)
    return pl.pallas_call(
        _moe_kernel,
        grid=(B, S // tm, N_GROUPS),
        in_specs=[pl.BlockSpec((1, tm, D), row),
                  pl.BlockSpec((1, 6, D), lambda b_, i, c: (b_, 0, 0)),
                  pl.BlockSpec((1, tm, D), row),
                  pl.BlockSpec((1, tm, LANES), row),
                  pl.BlockSpec((gs, D, D_FF), lambda b_, i, c: (c, 0, 0)),
                  pl.BlockSpec((gs, D, D_FF), lambda b_, i, c: (c, 0, 0)),
                  pl.BlockSpec((gs, D_FF, D), lambda b_, i, c: (c, 0, 0)),
                  pl.BlockSpec((1, D), const2),
                  pl.BlockSpec((1, D), const2)],
        out_specs=pl.BlockSpec((1, tm, D), row),
        out_shape=jax.ShapeDtypeStruct((B, S, D), F32),
        scratch_shapes=[pltpu.VMEM((tm, D), F32),
                        pltpu.VMEM((tm, D + 2 * LANES), MXU_DTYPE),
                        pltpu.VMEM((8, tm), F32),
                        pltpu.VMEM((8, tm), F32)],
        compiler_params=_cparams(("parallel", "parallel", "arbitrary")),
        name="moe_ffn",
    )(x1, mod, h2, gate, w1, w3, w2, g.reshape(1, D), b.reshape(1, D))


def _rope_tables(seq):
    lane = np.arange(LANES)

    def tables(period, offset, rot, theta):
        half = rot // 2
        d = lane % period - offset
        first = (d >= 0) & (d < half)
        second = (d >= half) & (d < rot)
        fidx = np.where(first, d, np.where(second, d - half, 0))
        inv_freq = theta ** (-2.0 * jnp.arange(half, dtype=F32) / rot)
        ang = jnp.arange(seq, dtype=F32)[:, None] * inv_freq[None, :]
        cos_l = jnp.cos(ang)[:, fidx]
        sin_l = jnp.sin(ang)[:, fidx]
        cos = jnp.where((first | second)[None, :], cos_l, 1.0)
        s_up = jnp.where(first[None, :], -sin_l, 0.0)
        s_dn = jnp.where(second[None, :], sin_l, 0.0)
        return [cos, s_up, s_dn]

    tabs = (tables(DIFF_QK, 0, DIFF_QK // 4, ROPE_THETA)
            + tables(DIL_DIM, 0, DIL_DIM // 4, ROPE_THETA)
            + tables(LANES, KR_LANE, MLA_ROPE, MLA_THETA))
    return jnp.stack(tabs).astype(F32)


def _prep_layer(l, w_in, w_uq, w_ukv, w_out, w1, w3, w2):
    D = w_in.shape[1]
    wi = w_in[l]
    zeros = lambda n: jnp.zeros((D, n), wi.dtype)
    wi = jnp.concatenate([wi[:, :C_KR], zeros(KR_LANE), wi[:, C_KR:], zeros(LANES - KR_LANE - MLA_ROPE)], axis=1)
    uq = w_uq[l].reshape(MLA_Q_RANK, MLA_HEADS, MLA_NOPE + MLA_ROPE)
    uq = jnp.pad(uq, ((0, 0), (0, 0), (0, LANES - MLA_NOPE - MLA_ROPE))).reshape(MLA_Q_RANK, MLA_HEADS * LANES)
    ukv = w_ukv[l].reshape(MLA_KV_RANK, MLA_HEADS, MLA_NOPE + MLA_V)
    uk = jnp.pad(ukv[..., :MLA_NOPE], ((0, 0), (0, 0), (0, LANES - MLA_NOPE))).reshape(MLA_KV_RANK, MLA_HEADS * LANES)
    uv = ukv[..., MLA_NOPE:].reshape(MLA_KV_RANK, MLA_HEADS * MLA_V)
    c = lambda t: t.astype(MXU_DTYPE)
    return dict(w_in=c(wi), w_uq=c(uq), w_uk=c(uk), w_uv=c(uv), w_out=c(w_out[l]),
                w1=c(w1[l]), w3=c(w3[l]), w2=c(w2[l]))


def _tile(n, pref):
    return pref if n % pref == 0 else n


TOKEN_TM = 512


def _trunk(x, mods, tabs, layers, p):
    B, S, D = x.shape
    assert S % (DIL_DILS[-1] * DIL_SUB) == 0 and S % DIL_HALO == 0, S
    assert D == layers[0]['w_out'].shape[1], (D, layers[0]['w_out'].shape)
    tm = _tile(S, TOKEN_TM)
    tq = _tile(S, ATTN_TQ * ATTN_PARTS)
    tk = _tile(S, ATTN_TK)
    for l, lw in enumerate(layers):
        mod = mods[l]
        lam_init = 0.8 - 0.6 * math.exp(-0.3 * l)
        outs = _proj(x, mod, tabs, lw['w_in'], p['g_cq'][l], p['g_ckv'][l], lw['w_uq'], lw['w_uk'], lw['w_uv'], tm,
                     ln_in=(p['ln_in_g'], p['ln_in_b']) if l == 0 else None)
        if l == 0:
            x, outs = outs[0], outs[1:]
        qa, ka, va, qb, kb, vb, mq, mk, mv = outs
        oa = _diff_attn(qa, ka, va, p['lam_pack'][l], p['g_sub'][l], lam_init, tq, tk)
        oc = _mla_attn(mq, mk, mv, tq, tk)
        ob = _dil_attn(qb, kb, vb)
        x1, h2, gate = _post(x, mod, oa, ob, oc, lw['w_out'], p['ln1_g'][l], p['ln1_b'][l],
                             p['w_r_hi'], p['w_r_lo'], p['b_r'], tm)
        x = _moe(x1, mod, h2, gate, lw['w1'], lw['w3'], lw['w2'], p['ln2_g'][l], p['ln2_b'][l],
                 _tile(S, MOE_TM))
    return x


def kernel(x_prompt, x_sample, c_prompt, c_sample, ln_in_g, ln_in_b, w_router, b_router, w_ada, b_ada, w_in, lam_q1, lam_k1, lam_q2, lam_k2, g_subln, g_cq, g_ckv, w_uq, w_ukv, w_out, ln1_g, ln1_b, w1, w3, w2, ln2_g, ln2_b):
    depth = w_in.shape[0]
    D = x_prompt.shape[-1]
    bp, bs = c_prompt.shape[0], c_sample.shape[0]
    rows = -(-(bp + bs) // 8) * 8
    c_all = jnp.concatenate([c_prompt, c_sample, jnp.zeros((rows - bp - bs, D), F32)], axis=0)
    mods = _ada(c_all, w_ada, b_ada)
    mods_p = mods[:, :bp].reshape(depth, bp, 6, D)
    mods_s = mods[:, bp:bp + bs].reshape(depth, bs, 6, D)

    pad_l = lambda t: jnp.pad(t, ((0, 0), (0, LANES - t.shape[-1])))
    pad_r = lambda t: jnp.pad(t, ((0, 0),) * (t.ndim - 2) + ((0, SMALL_ROWS - t.shape[-2]), (0, 0)))
    p = dict(
        ln_in_g=ln_in_g, ln_in_b=ln_in_b, g_cq=g_cq, g_ckv=pad_r(g_ckv[:, None, :]),
        ln1_g=ln1_g, ln1_b=ln1_b, ln2_g=ln2_g, ln2_b=ln2_b,
        lam_pack=pad_r(jnp.stack([pad_l(lam_q1), pad_l(lam_k1), pad_l(lam_q2), pad_l(lam_k2)], axis=1)),
        g_sub=g_subln.reshape(depth, DIFF_V, 1),
        b_r=pad_r(pad_l(b_router.reshape(1, -1))),
    )
    w_r = pad_l(w_router)
    p['w_r_hi'] = w_r.astype(MXU_DTYPE)
    p['w_r_lo'] = (w_r - p['w_r_hi'].astype(F32)).astype(MXU_DTYPE)
    layers = [_prep_layer(l, w_in, w_uq, w_ukv, w_out, w1, w3, w2) for l in range(depth)]
    tabs = _rope_tables(max(x_prompt.shape[1], x_sample.shape[1]))
    y_prompt = _trunk(x_prompt, mods_p, tabs, layers, p)
    y_sample = _trunk(x_sample, mods_s, tabs, layers, p)
    return (y_prompt, y_sample)
```

```python
import functools
import math

import numpy as np
import jax
import jax.numpy as jnp
from jax import lax
from jax.experimental import pallas as pl
from jax.experimental.pallas import tpu as pltpu

F32 = jnp.float32
MXU_DTYPE = jnp.bfloat16

LANES = 128
DEPTH = 4
DIFF_HEADS = 4
DIFF_QK = 32
DIFF_V = 64
DIL_HEADS = 6
DIL_DIM = 64
DIL_PAIRS = ((128, 1), (512, 4), (2048, 16))
DIL_RADIUS = 64
MLA_HEADS = 6
MLA_NOPE = 64
MLA_ROPE = 32
MLA_V = 64
MLA_Q_RANK = 256
MLA_KV_RANK = 128
MLA_THETA = 10000.0
ROPE_THETA = 500000.0
N_EXPERTS = 16
GROUP_SIZE = 4
D_FF = 256
ALPHA = (2 * DEPTH) ** 0.25
NEG = -1e30
LOG2E = math.log2(math.e)
ATTN_TQ = 512
ATTN_PARTS = 4
SMALL_ROWS = 32
ATTN_TK = 2048

C_Q1, C_Q2, C_K1, C_K2, C_VA = 0, 128, 256, 384, 512
C_QB, C_KB, C_VB = 768, 1152, 1536
C_CQ, C_CKV, C_KR, C_END = 1920, 2176, 2304, 2432
KR_LANE = 64

VMEM_LIMIT = 48 * 1024 * 1024


def _cparams(sem):
    return pltpu.CompilerParams(dimension_semantics=sem, vmem_limit_bytes=VMEM_LIMIT)


def _lane_iota(shape):
    return lax.broadcasted_iota(jnp.int32, shape, len(shape) - 1)


def _layer_norm(z, g, b):
    mu = jnp.mean(z, axis=-1, keepdims=True)
    zc = z - mu
    var = jnp.mean(zc * zc, axis=-1, keepdims=True)
    return zc * lax.rsqrt(var + 1e-5) * g + b


def _rope(t, tab_ref, typ, half, rows):
    c = tab_ref[3 * typ, rows, :]
    s_up = tab_ref[3 * typ + 1, rows, :]
    s_dn = tab_ref[3 * typ + 2, rows, :]
    return t * c + pltpu.roll(t, LANES - half, 1) * s_up + pltpu.roll(t, half, 1) * s_dn


def _ada_kernel(c_ref, w_ref, b_ref, o_ref):
    c = c_ref[...]
    cs = (c * (1.0 / (1.0 + jnp.exp(-c)))).astype(MXU_DTYPE)
    o_ref[0] = jnp.dot(cs, w_ref[0].astype(MXU_DTYPE), preferred_element_type=F32) + b_ref[0]


def _ada(c, w_ada, b_ada, tn=1536):
    R, D = c.shape
    L, _, N = w_ada.shape
    return pl.pallas_call(
        _ada_kernel,
        grid=(L, N // tn),
        in_specs=[pl.BlockSpec((R, D), lambda l, j: (0, 0)),
                  pl.BlockSpec((1, D, tn), lambda l, j: (l, 0, j)),
                  pl.BlockSpec((1, 1, tn), lambda l, j: (l, 0, j))],
        out_specs=pl.BlockSpec((1, R, tn), lambda l, j: (l, 0, j)),
        out_shape=jax.ShapeDtypeStruct((L, R, N), F32),
        compiler_params=_cparams(("parallel", "parallel")),
        name="ada",
    )(c, w_ada, b_ada.reshape(L, 1, N))


PROJ_PARTS = 2
V_DIM = 64
V_ROWS = V_DIM + 16


def _store_values_t(ref, vt, heads, cols):
    n = vt.shape[1]
    for h in range(heads):
        ref[0, h * V_ROWS:h * V_ROWS + V_DIM, cols] = vt[h * V_DIM:(h + 1) * V_DIM].astype(ref.dtype)
        ref[0, h * V_ROWS + V_DIM:(h + 1) * V_ROWS, cols] = jnp.ones((V_ROWS - V_DIM, n), ref.dtype)


def _proj_kernel(first, x_ref, mod_ref, tab_ref, w_ref, gcq_ref, gckv_ref, wuq_ref, wuk_ref, wuv_ref, *refs):
    if first:
        lng_ref, lnb_ref, xln_ref = refs[:3]
        refs = refs[3:]
    qa_ref, ka_ref, va_ref, qb_ref, kb_ref, vb_ref, mq_ref, mk_ref, mv_ref = refs
    sh1 = mod_ref[0, 0:1, :]
    sc1 = mod_ref[0, 1:2, :]
    tm = x_ref.shape[1]
    parts = PROJ_PARTS if tm % (PROJ_PARTS * LANES) == 0 else 1

    def rms(t, g):
        return (t * lax.rsqrt(jnp.mean(t * t, axis=-1, keepdims=True) + 1e-6) * g).astype(MXU_DTYPE)

    for r in range(parts):
        rows = slice(r * (tm // parts), (r + 1) * (tm // parts))
        x = x_ref[0, rows, :]
        if first:
            x = _layer_norm(x, lng_ref[...], lnb_ref[...])
            xln_ref[0, rows, :] = x
        h = (x * (1.0 + sc1) + sh1).astype(MXU_DTYPE)

        def mm(a, b, h=h):
            return jnp.dot(h, w_ref[:, a:b], preferred_element_type=F32)

        rope = functools.partial(_rope, tab_ref=tab_ref, rows=rows)

        qa = mm(C_Q1, C_K1)
        ka = mm(C_K1, C_VA)
        sa = DIFF_QK ** -0.5 * LOG2E
        for j in range(2):
            blk = slice(j * LANES, (j + 1) * LANES)
            qa_ref[0, rows, blk] = (rope(qa[:, blk], typ=0, half=DIFF_QK // 8) * sa).astype(qa_ref.dtype)
            ka_ref[0, rows, blk] = rope(ka[:, blk], typ=0, half=DIFF_QK // 8).astype(ka_ref.dtype)
        _store_values_t(va_ref, mm(C_VA, C_QB).T, DIFF_HEADS, rows)

        qb = mm(C_QB, C_KB)
        kb = mm(C_KB, C_VB)
        vb = mm(C_VB, C_CQ)
        sb = DIL_DIM ** -0.5 * LOG2E
        for j in range(DIL_HEADS // 2):
            blk = slice(j * LANES, (j + 1) * LANES)
            qb_ref[j, 0, rows, :] = (rope(qb[:, blk], typ=1, half=DIL_DIM // 8) * sb).astype(qb_ref.dtype)
            kb_ref[j, 0, rows, :] = rope(kb[:, blk], typ=1, half=DIL_DIM // 8).astype(kb_ref.dtype)
            vb_ref[j, 0, rows, :] = vb[:, blk].astype(vb_ref.dtype)

        cq = rms(mm(C_CQ, C_CKV), gcq_ref[...])
        ckv = rms(mm(C_CKV, C_KR), gckv_ref[0:1, :])
        kr = rope(mm(C_KR, C_END), typ=2, half=MLA_ROPE // 2)
        q = jnp.dot(cq, wuq_ref[...], preferred_element_type=F32)
        kn = jnp.dot(ckv, wuk_ref[...], preferred_element_type=F32)
        sc = (MLA_NOPE + MLA_ROPE) ** -0.5 * LOG2E
        for j in range(MLA_HEADS):
            blk = slice(j * LANES, (j + 1) * LANES)
            mq_ref[0, rows, blk] = (rope(q[:, blk], typ=2, half=MLA_ROPE // 2) * sc).astype(mq_ref.dtype)
            mk_ref[0, rows, blk] = (kn[:, blk] + kr).astype(mk_ref.dtype)
        _store_values_t(mv_ref, jnp.dot(ckv, wuv_ref[...], preferred_element_type=F32).T, MLA_HEADS, rows)


def _proj(x, mod, tabs, w_in, g_cq, g_ckv, w_uq, w_uk, w_uv, tm, ln_in=None):
    B, S, D = x.shape
    widths = (256, 256, DIFF_HEADS * V_ROWS, 384, 384, 384, 768, 768, MLA_HEADS * V_ROWS)
    transposed = (2, 8)
    row = lambda b, i: (b, i, 0)
    col = lambda b, i: (b, 0, i)
    const2 = lambda b, i: (0, 0)
    out_specs = [pl.BlockSpec((1, w, tm), col) if n in transposed else pl.BlockSpec((1, tm, w), row)
                 for n, w in enumerate(widths)]
    out_shape = [jax.ShapeDtypeStruct((B, w, S) if n in transposed else (B, S, w), MXU_DTYPE)
                 for n, w in enumerate(widths)]
    for n in (3, 4, 5):
        out_specs[n] = pl.BlockSpec((DIL_HEADS // 2, 1, tm, LANES), lambda b, i: (0, b, i, 0))
        out_shape[n] = jax.ShapeDtypeStruct((DIL_HEADS // 2, B, S, LANES), MXU_DTYPE)
    in_specs = [pl.BlockSpec((1, tm, D), row),
                pl.BlockSpec((1, 6, D), lambda b, i: (b, 0, 0)),
                pl.BlockSpec((9, tm, LANES), lambda b, i: (0, i, 0)),
                pl.BlockSpec(w_in.shape, const2),
                pl.BlockSpec((1, MLA_Q_RANK), const2),
                pl.BlockSpec((SMALL_ROWS, MLA_KV_RANK), const2),
                pl.BlockSpec(w_uq.shape, const2),
                pl.BlockSpec(w_uk.shape, const2),
                pl.BlockSpec(w_uv.shape, const2)]
    args = [x, mod, tabs, w_in, g_cq.reshape(1, -1), g_ckv, w_uq, w_uk, w_uv]
    first = ln_in is not None
    if first:
        in_specs += [pl.BlockSpec((1, D), const2), pl.BlockSpec((1, D), const2)]
        args += [ln_in[0].reshape(1, D), ln_in[1].reshape(1, D)]
        out_specs = [pl.BlockSpec((1, tm, D), row)] + out_specs
        out_shape = [jax.ShapeDtypeStruct((B, S, D), F32)] + out_shape
    return pl.pallas_call(
        functools.partial(_proj_kernel, first),
        grid=(B, S // tm),
        in_specs=in_specs,
        out_specs=out_specs,
        out_shape=out_shape,
        compiler_params=_cparams(("parallel", "parallel")),
        name="proj_in",
    )(*args)


REF_ROWS = 128
EXP_HEADROOM = 64.0


def _scores(chains):
    return [lax.dot_general(k, q, (((1,), (1,)), ((), ())), preferred_element_type=F32)
            for q, k, _ in chains]


def _softmax_steps_exact(chains, m_scr, acc_scr, base):
    for u, (st, (_, _, vt)) in enumerate(zip(_scores(chains), chains), start=base):
        m_prev = m_scr[u]
        m_new = jnp.maximum(m_prev, jnp.max(st, axis=0, keepdims=True))
        alpha = jnp.exp2(m_prev - m_new)
        p = jnp.exp2(st - m_new)
        acc_scr[u] = alpha * acc_scr[u] + jnp.dot(vt, p.astype(MXU_DTYPE), preferred_element_type=F32)
        m_scr[u] = m_new


def _softmax_steps(chains, m_scr, acc_scr, base=0):
    new_state = []
    worst = None
    for u, (st, (_, _, vt)) in enumerate(zip(_scores(chains), chains), start=base):
        m_prev = m_scr[u]
        m_ref = jnp.maximum(m_prev, jnp.max(st[:REF_ROWS], axis=0, keepdims=True))
        p = jnp.exp2(st - m_ref)
        m_new = jnp.maximum(m_prev, jnp.max(st, axis=0, keepdims=True))
        acc = jnp.exp2(m_ref - m_new) * (
            jnp.exp2(m_prev - m_ref) * acc_scr[u] + jnp.dot(vt, p.astype(MXU_DTYPE), preferred_element_type=F32))
        new_state.append((u, m_new, acc))
        gap = m_new - m_ref
        worst = gap if worst is None else jnp.maximum(worst, gap)
    safe = jnp.max(worst) <= EXP_HEADROOM

    @pl.when(safe)
    def _():
        for u, m_new, acc in new_state:
            m_scr[u] = m_new
            acc_scr[u] = acc

    @pl.when(jnp.logical_not(safe))
    def _():
        _softmax_steps_exact(chains, m_scr, acc_scr, base)


def _init_softmax(m_scr, acc_scr):
    m_scr[...] = jnp.full(m_scr.shape, NEG, F32)
    acc_scr[...] = jnp.zeros(acc_scr.shape, F32)


def _softmax_out(acc_scr, u):
    return acc_scr[u, :V_DIM] / acc_scr[u, V_DIM:V_DIM + 1]


def _diff_attn_kernel(lam_init, q1_ref, q2_ref, k1_ref, k2_ref, vt_ref, lamp_ref, g_ref, o_ref,
                      q_scr, m_scr, acc_scr):
    j = pl.program_id(1)
    ik = pl.program_id(3)

    t = m_scr.shape[2]
    parts = q1_ref.shape[1] // t

    @pl.when(ik == 0)
    def _():
        _init_softmax(m_scr, acc_scr)
        lane = _lane_iota((t, LANES))
        for r in range(parts):
            rows = slice(r * t, (r + 1) * t)
            for hh in range(2):
                h = 2 * j + hh
                msk = (lane >= h * DIFF_QK) & (lane < (h + 1) * DIFF_QK)
                q_scr[4 * r + 2 * hh] = jnp.where(msk, q1_ref[0, rows, :].astype(F32), 0.0).astype(q_scr.dtype)
                q_scr[4 * r + 2 * hh + 1] = jnp.where(msk, q2_ref[0, rows, :].astype(F32), 0.0).astype(q_scr.dtype)

    for r in range(parts):
        chains = []
        for hh in range(2):
            vt = vt_ref[0, hh * V_ROWS:(hh + 1) * V_ROWS, :]
            chains.append((q_scr[4 * r + 2 * hh], k1_ref[0], vt))
            chains.append((q_scr[4 * r + 2 * hh + 1], k2_ref[0], vt))
        _softmax_steps(chains, m_scr, acc_scr, 4 * r)

    @pl.when(ik == pl.num_programs(3) - 1)
    def _():
        lp = lamp_ref[0:4, :]
        lam = (jnp.exp(jnp.sum(lp[0:1] * lp[1:2], axis=-1, keepdims=True))
               - jnp.exp(jnp.sum(lp[2:3] * lp[3:4], axis=-1, keepdims=True)) + lam_init)
        for r in range(parts):
            for hh in range(2):
                u = 4 * r + 2 * hh
                o = _softmax_out(acc_scr, u) - lam * _softmax_out(acc_scr, u + 1)
                ms = jnp.mean(o * o, axis=0, keepdims=True)
                o_ref[0, hh * DIFF_V:(hh + 1) * DIFF_V, r * t:(r + 1) * t] = (
                    o * lax.rsqrt(ms + 1e-6) * g_ref[...] * (1.0 - lam_init)).astype(o_ref.dtype)


def _diff_attn(qa, ka, vat, lam_pack, g_sub, lam_init, tq, tk):
    B, S, _ = qa.shape
    parts = tq // ATTN_TQ if tq % ATTN_TQ == 0 else 1
    kern = functools.partial(_diff_attn_kernel, lam_init)
    return pl.pallas_call(
        kern,
        grid=(B, DIFF_HEADS // 2, S // tq, S // tk),
        in_specs=[pl.BlockSpec((1, tq, LANES), lambda b, j, iq, ik: (b, iq, 0)),
                  pl.BlockSpec((1, tq, LANES), lambda b, j, iq, ik: (b, iq, 1)),
                  pl.BlockSpec((1, tk, LANES), lambda b, j, iq, ik: (b, ik, 0)),
                  pl.BlockSpec((1, tk, LANES), lambda b, j, iq, ik: (b, ik, 1)),
                  pl.BlockSpec((1, 2 * V_ROWS, tk), lambda b, j, iq, ik: (b, j, ik)),
                  pl.BlockSpec((SMALL_ROWS, LANES), lambda b, j, iq, ik: (0, 0)),
                  pl.BlockSpec((DIFF_V, 1), lambda b, j, iq, ik: (0, 0))],
        out_specs=pl.BlockSpec((1, 2 * DIFF_V, tq), lambda b, j, iq, ik: (b, j, iq)),
        out_shape=jax.ShapeDtypeStruct((B, DIFF_HEADS * DIFF_V, S), MXU_DTYPE),
        scratch_shapes=[pltpu.VMEM((4 * parts, tq // parts, LANES), MXU_DTYPE),
                        pltpu.VMEM((4 * parts, 1, tq // parts), F32),
                        pltpu.VMEM((4 * parts, V_ROWS, tq // parts), F32)],
        compiler_params=_cparams(("parallel", "parallel", "parallel", "arbitrary")),
        name="diff_attn",
    )(qa, qa, ka, ka, vat, lam_pack, g_sub)


MLA_GROUP = 3


def _mla_attn_kernel(q_ref, k_ref, vt_ref, o_ref, m_scr, acc_scr):
    ik = pl.program_id(3)

    @pl.when(ik == 0)
    def _():
        _init_softmax(m_scr, acc_scr)

    t = m_scr.shape[2]
    parts = q_ref.shape[1] // t
    for r in range(parts):
        rows = slice(r * t, (r + 1) * t)
        chains = []
        for hh in range(MLA_GROUP):
            blk = slice(hh * LANES, (hh + 1) * LANES)
            chains.append((q_ref[0, rows, blk], k_ref[0, :, blk], vt_ref[0, hh * V_ROWS:(hh + 1) * V_ROWS, :]))
        _softmax_steps(chains, m_scr, acc_scr, r * MLA_GROUP)

    @pl.when(ik == pl.num_programs(3) - 1)
    def _():
        for r in range(parts):
            for hh in range(MLA_GROUP):
                o_ref[0, hh * MLA_V:(hh + 1) * MLA_V, r * t:(r + 1) * t] = _softmax_out(
                    acc_scr, r * MLA_GROUP + hh).astype(o_ref.dtype)


def _mla_attn(mq, mk, mvt, tq, tk):
    B, S, _ = mq.shape
    g = MLA_GROUP
    parts = tq // ATTN_TQ if tq % ATTN_TQ == 0 else 1
    return pl.pallas_call(
        _mla_attn_kernel,
        grid=(B, MLA_HEADS // g, S // tq, S // tk),
        in_specs=[pl.BlockSpec((1, tq, g * LANES), lambda b, j, iq, ik: (b, iq, j)),
                  pl.BlockSpec((1, tk, g * LANES), lambda b, j, iq, ik: (b, ik, j)),
                  pl.BlockSpec((1, g * V_ROWS, tk), lambda b, j, iq, ik: (b, j, ik))],
        out_specs=pl.BlockSpec((1, g * MLA_V, tq), lambda b, j, iq, ik: (b, j, iq)),
        out_shape=jax.ShapeDtypeStruct((B, MLA_HEADS * MLA_V, S), MXU_DTYPE),
        scratch_shapes=[pltpu.VMEM((g * parts, 1, tq // parts), F32),
                        pltpu.VMEM((g * parts, V_ROWS, tq // parts), F32)],
        compiler_params=_cparams(("parallel", "parallel", "parallel", "arbitrary")),
        name="mla_attn",
    )(mq, mk, mvt)


DIL_SUB = 128
DIL_WIDTH = DIL_HEADS * DIL_DIM


DIL_DILS = tuple(d for _, d in DIL_PAIRS)
DIL_HALO = DIL_RADIUS * DIL_DILS[-1]
DIL_TILE = 2048
DIL_SPLIT = 4


def _dil_kernel(seq, q_ref, kp_ref, kc_ref, kn_ref, vp_ref, vc_ref, vn_ref, o_ref,
                qbuf, kbuf, vbuf, pat_o, pat_l, qg, kg, vg, q4, k4, v4):
    i = pl.program_id(2)
    tt = q_ref.shape[2]
    H = DIL_HALO
    R = DIL_RADIUS
    qbuf[...] = q_ref[0, 0].astype(F32)
    kbuf[0:H] = kp_ref[0, 0].astype(F32)
    kbuf[H:H + tt] = kc_ref[0, 0].astype(F32)
    kbuf[H + tt:] = kn_ref[0, 0].astype(F32)
    vbuf[0:H] = vp_ref[0, 0].astype(F32)
    vbuf[H:H + tt] = vc_ref[0, 0].astype(F32)
    vbuf[H + tt:] = vn_ref[0, 0].astype(F32)
    vg[:, LANES:, :] = jnp.ones((vg.shape[0], vg.shape[1] - LANES, vg.shape[2]), vg.dtype)
    for r4 in range(DIL_SPLIT):
        q4[r4] = qbuf[pl.ds(r4, tt // DIL_SPLIT, stride=DIL_SPLIT), :]
        k4[r4] = kbuf[pl.ds(r4, (tt + 2 * H) // DIL_SPLIT, stride=DIL_SPLIT), :]
        v4[r4] = vbuf[pl.ds(r4, (tt + 2 * H) // DIL_SPLIT, stride=DIL_SPLIT), :]
    sub = DIL_SUB
    win = sub + 2 * R
    nc = tt // sub
    cidx = lax.broadcasted_iota(jnp.int32, (nc, win, sub), 0)
    row_k = lax.broadcasted_iota(jnp.int32, (nc, win, sub), 1)
    col_q = lax.broadcasted_iota(jnp.int32, (nc, win, sub), 2)
    rel = row_k - col_q
    band = (rel >= 0) & (rel <= 2 * R)
    lane = _lane_iota((sub, LANES))
    for p, d in enumerate(DIL_DILS):
        n_sub = tt // d // sub
        length = seq // d
        q_starts = [c // n_sub + d * sub * (c % n_sub) for c in range(nc)]
        for c in range(nc):
            r, u = c // n_sub, c % n_sub
            y0 = H // d + u * sub - R
            if d % DIL_SPLIT:
                qf = qbuf[pl.ds(q_starts[c], sub, stride=d), :]
                kw = kbuf[pl.ds(r + d * y0, win, stride=d), :]
                vw = vbuf[pl.ds(r + d * y0, win, stride=d), :]
            else:
                e = d // DIL_SPLIT
                r4, m = r % DIL_SPLIT, r // DIL_SPLIT
                qf = q4[r4, pl.ds(m + e * sub * u, sub, stride=e), :]
                kw = k4[r4, pl.ds(m + e * y0, win, stride=e), :]
                vw = v4[r4, pl.ds(m + e * y0, win, stride=e), :]
            qg[c, :sub] = jnp.where(lane < DIL_DIM, qf, 0.0).astype(qg.dtype)
            qg[c, sub:] = jnp.where(lane >= DIL_DIM, qf, 0.0).astype(qg.dtype)
            kg[c] = kw.astype(kg.dtype)
            vg[c, :LANES] = vw.T.astype(vg.dtype)
        u_of_c = cidx % n_sub if n_sub > 1 else jnp.zeros_like(cidx)
        kpos = i * (tt // d) + u_of_c * sub - R + row_k
        valid = band & (kpos >= 0) & (kpos < length)
        s = lax.dot_general(kg[...], qg[...], (((2,), (2,)), ((0,), (0,))),
                            preferred_element_type=F32)
        s = jnp.where(jnp.concatenate([valid, valid], axis=2), s, NEG)
        m = jnp.max(s, axis=1, keepdims=True)
        e = jnp.exp2(s - m)
        ot = lax.dot_general(vg[...], e.astype(MXU_DTYPE), (((2,), (1,)), ((0,), (0,))),
                             preferred_element_type=F32)
        l = ot[:, LANES:LANES + 1, :]
        ot = ot[:, :LANES, :] / l
        lse = m + jnp.log2(l)
        for c in range(nc):
            o_t = jnp.concatenate([ot[c, :DIL_DIM, :sub], ot[c, DIL_DIM:, sub:]], axis=0)
            l_t = jnp.concatenate([jnp.broadcast_to(lse[c, :, :sub], (DIL_DIM, sub)),
                                   jnp.broadcast_to(lse[c, :, sub:], (DIL_DIM, sub))], axis=0)
            pat_o[p, pl.ds(q_starts[c], sub, stride=d), :] = o_t.T
            pat_l[p, pl.ds(q_starts[c], sub, stride=d), :] = l_t.T

    l0, l1, l2 = pat_l[0], pat_l[1], pat_l[2]
    mx = jnp.maximum(jnp.maximum(l0, l1), l2)
    e0, e1, e2 = jnp.exp2(l0 - mx), jnp.exp2(l1 - mx), jnp.exp2(l2 - mx)
    den = e0 + e1 + e2
    o_ref[0] = ((e0 / den) * pat_o[0] + (e1 / den) * pat_o[1] + (e2 / den) * pat_o[2]).astype(o_ref.dtype)


def _dil_attn(qb, kb, vb):
    P, B, S, _ = qb.shape
    tt = _tile(S, DIL_TILE)
    H = DIL_HALO
    nh = S // H
    cur = pl.BlockSpec((1, 1, tt, LANES), lambda b, hp, i: (hp, b, i, 0))
    prv = pl.BlockSpec((1, 1, H, LANES), lambda b, hp, i: (hp, b, jnp.maximum(i * (tt // H) - 1, 0), 0))
    nxt = pl.BlockSpec((1, 1, H, LANES), lambda b, hp, i: (hp, b, jnp.minimum((i + 1) * (tt // H), nh - 1), 0))
    return pl.pallas_call(
        functools.partial(_dil_kernel, S),
        grid=(B, P, S // tt),
        in_specs=[cur, prv, cur, nxt, prv, cur, nxt],
        out_specs=pl.BlockSpec((1, tt, LANES), lambda b, hp, i: (b, i, hp)),
        out_shape=jax.ShapeDtypeStruct((B, S, P * LANES), MXU_DTYPE),
        scratch_shapes=[pltpu.VMEM((tt, LANES), F32),
                        pltpu.VMEM((tt + 2 * H, LANES), F32),
                        pltpu.VMEM((tt + 2 * H, LANES), F32),
                        pltpu.VMEM((len(DIL_DILS), tt, LANES), F32),
                        pltpu.VMEM((len(DIL_DILS), tt, LANES), F32),
                        pltpu.VMEM((tt // DIL_SUB, 2 * DIL_SUB, LANES), MXU_DTYPE),
                        pltpu.VMEM((tt // DIL_SUB, DIL_SUB + 2 * DIL_RADIUS, LANES), MXU_DTYPE),
                        pltpu.VMEM((tt // DIL_SUB, LANES + 16, DIL_SUB + 2 * DIL_RADIUS), MXU_DTYPE),
                        pltpu.VMEM((DIL_SPLIT, tt // DIL_SPLIT, LANES), F32),
                        pltpu.VMEM((DIL_SPLIT, (tt + 2 * H) // DIL_SPLIT, LANES), F32),
                        pltpu.VMEM((DIL_SPLIT, (tt + 2 * H) // DIL_SPLIT, LANES), F32)],
        compiler_params=_cparams(("parallel", "parallel", "parallel")),
        name="dil_attn",
    )(qb, kb, kb, kb, vb, vb, vb)


def _partner(x, k, lane):
    up = pltpu.roll(x, LANES - k, 1)
    dn = pltpu.roll(x, k, 1)
    return jnp.where((lane & k) == 0, up, dn)


GROUP_LANE = N_EXPERTS


def _route(logits, bias, lane):
    valid = lane < N_EXPERTS
    lane_f = lane.astype(F32)
    score = 1.0 / (1.0 + jnp.exp(-logits))
    biased = jnp.where(valid, score + bias, NEG)
    p1 = _partner(biased, 1, lane)
    hi = jnp.maximum(biased, p1)
    lo = jnp.minimum(biased, p1)
    hi_p = _partner(hi, 2, lane)
    lo_p = _partner(lo, 2, lane)
    top1 = jnp.maximum(hi, hi_p)
    top2 = jnp.maximum(jnp.minimum(hi, hi_p), jnp.maximum(lo, lo_p))
    gscore = jnp.where(valid, top1 + top2, NEG)
    gidx = (lane // GROUP_SIZE).astype(F32)
    big = float(LANES)
    gmax = jnp.max(gscore, axis=-1, keepdims=True)
    gsel = jnp.min(jnp.where(gscore == gmax, gidx, big), axis=-1, keepdims=True)
    cand = jnp.where((gidx == gsel) & valid, biased, NEG)
    m1 = jnp.max(cand, axis=-1, keepdims=True)
    i1 = jnp.min(jnp.where(cand == m1, lane_f, big), axis=-1, keepdims=True)
    oh1 = lane_f == i1
    cand2 = jnp.where(oh1, NEG, cand)
    m2 = jnp.max(cand2, axis=-1, keepdims=True)
    i2 = jnp.min(jnp.where((cand2 == m2) & jnp.logical_not(oh1), lane_f, big), axis=-1, keepdims=True)
    oh2 = lane_f == i2
    s1 = jnp.sum(jnp.where(oh1, score, 0.0), axis=-1, keepdims=True)
    s2 = jnp.sum(jnp.where(oh2, score, 0.0), axis=-1, keepdims=True)
    den = s1 + s2
    gate = jnp.where(oh1, s1 / den, 0.0) + jnp.where(oh2, s2 / den, 0.0)
    return jnp.where(lane == GROUP_LANE, gsel, gate)


POST_PARTS = 2


def _post_kernel(x_ref, mod_ref, oa_ref, ob_ref, oc_ref, wo_ref, g_ref, b_ref, wrh_ref, wrl_ref, br_ref,
                 x1_ref, h2_ref, gate_ref):
    na = DIFF_HEADS * DIFF_V
    nb = na + DIL_WIDTH
    tn = (((0,), (0,)), ((), ()))
    g1 = mod_ref[0, 2:3, :]
    sh2 = mod_ref[0, 3:4, :]
    sc2 = mod_ref[0, 4:5, :]
    tm = x_ref.shape[1]
    parts = POST_PARTS if tm % (POST_PARTS * LANES) == 0 else 1
    for r in range(parts):
        rows = slice(r * (tm // parts), (r + 1) * (tm // parts))
        y = (lax.dot_general(oa_ref[0, :, rows], wo_ref[0:na], tn, preferred_element_type=F32)
             + jnp.dot(ob_ref[0, rows, :], wo_ref[na:nb], preferred_element_type=F32)
             + lax.dot_general(oc_ref[0, :, rows], wo_ref[nb:], tn, preferred_element_type=F32))
        x1 = _layer_norm(ALPHA * x_ref[0, rows, :] + g1 * y, g_ref[...], b_ref[...])
        x1_ref[0, rows, :] = x1
        h2 = x1 * (1.0 + sc2) + sh2
        h2_ref[0, rows, :] = h2.astype(h2_ref.dtype)
        h_hi = h2.astype(MXU_DTYPE)
        h_lo = (h2 - h_hi.astype(F32)).astype(MXU_DTYPE)
        logits = (jnp.dot(h_hi, wrh_ref[...], preferred_element_type=F32)
                  + jnp.dot(h_lo, wrh_ref[...], preferred_element_type=F32)
                  + jnp.dot(h_hi, wrl_ref[...], preferred_element_type=F32))
        gate_ref[0, rows, :] = _route(logits, br_ref[0:1, :], _lane_iota(logits.shape))


def _post(x, mod, oa, ob, oc, w_out, g, b, w_r_hi, w_r_lo, b_r, tm):
    B, S, D = x.shape
    row = lambda b_, i: (b_, i, 0)
    const2 = lambda b_, i: (0, 0)
    return pl.pallas_call(
        _post_kernel,
        grid=(B, S // tm),
        in_specs=[pl.BlockSpec((1, tm, D), row),
                  pl.BlockSpec((1, 6, D), lambda b_, i: (b_, 0, 0)),
                  pl.BlockSpec((1, oa.shape[1], tm), lambda b_, i: (b_, 0, i)),
                  pl.BlockSpec((1, tm, DIL_WIDTH), row),
                  pl.BlockSpec((1, oc.shape[1], tm), lambda b_, i: (b_, 0, i)),
                  pl.BlockSpec(w_out.shape, const2),
                  pl.BlockSpec((1, D), const2),
                  pl.BlockSpec((1, D), const2),
                  pl.BlockSpec(w_r_hi.shape, const2),
                  pl.BlockSpec(w_r_lo.shape, const2),
                  pl.BlockSpec((SMALL_ROWS, LANES), const2)],
        out_specs=[pl.BlockSpec((1, tm, D), row),
                   pl.BlockSpec((1, tm, D), row),
                   pl.BlockSpec((1, tm, LANES), row)],
        out_shape=[jax.ShapeDtypeStruct((B, S, D), F32),
                   jax.ShapeDtypeStruct((B, S, D), MXU_DTYPE),
                   jax.ShapeDtypeStruct((B, S, LANES), F32)],
        compiler_params=_cparams(("parallel", "parallel")),
        name="post_mix",
    )(x, mod, oa, ob, oc, w_out, g.reshape(1, D), b.reshape(1, D), w_r_hi, w_r_lo, b_r)


MOE_TM = 1024
MOE_CHUNK = 256
MOE_TAIL = 64
N_GROUPS = N_EXPERTS // GROUP_SIZE


def _moe_kernel(x1_ref, mod_ref, h2_ref, gate_ref, w1_ref, w3_ref, w2_ref, g_ref, b_ref, o_ref,
                acc_scr, src_scr, sel_scr, rank_scr):
    g = pl.program_id(2)
    tm = h2_ref.shape[1]
    d = h2_ref.shape[2]
    C = MOE_CHUNK

    @pl.when(g == 0)
    def _():
        acc_scr[...] = jnp.zeros(acc_scr.shape, F32)
        gate = gate_ref[0]
        gate_hi = gate.astype(MXU_DTYPE)
        src_scr[:, 0:d] = h2_ref[0]
        src_scr[:, d:d + LANES] = gate_hi
        src_scr[:, d + LANES:] = (gate - gate_hi.astype(F32)).astype(MXU_DTYPE)
        gsel_t = gate.T[GROUP_LANE:GROUP_LANE + 1, :]
        grp = lax.broadcasted_iota(jnp.int32, sel_scr.shape, 0).astype(F32)
        sel = jnp.where(gsel_t == grp, 1.0, 0.0)
        r = lax.broadcasted_iota(jnp.int32, (tm, tm), 0)
        c = lax.broadcasted_iota(jnp.int32, (tm, tm), 1)
        before = jnp.where(r < c, 1.0, 0.0).astype(MXU_DTYPE)
        sel_scr[...] = sel
        rank_scr[...] = jnp.dot(sel.astype(MXU_DTYPE), before, preferred_element_type=F32)

    sel = sel_scr[pl.ds(g, 1), :]
    rank = rank_scr[pl.ds(g, 1), :]
    n_g = jnp.sum(sel).astype(jnp.int32)

    def chunk(base, rows):
        lane = _lane_iota((rows, LANES))
        row = lax.broadcasted_iota(jnp.int32, (rows, tm), 0).astype(F32)
        onehot = jnp.where((rank - base.astype(F32) == row) & (sel > 0.5), 1.0, 0.0).astype(MXU_DTYPE)
        picked = jnp.dot(onehot, src_scr[...], preferred_element_type=F32)
        xs = picked[:, 0:d].astype(MXU_DTYPE)
        gs = picked[:, d:d + LANES] + picked[:, d + LANES:]
        hids = []
        for e in range(GROUP_SIZE):
            ge = jnp.sum(jnp.where(lane == g * GROUP_SIZE + e, gs, 0.0), axis=-1, keepdims=True)
            a = jnp.dot(xs, w1_ref[e], preferred_element_type=F32)
            bb = jnp.dot(xs, w3_ref[e], preferred_element_type=F32)
            hids.append(((a * (1.0 / (1.0 + jnp.exp(-a)))) * bb * ge).astype(MXU_DTYPE))
        ys = jnp.dot(jnp.concatenate(hids, axis=1), w2_ref[...].reshape(GROUP_SIZE * D_FF, d),
                     preferred_element_type=F32)
        acc_scr[...] += lax.dot_general(onehot, ys.astype(MXU_DTYPE), (((0,), (0,)), ((), ())),
                                        preferred_element_type=F32)

    n_chunks = (n_g + C - 1) // C
    tail = n_g - (n_chunks - 1) * C
    merge = (n_chunks >= 2) & (tail <= MOE_TAIL)
    n_plain = jnp.where(merge, n_chunks - 2, n_chunks)

    def plain(ci, carry):
        chunk(ci * C, C)
        return carry

    lax.fori_loop(0, n_plain, plain, 0)

    @pl.when(merge)
    def _():
        chunk(n_plain * C, C + MOE_TAIL)

    @pl.when(g == pl.num_programs(2) - 1)
    def _():
        g2 = mod_ref[0, 5:6, :]
        o_ref[0] = _layer_norm(ALPHA * x1_ref[0] + g2 * acc_scr[...], g_ref[...], b_ref[...])


def _moe(x1, mod, h2, gate, w1, w3, w2, g, b, tm):
    B, S, D = x1.shape
    gs = GROUP_SIZE
    row = lambda b_, i, c: (b_, i, 0)
    const2 = lambda b_, i, c: (0, 0)
    return pl.pallas_call(
        _moe_kernel,
        grid=(B, S // tm, N_GROUPS),
        in_specs=[pl.BlockSpec((1, tm, D), row),
                  pl.BlockSpec((1, 6, D), lambda b_, i, c: (b_, 0, 0)),
                  pl.BlockSpec((1, tm, D), row),
                  pl.BlockSpec((1, tm, LANES), row),
                  pl.BlockSpec((gs, D, D_FF), lambda b_, i, c: (c, 0, 0)),
                  pl.BlockSpec((gs, D, D_FF), lambda b_, i, c: (c, 0, 0)),
                  pl.BlockSpec((gs, D_FF, D), lambda b_, i, c: (c, 0, 0)),
                  pl.BlockSpec((1, D), const2),
                  pl.BlockSpec((1, D), const2)],
        out_specs=pl.BlockSpec((1, tm, D), row),
        out_shape=jax.ShapeDtypeStruct((B, S, D), F32),
        scratch_shapes=[pltpu.VMEM((tm, D), F32),
                        pltpu.VMEM((tm, D + 2 * LANES), MXU_DTYPE),
                        pltpu.VMEM((8, tm), F32),
                        pltpu.VMEM((8, tm), F32)],
        compiler_params=_cparams(("parallel", "parallel", "arbitrary")),
        name="moe_ffn",
    )(x1, mod, h2, gate, w1, w3, w2, g.reshape(1, D), b.reshape(1, D))


def _rope_tables(seq):
    lane = np.arange(LANES)

    def tables(period, offset, rot, theta):
        half = rot // 2
        d = lane % period - offset
        first = (d >= 0) & (d < half)
        second = (d >= half) & (d < rot)
        fidx = np.where(first, d, np.where(second, d - half, 0))
        inv_freq = theta ** (-2.0 * jnp.arange(half, dtype=F32) / rot)
        ang = jnp.arange(seq, dtype=F32)[:, None] * inv_freq[None, :]
        cos_l = jnp.cos(ang)[:, fidx]
        sin_l = jnp.sin(ang)[:, fidx]
        cos = jnp.where((first | second)[None, :], cos_l, 1.0)
        s_up = jnp.where(first[None, :], -sin_l, 0.0)
        s_dn = jnp.where(second[None, :], sin_l, 0.0)
        return [cos, s_up, s_dn]

    tabs = (tables(DIFF_QK, 0, DIFF_QK // 4, ROPE_THETA)
            + tables(DIL_DIM, 0, DIL_DIM // 4, ROPE_THETA)
            + tables(LANES, KR_LANE, MLA_ROPE, MLA_THETA))
    return jnp.stack(tabs).astype(F32)


def _prep_layer(l, w_in, w_uq, w_ukv, w_out, w1, w3, w2):
    D = w_in.shape[1]
    wi = w_in[l]
    zeros = lambda n: jnp.zeros((D, n), wi.dtype)
    wi = jnp.concatenate([wi[:, :C_KR], zeros(KR_LANE), wi[:, C_KR:], zeros(LANES - KR_LANE - MLA_ROPE)], axis=1)
    uq = w_uq[l].reshape(MLA_Q_RANK, MLA_HEADS, MLA_NOPE + MLA_ROPE)
    uq = jnp.pad(uq, ((0, 0), (0, 0), (0, LANES - MLA_NOPE - MLA_ROPE))).reshape(MLA_Q_RANK, MLA_HEADS * LANES)
    ukv = w_ukv[l].reshape(MLA_KV_RANK, MLA_HEADS, MLA_NOPE + MLA_V)
    uk = jnp.pad(ukv[..., :MLA_NOPE], ((0, 0), (0, 0), (0, LANES - MLA_NOPE))).reshape(MLA_KV_RANK, MLA_HEADS * LANES)
    uv = ukv[..., MLA_NOPE:].reshape(MLA_KV_RANK, MLA_HEADS * MLA_V)
    c = lambda t: t.astype(MXU_DTYPE)
    return dict(w_in=c(wi), w_uq=c(uq), w_uk=c(uk), w_uv=c(uv), w_out=c(w_out[l]),
                w1=c(w1[l]), w3=c(w3[l]), w2=c(w2[l]))


def _tile(n, pref):
    return pref if n % pref == 0 else n


TOKEN_TM = 512


def _trunk(x, mods, tabs, layers, p):
    B, S, D = x.shape
    assert S % (DIL_DILS[-1] * DIL_SUB) == 0 and S % DIL_HALO == 0, S
    assert D == layers[0]['w_out'].shape[1], (D, layers[0]['w_out'].shape)
    tm = _tile(S, TOKEN_TM)
    tq = _tile(S, ATTN_TQ * ATTN_PARTS)
    tk = _tile(S, ATTN_TK)
    for l, lw in enumerate(layers):
        mod = mods[l]
        lam_init = 0.8 - 0.6 * math.exp(-0.3 * l)
        outs = _proj(x, mod, tabs, lw['w_in'], p['g_cq'][l], p['g_ckv'][l], lw['w_uq'], lw['w_uk'], lw['w_uv'], tm,
                     ln_in=(p['ln_in_g'], p['ln_in_b']) if l == 0 else None)
        if l == 0:
            x, outs = outs[0], outs[1:]
        qa, ka, va, qb, kb, vb, mq, mk, mv = outs
        oa = _diff_attn(qa, ka, va, p['lam_pack'][l], p['g_sub'][l], lam_init, tq, tk)
        oc = _mla_attn(mq, mk, mv, tq, tk)
        ob = _dil_attn(qb, kb, vb)
        x1, h2, gate = _post(x, mod, oa, ob, oc, lw['w_out'], p['ln1_g'][l], p['ln1_b'][l],
                             p['w_r_hi'], p['w_r_lo'], p['b_r'], tm)
        x = _moe(x1, mod, h2, gate, lw['w1'], lw['w3'], lw['w2'], p['ln2_g'][l], p['ln2_b'][l],
                 _tile(S, MOE_TM))
    return x


def kernel(x_prompt, x_sample, c_prompt, c_sample, ln_in_g, ln_in_b, w_router, b_router, w_ada, b_ada, w_in, lam_q1, lam_k1, lam_q2, lam_k2, g_subln, g_cq, g_ckv, w_uq, w_ukv, w_out, ln1_g, ln1_b, w1, w3, w2, ln2_g, ln2_b):
    depth = w_in.shape[0]
    D = x_prompt.shape[-1]
    bp, bs = c_prompt.shape[0], c_sample.shape[0]
    rows = -(-(bp + bs) // 8) * 8
    c_all = jnp.concatenate([c_prompt, c_sample, jnp.zeros((rows - bp - bs, D), F32)], axis=0)
    mods = _ada(c_all, w_ada, b_ada)
    mods_p = mods[:, :bp].reshape(depth, bp, 6, D)
    mods_s = mods[:, bp:bp + bs].reshape(depth, bs, 6, D)

    pad_l = lambda t: jnp.pad(t, ((0, 0), (0, LANES - t.shape[-1])))
    pad_r = lambda t: jnp.pad(t, ((0, 0),) * (t.ndim - 2) + ((0, SMALL_ROWS - t.shape[-2]), (0, 0)))
    p = dict(
        ln_in_g=ln_in_g, ln_in_b=ln_in_b, g_cq=g_cq, g_ckv=pad_r(g_ckv[:, None, :]),
        ln1_g=ln1_g, ln1_b=ln1_b, ln2_g=ln2_g, ln2_b=ln2_b,
        lam_pack=pad_r(jnp.stack([pad_l(lam_q1), pad_l(lam_k1), pad_l(lam_q2), pad_l(lam_k2)], axis=1)),
        g_sub=g_subln.reshape(depth, DIFF_V, 1),
        b_r=pad_r(pad_l(b_router.reshape(1, -1))),
    )
    w_r = pad_l(w_router)
    p['w_r_hi'] = w_r.astype(MXU_DTYPE)
    p['w_r_lo'] = (w_r - p['w_r_hi'].astype(F32)).astype(MXU_DTYPE)
    layers = [_prep_layer(l, w_in, w_uq, w_ukv, w_out, w1, w3, w2) for l in range(depth)]
    tabs = _rope_tables(max(x_prompt.shape[1], x_sample.shape[1]))
    y_prompt = _trunk(x_prompt, mods_p, tabs, layers, p)
    y_sample = _trunk(x_sample, mods_s, tabs, layers, p)
    return (y_prompt, y_sample)
```

```python
import functools
import math

import numpy as np
import jax
import jax.numpy as jnp
from jax import lax
from jax.experimental import pallas as pl
from jax.experimental.pallas import tpu as pltpu

F32 = jnp.float32
MXU_DTYPE = jnp.bfloat16

LANES = 128
DEPTH = 4
DIFF_HEADS = 4
DIFF_QK = 32
DIFF_V = 64
DIL_HEADS = 6
DIL_DIM = 64
DIL_PAIRS = ((128, 1), (512, 4), (2048, 16))
DIL_RADIUS = 64
MLA_HEADS = 6
MLA_NOPE = 64
MLA_ROPE = 32
MLA_V = 64
MLA_Q_RANK = 256
MLA_KV_RANK = 128
MLA_THETA = 10000.0
ROPE_THETA = 500000.0
N_EXPERTS = 16
GROUP_SIZE = 4
D_FF = 256
ALPHA = (2 * DEPTH) ** 0.25
NEG = -1e30
LOG2E = math.log2(math.e)
ATTN_TQ = 512
ATTN_PARTS = 4
SMALL_ROWS = 32
ATTN_TK = 2048

C_Q1, C_Q2, C_K1, C_K2, C_VA = 0, 128, 256, 384, 512
C_QB, C_KB, C_VB = 768, 1152, 1536
C_CQ, C_CKV, C_KR, C_END = 1920, 2176, 2304, 2432
KR_LANE = 64

VMEM_LIMIT = 56 * 1024 * 1024


def _cparams(sem):
    return pltpu.CompilerParams(dimension_semantics=sem, vmem_limit_bytes=VMEM_LIMIT)


def _lane_iota(shape):
    return lax.broadcasted_iota(jnp.int32, shape, len(shape) - 1)


def _layer_norm(z, g, b):
    mu = jnp.mean(z, axis=-1, keepdims=True)
    zc = z - mu
    var = jnp.mean(zc * zc, axis=-1, keepdims=True)
    return zc * lax.rsqrt(var + 1e-5) * g + b


def _rope(t, tab_ref, typ, half, rows):
    c = tab_ref[3 * typ, rows, :]
    s_up = tab_ref[3 * typ + 1, rows, :]
    s_dn = tab_ref[3 * typ + 2, rows, :]
    return t * c + pltpu.roll(t, LANES - half, 1) * s_up + pltpu.roll(t, half, 1) * s_dn


def _ada_kernel(c_ref, w_ref, b_ref, o_ref):
    c = c_ref[...]
    cs = (c * (1.0 / (1.0 + jnp.exp(-c)))).astype(MXU_DTYPE)
    o_ref[0] = jnp.dot(cs, w_ref[0].astype(MXU_DTYPE), preferred_element_type=F32) + b_ref[0]


def _ada(c, w_ada, b_ada, tn=1536):
    R, D = c.shape
    L, _, N = w_ada.shape
    return pl.pallas_call(
        _ada_kernel,
        grid=(L, N // tn),
        in_specs=[pl.BlockSpec((R, D), lambda l, j: (0, 0)),
                  pl.BlockSpec((1, D, tn), lambda l, j: (l, 0, j)),
                  pl.BlockSpec((1, 1, tn), lambda l, j: (l, 0, j))],
        out_specs=pl.BlockSpec((1, R, tn), lambda l, j: (l, 0, j)),
        out_shape=jax.ShapeDtypeStruct((L, R, N), F32),
        compiler_params=_cparams(("parallel", "parallel")),
        name="ada",
    )(c, w_ada, b_ada.reshape(L, 1, N))


PROJ_PARTS = 2
V_DIM = 64
V_ROWS = V_DIM + 16


def _store_values_t(ref, vt, heads, cols):
    n = vt.shape[1]
    for h in range(heads):
        ref[0, h * V_ROWS:h * V_ROWS + V_DIM, cols] = vt[h * V_DIM:(h + 1) * V_DIM].astype(ref.dtype)
        ref[0, h * V_ROWS + V_DIM:(h + 1) * V_ROWS, cols] = jnp.ones((V_ROWS - V_DIM, n), ref.dtype)


def _proj_kernel(first, x_ref, mod_ref, tab_ref, w_ref, gcq_ref, gckv_ref, wuq_ref, wuk_ref, wuv_ref, *refs):
    if first:
        lng_ref, lnb_ref, xln_ref = refs[:3]
        refs = refs[3:]
    qa_ref, ka_ref, va_ref, qb_ref, kb_ref, vb_ref, mq_ref, mk_ref, mv_ref = refs
    sh1 = mod_ref[0, 0:1, :]
    sc1 = mod_ref[0, 1:2, :]
    tm = x_ref.shape[1]
    parts = PROJ_PARTS if tm % (PROJ_PARTS * LANES) == 0 else 1

    def rms(t, g):
        return (t * lax.rsqrt(jnp.mean(t * t, axis=-1, keepdims=True) + 1e-6) * g).astype(MXU_DTYPE)

    for r in range(parts):
        rows = slice(r * (tm // parts), (r + 1) * (tm // parts))
        x = x_ref[0, rows, :]
        if first:
            x = _layer_norm(x, lng_ref[...], lnb_ref[...])
            xln_ref[0, rows, :] = x
        h = (x * (1.0 + sc1) + sh1).astype(MXU_DTYPE)

        def mm(a, b, h=h):
            return jnp.dot(h, w_ref[:, a:b], preferred_element_type=F32)

        rope = functools.partial(_rope, tab_ref=tab_ref, rows=rows)

        qa = mm(C_Q1, C_K1)
        ka = mm(C_K1, C_VA)
        sa = DIFF_QK ** -0.5 * LOG2E
        for j in range(2):
            blk = slice(j * LANES, (j + 1) * LANES)
            qa_ref[0, rows, blk] = (rope(qa[:, blk], typ=0, half=DIFF_QK // 8) * sa).astype(qa_ref.dtype)
            ka_ref[0, rows, blk] = rope(ka[:, blk], typ=0, half=DIFF_QK // 8).astype(ka_ref.dtype)
        _store_values_t(va_ref, mm(C_VA, C_QB).T, DIFF_HEADS, rows)

        qb = mm(C_QB, C_KB)
        kb = mm(C_KB, C_VB)
        vb = mm(C_VB, C_CQ)
        sb = DIL_DIM ** -0.5 * LOG2E
        for j in range(DIL_HEADS // 2):
            blk = slice(j * LANES, (j + 1) * LANES)
            qb_ref[j, 0, rows, :] = (rope(qb[:, blk], typ=1, half=DIL_DIM // 8) * sb).astype(qb_ref.dtype)
            kb_ref[j, 0, rows, :] = rope(kb[:, blk], typ=1, half=DIL_DIM // 8).astype(kb_ref.dtype)
            vb_ref[j, 0, rows, :] = vb[:, blk].astype(vb_ref.dtype)

        cq = rms(mm(C_CQ, C_CKV), gcq_ref[...])
        ckv = rms(mm(C_CKV, C_KR), gckv_ref[0:1, :])
        kr = rope(mm(C_KR, C_END), typ=2, half=MLA_ROPE // 2)
        q = jnp.dot(cq, wuq_ref[...], preferred_element_type=F32)
        kn = jnp.dot(ckv, wuk_ref[...], preferred_element_type=F32)
        sc = (MLA_NOPE + MLA_ROPE) ** -0.5 * LOG2E
        for j in range(MLA_HEADS):
            blk = slice(j * LANES, (j + 1) * LANES)
            mq_ref[0, rows, blk] = (rope(q[:, blk], typ=2, half=MLA_ROPE // 2) * sc).astype(mq_ref.dtype)
            mk_ref[0, rows, blk] = (kn[:, blk] + kr).astype(mk_ref.dtype)
        _store_values_t(mv_ref, jnp.dot(ckv, wuv_ref[...], preferred_element_type=F32).T, MLA_HEADS, rows)


def _proj(x, mod, tabs, w_in, g_cq, g_ckv, w_uq, w_uk, w_uv, tm, ln_in=None):
    B, S, D = x.shape
    widths = (256, 256, DIFF_HEADS * V_ROWS, 384, 384, 384, 768, 768, MLA_HEADS * V_ROWS)
    transposed = (2, 8)
    row = lambda b, i: (b, i, 0)
    col = lambda b, i: (b, 0, i)
    const2 = lambda b, i: (0, 0)
    out_specs = [pl.BlockSpec((1, w, tm), col) if n in transposed else pl.BlockSpec((1, tm, w), row)
                 for n, w in enumerate(widths)]
    out_shape = [jax.ShapeDtypeStruct((B, w, S) if n in transposed else (B, S, w), MXU_DTYPE)
                 for n, w in enumerate(widths)]
    for n in (3, 4, 5):
        out_specs[n] = pl.BlockSpec((DIL_HEADS // 2, 1, tm, LANES), lambda b, i: (0, b, i, 0))
        out_shape[n] = jax.ShapeDtypeStruct((DIL_HEADS // 2, B, S, LANES), MXU_DTYPE)
    in_specs = [pl.BlockSpec((1, tm, D), row),
                pl.BlockSpec((1, 6, D), lambda b, i: (b, 0, 0)),
                pl.BlockSpec((9, tm, LANES), lambda b, i: (0, i, 0)),
                pl.BlockSpec(w_in.shape, const2),
                pl.BlockSpec((1, MLA_Q_RANK), const2),
                pl.BlockSpec((SMALL_ROWS, MLA_KV_RANK), const2),
                pl.BlockSpec(w_uq.shape, const2),
                pl.BlockSpec(w_uk.shape, const2),
                pl.BlockSpec(w_uv.shape, const2)]
    args = [x, mod, tabs, w_in, g_cq.reshape(1, -1), g_ckv, w_uq, w_uk, w_uv]
    first = ln_in is not None
    if first:
        in_specs += [pl.BlockSpec((1, D), const2), pl.BlockSpec((1, D), const2)]
        args += [ln_in[0].reshape(1, D), ln_in[1].reshape(1, D)]
        out_specs = [pl.BlockSpec((1, tm, D), row)] + out_specs
        out_shape = [jax.ShapeDtypeStruct((B, S, D), F32)] + out_shape
    return pl.pallas_call(
        functools.partial(_proj_kernel, first),
        grid=(B, S // tm),
        in_specs=in_specs,
        out_specs=out_specs,
        out_shape=out_shape,
        compiler_params=_cparams(("parallel", "parallel")),
        name="proj_in",
    )(*args)


REF_ROWS = 128
EXP_HEADROOM = 64.0


def _scores(chains):
    return [lax.dot_general(k, q, (((1,), (1,)), ((), ())), preferred_element_type=F32)
            for q, k, _ in chains]


def _softmax_steps_exact(chains, m_scr, acc_scr, base):
    for u, (st, (_, _, vt)) in enumerate(zip(_scores(chains), chains), start=base):
        m_prev = m_scr[u]
        m_new = jnp.maximum(m_prev, jnp.max(st, axis=0, keepdims=True))
        alpha = jnp.exp2(m_prev - m_new)
        p = jnp.exp2(st - m_new)
        acc_scr[u] = alpha * acc_scr[u] + jnp.dot(vt, p.astype(MXU_DTYPE), preferred_element_type=F32)
        m_scr[u] = m_new


def _softmax_steps(chains, m_scr, acc_scr, base=0):
    new_state = []
    worst = None
    for u, (st, (_, _, vt)) in enumerate(zip(_scores(chains), chains), start=base):
        m_prev = m_scr[u]
        m_ref = jnp.maximum(m_prev, jnp.max(st[:REF_ROWS], axis=0, keepdims=True))
        p = jnp.exp2(st - m_ref)
        m_new = jnp.maximum(m_prev, jnp.max(st, axis=0, keepdims=True))
        acc = jnp.exp2(m_ref - m_new) * (
            jnp.exp2(m_prev - m_ref) * acc_scr[u] + jnp.dot(vt, p.astype(MXU_DTYPE), preferred_element_type=F32))
        new_state.append((u, m_new, acc))
        gap = m_new - m_ref
        worst = gap if worst is None else jnp.maximum(worst, gap)
    safe = jnp.max(worst) <= EXP_HEADROOM

    def resolve():
        @pl.when(safe)
        def _():
            for u, m_new, acc in new_state:
                m_scr[u] = m_new
                acc_scr[u] = acc

        @pl.when(jnp.logical_not(safe))
        def _():
            _softmax_steps_exact(chains, m_scr, acc_scr, base)

    return resolve


def _run_parts(part_chains, m_scr, acc_scr):
    pending = None
    base = 0
    for chains in part_chains:
        resolve = _softmax_steps(chains, m_scr, acc_scr, base)
        base += len(chains)
        if pending is not None:
            pending()
        pending = resolve
    pending()


def _init_softmax(m_scr, acc_scr):
    m_scr[...] = jnp.full(m_scr.shape, NEG, F32)
    acc_scr[...] = jnp.zeros(acc_scr.shape, F32)


def _softmax_out(acc_scr, u):
    return acc_scr[u, :V_DIM] / acc_scr[u, V_DIM:V_DIM + 1]


def _diff_attn_kernel(lam_init, q1_ref, q2_ref, k1_ref, k2_ref, vt_ref, lamp_ref, g_ref, o_ref,
                      q_scr, m_scr, acc_scr):
    j = pl.program_id(1)
    ik = pl.program_id(3)

    t = m_scr.shape[2]
    parts = q1_ref.shape[1] // t

    @pl.when(ik == 0)
    def _():
        _init_softmax(m_scr, acc_scr)
        lane = _lane_iota((t, LANES))
        for r in range(parts):
            rows = slice(r * t, (r + 1) * t)
            for hh in range(2):
                h = 2 * j + hh
                msk = (lane >= h * DIFF_QK) & (lane < (h + 1) * DIFF_QK)
                q_scr[4 * r + 2 * hh] = jnp.where(msk, q1_ref[0, rows, :].astype(F32), 0.0).astype(q_scr.dtype)
                q_scr[4 * r + 2 * hh + 1] = jnp.where(msk, q2_ref[0, rows, :].astype(F32), 0.0).astype(q_scr.dtype)

    part_chains = []
    for r in range(parts):
        chains = []
        for hh in range(2):
            vt = vt_ref[0, hh * V_ROWS:(hh + 1) * V_ROWS, :]
            chains.append((q_scr[4 * r + 2 * hh], k1_ref[0], vt))
            chains.append((q_scr[4 * r + 2 * hh + 1], k2_ref[0], vt))
        part_chains.append(chains)
    _run_parts(part_chains, m_scr, acc_scr)

    @pl.when(ik == pl.num_programs(3) - 1)
    def _():
        lp = lamp_ref[0:4, :]
        lam = (jnp.exp(jnp.sum(lp[0:1] * lp[1:2], axis=-1, keepdims=True))
               - jnp.exp(jnp.sum(lp[2:3] * lp[3:4], axis=-1, keepdims=True)) + lam_init)
        for r in range(parts):
            for hh in range(2):
                u = 4 * r + 2 * hh
                o = _softmax_out(acc_scr, u) - lam * _softmax_out(acc_scr, u + 1)
                ms = jnp.mean(o * o, axis=0, keepdims=True)
                o_ref[0, hh * DIFF_V:(hh + 1) * DIFF_V, r * t:(r + 1) * t] = (
                    o * lax.rsqrt(ms + 1e-6) * g_ref[...] * (1.0 - lam_init)).astype(o_ref.dtype)


def _diff_attn(qa, ka, vat, lam_pack, g_sub, lam_init, tq, tk):
    B, S, _ = qa.shape
    parts = tq // ATTN_TQ if tq % ATTN_TQ == 0 else 1
    kern = functools.partial(_diff_attn_kernel, lam_init)
    return pl.pallas_call(
        kern,
        grid=(B, DIFF_HEADS // 2, S // tq, S // tk),
        in_specs=[pl.BlockSpec((1, tq, LANES), lambda b, j, iq, ik: (b, iq, 0)),
                  pl.BlockSpec((1, tq, LANES), lambda b, j, iq, ik: (b, iq, 1)),
                  pl.BlockSpec((1, tk, LANES), lambda b, j, iq, ik: (b, ik, 0)),
                  pl.BlockSpec((1, tk, LANES), lambda b, j, iq, ik: (b, ik, 1)),
                  pl.BlockSpec((1, 2 * V_ROWS, tk), lambda b, j, iq, ik: (b, j, ik)),
                  pl.BlockSpec((SMALL_ROWS, LANES), lambda b, j, iq, ik: (0, 0)),
                  pl.BlockSpec((DIFF_V, 1), lambda b, j, iq, ik: (0, 0))],
        out_specs=pl.BlockSpec((1, 2 * DIFF_V, tq), lambda b, j, iq, ik: (b, j, iq)),
        out_shape=jax.ShapeDtypeStruct((B, DIFF_HEADS * DIFF_V, S), MXU_DTYPE),
        scratch_shapes=[pltpu.VMEM((4 * parts, tq // parts, LANES), MXU_DTYPE),
                        pltpu.VMEM((4 * parts, 1, tq // parts), F32),
                        pltpu.VMEM((4 * parts, V_ROWS, tq // parts), F32)],
        compiler_params=_cparams(("parallel", "parallel", "parallel", "arbitrary")),
        name="diff_attn",
    )(qa, qa, ka, ka, vat, lam_pack, g_sub)


MLA_GROUP = 3


def _mla_attn_kernel(q_ref, k_ref, vt_ref, o_ref, m_scr, acc_scr):
    ik = pl.program_id(3)

    @pl.when(ik == 0)
    def _():
        _init_softmax(m_scr, acc_scr)

    t = m_scr.shape[2]
    parts = q_ref.shape[1] // t
    part_chains = []
    for r in range(parts):
        rows = slice(r * t, (r + 1) * t)
        chains = []
        for hh in range(MLA_GROUP):
            blk = slice(hh * LANES, (hh + 1) * LANES)
            chains.append((q_ref[0, rows, blk], k_ref[0, :, blk], vt_ref[0, hh * V_ROWS:(hh + 1) * V_ROWS, :]))
        part_chains.append(chains)
    _run_parts(part_chains, m_scr, acc_scr)

    @pl.when(ik == pl.num_programs(3) - 1)
    def _():
        for r in range(parts):
            for hh in range(MLA_GROUP):
                o_ref[0, hh * MLA_V:(hh + 1) * MLA_V, r * t:(r + 1) * t] = _softmax_out(
                    acc_scr, r * MLA_GROUP + hh).astype(o_ref.dtype)


def _mla_attn(mq, mk, mvt, tq, tk):
    B, S, _ = mq.shape
    g = MLA_GROUP
    parts = tq // ATTN_TQ if tq % ATTN_TQ == 0 else 1
    return pl.pallas_call(
        _mla_attn_kernel,
        grid=(B, MLA_HEADS // g, S // tq, S // tk),
        in_specs=[pl.BlockSpec((1, tq, g * LANES), lambda b, j, iq, ik: (b, iq, j)),
                  pl.BlockSpec((1, tk, g * LANES), lambda b, j, iq, ik: (b, ik, j)),
                  pl.BlockSpec((1, g * V_ROWS, tk), lambda b, j, iq, ik: (b, j, ik))],
        out_specs=pl.BlockSpec((1, g * MLA_V, tq), lambda b, j, iq, ik: (b, j, iq)),
        out_shape=jax.ShapeDtypeStruct((B, MLA_HEADS * MLA_V, S), MXU_DTYPE),
        scratch_shapes=[pltpu.VMEM((g * parts, 1, tq // parts), F32),
                        pltpu.VMEM((g * parts, V_ROWS, tq // parts), F32)],
        compiler_params=_cparams(("parallel", "parallel", "parallel", "arbitrary")),
        name="mla_attn",
    )(mq, mk, mvt)


DIL_SUB = 128
DIL_WIDTH = DIL_HEADS * DIL_DIM


DIL_DILS = tuple(d for _, d in DIL_PAIRS)
DIL_HALO = DIL_RADIUS * DIL_DILS[-1]
DIL_TILE = 2048
DIL_SPLIT = 4


def _dil_kernel(seq, q_ref, kp_ref, kc_ref, kn_ref, vp_ref, vc_ref, vn_ref, o_ref,
                qbuf, kbuf, vbuf, pat_o, pat_l, qg, kg, vg, q4, k4, v4):
    i = pl.program_id(2)
    tt = q_ref.shape[2]
    H = DIL_HALO
    R = DIL_RADIUS
    qbuf[...] = q_ref[0, 0].astype(F32)
    kbuf[0:H] = kp_ref[0, 0].astype(F32)
    kbuf[H:H + tt] = kc_ref[0, 0].astype(F32)
    kbuf[H + tt:] = kn_ref[0, 0].astype(F32)
    vbuf[0:H] = vp_ref[0, 0].astype(F32)
    vbuf[H:H + tt] = vc_ref[0, 0].astype(F32)
    vbuf[H + tt:] = vn_ref[0, 0].astype(F32)
    vg[:, LANES:, :] = jnp.ones((vg.shape[0], vg.shape[1] - LANES, vg.shape[2]), vg.dtype)
    for r4 in range(DIL_SPLIT):
        q4[r4] = qbuf[pl.ds(r4, tt // DIL_SPLIT, stride=DIL_SPLIT), :]
        k4[r4] = kbuf[pl.ds(r4, (tt + 2 * H) // DIL_SPLIT, stride=DIL_SPLIT), :]
        v4[r4] = vbuf[pl.ds(r4, (tt + 2 * H) // DIL_SPLIT, stride=DIL_SPLIT), :]
    sub = DIL_SUB
    win = sub + 2 * R
    nc = tt // sub
    cidx = lax.broadcasted_iota(jnp.int32, (nc, win, sub), 0)
    row_k = lax.broadcasted_iota(jnp.int32, (nc, win, sub), 1)
    col_q = lax.broadcasted_iota(jnp.int32, (nc, win, sub), 2)
    rel = row_k - col_q
    band = (rel >= 0) & (rel <= 2 * R)
    lane = _lane_iota((sub, LANES))
    for p, d in enumerate(DIL_DILS):
        n_sub = tt // d // sub
        length = seq // d
        q_starts = [c // n_sub + d * sub * (c % n_sub) for c in range(nc)]
        for c in range(nc):
            r, u = c // n_sub, c % n_sub
            y0 = H // d + u * sub - R
            if d % DIL_SPLIT:
                qf = qbuf[pl.ds(q_starts[c], sub, stride=d), :]
                kw = kbuf[pl.ds(r + d * y0, win, stride=d), :]
                vw = vbuf[pl.ds(r + d * y0, win, stride=d), :]
            else:
                e = d // DIL_SPLIT
                r4, m = r % DIL_SPLIT, r // DIL_SPLIT
                qf = q4[r4, pl.ds(m + e * sub * u, sub, stride=e), :]
                kw = k4[r4, pl.ds(m + e * y0, win, stride=e), :]
                vw = v4[r4, pl.ds(m + e * y0, win, stride=e), :]
            qg[c, :sub] = jnp.where(lane < DIL_DIM, qf, 0.0).astype(qg.dtype)
            qg[c, sub:] = jnp.where(lane >= DIL_DIM, qf, 0.0).astype(qg.dtype)
            kg[c] = kw.astype(kg.dtype)
            vg[c, :LANES] = vw.T.astype(vg.dtype)
        u_of_c = cidx % n_sub if n_sub > 1 else jnp.zeros_like(cidx)
        kpos = i * (tt // d) + u_of_c * sub - R + row_k
        valid = band & (kpos >= 0) & (kpos < length)
        s = lax.dot_general(kg[...], qg[...], (((2,), (2,)), ((0,), (0,))),
                            preferred_element_type=F32)
        s = jnp.where(jnp.concatenate([valid, valid], axis=2), s, NEG)
        m = jnp.max(s, axis=1, keepdims=True)
        e = jnp.exp2(s - m)
        ot = lax.dot_general(vg[...], e.astype(MXU_DTYPE), (((2,), (1,)), ((0,), (0,))),
                             preferred_element_type=F32)
        l = ot[:, LANES:LANES + 1, :]
        ot = ot[:, :LANES, :] / l
        lse = m + jnp.log2(l)
        for c in range(nc):
            o_t = jnp.concatenate([ot[c, :DIL_DIM, :sub], ot[c, DIL_DIM:, sub:]], axis=0)
            l_t = jnp.concatenate([jnp.broadcast_to(lse[c, :, :sub], (DIL_DIM, sub)),
                                   jnp.broadcast_to(lse[c, :, sub:], (DIL_DIM, sub))], axis=0)
            pat_o[p, pl.ds(q_starts[c], sub, stride=d), :] = o_t.T
            pat_l[p, pl.ds(q_starts[c], sub, stride=d), :] = l_t.T

    l0, l1, l2 = pat_l[0], pat_l[1], pat_l[2]
    mx = jnp.maximum(jnp.maximum(l0, l1), l2)
    e0, e1, e2 = jnp.exp2(l0 - mx), jnp.exp2(l1 - mx), jnp.exp2(l2 - mx)
    den = e0 + e1 + e2
    o_ref[0] = ((e0 / den) * pat_o[0] + (e1 / den) * pat_o[1] + (e2 / den) * pat_o[2]).astype(o_ref.dtype)


def _dil_attn(qb, kb, vb):
    P, B, S, _ = qb.shape
    tt = _tile(S, DIL_TILE)
    H = DIL_HALO
    nh = S // H
    cur = pl.BlockSpec((1, 1, tt, LANES), lambda b, hp, i: (hp, b, i, 0))
    prv = pl.BlockSpec((1, 1, H, LANES), lambda b, hp, i: (hp, b, jnp.maximum(i * (tt // H) - 1, 0), 0))
    nxt = pl.BlockSpec((1, 1, H, LANES), lambda b, hp, i: (hp, b, jnp.minimum((i + 1) * (tt // H), nh - 1), 0))
    return pl.pallas_call(
        functools.partial(_dil_kernel, S),
        grid=(B, P, S // tt),
        in_specs=[cur, prv, cur, nxt, prv, cur, nxt],
        out_specs=pl.BlockSpec((1, tt, LANES), lambda b, hp, i: (b, i, hp)),
        out_shape=jax.ShapeDtypeStruct((B, S, P * LANES), MXU_DTYPE),
        scratch_shapes=[pltpu.VMEM((tt, LANES), F32),
                        pltpu.VMEM((tt + 2 * H, LANES), F32),
                        pltpu.VMEM((tt + 2 * H, LANES), F32),
                        pltpu.VMEM((len(DIL_DILS), tt, LANES), F32),
                        pltpu.VMEM((len(DIL_DILS), tt, LANES), F32),
                        pltpu.VMEM((tt // DIL_SUB, 2 * DIL_SUB, LANES), MXU_DTYPE),
                        pltpu.VMEM((tt // DIL_SUB, DIL_SUB + 2 * DIL_RADIUS, LANES), MXU_DTYPE),
                        pltpu.VMEM((tt // DIL_SUB, LANES + 16, DIL_SUB + 2 * DIL_RADIUS), MXU_DTYPE),
                        pltpu.VMEM((DIL_SPLIT, tt // DIL_SPLIT, LANES), F32),
                        pltpu.VMEM((DIL_SPLIT, (tt + 2 * H) // DIL_SPLIT, LANES), F32),
                        pltpu.VMEM((DIL_SPLIT, (tt + 2 * H) // DIL_SPLIT, LANES), F32)],
        compiler_params=_cparams(("parallel", "parallel", "parallel")),
        name="dil_attn",
    )(qb, kb, kb, kb, vb, vb, vb)


def _partner(x, k, lane):
    up = pltpu.roll(x, LANES - k, 1)
    dn = pltpu.roll(x, k, 1)
    return jnp.where((lane & k) == 0, up, dn)


GROUP_LANE = N_EXPERTS


def _route(logits, bias, lane):
    valid = lane < N_EXPERTS
    lane_f = lane.astype(F32)
    score = 1.0 / (1.0 + jnp.exp(-logits))
    biased = jnp.where(valid, score + bias, NEG)
    p1 = _partner(biased, 1, lane)
    hi = jnp.maximum(biased, p1)
    lo = jnp.minimum(biased, p1)
    hi_p = _partner(hi, 2, lane)
    lo_p = _partner(lo, 2, lane)
    top1 = jnp.maximum(hi, hi_p)
    top2 = jnp.maximum(jnp.minimum(hi, hi_p), jnp.maximum(lo, lo_p))
    gscore = jnp.where(valid, top1 + top2, NEG)
    gidx = (lane // GROUP_SIZE).astype(F32)
    big = float(LANES)
    gmax = jnp.max(gscore, axis=-1, keepdims=True)
    gsel = jnp.min(jnp.where(gscore == gmax, gidx, big), axis=-1, keepdims=True)
    cand = jnp.where((gidx == gsel) & valid, biased, NEG)
    m1 = jnp.max(cand, axis=-1, keepdims=True)
    i1 = jnp.min(jnp.where(cand == m1, lane_f, big), axis=-1, keepdims=True)
    oh1 = lane_f == i1
    cand2 = jnp.where(oh1, NEG, cand)
    m2 = jnp.max(cand2, axis=-1, keepdims=True)
    i2 = jnp.min(jnp.where((cand2 == m2) & jnp.logical_not(oh1), lane_f, big), axis=-1, keepdims=True)
    oh2 = lane_f == i2
    s1 = jnp.sum(jnp.where(oh1, score, 0.0), axis=-1, keepdims=True)
    s2 = jnp.sum(jnp.where(oh2, score, 0.0), axis=-1, keepdims=True)
    den = s1 + s2
    gate = jnp.where(oh1, s1 / den, 0.0) + jnp.where(oh2, s2 / den, 0.0)
    return jnp.where(lane == GROUP_LANE, gsel, gate)


POST_PARTS = 2


def _post_kernel(x_ref, mod_ref, oa_ref, ob_ref, oc_ref, wo_ref, g_ref, b_ref, wrh_ref, wrl_ref, br_ref,
                 x1_ref, h2_ref, gate_ref):
    na = DIFF_HEADS * DIFF_V
    nb = na + DIL_WIDTH
    tn = (((0,), (0,)), ((), ()))
    g1 = mod_ref[0, 2:3, :]
    sh2 = mod_ref[0, 3:4, :]
    sc2 = mod_ref[0, 4:5, :]
    tm = x_ref.shape[1]
    parts = POST_PARTS if tm % (POST_PARTS * LANES) == 0 else 1
    for r in range(parts):
        rows = slice(r * (tm // parts), (r + 1) * (tm // parts))
        y = (lax.dot_general(oa_ref[0, :, rows], wo_ref[0:na], tn, preferred_element_type=F32)
             + jnp.dot(ob_ref[0, rows, :], wo_ref[na:nb], preferred_element_type=F32)
             + lax.dot_general(oc_ref[0, :, rows], wo_ref[nb:], tn, preferred_element_type=F32))
        x1 = _layer_norm(ALPHA * x_ref[0, rows, :] + g1 * y, g_ref[...], b_ref[...])
        x1_ref[0, rows, :] = x1
        h2 = x1 * (1.0 + sc2) + sh2
        h2_ref[0, rows, :] = h2.astype(h2_ref.dtype)
        h_hi = h2.astype(MXU_DTYPE)
        h_lo = (h2 - h_hi.astype(F32)).astype(MXU_DTYPE)
        logits = (jnp.dot(h_hi, wrh_ref[...], preferred_element_type=F32)
                  + jnp.dot(h_lo, wrh_ref[...], preferred_element_type=F32)
                  + jnp.dot(h_hi, wrl_ref[...], preferred_element_type=F32))
        gate_ref[0, rows, :] = _route(logits, br_ref[0:1, :], _lane_iota(logits.shape))


def _post(x, mod, oa, ob, oc, w_out, g, b, w_r_hi, w_r_lo, b_r, tm):
    B, S, D = x.shape
    row = lambda b_, i: (b_, i, 0)
    const2 = lambda b_, i: (0, 0)
    return pl.pallas_call(
        _post_kernel,
        grid=(B, S // tm),
        in_specs=[pl.BlockSpec((1, tm, D), row),
                  pl.BlockSpec((1, 6, D), lambda b_, i: (b_, 0, 0)),
                  pl.BlockSpec((1, oa.shape[1], tm), lambda b_, i: (b_, 0, i)),
                  pl.BlockSpec((1, tm, DIL_WIDTH), row),
                  pl.BlockSpec((1, oc.shape[1], tm), lambda b_, i: (b_, 0, i)),
                  pl.BlockSpec(w_out.shape, const2),
                  pl.BlockSpec((1, D), const2),
                  pl.BlockSpec((1, D), const2),
                  pl.BlockSpec(w_r_hi.shape, const2),
                  pl.BlockSpec(w_r_lo.shape, const2),
                  pl.BlockSpec((SMALL_ROWS, LANES), const2)],
        out_specs=[pl.BlockSpec((1, tm, D), row),
                   pl.BlockSpec((1, tm, D), row),
                   pl.BlockSpec((1, tm, LANES), row)],
        out_shape=[jax.ShapeDtypeStruct((B, S, D), F32),
                   jax.ShapeDtypeStruct((B, S, D), MXU_DTYPE),
                   jax.ShapeDtypeStruct((B, S, LANES), F32)],
        compiler_params=_cparams(("parallel", "parallel")),
        name="post_mix",
    )(x, mod, oa, ob, oc, w_out, g.reshape(1, D), b.reshape(1, D), w_r_hi, w_r_lo, b_r)


MOE_TM = 1024
MOE_CHUNK = 256
MOE_TAIL = 64
N_GROUPS = N_EXPERTS // GROUP_SIZE


def _moe_kernel(x1_ref, mod_ref, h2_ref, gate_ref, w1_ref, w3_ref, w2_ref, g_ref, b_ref, o_ref,
                acc_scr, src_scr, sel_scr, rank_scr):
    g = pl.program_id(2)
    tm = h2_ref.shape[1]
    d = h2_ref.shape[2]
    C = MOE_CHUNK

    @pl.when(g == 0)
    def _():
        acc_scr[...] = jnp.zeros(acc_scr.shape, F32)
        gate = gate_ref[0]
        gate_hi = gate.astype(MXU_DTYPE)
        src_scr[:, 0:d] = h2_ref[0]
        src_scr[:, d:d + LANES] = gate_hi
        src_scr[:, d + LANES:] = (gate - gate_hi.astype(F32)).astype(MXU_DTYPE)
        gsel_t = gate.T[GROUP_LANE:GROUP_LANE + 1, :]
        grp = lax.broadcasted_iota(jnp.int32, sel_scr.shape, 0).astype(F32)
        sel = jnp.where(gsel_t == grp, 1.0, 0.0)
        r = lax.broadcasted_iota(jnp.int32, (tm, tm), 0)
        c = lax.broadcasted_iota(jnp.int32, (tm, tm), 1)
        before = jnp.where(r < c, 1.0, 0.0).astype(MXU_DTYPE)
        sel_scr[...] = sel
        rank_scr[...] = jnp.dot(sel.astype(MXU_DTYPE), before, preferred_element_type=F32)

    sel = sel_scr[pl.ds(g, 1), :]
    rank = rank_scr[pl.ds(g, 1), :]
    n_g = jnp.sum(sel).astype(jnp.int32)

    def chunk(base, rows):
        lane = _lane_iota((rows, LANES))
        row = lax.broadcasted_iota(jnp.int32, (rows, tm), 0).astype(F32)
        onehot = jnp.where((rank - base.astype(F32) == row) & (sel > 0.5), 1.0, 0.0).astype(MXU_DTYPE)
        picked = jnp.dot(onehot, src_scr[...], preferred_element_type=F32)
        xs = picked[:, 0:d].astype(MXU_DTYPE)
        gs = picked[:, d:d + LANES] + picked[:, d + LANES:]
        hids = []
        for e in range(GROUP_SIZE):
            ge = jnp.sum(jnp.where(lane == g * GROUP_SIZE + e, gs, 0.0), axis=-1, keepdims=True)
            a = jnp.dot(xs, w1_ref[e], preferred_element_type=F32)
            bb = jnp.dot(xs, w3_ref[e], preferred_element_type=F32)
            hids.append(((a * (1.0 / (1.0 + jnp.exp(-a)))) * bb * ge).astype(MXU_DTYPE))
        ys = jnp.dot(jnp.concatenate(hids, axis=1), w2_ref[...].reshape(GROUP_SIZE * D_FF, d),
                     preferred_element_type=F32)
        acc_scr[...] += lax.dot_general(onehot, ys.astype(MXU_DTYPE), (((0,), (0,)), ((), ())),
                                        preferred_element_type=F32)

    n_chunks = (n_g + C - 1) // C
    tail = n_g - (n_chunks - 1) * C
    merge = (n_chunks >= 2) & (tail <= MOE_TAIL)
    n_plain = jnp.where(merge, n_chunks - 2, n_chunks)

    def plain(ci, carry):
        chunk(ci * C, C)
        return carry

    lax.fori_loop(0, n_plain, plain, 0)

    @pl.when(merge)
    def _():
        chunk(n_plain * C, C + MOE_TAIL)

    @pl.when(g == pl.num_programs(2) - 1)
    def _():
        g2 = mod_ref[0, 5:6, :]
        o_ref[0] = _layer_norm(ALPHA * x1_ref[0] + g2 * acc_scr[...], g_ref[...], b_ref[...])


def _moe(x1, mod, h2, gate, w1, w3, w2, g, b, tm):
    B, S, D = x1.shape
    gs = GROUP_SIZE
    row = lambda b_, i, c: (b_, i, 0)
    const2 = lambda b_, i, c: (0, 0)
    return pl.pallas_call(
        _moe_kernel,
        grid=(B, S // tm, N_GROUPS),
        in_specs=[pl.BlockSpec((1, tm, D), row),
                  pl.BlockSpec((1, 6, D), lambda b_, i, c: (b_, 0, 0)),
                  pl.BlockSpec((1, tm, D), row),
                  pl.BlockSpec((1, tm, LANES), row),
                  pl.BlockSpec((gs, D, D_FF), lambda b_, i, c: (c, 0, 0)),
                  pl.BlockSpec((gs, D, D_FF), lambda b_, i, c: (c, 0, 0)),
                  pl.BlockSpec((gs, D_FF, D), lambda b_, i, c: (c, 0, 0)),
                  pl.BlockSpec((1, D), const2),
                  pl.BlockSpec((1, D), const2)],
        out_specs=pl.BlockSpec((1, tm, D), row),
        out_shape=jax.ShapeDtypeStruct((B, S, D), F32),
        scratch_shapes=[pltpu.VMEM((tm, D), F32),
                        pltpu.VMEM((tm, D + 2 * LANES), MXU_DTYPE),
                        pltpu.VMEM((8, tm), F32),
                        pltpu.VMEM((8, tm), F32)],
        compiler_params=_cparams(("parallel", "parallel", "arbitrary")),
        name="moe_ffn",
    )(x1, mod, h2, gate, w1, w3, w2, g.reshape(1, D), b.reshape(1, D))


def _rope_tables(seq):
    lane = np.arange(LANES)

    def tables(period, offset, rot, theta):
        half = rot // 2
        d = lane % period - offset
        first = (d >= 0) & (d < half)
        second = (d >= half) & (d < rot)
        fidx = np.where(first, d, np.where(second, d - half, 0))
        inv_freq = theta ** (-2.0 * jnp.arange(half, dtype=F32) / rot)
        ang = jnp.arange(seq, dtype=F32)[:, None] * inv_freq[None, :]
        cos_l = jnp.cos(ang)[:, fidx]
        sin_l = jnp.sin(ang)[:, fidx]
        cos = jnp.where((first | second)[None, :], cos_l, 1.0)
        s_up = jnp.where(first[None, :], -sin_l, 0.0)
        s_dn = jnp.where(second[None, :], sin_l, 0.0)
        return [cos, s_up, s_dn]

    tabs = (tables(DIFF_QK, 0, DIFF_QK // 4, ROPE_THETA)
            + tables(DIL_DIM, 0, DIL_DIM // 4, ROPE_THETA)
            + tables(LANES, KR_LANE, MLA_ROPE, MLA_THETA))
    return jnp.stack(tabs).astype(F32)


def _prep_layer(l, w_in, w_uq, w_ukv, w_out, w1, w3, w2):
    D = w_in.shape[1]
    wi = w_in[l]
    zeros = lambda n: jnp.zeros((D, n), wi.dtype)
    wi = jnp.concatenate([wi[:, :C_KR], zeros(KR_LANE), wi[:, C_KR:], zeros(LANES - KR_LANE - MLA_ROPE)], axis=1)
    uq = w_uq[l].reshape(MLA_Q_RANK, MLA_HEADS, MLA_NOPE + MLA_ROPE)
    uq = jnp.pad(uq, ((0, 0), (0, 0), (0, LANES - MLA_NOPE - MLA_ROPE))).reshape(MLA_Q_RANK, MLA_HEADS * LANES)
    ukv = w_ukv[l].reshape(MLA_KV_RANK, MLA_HEADS, MLA_NOPE + MLA_V)
    uk = jnp.pad(ukv[..., :MLA_NOPE], ((0, 0), (0, 0), (0, LANES - MLA_NOPE))).reshape(MLA_KV_RANK, MLA_HEADS * LANES)
    uv = ukv[..., MLA_NOPE:].reshape(MLA_KV_RANK, MLA_HEADS * MLA_V)
    c = lambda t: t.astype(MXU_DTYPE)
    return dict(w_in=c(wi), w_uq=c(uq), w_uk=c(uk), w_uv=c(uv), w_out=c(w_out[l]),
                w1=c(w1[l]), w3=c(w3[l]), w2=c(w2[l]))


def _tile(n, pref):
    return pref if n % pref == 0 else n


TOKEN_TM = 512


def _trunk(x, mods, tabs, layers, p):
    B, S, D = x.shape
    assert S % (DIL_DILS[-1] * DIL_SUB) == 0 and S % DIL_HALO == 0, S
    assert D == layers[0]['w_out'].shape[1], (D, layers[0]['w_out'].shape)
    tm = _tile(S, TOKEN_TM)
    tq = _tile(S, ATTN_TQ * ATTN_PARTS)
    tk = _tile(S, ATTN_TK)
    for l, lw in enumerate(layers):
        mod = mods[l]
        lam_init = 0.8 - 0.6 * math.exp(-0.3 * l)
        outs = _proj(x, mod, tabs, lw['w_in'], p['g_cq'][l], p['g_ckv'][l], lw['w_uq'], lw['w_uk'], lw['w_uv'], tm,
                     ln_in=(p['ln_in_g'], p['ln_in_b']) if l == 0 else None)
        if l == 0:
            x, outs = outs[0], outs[1:]
        qa, ka, va, qb, kb, vb, mq, mk, mv = outs
        oa = _diff_attn(qa, ka, va, p['lam_pack'][l], p['g_sub'][l], lam_init, tq, tk)
        oc = _mla_attn(mq, mk, mv, tq, tk)
        ob = _dil_attn(qb, kb, vb)
        x1, h2, gate = _post(x, mod, oa, ob, oc, lw['w_out'], p['ln1_g'][l], p['ln1_b'][l],
                             p['w_r_hi'], p['w_r_lo'], p['b_r'], tm)
        x = _moe(x1, mod, h2, gate, lw['w1'], lw['w3'], lw['w2'], p['ln2_g'][l], p['ln2_b'][l],
                 _tile(S, MOE_TM))
    return x


def kernel(x_prompt, x_sample, c_prompt, c_sample, ln_in_g, ln_in_b, w_router, b_router, w_ada, b_ada, w_in, lam_q1, lam_k1, lam_q2, lam_k2, g_subln, g_cq, g_ckv, w_uq, w_ukv, w_out, ln1_g, ln1_b, w1, w3, w2, ln2_g, ln2_b):
    depth = w_in.shape[0]
    D = x_prompt.shape[-1]
    bp, bs = c_prompt.shape[0], c_sample.shape[0]
    rows = -(-(bp + bs) // 8) * 8
    c_all = jnp.concatenate([c_prompt, c_sample, jnp.zeros((rows - bp - bs, D), F32)], axis=0)
    mods = _ada(c_all, w_ada, b_ada)
    mods_p = mods[:, :bp].reshape(depth, bp, 6, D)
    mods_s = mods[:, bp:bp + bs].reshape(depth, bs, 6, D)

    pad_l = lambda t: jnp.pad(t, ((0, 0), (0, LANES - t.shape[-1])))
    pad_r = lambda t: jnp.pad(t, ((0, 0),) * (t.ndim - 2) + ((0, SMALL_ROWS - t.shape[-2]), (0, 0)))
    p = dict(
        ln_in_g=ln_in_g, ln_in_b=ln_in_b, g_cq=g_cq, g_ckv=pad_r(g_ckv[:, None, :]),
        ln1_g=ln1_g, ln1_b=ln1_b, ln2_g=ln2_g, ln2_b=ln2_b,
        lam_pack=pad_r(jnp.stack([pad_l(lam_q1), pad_l(lam_k1), pad_l(lam_q2), pad_l(lam_k2)], axis=1)),
        g_sub=g_subln.reshape(depth, DIFF_V, 1),
        b_r=pad_r(pad_l(b_router.reshape(1, -1))),
    )
    w_r = pad_l(w_router)
    p['w_r_hi'] = w_r.astype(MXU_DTYPE)
    p['w_r_lo'] = (w_r - p['w_r_hi'].astype(F32)).astype(MXU_DTYPE)
    layers = [_prep_layer(l, w_in, w_uq, w_ukv, w_out, w1, w3, w2) for l in range(depth)]
    tabs = _rope_tables(max(x_prompt.shape[1], x_sample.shape[1]))
    y_prompt = _trunk(x_prompt, mods_p, tabs, layers, p)
    y_sample = _trunk(x_sample, mods_s, tabs, layers, p)
    return (y_prompt, y_sample)
```
